```python
import jax
import jax.numpy as jnp
from jax import lax
import numpy as np

D_MODEL = 1024
BATCH = 8
SEQ = 4096
DEPTH = 4

N_MEM = 256
BLOCK = 128
ROPE_THETA = 10000.0
EPS = 1e-6
NEG = -1e30
MAX_POS_OFFSET = 1024

A_HEADS = 8
A_KV_HEADS = 2
A_GROUP = A_HEADS // A_KV_HEADS
A_HD = 64
A_WINDOW = 128
B_HEADS = 8
B_Q_LORA = 384
B_KV_LORA = 256
B_NOPE = 64
B_ROPE = 32
B_V = 64
C_PATTERNS = ((128, 1), (512, 4), (2048, 16))
N_C_GROUPS = len(C_PATTERNS)
C_HEADS = 8
C_HD = 64
M_HEADS = 4
M_HD = 128

N_BRANCH = 4
BRANCH_W = 512
D_FF = 4 * D_MODEL

IN_SIZES = ((A_HEADS * A_HD, A_KV_HEADS * A_HD, A_KV_HEADS * A_HD, B_Q_LORA, B_KV_LORA, B_ROPE)
            + (C_HEADS * C_HD,) * (3 * N_C_GROUPS)
            + (M_HEADS * M_HD, N_BRANCH * D_MODEL))
N_IN = sum(IN_SIZES)
IN_SPLITS = tuple(int(s) for s in np.cumsum(IN_SIZES)[:-1])

kernel_name = 'hybrid_gated_swa_mla_dilated_mem_block'


def rms_norm(x, g):
    xf = x.astype(jnp.float32)
    y = xf * lax.rsqrt(jnp.mean(xf * xf, axis=-1, keepdims=True) + EPS)
    return (y * g.astype(jnp.float32)).astype(x.dtype)


def rope_tables(positions, dim):
    inv = ROPE_THETA ** (-jnp.arange(0, dim, 2, dtype=jnp.float32) / dim)
    ang = positions.astype(jnp.float32)[..., None] * inv
    return jnp.cos(ang), jnp.sin(ang)


def apply_rope(x, cos, sin):
    xf = x.astype(jnp.float32)
    x1, x2 = jnp.split(xf, 2, axis=-1)
    c = cos[:, :, None, :]
    s = sin[:, :, None, :]
    return jnp.concatenate([x1 * c - x2 * s, x2 * c + x1 * s], axis=-1).astype(x.dtype)


def banded_attention(q, k, v, max_dist, sinks=None):
    n, length, hk, grp, hd = q.shape
    nb = -(-length // BLOCK)
    pad = nb * BLOCK - length
    if pad:
        q = jnp.pad(q, ((0, 0), (0, pad), (0, 0), (0, 0), (0, 0)))
        k = jnp.pad(k, ((0, 0), (0, pad), (0, 0), (0, 0)))
        v = jnp.pad(v, ((0, 0), (0, pad), (0, 0), (0, 0)))
    qb = q.reshape(n, nb, BLOCK, hk, grp, hd)

    def two_blocks(t):
        tb = t.reshape(n, nb, BLOCK, hk, t.shape[-1])
        prev = jnp.pad(tb, ((0, 0), (1, 0), (0, 0), (0, 0), (0, 0)))[:, :-1]
        return jnp.concatenate([prev, tb], axis=2)

    kk = two_blocks(k)
    vv = two_blocks(v)
    s = jnp.einsum('nbqhgd,nbkhd->nbhgqk', qb, kk).astype(jnp.float32) * (hd ** -0.5)
    qi = jnp.arange(BLOCK)[:, None]
    kj = jnp.arange(2 * BLOCK)[None, :]
    dist = qi - kj + BLOCK
    band = (dist >= 0) & (dist <= max_dist)
    valid_prev = (jnp.arange(nb)[:, None, None] > 0) | (kj >= BLOCK)[None]
    mask = band[None] & valid_prev
    s = jnp.where(mask[None, :, None, None], s, NEG)
    lse = jax.nn.logsumexp(s, axis=-1)
    if sinks is not None:
        lse = jnp.logaddexp(lse, sinks.astype(jnp.float32)[None, None, :, :, None])
    p = jnp.exp(s - lse[..., None]).astype(v.dtype)
    o = jnp.einsum('nbhgqk,nbkhd->nbqhgd', p, vv).reshape(n, nb * BLOCK, hk, grp, hd)[:, :length]
    lse = lse.transpose(0, 1, 4, 2, 3).reshape(n, nb * BLOCK, hk, grp)[:, :length]
    return o, lse


def dilated_group(q, k, v, window, dilation):
    b, s, h, hd = q.shape
    length = s // dilation

    def gather(t):
        return t.reshape(b, length, dilation, h, hd).transpose(0, 2, 1, 3, 4).reshape(b * dilation, length, h, hd)

    o, lse = banded_attention(gather(q)[:, :, :, None, :], gather(k), gather(v), window // dilation)
    o = o[:, :, :, 0].reshape(b, dilation, length, h, hd).transpose(0, 2, 1, 3, 4).reshape(b, s, h, hd)
    lse = lse[..., 0].reshape(b, dilation, length, h).transpose(0, 2, 1, 3).reshape(b, s, h)
    return o, lse


def mla_causal_attention(qn, qp, kn, kp, v):
    b, s, h, _ = qn.shape
    nb = s // BLOCK
    scale = (qn.shape[-1] + qp.shape[-1]) ** -0.5
    kpos = jnp.arange(s)

    def one_block(i):
        start = i * BLOCK
        qnb = lax.dynamic_slice_in_dim(qn, start, BLOCK, axis=1)
        qpb = lax.dynamic_slice_in_dim(qp, start, BLOCK, axis=1)
        sc = (jnp.einsum('bqhd,bkhd->bhqk', qnb, kn)
              + jnp.einsum('bqhr,bkr->bhqk', qpb, kp)).astype(jnp.float32) * scale
        qpos = start + jnp.arange(BLOCK)
        sc = jnp.where(kpos[None, :] <= qpos[:, None], sc, NEG)
        p = jax.nn.softmax(sc, axis=-1).astype(v.dtype)
        return jnp.einsum('bhqk,bkhd->bqhd', p, v)

    o = lax.map(one_block, jnp.arange(nb))
    return o.transpose(1, 0, 2, 3, 4).reshape(b, s, h, v.shape[-1])


def memory_attention(q, k, v):
    s = jnp.einsum('bshd,bmhd->bhsm', q, k).astype(jnp.float32) * (q.shape[-1] ** -0.5)
    p = jax.nn.softmax(s, axis=-1).astype(v.dtype)
    return jnp.einsum('bhsm,bmhd->bshd', p, v)


def _fwd_setup_inputs(seed: int = 0) -> dict:
    key = jax.random.key(seed)
    ks = jax.random.split(key, 32)
    f32 = jnp.float32
    L = DEPTH
    D = D_MODEL

    def nrm(k, shape, scale):
        return jax.random.normal(k, shape, f32) * scale

    def gain(k, shape):
        return 1.0 + 0.1 * jax.random.normal(k, shape, f32)

    x = nrm(ks[0], (BATCH, SEQ, D), 1.0)
    mem = nrm(ks[1], (BATCH, N_MEM, D), 1.0)
    offsets = jax.random.randint(ks[2], (BATCH, 1), 0, MAX_POS_OFFSET, dtype=jnp.int32)
    positions = (offsets + jnp.arange(SEQ, dtype=jnp.int32)[None, :]).astype(jnp.int32)
    return {
        'x': x,
        'mem': mem,
        'positions': positions,
        'g_mix': gain(ks[3], (L, D)),
        'w_in': nrm(ks[4], (L, D, N_IN), D ** -0.5),
        'b_gate': nrm(ks[5], (L, N_BRANCH * D), 0.1),
        'a_qn': gain(ks[6], (L, A_HD)),
        'a_kn': gain(ks[7], (L, A_HD)),
        'a_sink': nrm(ks[8], (L, A_HEADS), 0.5),
        'b_qa_norm': gain(ks[9], (L, B_Q_LORA)),
        'b_kva_norm': gain(ks[10], (L, B_KV_LORA)),
        'b_w_uq': nrm(ks[11], (L, B_Q_LORA, B_HEADS * (B_NOPE + B_ROPE)), B_Q_LORA ** -0.5),
        'b_w_ukv': nrm(ks[12], (L, B_KV_LORA, B_HEADS * (B_NOPE + B_V)), B_KV_LORA ** -0.5),
        'b_qn': gain(ks[13], (L, B_NOPE + B_ROPE)),
        'b_kn': gain(ks[14], (L, B_NOPE + B_ROPE)),
        'c_qn': gain(ks[15], (L, N_C_GROUPS, C_HD)),
        'c_kn': gain(ks[16], (L, N_C_GROUPS, C_HD)),
        'm_g_mem': gain(ks[17], (L, D)),
        'm_w_kv': nrm(ks[18], (L, D, 2 * M_HEADS * M_HD), D ** -0.5),
        'm_qn': gain(ks[19], (L, M_HD)),
        'm_kn': gain(ks[20], (L, M_HD)),
        'w_branch': nrm(ks[21], (L, N_BRANCH, BRANCH_W, D), BRANCH_W ** -0.5),
        'w_out': nrm(ks[22], (L, D, D), D ** -0.5),
        'g_mlp': gain(ks[23], (L, D)),
        'w_up': nrm(ks[24], (L, D, D_FF), D ** -0.5),
        'w_down': nrm(ks[25], (L, D_FF, D), D_FF ** -0.5),
    }


def _fwd_reference(x, mem, positions, g_mix, w_in, b_gate, a_qn, a_kn, a_sink, b_qa_norm, b_kva_norm,
              b_w_uq, b_w_ukv, b_qn, b_kn, c_qn, c_kn, m_g_mem, m_w_kv, m_qn, m_kn,
              w_branch, w_out, g_mlp, w_up, w_down):
    B, S, D = x.shape
    cos_h, sin_h = rope_tables(positions, A_HD)
    cos_r, sin_r = rope_tables(positions, B_ROPE)
    for l in range(DEPTH):
        h = rms_norm(x, g_mix[l])
        parts = jnp.split(h @ w_in[l], IN_SPLITS, axis=-1)
        a_q, a_k, a_v, b_cq, b_ckv, b_kr = parts[:6]
        c_parts = parts[6:6 + 3 * N_C_GROUPS]
        m_q, gate_pre = parts[6 + 3 * N_C_GROUPS:]

        qa = apply_rope(rms_norm(a_q.reshape(B, S, A_HEADS, A_HD), a_qn[l]), cos_h, sin_h)
        ka = apply_rope(rms_norm(a_k.reshape(B, S, A_KV_HEADS, A_HD), a_kn[l]), cos_h, sin_h)
        va = a_v.reshape(B, S, A_KV_HEADS, A_HD)
        o_a, _ = banded_attention(qa.reshape(B, S, A_KV_HEADS, A_GROUP, A_HD), ka, va,
                                  A_WINDOW - 1, a_sink[l].reshape(A_KV_HEADS, A_GROUP))
        o_a = o_a.reshape(B, S, BRANCH_W)

        q_up = (rms_norm(b_cq, b_qa_norm[l]) @ b_w_uq[l]).reshape(B, S, B_HEADS, B_NOPE + B_ROPE)
        kv_up = (rms_norm(b_ckv, b_kva_norm[l]) @ b_w_ukv[l]).reshape(B, S, B_HEADS, B_NOPE + B_V)
        qn = rms_norm(q_up[..., :B_NOPE], b_qn[l, :B_NOPE])
        qp = apply_rope(rms_norm(q_up[..., B_NOPE:], b_qn[l, B_NOPE:]), cos_r, sin_r)
        kn = rms_norm(kv_up[..., :B_NOPE], b_kn[l, :B_NOPE])
        vb = kv_up[..., B_NOPE:]
        kp = apply_rope(rms_norm(b_kr, b_kn[l, B_NOPE:])[:, :, None, :], cos_r, sin_r)[:, :, 0, :]
        o_b = mla_causal_attention(qn, qp, kn, kp, vb).reshape(B, S, BRANCH_W)

        outs = []
        lses = []
        for g, (win, dil) in enumerate(C_PATTERNS):
            cq, ck, cv = c_parts[3 * g:3 * g + 3]
            qc = apply_rope(rms_norm(cq.reshape(B, S, C_HEADS, C_HD), c_qn[l, g]), cos_h, sin_h)
            kc = apply_rope(rms_norm(ck.reshape(B, S, C_HEADS, C_HD), c_kn[l, g]), cos_h, sin_h)
            vc = cv.reshape(B, S, C_HEADS, C_HD)
            o_g, lse_g = dilated_group(qc, kc, vc, win, dil)
            outs.append(o_g)
            lses.append(lse_g)
        wts = jax.nn.softmax(jnp.stack(lses, axis=0), axis=0).astype(x.dtype)
        o_c = jnp.einsum('gbsh,gbshd->bshd', wts, jnp.stack(outs, axis=0)).reshape(B, S, BRANCH_W)

        mkv = rms_norm(mem, m_g_mem[l]) @ m_w_kv[l]
        mk = rms_norm(mkv[..., :M_HEADS * M_HD].reshape(B, N_MEM, M_HEADS, M_HD), m_kn[l])
        mv = mkv[..., M_HEADS * M_HD:].reshape(B, N_MEM, M_HEADS, M_HD)
        mq = rms_norm(m_q.reshape(B, S, M_HEADS, M_HD), m_qn[l])
        o_m = memory_attention(mq, mk, mv).reshape(B, S, BRANCH_W)

        o = jnp.stack([o_a, o_b, o_c, o_m], axis=2)
        y = jnp.einsum('bsnc,ncd->bsnd', o, w_branch[l])
        gates = jax.nn.sigmoid((gate_pre + b_gate[l]).astype(jnp.float32)).astype(x.dtype)
        gates = gates.reshape(B, S, N_BRANCH, D)
        x = x + jnp.einsum('bsnd,de->bse', gates * y, w_out[l])

        u = rms_norm(x, g_mlp[l]) @ w_up[l]
        x = x + jnp.square(jax.nn.relu(u)) @ w_down[l]
    return x


import jax as _jax
import jax.numpy as _jnp

TWIN_FORMAT = 'train_step'
FWD_PARAMS = ['x', 'mem', 'positions', 'g_mix', 'w_in', 'b_gate', 'a_qn', 'a_kn', 'a_sink', 'b_qa_norm', 'b_kva_norm', 'b_w_uq', 'b_w_ukv', 'b_qn', 'b_kn', 'c_qn', 'c_kn', 'm_g_mem', 'm_w_kv', 'm_qn', 'm_kn', 'w_branch', 'w_out', 'g_mlp', 'w_up', 'w_down']
TWIN_WEIGHTS = ['g_mix', 'w_in', 'b_gate', 'a_qn', 'a_kn', 'a_sink', 'b_qa_norm', 'b_kva_norm', 'b_w_uq', 'b_w_ukv', 'b_qn', 'b_kn', 'c_qn', 'c_kn', 'm_g_mem', 'm_w_kv', 'm_qn', 'm_kn', 'w_branch', 'w_out', 'g_mlp', 'w_up', 'w_down']
TWIN_DIFF_INPUT = 'x'
TWIN_INPUTS = ['x', 'mem', 'positions', 'g_mix', 'w_in', 'b_gate', 'a_qn', 'a_kn', 'a_sink', 'b_qa_norm', 'b_kva_norm', 'b_w_uq', 'b_w_ukv', 'b_qn', 'b_kn', 'c_qn', 'c_kn', 'm_g_mem', 'm_w_kv', 'm_qn', 'm_kn', 'w_branch', 'w_out', 'g_mlp', 'w_up', 'w_down', 'loss_target', 'm_g_mix', 'm_w_in', 'm_b_gate', 'm_a_qn', 'm_a_kn', 'm_a_sink', 'm_b_qa_norm', 'm_b_kva_norm', 'm_b_w_uq', 'm_b_w_ukv', 'm_b_qn', 'm_b_kn', 'm_c_qn', 'm_c_kn', 'm_m_g_mem', 'm_m_w_kv', 'm_m_qn', 'm_m_kn', 'm_w_branch', 'm_w_out', 'm_g_mlp', 'm_w_up', 'm_w_down', 'v_g_mix', 'v_w_in', 'v_b_gate', 'v_a_qn', 'v_a_kn', 'v_a_sink', 'v_b_qa_norm', 'v_b_kva_norm', 'v_b_w_uq', 'v_b_w_ukv', 'v_b_qn', 'v_b_kn', 'v_c_qn', 'v_c_kn', 'v_m_g_mem', 'v_m_w_kv', 'v_m_qn', 'v_m_kn', 'v_w_branch', 'v_w_out', 'v_g_mlp', 'v_w_up', 'v_w_down']
TWIN_OUTPUTS = ['loss', 'grad_x', 'grad_g_mix', 'grad_w_in', 'grad_b_gate', 'grad_a_qn', 'grad_a_kn', 'grad_a_sink', 'grad_b_qa_norm', 'grad_b_kva_norm', 'grad_b_w_uq', 'grad_b_w_ukv', 'grad_b_qn', 'grad_b_kn', 'grad_c_qn', 'grad_c_kn', 'grad_m_g_mem', 'grad_m_w_kv', 'grad_m_qn', 'grad_m_kn', 'grad_w_branch', 'grad_w_out', 'grad_g_mlp', 'grad_w_up', 'grad_w_down', 'delta_g_mix', 'delta_w_in', 'delta_b_gate', 'delta_a_qn', 'delta_a_kn', 'delta_a_sink', 'delta_b_qa_norm', 'delta_b_kva_norm', 'delta_b_w_uq', 'delta_b_w_ukv', 'delta_b_qn', 'delta_b_kn', 'delta_c_qn', 'delta_c_kn', 'delta_m_g_mem', 'delta_m_w_kv', 'delta_m_qn', 'delta_m_kn', 'delta_w_branch', 'delta_w_out', 'delta_g_mlp', 'delta_w_up', 'delta_w_down', 'new_m_g_mix', 'new_m_w_in', 'new_m_b_gate', 'new_m_a_qn', 'new_m_a_kn', 'new_m_a_sink', 'new_m_b_qa_norm', 'new_m_b_kva_norm', 'new_m_b_w_uq', 'new_m_b_w_ukv', 'new_m_b_qn', 'new_m_b_kn', 'new_m_c_qn', 'new_m_c_kn', 'new_m_m_g_mem', 'new_m_m_w_kv', 'new_m_m_qn', 'new_m_m_kn', 'new_m_w_branch', 'new_m_w_out', 'new_m_g_mlp', 'new_m_w_up', 'new_m_w_down', 'new_v_g_mix', 'new_v_w_in', 'new_v_b_gate', 'new_v_a_qn', 'new_v_a_kn', 'new_v_a_sink', 'new_v_b_qa_norm', 'new_v_b_kva_norm', 'new_v_b_w_uq', 'new_v_b_w_ukv', 'new_v_b_qn', 'new_v_b_kn', 'new_v_c_qn', 'new_v_c_kn', 'new_v_m_g_mem', 'new_v_m_w_kv', 'new_v_m_qn', 'new_v_m_kn', 'new_v_w_branch', 'new_v_w_out', 'new_v_g_mlp', 'new_v_w_up', 'new_v_w_down']
TWIN_LEAF_KINDS = {'loss': 'loss', 'grad_x': 'grad_x', 'grad_g_mix': 'grad_w', 'grad_w_in': 'grad_w', 'grad_b_gate': 'grad_w', 'grad_a_qn': 'grad_w', 'grad_a_kn': 'grad_w', 'grad_a_sink': 'grad_w', 'grad_b_qa_norm': 'grad_w', 'grad_b_kva_norm': 'grad_w', 'grad_b_w_uq': 'grad_w', 'grad_b_w_ukv': 'grad_w', 'grad_b_qn': 'grad_w', 'grad_b_kn': 'grad_w', 'grad_c_qn': 'grad_w', 'grad_c_kn': 'grad_w', 'grad_m_g_mem': 'grad_w', 'grad_m_w_kv': 'grad_w', 'grad_m_qn': 'grad_w', 'grad_m_kn': 'grad_w', 'grad_w_branch': 'grad_w', 'grad_w_out': 'grad_w', 'grad_g_mlp': 'grad_w', 'grad_w_up': 'grad_w', 'grad_w_down': 'grad_w', 'delta_g_mix': 'delta_w', 'delta_w_in': 'delta_w', 'delta_b_gate': 'delta_w', 'delta_a_qn': 'delta_w', 'delta_a_kn': 'delta_w', 'delta_a_sink': 'delta_w', 'delta_b_qa_norm': 'delta_w', 'delta_b_kva_norm': 'delta_w', 'delta_b_w_uq': 'delta_w', 'delta_b_w_ukv': 'delta_w', 'delta_b_qn': 'delta_w', 'delta_b_kn': 'delta_w', 'delta_c_qn': 'delta_w', 'delta_c_kn': 'delta_w', 'delta_m_g_mem': 'delta_w', 'delta_m_w_kv': 'delta_w', 'delta_m_qn': 'delta_w', 'delta_m_kn': 'delta_w', 'delta_w_branch': 'delta_w', 'delta_w_out': 'delta_w', 'delta_g_mlp': 'delta_w', 'delta_w_up': 'delta_w', 'delta_w_down': 'delta_w', 'new_m_g_mix': 'new_m', 'new_m_w_in': 'new_m', 'new_m_b_gate': 'new_m', 'new_m_a_qn': 'new_m', 'new_m_a_kn': 'new_m', 'new_m_a_sink': 'new_m', 'new_m_b_qa_norm': 'new_m', 'new_m_b_kva_norm': 'new_m', 'new_m_b_w_uq': 'new_m', 'new_m_b_w_ukv': 'new_m', 'new_m_b_qn': 'new_m', 'new_m_b_kn': 'new_m', 'new_m_c_qn': 'new_m', 'new_m_c_kn': 'new_m', 'new_m_m_g_mem': 'new_m', 'new_m_m_w_kv': 'new_m', 'new_m_m_qn': 'new_m', 'new_m_m_kn': 'new_m', 'new_m_w_branch': 'new_m', 'new_m_w_out': 'new_m', 'new_m_g_mlp': 'new_m', 'new_m_w_up': 'new_m', 'new_m_w_down': 'new_m', 'new_v_g_mix': 'new_v', 'new_v_w_in': 'new_v', 'new_v_b_gate': 'new_v', 'new_v_a_qn': 'new_v', 'new_v_a_kn': 'new_v', 'new_v_a_sink': 'new_v', 'new_v_b_qa_norm': 'new_v', 'new_v_b_kva_norm': 'new_v', 'new_v_b_w_uq': 'new_v', 'new_v_b_w_ukv': 'new_v', 'new_v_b_qn': 'new_v', 'new_v_b_kn': 'new_v', 'new_v_c_qn': 'new_v', 'new_v_c_kn': 'new_v', 'new_v_m_g_mem': 'new_v', 'new_v_m_w_kv': 'new_v', 'new_v_m_qn': 'new_v', 'new_v_m_kn': 'new_v', 'new_v_w_branch': 'new_v', 'new_v_w_out': 'new_v', 'new_v_g_mlp': 'new_v', 'new_v_w_up': 'new_v', 'new_v_w_down': 'new_v'}


def _forward(args):
    return _fwd_reference(*[args[k] for k in FWD_PARAMS])


def _output_shape():
    out = _jax.eval_shape(lambda: _forward(_fwd_setup_inputs(0)))
    return out.shape, out.dtype

N_MICROBATCH = 1
ADAM_LR = 0.001
ADAM_B1 = 0.9
ADAM_B2 = 0.999
ADAM_EPS = 1e-08
ADAM_WD = 0.01
ADAM_STEP = 10
PER_EXAMPLE_BATCH_AXIS = {'x': 0, 'mem': 0, 'positions': 0, 'loss_target': 0}
SHARED_INPUTS = []
_WEIGHT_DTYPES = {'g_mix': _jnp.float32, 'w_in': _jnp.float32, 'b_gate': _jnp.float32, 'a_qn': _jnp.float32, 'a_kn': _jnp.float32, 'a_sink': _jnp.float32, 'b_qa_norm': _jnp.float32, 'b_kva_norm': _jnp.float32, 'b_w_uq': _jnp.float32, 'b_w_ukv': _jnp.float32, 'b_qn': _jnp.float32, 'b_kn': _jnp.float32, 'c_qn': _jnp.float32, 'c_kn': _jnp.float32, 'm_g_mem': _jnp.float32, 'm_w_kv': _jnp.float32, 'm_qn': _jnp.float32, 'm_kn': _jnp.float32, 'w_branch': _jnp.float32, 'w_out': _jnp.float32, 'g_mlp': _jnp.float32, 'w_up': _jnp.float32, 'w_down': _jnp.float32}
MOMENT_SCALE = {'g_mix': 1.484054e+01, 'w_in': 4.666283e+00, 'b_gate': 2.875366e+00, 'a_qn': 1.301568e+00, 'a_kn': 1.245647e+00, 'a_sink': 9.772931e-01, 'b_qa_norm': 4.795318e-01, 'b_kva_norm': 2.061420e+01, 'b_w_uq': 3.127239e-01, 'b_w_ukv': 1.017259e+01, 'b_qn': 9.272376e-01, 'b_kn': 9.457911e-01, 'c_qn': 1.436651e+00, 'c_kn': 1.410804e+00, 'm_g_mem': 1.286546e+00, 'm_w_kv': 1.214602e+00, 'm_qn': 1.290160e+00, 'm_kn': 1.291542e+00, 'w_branch': 7.112060e+00, 'w_out': 1.370204e+01, 'g_mlp': 1.020432e+02, 'w_up': 9.977128e+00, 'w_down': 3.447124e+01}


def _to_microbatches(a, axis):
    t = _jnp.moveaxis(a, axis, 0)
    t = t.reshape((N_MICROBATCH, t.shape[0] // N_MICROBATCH) + t.shape[1:])
    return _jnp.moveaxis(t, 1, axis + 1)


def setup_inputs(seed: int = 0) -> dict:
    inp = _fwd_setup_inputs(seed)
    key = _jax.random.fold_in(_jax.random.key(seed), 7919)
    shape, _ = _output_shape()
    out = dict(inp)
    out["loss_target"] = _jax.random.normal(_jax.random.fold_in(key, 0), shape, _jnp.float32)
    for i, name in enumerate(TWIN_WEIGHTS):
        w = inp[name].astype(_jnp.float32)
        if MOMENT_SCALE is None:
            s = _jnp.sqrt(_jnp.mean(_jnp.square(w)) + 1e-30)
        else:
            s = MOMENT_SCALE[name]
        km, kv = _jax.random.split(_jax.random.fold_in(key, i + 1))
        out[name] = w
        out["m_" + name] = s * _jax.random.normal(km, w.shape, _jnp.float32)
        out["v_" + name] = (s * s) * _jax.random.uniform(kv, w.shape, _jnp.float32, 0.5, 1.5)
    if N_MICROBATCH > 1:
        for name, axis in PER_EXAMPLE_BATCH_AXIS.items():
            out[name] = _to_microbatches(out[name], axis)
    return {'x': out['x'], 'mem': out['mem'], 'positions': out['positions'], 'g_mix': out['g_mix'], 'w_in': out['w_in'], 'b_gate': out['b_gate'], 'a_qn': out['a_qn'], 'a_kn': out['a_kn'], 'a_sink': out['a_sink'], 'b_qa_norm': out['b_qa_norm'], 'b_kva_norm': out['b_kva_norm'], 'b_w_uq': out['b_w_uq'], 'b_w_ukv': out['b_w_ukv'], 'b_qn': out['b_qn'], 'b_kn': out['b_kn'], 'c_qn': out['c_qn'], 'c_kn': out['c_kn'], 'm_g_mem': out['m_g_mem'], 'm_w_kv': out['m_w_kv'], 'm_qn': out['m_qn'], 'm_kn': out['m_kn'], 'w_branch': out['w_branch'], 'w_out': out['w_out'], 'g_mlp': out['g_mlp'], 'w_up': out['w_up'], 'w_down': out['w_down'], 'loss_target': out['loss_target'], 'm_g_mix': out['m_g_mix'], 'm_w_in': out['m_w_in'], 'm_b_gate': out['m_b_gate'], 'm_a_qn': out['m_a_qn'], 'm_a_kn': out['m_a_kn'], 'm_a_sink': out['m_a_sink'], 'm_b_qa_norm': out['m_b_qa_norm'], 'm_b_kva_norm': out['m_b_kva_norm'], 'm_b_w_uq': out['m_b_w_uq'], 'm_b_w_ukv': out['m_b_w_ukv'], 'm_b_qn': out['m_b_qn'], 'm_b_kn': out['m_b_kn'], 'm_c_qn': out['m_c_qn'], 'm_c_kn': out['m_c_kn'], 'm_m_g_mem': out['m_m_g_mem'], 'm_m_w_kv': out['m_m_w_kv'], 'm_m_qn': out['m_m_qn'], 'm_m_kn': out['m_m_kn'], 'm_w_branch': out['m_w_branch'], 'm_w_out': out['m_w_out'], 'm_g_mlp': out['m_g_mlp'], 'm_w_up': out['m_w_up'], 'm_w_down': out['m_w_down'], 'v_g_mix': out['v_g_mix'], 'v_w_in': out['v_w_in'], 'v_b_gate': out['v_b_gate'], 'v_a_qn': out['v_a_qn'], 'v_a_kn': out['v_a_kn'], 'v_a_sink': out['v_a_sink'], 'v_b_qa_norm': out['v_b_qa_norm'], 'v_b_kva_norm': out['v_b_kva_norm'], 'v_b_w_uq': out['v_b_w_uq'], 'v_b_w_ukv': out['v_b_w_ukv'], 'v_b_qn': out['v_b_qn'], 'v_b_kn': out['v_b_kn'], 'v_c_qn': out['v_c_qn'], 'v_c_kn': out['v_c_kn'], 'v_m_g_mem': out['v_m_g_mem'], 'v_m_w_kv': out['v_m_w_kv'], 'v_m_qn': out['v_m_qn'], 'v_m_kn': out['v_m_kn'], 'v_w_branch': out['v_w_branch'], 'v_w_out': out['v_w_out'], 'v_g_mlp': out['v_g_mlp'], 'v_w_up': out['v_w_up'], 'v_w_down': out['v_w_down']}


def _loss(weights, diff, rest, loss_target):
    with _jax.named_scope("forward"):
        args = {**rest, TWIN_DIFF_INPUT: diff, **{k: w.astype(_WEIGHT_DTYPES[k]) for k, w in weights.items()}}
        y = _forward(args)
    with _jax.named_scope("loss_head"):
        err = _jnp.square(y.astype(_jnp.float32) - loss_target)
        return 0.5 * _jnp.sum(_jnp.mean(err, axis=-1)) if err.ndim else 0.5 * err


def _adamw(w, g, m, v):
    m = ADAM_B1 * m + (1.0 - ADAM_B1) * g
    v = ADAM_B2 * v + (1.0 - ADAM_B2) * _jnp.square(g)
    m_hat = m / (1.0 - ADAM_B1 ** ADAM_STEP)
    v_hat = v / (1.0 - ADAM_B2 ** ADAM_STEP)
    delta = -ADAM_LR * (m_hat / (_jnp.sqrt(v_hat) + ADAM_EPS) + ADAM_WD * w)
    return delta, m, v


def reference(x, mem, positions, g_mix, w_in, b_gate, a_qn, a_kn, a_sink, b_qa_norm, b_kva_norm, b_w_uq, b_w_ukv, b_qn, b_kn, c_qn, c_kn, m_g_mem, m_w_kv, m_qn, m_kn, w_branch, w_out, g_mlp, w_up, w_down, loss_target, m_g_mix, m_w_in, m_b_gate, m_a_qn, m_a_kn, m_a_sink, m_b_qa_norm, m_b_kva_norm, m_b_w_uq, m_b_w_ukv, m_b_qn, m_b_kn, m_c_qn, m_c_kn, m_m_g_mem, m_m_w_kv, m_m_qn, m_m_kn, m_w_branch, m_w_out, m_g_mlp, m_w_up, m_w_down, v_g_mix, v_w_in, v_b_gate, v_a_qn, v_a_kn, v_a_sink, v_b_qa_norm, v_b_kva_norm, v_b_w_uq, v_b_w_ukv, v_b_qn, v_b_kn, v_c_qn, v_c_kn, v_m_g_mem, v_m_w_kv, v_m_qn, v_m_kn, v_w_branch, v_w_out, v_g_mlp, v_w_up, v_w_down):
    given = dict(x=x, mem=mem, positions=positions, g_mix=g_mix, w_in=w_in, b_gate=b_gate, a_qn=a_qn, a_kn=a_kn, a_sink=a_sink, b_qa_norm=b_qa_norm, b_kva_norm=b_kva_norm, b_w_uq=b_w_uq, b_w_ukv=b_w_ukv, b_qn=b_qn, b_kn=b_kn, c_qn=c_qn, c_kn=c_kn, m_g_mem=m_g_mem, m_w_kv=m_w_kv, m_qn=m_qn, m_kn=m_kn, w_branch=w_branch, w_out=w_out, g_mlp=g_mlp, w_up=w_up, w_down=w_down, loss_target=loss_target, m_g_mix=m_g_mix, m_w_in=m_w_in, m_b_gate=m_b_gate, m_a_qn=m_a_qn, m_a_kn=m_a_kn, m_a_sink=m_a_sink, m_b_qa_norm=m_b_qa_norm, m_b_kva_norm=m_b_kva_norm, m_b_w_uq=m_b_w_uq, m_b_w_ukv=m_b_w_ukv, m_b_qn=m_b_qn, m_b_kn=m_b_kn, m_c_qn=m_c_qn, m_c_kn=m_c_kn, m_m_g_mem=m_m_g_mem, m_m_w_kv=m_m_w_kv, m_m_qn=m_m_qn, m_m_kn=m_m_kn, m_w_branch=m_w_branch, m_w_out=m_w_out, m_g_mlp=m_g_mlp, m_w_up=m_w_up, m_w_down=m_w_down, v_g_mix=v_g_mix, v_w_in=v_w_in, v_b_gate=v_b_gate, v_a_qn=v_a_qn, v_a_kn=v_a_kn, v_a_sink=v_a_sink, v_b_qa_norm=v_b_qa_norm, v_b_kva_norm=v_b_kva_norm, v_b_w_uq=v_b_w_uq, v_b_w_ukv=v_b_w_ukv, v_b_qn=v_b_qn, v_b_kn=v_b_kn, v_c_qn=v_c_qn, v_c_kn=v_c_kn, v_m_g_mem=v_m_g_mem, v_m_w_kv=v_m_w_kv, v_m_qn=v_m_qn, v_m_kn=v_m_kn, v_w_branch=v_w_branch, v_w_out=v_w_out, v_g_mlp=v_g_mlp, v_w_up=v_w_up, v_w_down=v_w_down)
    weights = {n: given[n] for n in TWIN_WEIGHTS}
    shared = {n: given[n] for n in SHARED_INPUTS}
    per_example = {n: given[n] for n in ['x', 'mem', 'positions']}
    grad_fn = _jax.value_and_grad(_loss, argnums=(0, 1))

    def one_microbatch(ex, loss_target):
        ex = dict(ex)
        diff = ex.pop(TWIN_DIFF_INPUT)
        return grad_fn(weights, diff, {**shared, **ex}, loss_target)

    if N_MICROBATCH == 1:
        loss, (grad_w, grad_x) = one_microbatch(per_example, given["loss_target"])
    else:
        def body(carry, xs):
            loss_sum, grad_sum = carry
            l_k, (gw_k, gx_k) = one_microbatch(xs[0], xs[1])
            with _jax.named_scope("update"):
                return (loss_sum + l_k, _jax.tree.map(_jnp.add, grad_sum, gw_k)), gx_k

        init = (_jnp.zeros((), _jnp.float32), _jax.tree.map(_jnp.zeros_like, weights))
        (loss, grad_w), grad_x = _jax.lax.scan(body, init, (per_example, given["loss_target"]))
    with _jax.named_scope("update"):
        delta_w, new_m, new_v = {}, {}, {}
        for n in TWIN_WEIGHTS:
            delta_w[n], new_m[n], new_v[n] = _adamw(weights[n], grad_w[n], given["m_" + n], given["v_" + n])
    return (loss, grad_x, *[grad_w[n] for n in TWIN_WEIGHTS], *[delta_w[n] for n in TWIN_WEIGHTS],
            *[new_m[n] for n in TWIN_WEIGHTS], *[new_v[n] for n in TWIN_WEIGHTS])
```

```python
import functools

import numpy as np
import jax
import jax.numpy as jnp
from jax import lax
from jax.experimental import pallas as pl
from jax.experimental.pallas import tpu as pltpu

F32 = jnp.float32
BF16 = jnp.bfloat16

LANES = 128
N_DEV = 8
AXES = ("x", "y", "c")
VMEM_LIMIT = 48 * 1024 * 1024

EPS = 1e-6
NEG = -1e30
ROPE_THETA = 10000.0
BLOCK = 128

A_HEADS, A_KV_HEADS, A_HD, A_WINDOW = 8, 2, 64, 128
B_HEADS, B_Q_LORA, B_KV_LORA, B_NOPE, B_ROPE, B_V = 8, 384, 256, 64, 32, 64
C_PATTERNS = ((128, 1), (512, 4), (2048, 16))
C_HEADS, C_HD = 8, 64
M_HEADS, M_HD = 4, 128
BRANCH_W = 512
N_BRANCH = 4

ADAM_LR, ADAM_B1, ADAM_B2, ADAM_EPS, ADAM_WD, ADAM_STEP = 0.001, 0.9, 0.999, 1e-08, 0.01, 10

NT_DIMS = (((1,), (1,)), ((), ()))
TN_DIMS = (((0,), (0,)), ((), ()))
NN_DIMS = (((1,), (0,)), ((), ()))


def _pick(n, cands):
    for c in cands:
        if n % c == 0:
            return c
    return n


def _cparams(sem):
    return pltpu.CompilerParams(dimension_semantics=sem, vmem_limit_bytes=VMEM_LIMIT)


def _dot(a, b, dims=NN_DIMS):
    return lax.dot_general(a.astype(BF16), b.astype(BF16), dims, preferred_element_type=F32)


def _in_layout(d_model):
    orig = ([("a_q", 512), ("a_k", 128), ("a_v", 128), ("b_cq", B_Q_LORA), ("b_ckv", B_KV_LORA), ("b_kr", B_ROPE)]
            + [("c%d" % i, 512) for i in range(9)] + [("m_q", 512), ("gate", N_BRANCH * d_model)])
    order = ["gate", "a_q"] + ["c%d" % i for i in range(9)] + ["m_q", "b_ckv", "b_cq", "a_k", "a_v", "b_kr"]
    o_off, off = {}, 0
    for name, w in orig:
        o_off[name] = (off, w)
        off += w
    lay, noff = {}, 0
    for name in order:
        oo, w = o_off[name]
        nw = LANES if name == "b_kr" else w
        assert noff % nw == 0, (name, noff, nw)
        lay[name] = dict(orig=oo, w=w, off=noff, nw=nw)
        noff += nw
    assert noff % 512 == 0
    return lay, [n for n, _ in orig], order, off, noff


def _permute_cols(w, lay, order):
    cols = []
    for name in order:
        e = lay[name]
        p = w[..., e["orig"]:e["orig"] + e["w"]]
        if e["nw"] != e["w"]:
            p = jnp.pad(p, [(0, 0)] * (w.ndim - 1) + [(0, e["nw"] - e["w"])])
        cols.append(p)
    return jnp.concatenate(cols, axis=-1)


def _unpermute_cols(w, lay, orig_names):
    return jnp.concatenate([w[..., lay[n]["off"]:lay[n]["off"] + lay[n]["w"]] for n in orig_names], axis=-1)


def _pattern(groups):
    ids = np.repeat(np.arange(len(groups)), groups)
    assert ids.shape[0] == LANES
    gmat = (ids[:, None] == ids[None, :]).astype(np.float32)
    invn = (1.0 / np.asarray(groups, np.float32))[ids][None, :]
    return jnp.asarray(gmat, BF16), jnp.asarray(invn, F32)


def _gsum(v, gmat):
    hi = v.astype(BF16)
    r1 = v - hi.astype(F32)
    mid = r1.astype(BF16)
    lo = (r1 - mid.astype(F32)).astype(BF16)
    dot = lambda p: jnp.dot(p, gmat, preferred_element_type=F32)
    return dot(hi) + dot(mid) + dot(lo)


def _swap_half(v, lomask, half):
    return jnp.where(lomask > 0.5, pltpu.roll(v, LANES - half, 1), pltpu.roll(v, half, 1))


def _rows_spec(tr, width, off=0):
    return pl.BlockSpec((tr, width), lambda i: (i, off))


def _full_spec(shape):
    return pl.BlockSpec(shape, lambda i: (0,) * len(shape))


def _mm(a, b, *, ta=False, tb=False, out_dtype=F32, mode=None, extra=None, name):
    if ta:
        kd, m = a.shape
    else:
        m, kd = a.shape
    if tb:
        n, kb = b.shape
    else:
        kb, n = b.shape
    assert kd == kb, (a.shape, b.shape, ta, tb)
    tm = _pick(m, (1024, 512, 256, 128))
    tn = _pick(n, (1024, 1536, 512, 384, 256, 128))
    tk = _pick(kd, (512, 384, 256, 128))
    nk = kd // tk
    a_spec = (pl.BlockSpec((tk, tm), lambda i, j, k: (k, i)) if ta else pl.BlockSpec((tm, tk), lambda i, j, k: (i, k)))
    b_spec = (pl.BlockSpec((tn, tk), lambda i, j, k: (j, k)) if tb else pl.BlockSpec((tk, tn), lambda i, j, k: (k, j)))
    o_spec = pl.BlockSpec((tm, tn), lambda i, j, k: (i, j))
    dims = (((0 if ta else 1,), (1 if tb else 0,)), ((), ()))
    has_extra = mode in ("add", "drelu2")

    def body(*refs):
        a_ref, b_ref = refs[0], refs[1]
        e_ref = refs[2] if has_extra else None
        outs = refs[2 + has_extra:-1]
        acc = refs[-1]
        k = pl.program_id(2)

        @pl.when(k == 0)
        def _():
            acc[...] = jnp.zeros_like(acc)

        acc[...] += _dot(a_ref[...], b_ref[...], dims)

        @pl.when(k == nk - 1)
        def _():
            r = acc[...]
            if mode == "add":
                outs[0][...] = (e_ref[...] + r).astype(out_dtype)
            elif mode == "relu2":
                outs[0][...] = r
                outs[1][...] = jnp.square(jnp.maximum(r, 0.0)).astype(BF16)
            elif mode == "drelu2":
                outs[0][...] = (r * (2.0 * jnp.maximum(e_ref[...], 0.0))).astype(out_dtype)
            else:
                outs[0][...] = r.astype(out_dtype)

    in_specs = [a_spec, b_spec] + ([o_spec] if has_extra else [])
    args = [a, b] + ([extra] if has_extra else [])
    if mode == "relu2":
        out_shape = (jax.ShapeDtypeStruct((m, n), F32), jax.ShapeDtypeStruct((m, n), BF16))
        out_specs = (o_spec, o_spec)
    else:
        out_shape = jax.ShapeDtypeStruct((m, n), out_dtype)
        out_specs = o_spec
    return pl.pallas_call(
        body, grid=(m // tm, n // tn, nk), in_specs=in_specs, out_specs=out_specs, out_shape=out_shape,
        scratch_shapes=[pltpu.VMEM((tm, tn), F32)],
        compiler_params=_cparams(("parallel", "parallel", "arbitrary")), name=name)(*args)


def _rmsnorm_fwd(xv, gain, *, name, tr=512):
    x, width, off = xv
    rows = x.shape[0]
    tr = min(tr, rows)

    def body(x_ref, g_ref, o_ref):
        xf = x_ref[...]
        rstd = lax.rsqrt(jnp.mean(xf * xf, axis=-1, keepdims=True) + EPS)
        o_ref[...] = (xf * rstd * g_ref[...]).astype(BF16)

    return pl.pallas_call(
        body, grid=(rows // tr,), in_specs=[_rows_spec(tr, width, off), _full_spec((1, width))],
        out_specs=_rows_spec(tr, width), out_shape=jax.ShapeDtypeStruct((rows, width), BF16),
        compiler_params=_cparams(("parallel",)), name=name)(x, gain.reshape(1, width))


def _rmsnorm_bwd(dh, xv, gain, *, dres=None, out_dtype=F32, name, tr=512):
    x, width, off = xv
    rows = x.shape[0]
    tr = min(tr, rows)
    has_res = dres is not None

    def body(*refs):
        dh_ref, x_ref, g_ref = refs[:3]
        r_ref = refs[3] if has_res else None
        dx_ref, dg_ref = refs[3 + has_res:]
        i = pl.program_id(0)
        xf = x_ref[...]
        rstd = lax.rsqrt(jnp.mean(xf * xf, axis=-1, keepdims=True) + EPS)
        xh = xf * rstd
        d = dh_ref[...]
        dxh = d * g_ref[...]
        dx = rstd * (dxh - xh * jnp.mean(dxh * xh, axis=-1, keepdims=True))
        if has_res:
            dx = dx + r_ref[...]
        dx_ref[...] = dx.astype(out_dtype)

        @pl.when(i == 0)
        def _():
            dg_ref[...] = jnp.zeros_like(dg_ref)

        dg_ref[...] += jnp.sum(d * xh, axis=0, keepdims=True)

    in_specs = [_rows_spec(tr, width), _rows_spec(tr, width, off), _full_spec((1, width))]
    args = [dh, x, gain.reshape(1, width)]
    if has_res:
        in_specs.append(_rows_spec(tr, width))
        args.append(dres)
    return pl.pallas_call(
        body, grid=(rows // tr,), in_specs=in_specs,
        out_specs=(_rows_spec(tr, width), _full_spec((1, width))),
        out_shape=(jax.ShapeDtypeStruct((rows, width), out_dtype), jax.ShapeDtypeStruct((1, width), F32)),
        compiler_params=_cparams(("arbitrary",)), name=name)(*args)


def _headnorm_fwd(xv, gain_row, pat, rope, *, addend=None, name, tr=512):
    x, width, off = xv
    rows = x.shape[0]
    tr = min(tr, rows)
    gmat, invn = pat
    nblk = width // LANES
    has_rope = rope is not None
    has_add = addend is not None
    half = rope[3] if has_rope else 0

    def body(*refs):
        x_ref, g_ref, gm_ref, in_ref = refs[:4]
        pos = 4
        if has_rope:
            c_ref, s_ref, lo_ref = refs[pos:pos + 3]
            pos += 3
        if has_add:
            a_ref = refs[pos]
            pos += 1
        o_ref = refs[pos]
        gm = gm_ref[...]
        inv = in_ref[...]
        if has_add:
            add = pltpu.roll(a_ref[...].astype(F32), 64, 1)
        for j in range(nblk):
            sl = slice(j * LANES, (j + 1) * LANES)
            xb = x_ref[:, sl]
            rstd = lax.rsqrt(_gsum(xb * xb, gm) * inv + EPS)
            y = xb * rstd * g_ref[:, sl]
            if has_rope:
                y = y * c_ref[...] + _swap_half(y, lo_ref[...], half) * s_ref[...]
            if has_add:
                y = y + add
            o_ref[:, sl] = y.astype(BF16)

    in_specs = [_rows_spec(tr, width, off), _full_spec((1, width)), _full_spec((LANES, LANES)), _full_spec((1, LANES))]
    args = [x, gain_row.reshape(1, width), gmat, invn]
    if has_rope:
        in_specs += [_rows_spec(tr, LANES), _rows_spec(tr, LANES), _full_spec((1, LANES))]
        args += [rope[0], rope[1], rope[2]]
    if has_add:
        in_specs.append(_rows_spec(tr, LANES))
        args.append(addend)
    return pl.pallas_call(
        body, grid=(rows // tr,), in_specs=in_specs, out_specs=_rows_spec(tr, width),
        out_shape=jax.ShapeDtypeStruct((rows, width), BF16),
        compiler_params=_cparams(("parallel",)), name=name)(*args)


def _headnorm_bwd(dy, xv, gain_row, pat, rope, *, name, tr=512):
    x, width, off = xv
    rows = x.shape[0]
    tr = min(tr, rows)
    gmat, invn = pat
    nblk = width // LANES
    has_rope = rope is not None
    half = rope[3] if has_rope else 0

    def body(*refs):
        dy_ref, x_ref, g_ref, gm_ref, in_ref = refs[:5]
        pos = 5
        if has_rope:
            c_ref, s_ref, lo_ref = refs[pos:pos + 3]
            pos += 3
        dx_ref, dg_ref = refs[pos:]
        i = pl.program_id(0)
        gm = gm_ref[...]
        inv = in_ref[...]

        @pl.when(i == 0)
        def _():
            dg_ref[...] = jnp.zeros_like(dg_ref)

        for j in range(nblk):
            sl = slice(j * LANES, (j + 1) * LANES)
            d = dy_ref[:, sl]
            if has_rope:
                d = d * c_ref[...] + _swap_half(d * s_ref[...], lo_ref[...], half)
            xb = x_ref[:, sl]
            rstd = lax.rsqrt(_gsum(xb * xb, gm) * inv + EPS)
            xh = xb * rstd
            dg_ref[:, sl] += jnp.sum(d * xh, axis=0, keepdims=True)
            dxh = d * g_ref[:, sl]
            dx = rstd * (dxh - xh * (_gsum(dxh * xh, gm) * inv))
            dx_ref[:, sl] = dx.astype(BF16)

    in_specs = [_rows_spec(tr, width), _rows_spec(tr, width, off), _full_spec((1, width)),
                _full_spec((LANES, LANES)), _full_spec((1, LANES))]
    args = [dy, x, gain_row.reshape(1, width), gmat, invn]
    if has_rope:
        in_specs += [_rows_spec(tr, LANES), _rows_spec(tr, LANES), _full_spec((1, LANES))]
        args += [rope[0], rope[1], rope[2]]
    return pl.pallas_call(
        body, grid=(rows // tr,), in_specs=in_specs,
        out_specs=(_rows_spec(tr, width), _full_spec((1, width))),
        out_shape=(jax.ShapeDtypeStruct((rows, width), BF16), jax.ShapeDtypeStruct((1, width), F32)),
        compiler_params=_cparams(("arbitrary",)), name=name)(*args)


def _merge_fwd(zv, b_gate, ys, *, name, tr=256):
    z, gw, off = zv
    rows, d = ys[0].shape
    tr = min(tr, rows)

    def body(z_ref, b_ref, y0, y1, y2, y3, t_ref):
        acc = None
        for n, y_ref in enumerate((y0, y1, y2, y3)):
            sl = slice(n * d, (n + 1) * d)
            term = jax.nn.sigmoid(z_ref[:, sl] + b_ref[:, sl]) * y_ref[...]
            acc = term if acc is None else acc + term
        t_ref[...] = acc.astype(BF16)

    return pl.pallas_call(
        body, grid=(rows // tr,),
        in_specs=[_rows_spec(tr, gw, off), _full_spec((1, gw))] + [_rows_spec(tr, d)] * 4,
        out_specs=_rows_spec(tr, d), out_shape=jax.ShapeDtypeStruct((rows, d), BF16),
        compiler_params=_cparams(("parallel",)), name=name)(z, b_gate.reshape(1, gw), *ys)


def _merge_bwd(dt, zv, b_gate, ys, *, name, tr=256):
    z, gw, off = zv
    rows, d = dt.shape
    tr = min(tr, rows)

    def body(dt_ref, z_ref, b_ref, y0, y1, y2, y3, d0, d1, d2, d3, dg_ref, db_ref):
        i = pl.program_id(0)

        @pl.when(i == 0)
        def _():
            db_ref[...] = jnp.zeros_like(db_ref)

        dtv = dt_ref[...]
        for n, (y_ref, d_ref) in enumerate(((y0, d0), (y1, d1), (y2, d2), (y3, d3))):
            sl = slice(n * d, (n + 1) * d)
            gate = jax.nn.sigmoid(z_ref[:, sl] + b_ref[:, sl])
            d_ref[...] = (dtv * gate).astype(BF16)
            dpre = dtv * y_ref[...] * (gate * (1.0 - gate))
            dg_ref[:, sl] = dpre.astype(BF16)
            db_ref[:, sl] += jnp.sum(dpre, axis=0, keepdims=True)

    outs = pl.pallas_call(
        body, grid=(rows // tr,),
        in_specs=[_rows_spec(tr, d), _rows_spec(tr, gw, off), _full_spec((1, gw))] + [_rows_spec(tr, d)] * 4,
        out_specs=tuple([_rows_spec(tr, d)] * 4 + [_rows_spec(tr, gw), _full_spec((1, gw))]),
        out_shape=tuple([jax.ShapeDtypeStruct((rows, d), BF16)] * 4
                        + [jax.ShapeDtypeStruct((rows, gw), BF16), jax.ShapeDtypeStruct((1, gw), F32)]),
        compiler_params=_cparams(("arbitrary",)), name=name)(dt, z, b_gate.reshape(1, gw), *ys)
    return outs[:4], outs[4], outs[5]


def _combine_fwd(os_, lses, *, name, tr=512):
    rows, width = os_[0].shape

    def body(o0, o1, o2, l0, l1, l2, out_ref):
        la, lb, lc = l0[...], l1[...], l2[...]
        mx = jnp.maximum(jnp.maximum(la, lb), lc)
        ea, eb, ec = jnp.exp(la - mx), jnp.exp(lb - mx), jnp.exp(lc - mx)
        inv = 1.0 / (ea + eb + ec)
        out_ref[...] = (ea * o0[...] + eb * o1[...] + ec * o2[...]) * inv

    return pl.pallas_call(
        body, grid=(rows // tr,), in_specs=[_rows_spec(tr, width)] * 6, out_specs=_rows_spec(tr, width),
        out_shape=jax.ShapeDtypeStruct((rows, width), F32),
        compiler_params=_cparams(("parallel",)), name=name)(*os_, *lses)


def _combine_bwd(do, o, lses, pat, *, name, tr=512):
    rows, width = do.shape
    gmat, _ = pat
    nblk = width // LANES

    def body(do_ref, o_ref, l0, l1, l2, gm_ref, d0, d1, d2, e0, e1, e2):
        gm = gm_ref[...]
        for j in range(nblk):
            sl = slice(j * LANES, (j + 1) * LANES)
            la, lb, lc = l0[:, sl], l1[:, sl], l2[:, sl]
            mx = jnp.maximum(jnp.maximum(la, lb), lc)
            ea, eb, ec = jnp.exp(la - mx), jnp.exp(lb - mx), jnp.exp(lc - mx)
            inv = 1.0 / (ea + eb + ec)
            dov = do_ref[:, sl]
            tot = _gsum(dov * o_ref[:, sl], gm)
            for e, d_ref, e_ref in ((ea, d0, e0), (eb, d1, e1), (ec, d2, e2)):
                w = e * inv
                d_ref[:, sl] = w * dov
                e_ref[:, sl] = w * tot

    outs = pl.pallas_call(
        body, grid=(rows // tr,),
        in_specs=[_rows_spec(tr, width)] * 5 + [_full_spec((LANES, LANES))],
        out_specs=tuple([_rows_spec(tr, width)] * 6),
        out_shape=tuple([jax.ShapeDtypeStruct((rows, width), F32)] * 6),
        compiler_params=_cparams(("parallel",)), name=name)(do, o, *lses, gmat)
    return outs[:3], outs[3:]


def _delta(do, o, pat, *, lse=None, sink_row=None, name, tr=512):
    rows, width = do.shape
    gmat, _ = pat
    nblk = width // LANES
    has_sink = sink_row is not None

    def body(*refs):
        do_ref, o_ref, gm_ref = refs[:3]
        if has_sink:
            l_ref, s_ref, dl_ref, ds_ref = refs[3:]
        else:
            dl_ref = refs[3]
        gm = gm_ref[...]
        if has_sink:
            @pl.when(pl.program_id(0) == 0)
            def _():
                ds_ref[...] = jnp.zeros_like(ds_ref)
        for j in range(nblk):
            sl = slice(j * LANES, (j + 1) * LANES)
            dl = _gsum(do_ref[:, sl] * o_ref[:, sl], gm)
            dl_ref[:, sl] = dl
            if has_sink:
                ds_ref[:, sl] += jnp.sum(-jnp.exp(s_ref[:, sl] - l_ref[:, sl]) * dl, axis=0, keepdims=True)

    in_specs = [_rows_spec(tr, width), _rows_spec(tr, width), _full_spec((LANES, LANES))]
    args = [do, o, gmat]
    out_specs = [_rows_spec(tr, width)]
    out_shape = [jax.ShapeDtypeStruct((rows, width), F32)]
    if has_sink:
        in_specs += [_rows_spec(tr, width), _full_spec((1, width))]
        args += [lse, sink_row]
        out_specs.append(_full_spec((1, width)))
        out_shape.append(jax.ShapeDtypeStruct((1, width), F32))
    outs = pl.pallas_call(
        body, grid=(rows // tr,), in_specs=in_specs, out_specs=tuple(out_specs), out_shape=tuple(out_shape),
        compiler_params=_cparams(("arbitrary",)), name=name)(*args)
    return outs if has_sink else (outs[0], None)


def _kp_reduce(dk, *, name, tr=512):
    rows, width = dk.shape
    nblk = width // LANES

    def body(dk_ref, o_ref):
        acc = dk_ref[:, 0:LANES]
        for j in range(1, nblk):
            acc = acc + dk_ref[:, j * LANES:(j + 1) * LANES]
        lane = lax.broadcasted_iota(jnp.int32, acc.shape, 1)
        o_ref[...] = jnp.where(lane < B_ROPE, pltpu.roll(acc, 64, 1), 0.0)

    return pl.pallas_call(
        body, grid=(rows // tr,), in_specs=[_rows_spec(tr, width)], out_specs=_rows_spec(tr, LANES),
        out_shape=jax.ShapeDtypeStruct((rows, LANES), F32),
        compiler_params=_cparams(("parallel",)), name=name)(dk)


def _loss_head(y, target, *, name, tr=512):
    rows, d = y.shape

    def body(y_ref, t_ref, dy_ref, l_ref):
        @pl.when(pl.program_id(0) == 0)
        def _():
            l_ref[...] = jnp.zeros_like(l_ref)

        diff = y_ref[...] - t_ref[...]
        dy_ref[...] = diff * (1.0 / d)
        part = 0.5 * jnp.sum(jnp.mean(diff * diff, axis=-1, keepdims=True), axis=0, keepdims=True)
        l_ref[...] += jnp.broadcast_to(part, l_ref.shape)

    return pl.pallas_call(
        body, grid=(rows // tr,), in_specs=[_rows_spec(tr, d)] * 2,
        out_specs=(_rows_spec(tr, d), _full_spec((1, LANES))),
        out_shape=(jax.ShapeDtypeStruct((rows, d), F32), jax.ShapeDtypeStruct((1, LANES), F32)),
        compiler_params=_cparams(("arbitrary",)), name=name)(y, target)


def _adam(recv, w, m, v, *, name):
    rows, cols = w.shape
    tr = _pick(rows, (128, 64, 32, 16, 8))
    c1 = 1.0 / (1.0 - ADAM_B1 ** ADAM_STEP)
    c2 = 1.0 / (1.0 - ADAM_B2 ** ADAM_STEP)

    def body(r_ref, w_ref, m_ref, v_ref, g_out, d_out, m_out, v_out):
        g = r_ref[0]
        for j in range(1, N_DEV):
            g = g + r_ref[j]
        mn = ADAM_B1 * m_ref[...] + (1.0 - ADAM_B1) * g
        vn = ADAM_B2 * v_ref[...] + (1.0 - ADAM_B2) * (g * g)
        g_out[...] = g
        m_out[...] = mn
        v_out[...] = vn
        d_out[...] = -ADAM_LR * ((mn * c1) / (jnp.sqrt(vn * c2) + ADAM_EPS) + ADAM_WD * w_ref[...])

    spec = _rows_spec(tr, cols)
    sds = jax.ShapeDtypeStruct((rows, cols), F32)
    return pl.pallas_call(
        body, grid=(rows // tr,),
        in_specs=[pl.BlockSpec((N_DEV, tr, cols), lambda i: (0, i, 0)), spec, spec, spec],
        out_specs=(spec,) * 4, out_shape=(sds,) * 4,
        compiler_params=_cparams(("parallel",)), name=name)(recv, w, m, v)


def _band_mask(i, maxdist, nrow_blocks):
    row = lax.broadcasted_iota(jnp.int32, (BLOCK, 2 * BLOCK), 0)
    col = lax.broadcasted_iota(jnp.int32, (BLOCK, 2 * BLOCK), 1)
    dist = row - col + BLOCK
    return (dist >= 0) & (dist <= maxdist) & ((i > 0) | (col >= BLOCK))


def _band_views(q, k, zv, dil):
    s = q.shape[0]
    z, kvw, voff = zv
    if dil == 1:
        return q, k, z, z.shape[1] // kvw, voff
    vd = z[:, voff * kvw:(voff + 1) * kvw].reshape(s // dil, dil * kvw)
    return q.reshape(s // dil, dil * q.shape[1]), k.reshape(s // dil, dil * k.shape[1]), vd, 1, 0


def _band_fwd(q, k, zv, *, dil, group, maxdist, sink_row=None, name):
    s, qw = q.shape
    kvw = zv[1]
    qd, kd, zd, zblocks, voff = _band_views(q, k, zv, dil)
    nb = s // dil // BLOCK
    nheads = qw // 64
    scale = 64 ** -0.5
    has_sink = sink_row is not None

    def body(*refs):
        q_ref, kp_ref, kc_ref, vp_ref, vc_ref = refs[:5]
        s_ref = refs[5] if has_sink else None
        o_ref, l_ref = refs[5 + has_sink:]
        i = pl.program_id(1)
        mask = _band_mask(i, maxdist, nb)
        k2 = jnp.concatenate([kp_ref[...], kc_ref[...]], axis=0)
        v2 = jnp.concatenate([vp_ref[...], vc_ref[...]], axis=0).astype(BF16)
        for h in range(nheads):
            g = h // group
            sc = _dot(q_ref[:, 64 * h:64 * h + 64], k2[:, 64 * g:64 * g + 64], NT_DIMS) * scale
            sc = jnp.where(mask, sc, NEG)
            mx = jnp.max(sc, axis=1, keepdims=True)
            if has_sink:
                snk = s_ref[:, 64 * h:64 * h + 1]
                mx = jnp.maximum(mx, snk)
            e = jnp.exp(sc - mx)
            den = jnp.sum(e, axis=1, keepdims=True)
            if has_sink:
                den = den + jnp.exp(snk - mx)
            p = e * (1.0 / den)
            o_ref[:, 64 * h:64 * h + 64] = _dot(p, v2[:, 64 * g:64 * g + 64])
            l_ref[:, 64 * h:64 * h + 64] = jnp.broadcast_to(mx + jnp.log(den), (BLOCK, 64))

    prev = lambda r, i: jnp.maximum(i - 1, 0)
    in_specs = [
        pl.BlockSpec((BLOCK, qw), lambda r, i: (i, r)),
        pl.BlockSpec((BLOCK, kvw), lambda r, i: (prev(r, i), r)),
        pl.BlockSpec((BLOCK, kvw), lambda r, i: (i, r)),
        pl.BlockSpec((BLOCK, kvw), lambda r, i: (prev(r, i), r * zblocks + voff)),
        pl.BlockSpec((BLOCK, kvw), lambda r, i: (i, r * zblocks + voff)),
    ]
    args = [qd, kd, kd, zd, zd]
    if has_sink:
        in_specs.append(pl.BlockSpec((1, qw), lambda r, i: (0, 0)))
        args.append(sink_row)
    ospec = pl.BlockSpec((BLOCK, qw), lambda r, i: (i, r))
    sds = jax.ShapeDtypeStruct((s // dil, dil * qw), F32)
    o, lse = pl.pallas_call(
        body, grid=(dil, nb), in_specs=in_specs, out_specs=(ospec, ospec), out_shape=(sds, sds),
        compiler_params=_cparams(("parallel", "parallel")), name=name)(*args)
    return o.reshape(s, qw), lse.reshape(s, qw)


def _band_dq(q, k, zv, do, lse, delta, *, dil, group, maxdist, name):
    s, qw = q.shape
    kvw = zv[1]
    qd, kd, zd, zblocks, voff = _band_views(q, k, zv, dil)
    rs = lambda t: t.reshape(s // dil, dil * qw)
    nb = s // dil // BLOCK
    nheads = qw // 64
    scale = 64 ** -0.5

    def body(q_ref, kp_ref, kc_ref, vp_ref, vc_ref, do_ref, l_ref, d_ref, dq_ref):
        i = pl.program_id(1)
        mask = _band_mask(i, maxdist, nb)
        k2 = jnp.concatenate([kp_ref[...], kc_ref[...]], axis=0)
        v2 = jnp.concatenate([vp_ref[...], vc_ref[...]], axis=0).astype(BF16)
        for h in range(nheads):
            g = h // group
            hs = slice(64 * h, 64 * h + 64)
            kg = k2[:, 64 * g:64 * g + 64]
            sc = _dot(q_ref[:, hs], kg, NT_DIMS) * scale
            p = jnp.exp(jnp.where(mask, sc, NEG) - l_ref[:, 64 * h:64 * h + 1])
            dp = _dot(do_ref[:, hs], v2[:, 64 * g:64 * g + 64], NT_DIMS)
            ds = p * (dp - d_ref[:, 64 * h:64 * h + 1]) * scale
            dq_ref[:, hs] = _dot(ds, kg)

    prev = lambda r, i: jnp.maximum(i - 1, 0)
    qspec = pl.BlockSpec((BLOCK, qw), lambda r, i: (i, r))
    in_specs = [
        qspec,
        pl.BlockSpec((BLOCK, kvw), lambda r, i: (prev(r, i), r)),
        pl.BlockSpec((BLOCK, kvw), lambda r, i: (i, r)),
        pl.BlockSpec((BLOCK, kvw), lambda r, i: (prev(r, i), r * zblocks + voff)),
        pl.BlockSpec((BLOCK, kvw), lambda r, i: (i, r * zblocks + voff)),
        qspec, qspec, qspec,
    ]
    dq = pl.pallas_call(
        body, grid=(dil, nb), in_specs=in_specs, out_specs=qspec,
        out_shape=jax.ShapeDtypeStruct((s // dil, dil * qw), F32),
        compiler_params=_cparams(("parallel", "parallel")), name=name)(qd, kd, kd, zd, zd, rs(do), rs(lse), rs(delta))
    return dq.reshape(s, qw)


def _band_dkv(q, k, zv, do, lse, delta, *, dil, group, maxdist, name):
    s, qw = q.shape
    kvw = zv[1]
    qd, kd, zd, zblocks, voff = _band_views(q, k, zv, dil)
    rs = lambda t: t.reshape(s // dil, dil * qw)
    nb = s // dil // BLOCK
    nheads = qw // 64
    nkv = kvw // 64
    scale = 64 ** -0.5

    def body(k_ref, v_ref, q0, q1, do0, do1, l0, l1, d0, d1, dk_ref, dv_ref):
        j = pl.program_id(1)
        row = lax.broadcasted_iota(jnp.int32, (2 * BLOCK, BLOCK), 0)
        col = lax.broadcasted_iota(jnp.int32, (2 * BLOCK, BLOCK), 1)
        dist = row - col
        mask = (dist >= 0) & (dist <= maxdist) & ((j < nb - 1) | (row < BLOCK))
        q2 = jnp.concatenate([q0[...], q1[...]], axis=0)
        do2 = jnp.concatenate([do0[...], do1[...]], axis=0).astype(BF16)
        vb = v_ref[...].astype(BF16)
        dks = [None] * nkv
        dvs = [None] * nkv
        for h in range(nheads):
            g = h // group
            hs = slice(64 * h, 64 * h + 64)
            lse2 = jnp.concatenate([l0[:, 64 * h:64 * h + 1], l1[:, 64 * h:64 * h + 1]], axis=0)
            dl2 = jnp.concatenate([d0[:, 64 * h:64 * h + 1], d1[:, 64 * h:64 * h + 1]], axis=0)
            sc = _dot(q2[:, hs], k_ref[:, 64 * g:64 * g + 64], NT_DIMS) * scale
            p = jnp.exp(jnp.where(mask, sc, NEG) - lse2)
            dvh = _dot(p, do2[:, hs], TN_DIMS)
            dp = _dot(do2[:, hs], vb[:, 64 * g:64 * g + 64], NT_DIMS)
            ds = p * (dp - dl2) * scale
            dkh = _dot(ds, q2[:, hs], TN_DIMS)
            dks[g] = dkh if dks[g] is None else dks[g] + dkh
            dvs[g] = dvh if dvs[g] is None else dvs[g] + dvh
        for g in range(nkv):
            dk_ref[:, 64 * g:64 * g + 64] = dks[g]
            dv_ref[:, 64 * g:64 * g + 64] = dvs[g].astype(BF16)

    nxt = lambda r, j: jnp.minimum(j + 1, nb - 1)
    q0s = pl.BlockSpec((BLOCK, qw), lambda r, j: (j, r))
    q1s = pl.BlockSpec((BLOCK, qw), lambda r, j: (nxt(r, j), r))
    kspec = pl.BlockSpec((BLOCK, kvw), lambda r, j: (j, r))
    in_specs = [kspec, pl.BlockSpec((BLOCK, kvw), lambda r, j: (j, r * zblocks + voff)),
                q0s, q1s, q0s, q1s, q0s, q1s, q0s, q1s]
    dod, lsd, dld = rs(do), rs(lse), rs(delta)
    dk, dv = pl.pallas_call(
        body, grid=(dil, nb), in_specs=in_specs, out_specs=(kspec, kspec),
        out_shape=(jax.ShapeDtypeStruct((s // dil, dil * kvw), F32), jax.ShapeDtypeStruct((s // dil, dil * kvw), BF16)),
        compiler_params=_cparams(("parallel", "parallel")), name=name)(kd, zd, qd, qd, dod, dod, lsd, lsd, dld, dld)
    return dk.reshape(s, kvw), dv.reshape(s, kvw)


HPG = 2


def _flash_tiles(sq, sk):
    return _pick(sq, (512, 256, 128)), _pick(sk, (512, 256, 128))


def _flash_fwd(q, k, vv, *, dv, causal, scale, name):
    sq, qw = q.shape
    sk = k.shape[0]
    varr, vbw, voff = vv
    assert vbw == HPG * dv
    ngrp = qw // (HPG * LANES)
    tq, tk = _flash_tiles(sq, sk)
    nq, nk = sq // tq, sk // tk
    if causal:
        assert tq == tk and sq == sk

    def body(q_ref, k_ref, v_ref, o_ref, l_ref, m_s, l_s, acc_s):
        i, j = pl.program_id(1), pl.program_id(2)

        @pl.when(j == 0)
        def _():
            m_s[...] = jnp.full_like(m_s, NEG)
            l_s[...] = jnp.zeros_like(l_s)
            acc_s[...] = jnp.zeros_like(acc_s)

        def step():
            if causal:
                row = lax.broadcasted_iota(jnp.int32, (tq, tk), 0) + i * tq
                col = lax.broadcasted_iota(jnp.int32, (tq, tk), 1) + j * tk
                mask = col <= row
            for h in range(HPG):
                sc = _dot(q_ref[:, LANES * h:LANES * (h + 1)], k_ref[:, LANES * h:LANES * (h + 1)], NT_DIMS) * scale
                if causal:
                    sc = jnp.where(mask, sc, NEG)
                m_prev = m_s[h]
                m_new = jnp.maximum(m_prev, jnp.max(sc, axis=1, keepdims=True))
                alpha = jnp.exp(m_prev - m_new)
                p = jnp.exp(sc - m_new[:, 0:1])
                l_s[h] = alpha * l_s[h] + jnp.sum(p, axis=1, keepdims=True)
                acc_s[h] = acc_s[h] * alpha[:, 0:dv] + _dot(p, v_ref[:, dv * h:dv * (h + 1)])
                m_s[h] = m_new

        if causal:
            pl.when(j <= i)(step)
        else:
            step()

        @pl.when(j == nk - 1)
        def _():
            for h in range(HPG):
                o_ref[:, dv * h:dv * (h + 1)] = acc_s[h] * (1.0 / l_s[h][:, 0:dv])
                l_ref[:, LANES * h:LANES * (h + 1)] = m_s[h] + jnp.log(l_s[h])

    kidx = (lambda g, i, j: jnp.minimum(j, i)) if causal else (lambda g, i, j: j)
    in_specs = [
        pl.BlockSpec((tq, HPG * LANES), lambda g, i, j: (i, g)),
        pl.BlockSpec((tk, HPG * LANES), lambda g, i, j: (kidx(g, i, j), g)),
        pl.BlockSpec((tk, vbw), lambda g, i, j: (kidx(g, i, j), voff + g)),
    ]
    return pl.pallas_call(
        body, grid=(ngrp, nq, nk), in_specs=in_specs,
        out_specs=(pl.BlockSpec((tq, HPG * dv), lambda g, i, j: (i, g)),
                   pl.BlockSpec((tq, HPG * LANES), lambda g, i, j: (i, g))),
        out_shape=(jax.ShapeDtypeStruct((sq, ngrp * HPG * dv), F32), jax.ShapeDtypeStruct((sq, qw), F32)),
        scratch_shapes=[pltpu.VMEM((HPG, tq, LANES), F32), pltpu.VMEM((HPG, tq, LANES), F32),
                        pltpu.VMEM((HPG, tq, dv), F32)],
        compiler_params=_cparams(("parallel", "parallel", "arbitrary")), name=name)(q, k, varr)


def _flash_dq(q, k, vv, do, lse, delta, *, dv, causal, scale, name):
    sq, qw = q.shape
    sk = k.shape[0]
    varr, vbw, voff = vv
    ngrp = qw // (HPG * LANES)
    tq, tk = _flash_tiles(sq, sk)
    nq, nk = sq // tq, sk // tk

    def body(q_ref, k_ref, v_ref, do_ref, l_ref, d_ref, dq_ref, acc_s):
        i, j = pl.program_id(1), pl.program_id(2)

        @pl.when(j == 0)
        def _():
            acc_s[...] = jnp.zeros_like(acc_s)

        def step():
            if causal:
                row = lax.broadcasted_iota(jnp.int32, (tq, tk), 0) + i * tq
                col = lax.broadcasted_iota(jnp.int32, (tq, tk), 1) + j * tk
                mask = col <= row
            for h in range(HPG):
                kh = k_ref[:, LANES * h:LANES * (h + 1)]
                sc = _dot(q_ref[:, LANES * h:LANES * (h + 1)], kh, NT_DIMS) * scale
                if causal:
                    sc = jnp.where(mask, sc, NEG)
                p = jnp.exp(sc - l_ref[:, LANES * h:LANES * h + 1])
                dp = _dot(do_ref[:, dv * h:dv * (h + 1)], v_ref[:, dv * h:dv * (h + 1)], NT_DIMS)
                ds = p * (dp - d_ref[:, dv * h:dv * h + 1]) * scale
                acc_s[h] += _dot(ds, kh)

        if causal:
            pl.when(j <= i)(step)
        else:
            step()

        @pl.when(j == nk - 1)
        def _():
            for h in range(HPG):
                dq_ref[:, LANES * h:LANES * (h + 1)] = acc_s[h]

    kidx = (lambda g, i, j: jnp.minimum(j, i)) if causal else (lambda g, i, j: j)
    qspec = pl.BlockSpec((tq, HPG * LANES), lambda g, i, j: (i, g))
    ospec = pl.BlockSpec((tq, HPG * dv), lambda g, i, j: (i, g))
    in_specs = [
        qspec,
        pl.BlockSpec((tk, HPG * LANES), lambda g, i, j: (kidx(g, i, j), g)),
        pl.BlockSpec((tk, vbw), lambda g, i, j: (kidx(g, i, j), voff + g)),
        ospec, qspec, ospec,
    ]
    return pl.pallas_call(
        body, grid=(ngrp, nq, nk), in_specs=in_specs, out_specs=qspec,
        out_shape=jax.ShapeDtypeStruct((sq, qw), F32),
        scratch_shapes=[pltpu.VMEM((HPG, tq, LANES), F32)],
        compiler_params=_cparams(("parallel", "parallel", "arbitrary")), name=name)(q, k, varr, do, lse, delta)


def _flash_dkv(q, k, vv, do, lse, delta, *, dv, causal, scale, name):
    sq, qw = q.shape
    sk = k.shape[0]
    varr, vbw, voff = vv
    ngrp = qw // (HPG * LANES)
    tq, tk = _flash_tiles(sq, sk)
    nq, nk = sq // tq, sk // tk

    def body(q_ref, k_ref, v_ref, do_ref, l_ref, d_ref, dk_ref, dv_ref, dk_s, dv_s):
        j, i = pl.program_id(1), pl.program_id(2)

        @pl.when(i == 0)
        def _():
            dk_s[...] = jnp.zeros_like(dk_s)
            dv_s[...] = jnp.zeros_like(dv_s)

        def step():
            if causal:
                row = lax.broadcasted_iota(jnp.int32, (tq, tk), 0) + i * tq
                col = lax.broadcasted_iota(jnp.int32, (tq, tk), 1) + j * tk
                mask = col <= row
            for h in range(HPG):
                qh = q_ref[:, LANES * h:LANES * (h + 1)]
                doh = do_ref[:, dv * h:dv * (h + 1)]
                sc = _dot(qh, k_ref[:, LANES * h:LANES * (h + 1)], NT_DIMS) * scale
                if causal:
                    sc = jnp.where(mask, sc, NEG)
                p = jnp.exp(sc - l_ref[:, LANES * h:LANES * h + 1])
                dv_s[h] += _dot(p, doh, TN_DIMS)
                dp = _dot(doh, v_ref[:, dv * h:dv * (h + 1)], NT_DIMS)
                ds = p * (dp - d_ref[:, dv * h:dv * h + 1]) * scale
                dk_s[h] += _dot(ds, qh, TN_DIMS)

        if causal:
            pl.when(i >= j)(step)
        else:
            step()

        @pl.when(i == nq - 1)
        def _():
            for h in range(HPG):
                dk_ref[:, LANES * h:LANES * (h + 1)] = dk_s[h]
                dv_ref[:, dv * h:dv * (h + 1)] = dv_s[h].astype(BF16)

    qidx = (lambda g, j, i: jnp.maximum(i, j)) if causal else (lambda g, j, i: i)
    qspec = pl.BlockSpec((tq, HPG * LANES), lambda g, j, i: (qidx(g, j, i), g))
    ospec = pl.BlockSpec((tq, HPG * dv), lambda g, j, i: (qidx(g, j, i), g))
    kspec = pl.BlockSpec((tk, HPG * LANES), lambda g, j, i: (j, g))
    in_specs = [qspec, kspec, pl.BlockSpec((tk, vbw), lambda g, j, i: (j, voff + g)), ospec, qspec, ospec]
    return pl.pallas_call(
        body, grid=(ngrp, nk, nq), in_specs=in_specs,
        out_specs=(kspec, pl.BlockSpec((tk, HPG * dv), lambda g, j, i: (j, g))),
        out_shape=(jax.ShapeDtypeStruct((sk, qw), F32), jax.ShapeDtypeStruct((sk, ngrp * HPG * dv), BF16)),
        scratch_shapes=[pltpu.VMEM((HPG, tk, LANES), F32), pltpu.VMEM((HPG, tk, dv), F32)],
        compiler_params=_cparams(("parallel", "parallel", "arbitrary")), name=name)(q, k, varr, do, lse, delta)


def _exchange(src, *, scatter, name):
    shard = src.shape[1:] if scatter else src.shape

    def body(src_ref, out_ref, send_sems, recv_sems, local_sem):
        me = 4 * lax.axis_index("x") + 2 * lax.axis_index("y") + lax.axis_index("c")
        own = src_ref.at[me] if scatter else src_ref
        local = pltpu.make_async_copy(own, out_ref.at[me], local_sem)
        local.start()
        copies = []
        for d in range(1, N_DEV):
            peer = (me + d) % N_DEV
            piece = src_ref.at[peer] if scatter else src_ref
            cp = pltpu.make_async_remote_copy(
                src_ref=piece, dst_ref=out_ref.at[me], send_sem=send_sems.at[d - 1], recv_sem=recv_sems.at[d - 1],
                device_id=(peer // 4, (peer // 2) % 2, peer % 2), device_id_type=pl.DeviceIdType.MESH)
            cp.start()
            copies.append(cp)
        for cp in copies:
            cp.wait()
        local.wait()

    return pl.pallas_call(
        body, in_specs=[pl.BlockSpec(memory_space=pl.ANY)], out_specs=pl.BlockSpec(memory_space=pl.ANY),
        out_shape=jax.ShapeDtypeStruct((N_DEV,) + tuple(shard), src.dtype),
        scratch_shapes=[pltpu.SemaphoreType.DMA((N_DEV - 1,)), pltpu.SemaphoreType.DMA((N_DEV - 1,)),
                        pltpu.SemaphoreType.DMA(())],
        compiler_params=pltpu.CompilerParams(has_side_effects=True), name=name)(src)


def _unshard(wg, axis):
    nd = wg.ndim - 1
    perm = list(range(1, axis + 1)) + [0] + list(range(axis + 1, nd + 1))
    t = wg.transpose(perm)
    shp = list(wg.shape[1:])
    shp[axis] *= N_DEV
    return t.reshape(shp)


def _shard_major(full, axis):
    shp = list(full.shape)
    shp[axis:axis + 1] = [N_DEV, shp[axis] // N_DEV]
    t = full.reshape(shp)
    perm = [axis] + list(range(axis)) + list(range(axis + 1, len(shp)))
    return t.transpose(perm)


BIG = ("w_in", "b_w_uq", "b_w_ukv", "m_w_kv", "w_branch", "w_out", "w_up", "w_down")
BIG_AXIS = dict(w_in=2, b_w_uq=2, b_w_ukv=2, m_w_kv=1, w_branch=3, w_out=1, w_up=2, w_down=1)
SMALL = ("g_mix", "b_gate", "a_qn", "a_kn", "a_sink", "b_qa_norm", "b_kva_norm", "b_qn", "b_kn", "c_qn", "c_kn",
         "m_g_mem", "m_qn", "m_kn", "g_mlp")
WEIGHTS = ("g_mix", "w_in", "b_gate", "a_qn", "a_kn", "a_sink", "b_qa_norm", "b_kva_norm", "b_w_uq", "b_w_ukv", "b_qn",
           "b_kn", "c_qn", "c_kn", "m_g_mem", "m_w_kv", "m_qn", "m_kn", "w_branch", "w_out", "g_mlp", "w_up", "w_down")


def _rope_tables(positions, dim):
    inv = ROPE_THETA ** (-jnp.arange(0, dim, 2, dtype=F32) / dim)
    ang = positions.astype(F32)[:, None] * inv
    return jnp.cos(ang), jnp.sin(ang)


def _lane_mask(lo, hi):
    lane = np.arange(LANES)
    return jnp.asarray(((lane >= lo) & (lane < hi)).astype(np.float32)[None, :])


def _step(x, mem, positions, w, moments_m, moments_v, loss_target):
    s_len, d_model = x.shape
    n_layers = w["g_mix"].shape[0]
    lay, orig_names, order, n_in, nw = _in_layout(d_model)
    unit = lambda name: lay[name]["off"] // lay[name]["nw"]

    full = {}
    for name in BIG:
        wg = _exchange(w[name].astype(BF16), scatter=False, name="ag_" + name)
        full[name] = _unshard(wg, BIG_AXIS[name])
    w_in_p = _permute_cols(full["w_in"], lay, order)
    uq = full["b_w_uq"].reshape(n_layers, B_Q_LORA, B_HEADS, B_NOPE + B_ROPE)
    w_uq_p = jnp.pad(uq, ((0, 0), (0, 0), (0, 0), (0, LANES - B_NOPE - B_ROPE))).reshape(n_layers, B_Q_LORA, B_HEADS * LANES)
    ukv = full["b_w_ukv"].reshape(n_layers, B_KV_LORA, B_HEADS, B_NOPE + B_V)
    w_ukv_p = jnp.concatenate([
        jnp.pad(ukv[..., :B_NOPE], ((0, 0), (0, 0), (0, 0), (0, LANES - B_NOPE))).reshape(n_layers, B_KV_LORA, B_HEADS * LANES),
        ukv[..., B_NOPE:].reshape(n_layers, B_KV_LORA, B_HEADS * B_V)], axis=-1)

    p64 = _pattern([64, 64])
    p128 = _pattern([128])
    pq = _pattern([64, 32, 32])
    pkr = _pattern([32, 32, 32, 32])
    cos_h, sin_h = _rope_tables(positions, A_HD)
    cos_r, sin_r = _rope_tables(positions, B_ROPE)
    ones = lambda n: jnp.ones((s_len, n), F32)
    zeros = lambda n: jnp.zeros((s_len, n), F32)
    rope64 = (jnp.tile(cos_h, (1, 4)), jnp.tile(jnp.concatenate([-sin_h, sin_h], axis=1), (1, 2)),
              jnp.asarray((np.arange(LANES) % 64 < 32).astype(np.float32)[None, :]), 32)
    rope_q = (jnp.concatenate([ones(64), cos_r, cos_r, ones(32)], axis=1),
              jnp.concatenate([zeros(64), -sin_r, sin_r, zeros(32)], axis=1), _lane_mask(64, 80), 16)
    rope_k = (jnp.concatenate([cos_r, cos_r, ones(96)], axis=1),
              jnp.concatenate([-sin_r, sin_r, zeros(96)], axis=1), _lane_mask(0, 16), 16)

    scale_b = (B_NOPE + B_ROPE) ** -0.5
    scale_m = M_HD ** -0.5
    zpad = lambda n: jnp.zeros((n,), F32)

    def gains(l):
        return dict(
            a_q=jnp.tile(w["a_qn"][l], A_HEADS), a_k=jnp.tile(w["a_kn"][l], A_KV_HEADS),
            b_q=jnp.tile(jnp.concatenate([w["b_qn"][l], zpad(32)]), B_HEADS),
            b_k=jnp.tile(jnp.concatenate([w["b_kn"][l, :B_NOPE], zpad(64)]), B_HEADS),
            b_kr=jnp.concatenate([w["b_kn"][l, B_NOPE:], zpad(96)]),
            c_q=[jnp.tile(w["c_qn"][l, g], C_HEADS) for g in range(3)],
            c_k=[jnp.tile(w["c_kn"][l, g], C_HEADS) for g in range(3)],
            m_q=jnp.tile(w["m_qn"][l], M_HEADS), m_k=jnp.tile(w["m_kn"][l], M_HEADS),
            sink=jnp.repeat(w["a_sink"][l], A_HD).reshape(1, BRANCH_W))

    saved = []
    cur = x
    for l in range(n_layers):
        gn = gains(l)
        sv = dict(x=cur)
        h = _rmsnorm_fwd((cur, d_model, 0), w["g_mix"][l], name="f_norm_mix")
        z = _mm(h, w_in_p[l], name="f_mm_in")
        sv.update(h=h, z=z)
        qa = _headnorm_fwd((z, 512, unit("a_q")), gn["a_q"], p64, rope64, name="f_hn_aq")
        ka = _headnorm_fwd((z, 128, unit("a_k")), gn["a_k"], p64, rope64, name="f_hn_ak")
        va = (z, 128, unit("a_v"))
        o_a, lse_a = _band_fwd(qa, ka, va, dil=1, group=A_HEADS // A_KV_HEADS, maxdist=A_WINDOW - 1,
                               sink_row=gn["sink"], name="f_band_a")
        sv.update(qa=qa, ka=ka, o_a=o_a, lse_a=lse_a)
        cq = _rmsnorm_fwd((z, B_Q_LORA, unit("b_cq")), w["b_qa_norm"][l], name="f_norm_cq")
        q_up = _mm(cq, w_uq_p[l], name="f_mm_uq")
        ckv = _rmsnorm_fwd((z, B_KV_LORA, unit("b_ckv")), w["b_kva_norm"][l], name="f_norm_ckv")
        kv_up = _mm(ckv, w_ukv_p[l], name="f_mm_ukv")
        kp = _headnorm_fwd((z, LANES, unit("b_kr")), gn["b_kr"], pkr, rope_k, name="f_hn_kr")
        qb = _headnorm_fwd((q_up, B_HEADS * LANES, 0), gn["b_q"], pq, rope_q, name="f_hn_bq")
        kb = _headnorm_fwd((kv_up, B_HEADS * LANES, 0), gn["b_k"], p64, None, addend=kp, name="f_hn_bk")
        vb = (kv_up, HPG * B_V, B_HEADS * LANES // (HPG * B_V))
        o_b, lse_b = _flash_fwd(qb, kb, vb, dv=B_V, causal=True, scale=scale_b, name="f_flash_b")
        sv.update(cq=cq, q_up=q_up, ckv=ckv, kv_up=kv_up, qb=qb, kb=kb, o_b=o_b, lse_b=lse_b)
        qc, kc, o_cg, lse_cg = [], [], [], []
        for g, (win, dil) in enumerate(C_PATTERNS):
            qg = _headnorm_fwd((z, 512, unit("c%d" % (3 * g))), gn["c_q"][g], p64, rope64, name="f_hn_cq%d" % g)
            kg = _headnorm_fwd((z, 512, unit("c%d" % (3 * g + 1))), gn["c_k"][g], p64, rope64, name="f_hn_ck%d" % g)
            og, lg = _band_fwd(qg, kg, (z, 512, unit("c%d" % (3 * g + 2))), dil=dil, group=1, maxdist=win // dil,
                               name="f_band_c%d" % g)
            qc.append(qg), kc.append(kg), o_cg.append(og), lse_cg.append(lg)
        o_c = _combine_fwd(o_cg, lse_cg, name="f_combine")
        sv.update(qc=qc, kc=kc, o_cg=o_cg, lse_cg=lse_cg, o_c=o_c)
        memn = _rmsnorm_fwd((mem, d_model, 0), w["m_g_mem"][l], name="f_norm_mem")
        mkv = _mm(memn, full["m_w_kv"][l], name="f_mm_mkv")
        mk = _headnorm_fwd((mkv, M_HEADS * M_HD, 0), gn["m_k"], p128, None, name="f_hn_mk")
        mq = _headnorm_fwd((z, M_HEADS * M_HD, unit("m_q")), gn["m_q"], p128, None, name="f_hn_mq")
        mv = (mkv, HPG * M_HD, M_HEADS * M_HD // (HPG * M_HD))
        o_m, lse_m = _flash_fwd(mq, mk, mv, dv=M_HD, causal=False, scale=scale_m, name="f_flash_m")
        sv.update(memn=memn, mkv=mkv, mk=mk, mq=mq, o_m=o_m, lse_m=lse_m)
        outs = (o_a, o_b, o_c, o_m)
        ys = [_mm(outs[n], full["w_branch"][l, n], name="f_mm_branch") for n in range(N_BRANCH)]
        t = _merge_fwd((z, N_BRANCH * d_model, 0), w["b_gate"][l], ys, name="f_merge")
        x2 = _mm(t, full["w_out"][l], mode="add", extra=cur, name="f_mm_out")
        h2 = _rmsnorm_fwd((x2, d_model, 0), w["g_mlp"][l], name="f_norm_mlp")
        u, act = _mm(h2, full["w_up"][l], mode="relu2", name="f_mm_up")
        x3 = _mm(act, full["w_down"][l], mode="add", extra=x2, name="f_mm_down")
        sv.update(ys=ys, t=t, x2=x2, h2=h2, u=u, act=act)
        saved.append(sv)
        cur = x3

    dx, loss_row = _loss_head(cur, loss_target, name="loss_head")
    loss = lax.psum(loss_row[0, 0], AXES)

    gfull = {n: [None] * n_layers for n in BIG}
    gsmall = {n: [None] * n_layers for n in SMALL}
    head_sum = lambda row, nh, hd: row.reshape(nh, hd).sum(axis=0)
    for l in reversed(range(n_layers)):
        sv = saved[l]
        gn = gains(l)
        z = sv["z"]
        gfull["w_down"][l] = _mm(sv["act"], dx, ta=True, name="b_mm_wdown")
        du = _mm(dx, full["w_down"][l], tb=True, mode="drelu2", extra=sv["u"], out_dtype=BF16, name="b_mm_du")
        gfull["w_up"][l] = _mm(sv["h2"], du, ta=True, name="b_mm_wup")
        dh2 = _mm(du, full["w_up"][l], tb=True, name="b_mm_dh2")
        dx2, dg = _rmsnorm_bwd(dh2, (sv["x2"], d_model, 0), w["g_mlp"][l], dres=dx, name="b_norm_mlp")
        gsmall["g_mlp"][l] = dg[0]
        gfull["w_out"][l] = _mm(sv["t"], dx2, ta=True, name="b_mm_wout")
        dt = _mm(dx2, full["w_out"][l], tb=True, name="b_mm_dt")
        dys, dgate, dbg = _merge_bwd(dt, (z, N_BRANCH * d_model, 0), w["b_gate"][l], sv["ys"], name="b_merge")
        gsmall["b_gate"][l] = dbg[0]
        outs = (sv["o_a"], sv["o_b"], sv["o_c"], sv["o_m"])
        gfull["w_branch"][l] = jnp.stack(
            [_mm(outs[n], dys[n], ta=True, name="b_mm_wbranch") for n in range(N_BRANCH)])
        do_a, do_b, do_c, do_m = [_mm(dys[n], full["w_branch"][l, n], tb=True, name="b_mm_dbranch")
                                  for n in range(N_BRANCH)]
        dparts = {"gate": dgate}
        va = (z, 128, unit("a_v"))
        kw_a = dict(dil=1, group=A_HEADS // A_KV_HEADS, maxdist=A_WINDOW - 1)
        dl_a, dsink = _delta(do_a, sv["o_a"], p64, lse=sv["lse_a"], sink_row=gn["sink"], name="b_delta_a")
        gsmall["a_sink"][l] = dsink[0, ::A_HD]
        dqa = _band_dq(sv["qa"], sv["ka"], va, do_a, sv["lse_a"], dl_a, name="b_band_dq_a", **kw_a)
        dka, dva = _band_dkv(sv["qa"], sv["ka"], va, do_a, sv["lse_a"], dl_a, name="b_band_dkv_a", **kw_a)
        dparts["a_q"], dg = _headnorm_bwd(dqa, (z, 512, unit("a_q")), gn["a_q"], p64, rope64, name="b_hn_aq")
        gsmall["a_qn"][l] = head_sum(dg, A_HEADS, A_HD)
        dparts["a_k"], dg = _headnorm_bwd(dka, (z, 128, unit("a_k")), gn["a_k"], p64, rope64, name="b_hn_ak")
        gsmall["a_kn"][l] = head_sum(dg, A_KV_HEADS, A_HD)
        dparts["a_v"] = dva
        vb = (sv["kv_up"], HPG * B_V, B_HEADS * LANES // (HPG * B_V))
        dl_b, _ = _delta(do_b, sv["o_b"], p64, name="b_delta_b")
        kw_b = dict(dv=B_V, causal=True, scale=scale_b)
        dqb = _flash_dq(sv["qb"], sv["kb"], vb, do_b, sv["lse_b"], dl_b, name="b_flash_dq_b", **kw_b)
        dkb, dvb = _flash_dkv(sv["qb"], sv["kb"], vb, do_b, sv["lse_b"], dl_b, name="b_flash_dkv_b", **kw_b)
        d_qup, dgq = _headnorm_bwd(dqb, (sv["q_up"], B_HEADS * LANES, 0), gn["b_q"], pq, rope_q, name="b_hn_bq")
        d_kn, dgk = _headnorm_bwd(dkb, (sv["kv_up"], B_HEADS * LANES, 0), gn["b_k"], p64, None, name="b_hn_bk")
        dkp = _kp_reduce(dkb, name="b_kp_reduce")
        dparts["b_kr"], dgkr = _headnorm_bwd(dkp, (z, LANES, unit("b_kr")), gn["b_kr"], pkr, rope_k, name="b_hn_kr")
        gsmall["b_qn"][l] = head_sum(dgq, B_HEADS, LANES)[:B_NOPE + B_ROPE]
        gsmall["b_kn"][l] = jnp.concatenate([head_sum(dgk, B_HEADS, LANES)[:B_NOPE], dgkr[0, :B_ROPE]])
        d_kvup = jnp.concatenate([d_kn, dvb], axis=1)
        g_ukv = _mm(sv["ckv"], d_kvup, ta=True, name="b_mm_wukv")
        dckv = _mm(d_kvup, w_ukv_p[l], tb=True, name="b_mm_dckv")
        dparts["b_ckv"], dg = _rmsnorm_bwd(dckv, (z, B_KV_LORA, unit("b_ckv")), w["b_kva_norm"][l], out_dtype=BF16,
                                          name="b_norm_ckv")
        gsmall["b_kva_norm"][l] = dg[0]
        g_uq = _mm(sv["cq"], d_qup, ta=True, name="b_mm_wuq")
        dcq = _mm(d_qup, w_uq_p[l], tb=True, name="b_mm_dcq")
        dparts["b_cq"], dg = _rmsnorm_bwd(dcq, (z, B_Q_LORA, unit("b_cq")), w["b_qa_norm"][l], out_dtype=BF16,
                                         name="b_norm_cq")
        gsmall["b_qa_norm"][l] = dg[0]
        gfull["b_w_uq"][l] = g_uq.reshape(B_Q_LORA, B_HEADS, LANES)[:, :, :B_NOPE + B_ROPE].reshape(B_Q_LORA, -1)
        gfull["b_w_ukv"][l] = jnp.concatenate(
            [g_ukv[:, :B_HEADS * LANES].reshape(B_KV_LORA, B_HEADS, LANES)[:, :, :B_NOPE],
             g_ukv[:, B_HEADS * LANES:].reshape(B_KV_LORA, B_HEADS, B_V)], axis=-1).reshape(B_KV_LORA, -1)
        do_cg, dl_cg = _combine_bwd(do_c, sv["o_c"], sv["lse_cg"], p64, name="b_combine")
        dcq_n, dck_n = [], []
        for g, (win, dil) in enumerate(C_PATTERNS):
            vc = (z, 512, unit("c%d" % (3 * g + 2)))
            kw_c = dict(dil=dil, group=1, maxdist=win // dil)
            args = (sv["qc"][g], sv["kc"][g], vc, do_cg[g], sv["lse_cg"][g], dl_cg[g])
            dq = _band_dq(*args, name="b_band_dq_c%d" % g, **kw_c)
            dk, dv_ = _band_dkv(*args, name="b_band_dkv_c%d" % g, **kw_c)
            dparts["c%d" % (3 * g)], dg = _headnorm_bwd(dq, (z, 512, unit("c%d" % (3 * g))), gn["c_q"][g], p64, rope64,
                                                        name="b_hn_cq%d" % g)
            dcq_n.append(head_sum(dg, C_HEADS, C_HD))
            dparts["c%d" % (3 * g + 1)], dg = _headnorm_bwd(dk, (z, 512, unit("c%d" % (3 * g + 1))), gn["c_k"][g], p64,
                                                            rope64, name="b_hn_ck%d" % g)
            dck_n.append(head_sum(dg, C_HEADS, C_HD))
            dparts["c%d" % (3 * g + 2)] = dv_
        gsmall["c_qn"][l] = jnp.stack(dcq_n)
        gsmall["c_kn"][l] = jnp.stack(dck_n)
        mv = (sv["mkv"], HPG * M_HD, M_HEADS * M_HD // (HPG * M_HD))
        dl_m, _ = _delta(do_m, sv["o_m"], p128, name="b_delta_m")
        kw_m = dict(dv=M_HD, causal=False, scale=scale_m)
        dmq = _flash_dq(sv["mq"], sv["mk"], mv, do_m, sv["lse_m"], dl_m, name="b_flash_dq_m", **kw_m)
        dmk, dmv = _flash_dkv(sv["mq"], sv["mk"], mv, do_m, sv["lse_m"], dl_m, name="b_flash_dkv_m", **kw_m)
        dparts["m_q"], dg = _headnorm_bwd(dmq, (z, M_HEADS * M_HD, unit("m_q")), gn["m_q"], p128, None, name="b_hn_mq")
        gsmall["m_qn"][l] = head_sum(dg, M_HEADS, M_HD)
        d_mk, dg = _headnorm_bwd(dmk, (sv["mkv"], M_HEADS * M_HD, 0), gn["m_k"], p128, None, name="b_hn_mk")
        gsmall["m_kn"][l] = head_sum(dg, M_HEADS, M_HD)
        d_mkv = jnp.concatenate([d_mk, dmv], axis=1)
        gfull["m_w_kv"][l] = _mm(sv["memn"], d_mkv, ta=True, name="b_mm_wmkv")
        dmemn = _mm(d_mkv, full["m_w_kv"][l], tb=True, name="b_mm_dmemn")
        _, dg = _rmsnorm_bwd(dmemn, (mem, d_model, 0), w["m_g_mem"][l], name="b_norm_mem")
        gsmall["m_g_mem"][l] = dg[0]
        dz = jnp.concatenate([dparts[n] for n in order], axis=1)
        g_in = _mm(sv["h"], dz, ta=True, name="b_mm_win")
        gfull["w_in"][l] = _unpermute_cols(g_in, lay, orig_names)
        dh = _mm(dz, w_in_p[l], tb=True, name="b_mm_dh")
        dx, dg = _rmsnorm_bwd(dh, (sv["x"], d_model, 0), w["g_mix"][l], dres=dx2, name="b_norm_mix")
        gsmall["g_mix"][l] = dg[0]

    out_g, out_d, out_m, out_v = {}, {}, {}, {}
    for name in BIG:
        gsm = _shard_major(jnp.stack(gfull[name]), BIG_AXIS[name])
        recv = _exchange(gsm, scatter=True, name="rs_" + name)
        shard = w[name].shape
        two = lambda a: a.reshape(-1, shard[-1])
        g, dlt, mn, vn = _adam(recv.reshape(N_DEV, -1, shard[-1]), two(w[name]), two(moments_m[name]),
                               two(moments_v[name]), name="adam_" + name)
        out_g[name], out_d[name], out_m[name], out_v[name] = (a.reshape(shard) for a in (g, dlt, mn, vn))
    sizes = [int(np.prod(w[n].shape)) for n in SMALL]
    total = sum(sizes)
    padded = -(-total // (8 * LANES)) * (8 * LANES)
    pack = lambda d: jnp.pad(jnp.concatenate([jnp.reshape(d[n], (-1,)) for n in SMALL]), (0, padded - total)).reshape(-1, LANES)
    gpack = pack({n: jnp.stack(gsmall[n]) for n in SMALL})
    recv = _exchange(gpack, scatter=False, name="ag_small_grads")
    packed = _adam(recv, pack(w), pack(moments_m), pack(moments_v), name="adam_small")
    offs = np.cumsum([0] + sizes)
    for a, out in zip(packed, (out_g, out_d, out_m, out_v)):
        flat = a.reshape(-1)
        for n, o, sz in zip(SMALL, offs[:-1], sizes):
            out[n] = flat[o:o + sz].reshape(w[n].shape)
    return loss, dx, out_g, out_d, out_m, out_v


def kernel(x, mem, positions, g_mix, w_in, b_gate, a_qn, a_kn, a_sink, b_qa_norm, b_kva_norm, b_w_uq, b_w_ukv, b_qn, b_kn, c_qn, c_kn, m_g_mem, m_w_kv, m_qn, m_kn, w_branch, w_out, g_mlp, w_up, w_down, loss_target, m_g_mix, m_w_in, m_b_gate, m_a_qn, m_a_kn, m_a_sink, m_b_qa_norm, m_b_kva_norm, m_b_w_uq, m_b_w_ukv, m_b_qn, m_b_kn, m_c_qn, m_c_kn, m_m_g_mem, m_m_w_kv, m_m_qn, m_m_kn, m_w_branch, m_w_out, m_g_mlp, m_w_up, m_w_down, v_g_mix, v_w_in, v_b_gate, v_a_qn, v_a_kn, v_a_sink, v_b_qa_norm, v_b_kva_norm, v_b_w_uq, v_b_w_ukv, v_b_qn, v_b_kn, v_c_qn, v_c_kn, v_m_g_mem, v_m_w_kv, v_m_qn, v_m_kn, v_w_branch, v_w_out, v_g_mlp, v_w_up, v_w_down):
    w = dict(g_mix=g_mix, w_in=w_in, b_gate=b_gate, a_qn=a_qn, a_kn=a_kn, a_sink=a_sink, b_qa_norm=b_qa_norm,
             b_kva_norm=b_kva_norm, b_w_uq=b_w_uq, b_w_ukv=b_w_ukv, b_qn=b_qn, b_kn=b_kn, c_qn=c_qn, c_kn=c_kn,
             m_g_mem=m_g_mem, m_w_kv=m_w_kv, m_qn=m_qn, m_kn=m_kn, w_branch=w_branch, w_out=w_out, g_mlp=g_mlp,
             w_up=w_up, w_down=w_down)
    mm_ = dict(g_mix=m_g_mix, w_in=m_w_in, b_gate=m_b_gate, a_qn=m_a_qn, a_kn=m_a_kn, a_sink=m_a_sink,
               b_qa_norm=m_b_qa_norm, b_kva_norm=m_b_kva_norm, b_w_uq=m_b_w_uq, b_w_ukv=m_b_w_ukv, b_qn=m_b_qn,
               b_kn=m_b_kn, c_qn=m_c_qn, c_kn=m_c_kn, m_g_mem=m_m_g_mem, m_w_kv=m_m_w_kv, m_qn=m_m_qn, m_kn=m_m_kn,
               w_branch=m_w_branch, w_out=m_w_out, g_mlp=m_g_mlp, w_up=m_w_up, w_down=m_w_down)
    vv = dict(g_mix=v_g_mix, w_in=v_w_in, b_gate=v_b_gate, a_qn=v_a_qn, a_kn=v_a_kn, a_sink=v_a_sink,
              b_qa_norm=v_b_qa_norm, b_kva_norm=v_b_kva_norm, b_w_uq=v_b_w_uq, b_w_ukv=v_b_w_ukv, b_qn=v_b_qn,
              b_kn=v_b_kn, c_qn=v_c_qn, c_kn=v_c_kn, m_g_mem=v_m_g_mem, m_w_kv=v_m_w_kv, m_qn=v_m_qn, m_kn=v_m_kn,
              w_branch=v_w_branch, w_out=v_w_out, g_mlp=v_g_mlp, w_up=v_w_up, w_down=v_w_down)
    loss, dx, g, dlt, mn, vn = _step(x[0], mem[0], positions[0], w, mm_, vv, loss_target[0])
    return (loss, dx[None], *[g[n] for n in WEIGHTS], *[dlt[n] for n in WEIGHTS], *[mn[n] for n in WEIGHTS],
            *[vn[n] for n in WEIGHTS])
```

```python
import functools

import numpy as np
import jax
import jax.numpy as jnp
from jax import lax
from jax.experimental import pallas as pl
from jax.experimental.pallas import tpu as pltpu

F32 = jnp.float32
BF16 = jnp.bfloat16

LANES = 128
N_DEV = 8
AXES = ("x", "y", "c")
VMEM_LIMIT = 48 * 1024 * 1024

EPS = 1e-6
NEG = -1e30
ROPE_THETA = 10000.0
BLOCK = 128

A_HEADS, A_KV_HEADS, A_HD, A_WINDOW = 8, 2, 64, 128
B_HEADS, B_Q_LORA, B_KV_LORA, B_NOPE, B_ROPE, B_V = 8, 384, 256, 64, 32, 64
C_PATTERNS = ((128, 1), (512, 4), (2048, 16))
C_HEADS, C_HD = 8, 64
M_HEADS, M_HD = 4, 128
BRANCH_W = 512
N_BRANCH = 4

ADAM_LR, ADAM_B1, ADAM_B2, ADAM_EPS, ADAM_WD, ADAM_STEP = 0.001, 0.9, 0.999, 1e-08, 0.01, 10

NT_DIMS = (((1,), (1,)), ((), ()))
TN_DIMS = (((0,), (0,)), ((), ()))
NN_DIMS = (((1,), (0,)), ((), ()))


def _pick(n, cands):
    for c in cands:
        if n % c == 0:
            return c
    return n


def _cparams(sem):
    return pltpu.CompilerParams(dimension_semantics=sem, vmem_limit_bytes=VMEM_LIMIT)


def _dot(a, b, dims=NN_DIMS):
    return lax.dot_general(a.astype(BF16), b.astype(BF16), dims, preferred_element_type=F32)


def _in_layout(d_model):
    orig = ([("a_q", 512), ("a_k", 128), ("a_v", 128), ("b_cq", B_Q_LORA), ("b_ckv", B_KV_LORA), ("b_kr", B_ROPE)]
            + [("c%d" % i, 512) for i in range(9)] + [("m_q", 512), ("gate", N_BRANCH * d_model)])
    order = ["gate", "a_q"] + ["c%d" % i for i in range(9)] + ["m_q", "b_ckv", "b_cq", "a_k", "a_v", "b_kr"]
    o_off, off = {}, 0
    for name, w in orig:
        o_off[name] = (off, w)
        off += w
    lay, noff = {}, 0
    for name in order:
        oo, w = o_off[name]
        nw = LANES if name == "b_kr" else w
        assert noff % nw == 0, (name, noff, nw)
        lay[name] = dict(orig=oo, w=w, off=noff, nw=nw)
        noff += nw
    assert noff % 512 == 0
    return lay, [n for n, _ in orig], order, off, noff


def _permute_cols(w, lay, order):
    cols = []
    for name in order:
        e = lay[name]
        p = w[..., e["orig"]:e["orig"] + e["w"]]
        if e["nw"] != e["w"]:
            p = jnp.pad(p, [(0, 0)] * (w.ndim - 1) + [(0, e["nw"] - e["w"])])
        cols.append(p)
    return jnp.concatenate(cols, axis=-1)


def _unpermute_cols(w, lay, orig_names):
    return jnp.concatenate([w[..., lay[n]["off"]:lay[n]["off"] + lay[n]["w"]] for n in orig_names], axis=-1)


def _pattern(groups):
    ids = np.repeat(np.arange(len(groups)), groups)
    assert ids.shape[0] == LANES
    gmat = (ids[:, None] == ids[None, :]).astype(np.float32)
    invn = (1.0 / np.asarray(groups, np.float32))[ids][None, :]
    return jnp.asarray(gmat, BF16), jnp.asarray(invn, F32)


def _gsum(v, gmat):
    hi = v.astype(BF16)
    r1 = v - hi.astype(F32)
    mid = r1.astype(BF16)
    lo = (r1 - mid.astype(F32)).astype(BF16)
    dot = lambda p: jnp.dot(p, gmat, preferred_element_type=F32)
    return dot(hi) + dot(mid) + dot(lo)


def _swap_half(v, lomask, half):
    return jnp.where(lomask > 0.5, pltpu.roll(v, LANES - half, 1), pltpu.roll(v, half, 1))


def _rows_spec(tr, width, off=0):
    return pl.BlockSpec((tr, width), lambda i: (i, off))


def _full_spec(shape):
    return pl.BlockSpec(shape, lambda i: (0,) * len(shape))


def _mm(a, b, *, ta=False, tb=False, out_dtype=F32, mode=None, extra=None, name, xchg=None):
    if ta:
        kd, m = a.shape
    else:
        m, kd = a.shape
    if tb:
        n, kb = b.shape
    else:
        kb, n = b.shape
    assert kd == kb, (a.shape, b.shape, ta, tb)
    tm = _pick(m, (1024, 512, 256, 128))
    tn = _pick(n, (1024, 1536, 512, 384, 256, 128))
    tk = _pick(kd, (512, 384, 256, 128))
    nk = kd // tk
    a_spec = (pl.BlockSpec((tk, tm), lambda i, j, k: (k, i)) if ta else pl.BlockSpec((tm, tk), lambda i, j, k: (i, k)))
    b_spec = (pl.BlockSpec((tn, tk), lambda i, j, k: (j, k)) if tb else pl.BlockSpec((tk, tn), lambda i, j, k: (k, j)))
    o_spec = pl.BlockSpec((tm, tn), lambda i, j, k: (i, j))
    dims = (((0 if ta else 1,), (1 if tb else 0,)), ((), ()))
    has_extra = mode in ("add", "drelu2")

    def body(*refs):
        a_ref, b_ref = refs[0], refs[1]
        e_ref = refs[2] if has_extra else None
        outs = refs[2 + has_extra:-1]
        acc = refs[-1]
        k = pl.program_id(2)

        @pl.when(k == 0)
        def _():
            acc[...] = jnp.zeros_like(acc)

        acc[...] += _dot(a_ref[...], b_ref[...], dims)

        @pl.when(k == nk - 1)
        def _():
            r = acc[...]
            if mode == "add":
                outs[0][...] = (e_ref[...] + r).astype(out_dtype)
            elif mode == "relu2":
                outs[0][...] = r
                outs[1][...] = jnp.square(jnp.maximum(r, 0.0)).astype(BF16)
            elif mode == "drelu2":
                outs[0][...] = (r * (2.0 * jnp.maximum(e_ref[...], 0.0))).astype(out_dtype)
            else:
                outs[0][...] = r.astype(out_dtype)

    in_specs = [a_spec, b_spec] + ([o_spec] if has_extra else [])
    args = [a, b] + ([extra] if has_extra else [])
    if mode == "relu2":
        out_shape = (jax.ShapeDtypeStruct((m, n), F32), jax.ShapeDtypeStruct((m, n), BF16))
        out_specs = (o_spec, o_spec)
    else:
        out_shape = jax.ShapeDtypeStruct((m, n), out_dtype)
        out_specs = o_spec
    return _pcall(
        body, grid=(m // tm, n // tn, nk), in_specs=in_specs, out_specs=out_specs, out_shape=out_shape,
        scratch_shapes=[pltpu.VMEM((tm, tn), F32)], sem=("parallel", "parallel", "arbitrary"), name=name, args=args,
        xchg=xchg)


def _rmsnorm_fwd(xv, gain, *, name, tr=512):
    x, width, off = xv
    rows = x.shape[0]
    tr = min(tr, rows)

    def body(x_ref, g_ref, o_ref):
        xf = x_ref[...]
        rstd = lax.rsqrt(jnp.mean(xf * xf, axis=-1, keepdims=True) + EPS)
        o_ref[...] = (xf * rstd * g_ref[...]).astype(BF16)

    return pl.pallas_call(
        body, grid=(rows // tr,), in_specs=[_rows_spec(tr, width, off), _full_spec((1, width))],
        out_specs=_rows_spec(tr, width), out_shape=jax.ShapeDtypeStruct((rows, width), BF16),
        compiler_params=_cparams(("parallel",)), name=name)(x, gain.reshape(1, width))


def _rmsnorm_bwd(dh, xv, gain, *, dres=None, out_dtype=F32, name, tr=512):
    x, width, off = xv
    rows = x.shape[0]
    tr = min(tr, rows)
    has_res = dres is not None

    def body(*refs):
        dh_ref, x_ref, g_ref = refs[:3]
        r_ref = refs[3] if has_res else None
        dx_ref, dg_ref = refs[3 + has_res:]
        i = pl.program_id(0)
        xf = x_ref[...]
        rstd = lax.rsqrt(jnp.mean(xf * xf, axis=-1, keepdims=True) + EPS)
        xh = xf * rstd
        d = dh_ref[...]
        dxh = d * g_ref[...]
        dx = rstd * (dxh - xh * jnp.mean(dxh * xh, axis=-1, keepdims=True))
        if has_res:
            dx = dx + r_ref[...]
        dx_ref[...] = dx.astype(out_dtype)

        @pl.when(i == 0)
        def _():
            dg_ref[...] = jnp.zeros_like(dg_ref)

        dg_ref[...] += jnp.sum(d * xh, axis=0, keepdims=True)

    in_specs = [_rows_spec(tr, width), _rows_spec(tr, width, off), _full_spec((1, width))]
    args = [dh, x, gain.reshape(1, width)]
    if has_res:
        in_specs.append(_rows_spec(tr, width))
        args.append(dres)
    return pl.pallas_call(
        body, grid=(rows // tr,), in_specs=in_specs,
        out_specs=(_rows_spec(tr, width), _full_spec((1, width))),
        out_shape=(jax.ShapeDtypeStruct((rows, width), out_dtype), jax.ShapeDtypeStruct((1, width), F32)),
        compiler_params=_cparams(("arbitrary",)), name=name)(*args)


def _headnorm_fwd(xv, gain_row, pat, rope, *, addend=None, name, tr=512):
    x, width, off = xv
    rows = x.shape[0]
    tr = min(tr, rows)
    gmat, invn = pat
    nblk = width // LANES
    has_rope = rope is not None
    has_add = addend is not None
    half = rope[3] if has_rope else 0

    def body(*refs):
        x_ref, g_ref, gm_ref, in_ref = refs[:4]
        pos = 4
        if has_rope:
            c_ref, s_ref, lo_ref = refs[pos:pos + 3]
            pos += 3
        if has_add:
            a_ref = refs[pos]
            pos += 1
        o_ref = refs[pos]
        gm = gm_ref[...]
        inv = in_ref[...]
        if has_add:
            add = pltpu.roll(a_ref[...].astype(F32), 64, 1)
        for j in range(nblk):
            sl = slice(j * LANES, (j + 1) * LANES)
            xb = x_ref[:, sl]
            rstd = lax.rsqrt(_gsum(xb * xb, gm) * inv + EPS)
            y = xb * rstd * g_ref[:, sl]
            if has_rope:
                y = y * c_ref[...] + _swap_half(y, lo_ref[...], half) * s_ref[...]
            if has_add:
                y = y + add
            o_ref[:, sl] = y.astype(BF16)

    in_specs = [_rows_spec(tr, width, off), _full_spec((1, width)), _full_spec((LANES, LANES)), _full_spec((1, LANES))]
    args = [x, gain_row.reshape(1, width), gmat, invn]
    if has_rope:
        in_specs += [_rows_spec(tr, LANES), _rows_spec(tr, LANES), _full_spec((1, LANES))]
        args += [rope[0], rope[1], rope[2]]
    if has_add:
        in_specs.append(_rows_spec(tr, LANES))
        args.append(addend)
    return pl.pallas_call(
        body, grid=(rows // tr,), in_specs=in_specs, out_specs=_rows_spec(tr, width),
        out_shape=jax.ShapeDtypeStruct((rows, width), BF16),
        compiler_params=_cparams(("parallel",)), name=name)(*args)


def _headnorm_bwd(dy, xv, gain_row, pat, rope, *, name, tr=512):
    x, width, off = xv
    rows = x.shape[0]
    tr = min(tr, rows)
    gmat, invn = pat
    nblk = width // LANES
    has_rope = rope is not None
    half = rope[3] if has_rope else 0

    def body(*refs):
        dy_ref, x_ref, g_ref, gm_ref, in_ref = refs[:5]
        pos = 5
        if has_rope:
            c_ref, s_ref, lo_ref = refs[pos:pos + 3]
            pos += 3
        dx_ref, dg_ref = refs[pos:]
        i = pl.program_id(0)
        gm = gm_ref[...]
        inv = in_ref[...]

        @pl.when(i == 0)
        def _():
            dg_ref[...] = jnp.zeros_like(dg_ref)

        for j in range(nblk):
            sl = slice(j * LANES, (j + 1) * LANES)
            d = dy_ref[:, sl]
            if has_rope:
                d = d * c_ref[...] + _swap_half(d * s_ref[...], lo_ref[...], half)
            xb = x_ref[:, sl]
            rstd = lax.rsqrt(_gsum(xb * xb, gm) * inv + EPS)
            xh = xb * rstd
            dg_ref[:, sl] += jnp.sum(d * xh, axis=0, keepdims=True)
            dxh = d * g_ref[:, sl]
            dx = rstd * (dxh - xh * (_gsum(dxh * xh, gm) * inv))
            dx_ref[:, sl] = dx.astype(BF16)

    in_specs = [_rows_spec(tr, width), _rows_spec(tr, width, off), _full_spec((1, width)),
                _full_spec((LANES, LANES)), _full_spec((1, LANES))]
    args = [dy, x, gain_row.reshape(1, width), gmat, invn]
    if has_rope:
        in_specs += [_rows_spec(tr, LANES), _rows_spec(tr, LANES), _full_spec((1, LANES))]
        args += [rope[0], rope[1], rope[2]]
    return pl.pallas_call(
        body, grid=(rows // tr,), in_specs=in_specs,
        out_specs=(_rows_spec(tr, width), _full_spec((1, width))),
        out_shape=(jax.ShapeDtypeStruct((rows, width), BF16), jax.ShapeDtypeStruct((1, width), F32)),
        compiler_params=_cparams(("arbitrary",)), name=name)(*args)


def _merge_fwd(zv, b_gate, ys, *, name, tr=256):
    z, gw, off = zv
    rows, d = ys[0].shape
    tr = min(tr, rows)

    def body(z_ref, b_ref, y0, y1, y2, y3, t_ref):
        acc = None
        for n, y_ref in enumerate((y0, y1, y2, y3)):
            sl = slice(n * d, (n + 1) * d)
            term = jax.nn.sigmoid(z_ref[:, sl] + b_ref[:, sl]) * y_ref[...]
            acc = term if acc is None else acc + term
        t_ref[...] = acc.astype(BF16)

    return pl.pallas_call(
        body, grid=(rows // tr,),
        in_specs=[_rows_spec(tr, gw, off), _full_spec((1, gw))] + [_rows_spec(tr, d)] * 4,
        out_specs=_rows_spec(tr, d), out_shape=jax.ShapeDtypeStruct((rows, d), BF16),
        compiler_params=_cparams(("parallel",)), name=name)(z, b_gate.reshape(1, gw), *ys)


def _merge_bwd(dt, zv, b_gate, ys, *, name, tr=256):
    z, gw, off = zv
    rows, d = dt.shape
    tr = min(tr, rows)

    def body(dt_ref, z_ref, b_ref, y0, y1, y2, y3, d0, d1, d2, d3, dg_ref, db_ref):
        i = pl.program_id(0)

        @pl.when(i == 0)
        def _():
            db_ref[...] = jnp.zeros_like(db_ref)

        dtv = dt_ref[...]
        for n, (y_ref, d_ref) in enumerate(((y0, d0), (y1, d1), (y2, d2), (y3, d3))):
            sl = slice(n * d, (n + 1) * d)
            gate = jax.nn.sigmoid(z_ref[:, sl] + b_ref[:, sl])
            d_ref[...] = (dtv * gate).astype(BF16)
            dpre = dtv * y_ref[...] * (gate * (1.0 - gate))
            dg_ref[:, sl] = dpre.astype(BF16)
            db_ref[:, sl] += jnp.sum(dpre, axis=0, keepdims=True)

    outs = pl.pallas_call(
        body, grid=(rows // tr,),
        in_specs=[_rows_spec(tr, d), _rows_spec(tr, gw, off), _full_spec((1, gw))] + [_rows_spec(tr, d)] * 4,
        out_specs=tuple([_rows_spec(tr, d)] * 4 + [_rows_spec(tr, gw), _full_spec((1, gw))]),
        out_shape=tuple([jax.ShapeDtypeStruct((rows, d), BF16)] * 4
                        + [jax.ShapeDtypeStruct((rows, gw), BF16), jax.ShapeDtypeStruct((1, gw), F32)]),
        compiler_params=_cparams(("arbitrary",)), name=name)(dt, z, b_gate.reshape(1, gw), *ys)
    return outs[:4], outs[4], outs[5]


def _combine_fwd(os_, lses, *, name, tr=512):
    rows, width = os_[0].shape

    def body(o0, o1, o2, l0, l1, l2, out_ref):
        la, lb, lc = l0[...], l1[...], l2[...]
        mx = jnp.maximum(jnp.maximum(la, lb), lc)
        ea, eb, ec = jnp.exp(la - mx), jnp.exp(lb - mx), jnp.exp(lc - mx)
        inv = 1.0 / (ea + eb + ec)
        out_ref[...] = (ea * o0[...] + eb * o1[...] + ec * o2[...]) * inv

    return pl.pallas_call(
        body, grid=(rows // tr,), in_specs=[_rows_spec(tr, width)] * 6, out_specs=_rows_spec(tr, width),
        out_shape=jax.ShapeDtypeStruct((rows, width), F32),
        compiler_params=_cparams(("parallel",)), name=name)(*os_, *lses)


def _combine_bwd(do, o, lses, pat, *, name, tr=512):
    rows, width = do.shape
    gmat, _ = pat
    nblk = width // LANES

    def body(do_ref, o_ref, l0, l1, l2, gm_ref, d0, d1, d2, e0, e1, e2):
        gm = gm_ref[...]
        for j in range(nblk):
            sl = slice(j * LANES, (j + 1) * LANES)
            la, lb, lc = l0[:, sl], l1[:, sl], l2[:, sl]
            mx = jnp.maximum(jnp.maximum(la, lb), lc)
            ea, eb, ec = jnp.exp(la - mx), jnp.exp(lb - mx), jnp.exp(lc - mx)
            inv = 1.0 / (ea + eb + ec)
            dov = do_ref[:, sl]
            tot = _gsum(dov * o_ref[:, sl], gm)
            for e, d_ref, e_ref in ((ea, d0, e0), (eb, d1, e1), (ec, d2, e2)):
                w = e * inv
                d_ref[:, sl] = w * dov
                e_ref[:, sl] = w * tot

    outs = pl.pallas_call(
        body, grid=(rows // tr,),
        in_specs=[_rows_spec(tr, width)] * 5 + [_full_spec((LANES, LANES))],
        out_specs=tuple([_rows_spec(tr, width)] * 6),
        out_shape=tuple([jax.ShapeDtypeStruct((rows, width), F32)] * 6),
        compiler_params=_cparams(("parallel",)), name=name)(do, o, *lses, gmat)
    return outs[:3], outs[3:]


def _delta(do, o, pat, *, lse=None, sink_row=None, name, tr=512):
    rows, width = do.shape
    gmat, _ = pat
    nblk = width // LANES
    has_sink = sink_row is not None

    def body(*refs):
        do_ref, o_ref, gm_ref = refs[:3]
        if has_sink:
            l_ref, s_ref, dl_ref, ds_ref = refs[3:]
        else:
            dl_ref = refs[3]
        gm = gm_ref[...]
        if has_sink:
            @pl.when(pl.program_id(0) == 0)
            def _():
                ds_ref[...] = jnp.zeros_like(ds_ref)
        for j in range(nblk):
            sl = slice(j * LANES, (j + 1) * LANES)
            dl = _gsum(do_ref[:, sl] * o_ref[:, sl], gm)
            dl_ref[:, sl] = dl
            if has_sink:
                ds_ref[:, sl] += jnp.sum(-jnp.exp(s_ref[:, sl] - l_ref[:, sl]) * dl, axis=0, keepdims=True)

    in_specs = [_rows_spec(tr, width), _rows_spec(tr, width), _full_spec((LANES, LANES))]
    args = [do, o, gmat]
    out_specs = [_rows_spec(tr, width)]
    out_shape = [jax.ShapeDtypeStruct((rows, width), F32)]
    if has_sink:
        in_specs += [_rows_spec(tr, width), _full_spec((1, width))]
        args += [lse, sink_row]
        out_specs.append(_full_spec((1, width)))
        out_shape.append(jax.ShapeDtypeStruct((1, width), F32))
    outs = pl.pallas_call(
        body, grid=(rows // tr,), in_specs=in_specs, out_specs=tuple(out_specs), out_shape=tuple(out_shape),
        compiler_params=_cparams(("arbitrary",)), name=name)(*args)
    return outs if has_sink else (outs[0], None)


def _kp_reduce(dk, *, name, tr=512):
    rows, width = dk.shape
    nblk = width // LANES

    def body(dk_ref, o_ref):
        acc = dk_ref[:, 0:LANES]
        for j in range(1, nblk):
            acc = acc + dk_ref[:, j * LANES:(j + 1) * LANES]
        lane = lax.broadcasted_iota(jnp.int32, acc.shape, 1)
        o_ref[...] = jnp.where(lane < B_ROPE, pltpu.roll(acc, 64, 1), 0.0)

    return pl.pallas_call(
        body, grid=(rows // tr,), in_specs=[_rows_spec(tr, width)], out_specs=_rows_spec(tr, LANES),
        out_shape=jax.ShapeDtypeStruct((rows, LANES), F32),
        compiler_params=_cparams(("parallel",)), name=name)(dk)


def _loss_head(y, target, *, name, tr=512):
    rows, d = y.shape

    def body(y_ref, t_ref, dy_ref, l_ref):
        @pl.when(pl.program_id(0) == 0)
        def _():
            l_ref[...] = jnp.zeros_like(l_ref)

        diff = y_ref[...] - t_ref[...]
        dy_ref[...] = diff * (1.0 / d)
        part = 0.5 * jnp.sum(jnp.mean(diff * diff, axis=-1, keepdims=True), axis=0, keepdims=True)
        l_ref[...] += jnp.broadcast_to(part, l_ref.shape)

    return pl.pallas_call(
        body, grid=(rows // tr,), in_specs=[_rows_spec(tr, d)] * 2,
        out_specs=(_rows_spec(tr, d), _full_spec((1, LANES))),
        out_shape=(jax.ShapeDtypeStruct((rows, d), F32), jax.ShapeDtypeStruct((1, LANES), F32)),
        compiler_params=_cparams(("arbitrary",)), name=name)(y, target)


def _adam(recvs, w, m, v, *, name):
    nl = len(recvs)
    rows_l, cols = recvs[0].shape[1:]
    assert w.shape == (nl * rows_l, cols)
    tr = _pick(rows_l, (128, 64, 32, 16, 8))
    nt = rows_l // tr
    c1 = 1.0 / (1.0 - ADAM_B1 ** ADAM_STEP)
    c2 = 1.0 / (1.0 - ADAM_B2 ** ADAM_STEP)

    def body(*refs):
        r_refs = refs[:nl]
        w_ref, m_ref, v_ref, g_out, d_out, m_out, v_out = refs[nl:]
        layer = pl.program_id(0)

        def update(r_ref):
            g = r_ref[0].astype(F32)
            for j in range(1, N_DEV):
                g = g + r_ref[j].astype(F32)
            mn = ADAM_B1 * m_ref[...] + (1.0 - ADAM_B1) * g
            vn = ADAM_B2 * v_ref[...] + (1.0 - ADAM_B2) * (g * g)
            g_out[...] = g
            m_out[...] = mn
            v_out[...] = vn
            d_out[...] = -ADAM_LR * ((mn * c1) / (jnp.sqrt(vn * c2) + ADAM_EPS) + ADAM_WD * w_ref[...])

        for k in range(nl):
            pl.when(layer == k)(functools.partial(update, r_refs[k]))

    spec = pl.BlockSpec((tr, cols), lambda l, i: (l * nt + i, 0))
    rspec = lambda k: pl.BlockSpec((N_DEV, tr, cols), lambda l, i: (0, jnp.where(l == k, i, 0), 0))
    sds = jax.ShapeDtypeStruct(w.shape, F32)
    return pl.pallas_call(
        body, grid=(nl, nt), in_specs=[rspec(k) for k in range(nl)] + [spec, spec, spec],
        out_specs=(spec,) * 4, out_shape=(sds,) * 4,
        compiler_params=_cparams(("arbitrary", "arbitrary")), name=name)(*recvs, w, m, v)


def _band_mask(i, maxdist, nrow_blocks):
    row = lax.broadcasted_iota(jnp.int32, (BLOCK, 2 * BLOCK), 0)
    col = lax.broadcasted_iota(jnp.int32, (BLOCK, 2 * BLOCK), 1)
    dist = row - col + BLOCK
    return (dist >= 0) & (dist <= maxdist) & ((i > 0) | (col >= BLOCK))


def _band_views(q, k, zv, dil):
    s = q.shape[0]
    z, kvw, voff = zv
    if dil == 1:
        return q, k, z, z.shape[1] // kvw, voff
    vd = z[:, voff * kvw:(voff + 1) * kvw].reshape(s // dil, dil * kvw)
    return q.reshape(s // dil, dil * q.shape[1]), k.reshape(s // dil, dil * k.shape[1]), vd, 1, 0


def _band_fwd(q, k, zv, *, dil, group, maxdist, sink_row=None, name):
    s, qw = q.shape
    kvw = zv[1]
    qd, kd, zd, zblocks, voff = _band_views(q, k, zv, dil)
    nb = s // dil // BLOCK
    nheads = qw // 64
    scale = 64 ** -0.5
    has_sink = sink_row is not None

    def body(*refs):
        q_ref, kp_ref, kc_ref, vp_ref, vc_ref = refs[:5]
        s_ref = refs[5] if has_sink else None
        o_ref, l_ref = refs[5 + has_sink:]
        i = pl.program_id(1)
        mask = _band_mask(i, maxdist, nb)
        k2 = jnp.concatenate([kp_ref[...], kc_ref[...]], axis=0)
        v2 = jnp.concatenate([vp_ref[...], vc_ref[...]], axis=0).astype(BF16)
        for h in range(nheads):
            g = h // group
            sc = _dot(q_ref[:, 64 * h:64 * h + 64], k2[:, 64 * g:64 * g + 64], NT_DIMS) * scale
            sc = jnp.where(mask, sc, NEG)
            mx = jnp.max(sc, axis=1, keepdims=True)
            if has_sink:
                snk = s_ref[:, 64 * h:64 * h + 1]
                mx = jnp.maximum(mx, snk)
            e = jnp.exp(sc - mx)
            den = jnp.sum(e, axis=1, keepdims=True)
            if has_sink:
                den = den + jnp.exp(snk - mx)
            p = e * (1.0 / den)
            o_ref[:, 64 * h:64 * h + 64] = _dot(p, v2[:, 64 * g:64 * g + 64])
            l_ref[:, 64 * h:64 * h + 64] = jnp.broadcast_to(mx + jnp.log(den), (BLOCK, 64))

    prev = lambda r, i: jnp.maximum(i - 1, 0)
    in_specs = [
        pl.BlockSpec((BLOCK, qw), lambda r, i: (i, r)),
        pl.BlockSpec((BLOCK, kvw), lambda r, i: (prev(r, i), r)),
        pl.BlockSpec((BLOCK, kvw), lambda r, i: (i, r)),
        pl.BlockSpec((BLOCK, kvw), lambda r, i: (prev(r, i), r * zblocks + voff)),
        pl.BlockSpec((BLOCK, kvw), lambda r, i: (i, r * zblocks + voff)),
    ]
    args = [qd, kd, kd, zd, zd]
    if has_sink:
        in_specs.append(pl.BlockSpec((1, qw), lambda r, i: (0, 0)))
        args.append(sink_row)
    ospec = pl.BlockSpec((BLOCK, qw), lambda r, i: (i, r))
    sds = jax.ShapeDtypeStruct((s // dil, dil * qw), F32)
    o, lse = pl.pallas_call(
        body, grid=(dil, nb), in_specs=in_specs, out_specs=(ospec, ospec), out_shape=(sds, sds),
        compiler_params=_cparams(("parallel", "parallel")), name=name)(*args)
    return o.reshape(s, qw), lse.reshape(s, qw)


def _band_dq(q, k, zv, do, lse, delta, *, dil, group, maxdist, name):
    s, qw = q.shape
    kvw = zv[1]
    qd, kd, zd, zblocks, voff = _band_views(q, k, zv, dil)
    rs = lambda t: t.reshape(s // dil, dil * qw)
    nb = s // dil // BLOCK
    nheads = qw // 64
    scale = 64 ** -0.5

    def body(q_ref, kp_ref, kc_ref, vp_ref, vc_ref, do_ref, l_ref, d_ref, dq_ref):
        i = pl.program_id(1)
        mask = _band_mask(i, maxdist, nb)
        k2 = jnp.concatenate([kp_ref[...], kc_ref[...]], axis=0)
        v2 = jnp.concatenate([vp_ref[...], vc_ref[...]], axis=0).astype(BF16)
        for h in range(nheads):
            g = h // group
            hs = slice(64 * h, 64 * h + 64)
            kg = k2[:, 64 * g:64 * g + 64]
            sc = _dot(q_ref[:, hs], kg, NT_DIMS) * scale
            p = jnp.exp(jnp.where(mask, sc, NEG) - l_ref[:, 64 * h:64 * h + 1])
            dp = _dot(do_ref[:, hs], v2[:, 64 * g:64 * g + 64], NT_DIMS)
            ds = p * (dp - d_ref[:, 64 * h:64 * h + 1]) * scale
            dq_ref[:, hs] = _dot(ds, kg)

    prev = lambda r, i: jnp.maximum(i - 1, 0)
    qspec = pl.BlockSpec((BLOCK, qw), lambda r, i: (i, r))
    in_specs = [
        qspec,
        pl.BlockSpec((BLOCK, kvw), lambda r, i: (prev(r, i), r)),
        pl.BlockSpec((BLOCK, kvw), lambda r, i: (i, r)),
        pl.BlockSpec((BLOCK, kvw), lambda r, i: (prev(r, i), r * zblocks + voff)),
        pl.BlockSpec((BLOCK, kvw), lambda r, i: (i, r * zblocks + voff)),
        qspec, qspec, qspec,
    ]
    dq = pl.pallas_call(
        body, grid=(dil, nb), in_specs=in_specs, out_specs=qspec,
        out_shape=jax.ShapeDtypeStruct((s // dil, dil * qw), F32),
        compiler_params=_cparams(("parallel", "parallel")), name=name)(qd, kd, kd, zd, zd, rs(do), rs(lse), rs(delta))
    return dq.reshape(s, qw)


def _band_dkv(q, k, zv, do, lse, delta, *, dil, group, maxdist, name):
    s, qw = q.shape
    kvw = zv[1]
    qd, kd, zd, zblocks, voff = _band_views(q, k, zv, dil)
    rs = lambda t: t.reshape(s // dil, dil * qw)
    nb = s // dil // BLOCK
    nheads = qw // 64
    nkv = kvw // 64
    scale = 64 ** -0.5

    def body(k_ref, v_ref, q0, q1, do0, do1, l0, l1, d0, d1, dk_ref, dv_ref):
        j = pl.program_id(1)
        row = lax.broadcasted_iota(jnp.int32, (2 * BLOCK, BLOCK), 0)
        col = lax.broadcasted_iota(jnp.int32, (2 * BLOCK, BLOCK), 1)
        dist = row - col
        mask = (dist >= 0) & (dist <= maxdist) & ((j < nb - 1) | (row < BLOCK))
        q2 = jnp.concatenate([q0[...], q1[...]], axis=0)
        do2 = jnp.concatenate([do0[...], do1[...]], axis=0).astype(BF16)
        vb = v_ref[...].astype(BF16)
        dks = [None] * nkv
        dvs = [None] * nkv
        for h in range(nheads):
            g = h // group
            hs = slice(64 * h, 64 * h + 64)
            lse2 = jnp.concatenate([l0[:, 64 * h:64 * h + 1], l1[:, 64 * h:64 * h + 1]], axis=0)
            dl2 = jnp.concatenate([d0[:, 64 * h:64 * h + 1], d1[:, 64 * h:64 * h + 1]], axis=0)
            sc = _dot(q2[:, hs], k_ref[:, 64 * g:64 * g + 64], NT_DIMS) * scale
            p = jnp.exp(jnp.where(mask, sc, NEG) - lse2)
            dvh = _dot(p, do2[:, hs], TN_DIMS)
            dp = _dot(do2[:, hs], vb[:, 64 * g:64 * g + 64], NT_DIMS)
            ds = p * (dp - dl2) * scale
            dkh = _dot(ds, q2[:, hs], TN_DIMS)
            dks[g] = dkh if dks[g] is None else dks[g] + dkh
            dvs[g] = dvh if dvs[g] is None else dvs[g] + dvh
        for g in range(nkv):
            dk_ref[:, 64 * g:64 * g + 64] = dks[g]
            dv_ref[:, 64 * g:64 * g + 64] = dvs[g].astype(BF16)

    nxt = lambda r, j: jnp.minimum(j + 1, nb - 1)
    q0s = pl.BlockSpec((BLOCK, qw), lambda r, j: (j, r))
    q1s = pl.BlockSpec((BLOCK, qw), lambda r, j: (nxt(r, j), r))
    kspec = pl.BlockSpec((BLOCK, kvw), lambda r, j: (j, r))
    in_specs = [kspec, pl.BlockSpec((BLOCK, kvw), lambda r, j: (j, r * zblocks + voff)),
                q0s, q1s, q0s, q1s, q0s, q1s, q0s, q1s]
    dod, lsd, dld = rs(do), rs(lse), rs(delta)
    dk, dv = pl.pallas_call(
        body, grid=(dil, nb), in_specs=in_specs, out_specs=(kspec, kspec),
        out_shape=(jax.ShapeDtypeStruct((s // dil, dil * kvw), F32), jax.ShapeDtypeStruct((s // dil, dil * kvw), BF16)),
        compiler_params=_cparams(("parallel", "parallel")), name=name)(kd, zd, qd, qd, dod, dod, lsd, lsd, dld, dld)
    return dk.reshape(s, kvw), dv.reshape(s, kvw)


HPG = 2


def _flash_tiles(sq, sk):
    return _pick(sq, (512, 256, 128)), _pick(sk, (512, 256, 128))


def _diag_mask(t):
    return lax.broadcasted_iota(jnp.int32, (t, t), 1) <= lax.broadcasted_iota(jnp.int32, (t, t), 0)


def _flash_fwd(q, k, vv, *, dv, causal, scale, name, xchg=None):
    sq, qw = q.shape
    sk = k.shape[0]
    varr, vbw, voff = vv
    assert vbw == HPG * dv
    ngrp = qw // (HPG * LANES)
    tq, tk = _flash_tiles(sq, sk)
    nq, nk = sq // tq, sk // tk
    if causal:
        assert tq == tk and sq == sk

    def body(q_ref, k_ref, v_ref, o_ref, l_ref, m_s, l_s, acc_s):
        i, j = pl.program_id(1), pl.program_id(2)

        @pl.when(j == 0)
        def _():
            m_s[...] = jnp.full_like(m_s, NEG)
            l_s[...] = jnp.zeros_like(l_s)
            acc_s[...] = jnp.zeros_like(acc_s)

        def step(masked):
            if masked:
                mask = _diag_mask(tq)
            for h in range(HPG):
                sc = _dot(q_ref[:, LANES * h:LANES * (h + 1)], k_ref[:, LANES * h:LANES * (h + 1)], NT_DIMS) * scale
                if masked:
                    sc = jnp.where(mask, sc, NEG)
                m_prev = m_s[h]
                m_new = jnp.maximum(m_prev, jnp.max(sc, axis=1, keepdims=True))
                alpha = jnp.exp(m_prev - m_new)
                p = jnp.exp(sc - m_new[:, 0:1])
                l_s[h] = alpha * l_s[h] + jnp.sum(p, axis=1, keepdims=True)
                acc_s[h] = acc_s[h] * alpha[:, 0:dv] + _dot(p, v_ref[:, dv * h:dv * (h + 1)])
                m_s[h] = m_new

        if causal:
            pl.when(j < i)(functools.partial(step, False))
            pl.when(j == i)(functools.partial(step, True))
        else:
            step(False)

        @pl.when(j == nk - 1)
        def _():
            for h in range(HPG):
                o_ref[:, dv * h:dv * (h + 1)] = acc_s[h] * (1.0 / l_s[h][:, 0:dv])
                l_ref[:, LANES * h:LANES * (h + 1)] = m_s[h] + jnp.log(l_s[h])

    kidx = (lambda g, i, j: jnp.minimum(j, i)) if causal else (lambda g, i, j: j)
    in_specs = [
        pl.BlockSpec((tq, HPG * LANES), lambda g, i, j: (i, g)),
        pl.BlockSpec((tk, HPG * LANES), lambda g, i, j: (kidx(g, i, j), g)),
        pl.BlockSpec((tk, vbw), lambda g, i, j: (kidx(g, i, j), voff + g)),
    ]
    return _pcall(
        body, grid=(ngrp, nq, nk), in_specs=in_specs,
        out_specs=(pl.BlockSpec((tq, HPG * dv), lambda g, i, j: (i, g)),
                   pl.BlockSpec((tq, HPG * LANES), lambda g, i, j: (i, g))),
        out_shape=(jax.ShapeDtypeStruct((sq, ngrp * HPG * dv), F32), jax.ShapeDtypeStruct((sq, qw), F32)),
        scratch_shapes=[pltpu.VMEM((HPG, tq, LANES), F32), pltpu.VMEM((HPG, tq, LANES), F32),
                        pltpu.VMEM((HPG, tq, dv), F32)],
        sem=("parallel", "parallel", "arbitrary"), name=name, args=(q, k, varr), xchg=xchg)


def _flash_dq(q, k, vv, do, lse, delta, *, dv, causal, scale, name, xchg=None):
    sq, qw = q.shape
    sk = k.shape[0]
    varr, vbw, voff = vv
    ngrp = qw // (HPG * LANES)
    tq, tk = _flash_tiles(sq, sk)
    nq, nk = sq // tq, sk // tk

    def body(q_ref, k_ref, v_ref, do_ref, l_ref, d_ref, dq_ref, acc_s):
        i, j = pl.program_id(1), pl.program_id(2)

        @pl.when(j == 0)
        def _():
            acc_s[...] = jnp.zeros_like(acc_s)

        def step(masked):
            if masked:
                mask = _diag_mask(tq)
            for h in range(HPG):
                kh = k_ref[:, LANES * h:LANES * (h + 1)]
                sc = _dot(q_ref[:, LANES * h:LANES * (h + 1)], kh, NT_DIMS) * scale
                if masked:
                    sc = jnp.where(mask, sc, NEG)
                p = jnp.exp(sc - l_ref[:, LANES * h:LANES * h + 1])
                dp = _dot(do_ref[:, dv * h:dv * (h + 1)], v_ref[:, dv * h:dv * (h + 1)], NT_DIMS)
                ds = p * (dp - d_ref[:, dv * h:dv * h + 1]) * scale
                acc_s[h] += _dot(ds, kh)

        if causal:
            pl.when(j < i)(functools.partial(step, False))
            pl.when(j == i)(functools.partial(step, True))
        else:
            step(False)

        @pl.when(j == nk - 1)
        def _():
            for h in range(HPG):
                dq_ref[:, LANES * h:LANES * (h + 1)] = acc_s[h]

    kidx = (lambda g, i, j: jnp.minimum(j, i)) if causal else (lambda g, i, j: j)
    qspec = pl.BlockSpec((tq, HPG * LANES), lambda g, i, j: (i, g))
    ospec = pl.BlockSpec((tq, HPG * dv), lambda g, i, j: (i, g))
    in_specs = [
        qspec,
        pl.BlockSpec((tk, HPG * LANES), lambda g, i, j: (kidx(g, i, j), g)),
        pl.BlockSpec((tk, vbw), lambda g, i, j: (kidx(g, i, j), voff + g)),
        ospec, qspec, ospec,
    ]
    return _pcall(
        body, grid=(ngrp, nq, nk), in_specs=in_specs, out_specs=qspec,
        out_shape=jax.ShapeDtypeStruct((sq, qw), F32),
        scratch_shapes=[pltpu.VMEM((HPG, tq, LANES), F32)],
        sem=("parallel", "parallel", "arbitrary"), name=name, args=(q, k, varr, do, lse, delta), xchg=xchg)


def _flash_dkv(q, k, vv, do, lse, delta, *, dv, causal, scale, name, xchg=None):
    sq, qw = q.shape
    sk = k.shape[0]
    varr, vbw, voff = vv
    ngrp = qw // (HPG * LANES)
    tq, tk = _flash_tiles(sq, sk)
    nq, nk = sq // tq, sk // tk

    def body(q_ref, k_ref, v_ref, do_ref, l_ref, d_ref, dk_ref, dv_ref, dk_s, dv_s):
        j, i = pl.program_id(1), pl.program_id(2)

        @pl.when(i == 0)
        def _():
            dk_s[...] = jnp.zeros_like(dk_s)
            dv_s[...] = jnp.zeros_like(dv_s)

        def step(masked):
            if masked:
                mask = _diag_mask(tq)
            for h in range(HPG):
                qh = q_ref[:, LANES * h:LANES * (h + 1)]
                doh = do_ref[:, dv * h:dv * (h + 1)]
                sc = _dot(qh, k_ref[:, LANES * h:LANES * (h + 1)], NT_DIMS) * scale
                if masked:
                    sc = jnp.where(mask, sc, NEG)
                p = jnp.exp(sc - l_ref[:, LANES * h:LANES * h + 1])
                dv_s[h] += _dot(p, doh, TN_DIMS)
                dp = _dot(doh, v_ref[:, dv * h:dv * (h + 1)], NT_DIMS)
                ds = p * (dp - d_ref[:, dv * h:dv * h + 1]) * scale
                dk_s[h] += _dot(ds, qh, TN_DIMS)

        if causal:
            pl.when(i > j)(functools.partial(step, False))
            pl.when(i == j)(functools.partial(step, True))
        else:
            step(False)

        @pl.when(i == nq - 1)
        def _():
            for h in range(HPG):
                dk_ref[:, LANES * h:LANES * (h + 1)] = dk_s[h]
                dv_ref[:, dv * h:dv * (h + 1)] = dv_s[h].astype(BF16)

    qidx = (lambda g, j, i: jnp.maximum(i, j)) if causal else (lambda g, j, i: i)
    qspec = pl.BlockSpec((tq, HPG * LANES), lambda g, j, i: (qidx(g, j, i), g))
    ospec = pl.BlockSpec((tq, HPG * dv), lambda g, j, i: (qidx(g, j, i), g))
    kspec = pl.BlockSpec((tk, HPG * LANES), lambda g, j, i: (j, g))
    in_specs = [qspec, kspec, pl.BlockSpec((tk, vbw), lambda g, j, i: (j, voff + g)), ospec, qspec, ospec]
    return _pcall(
        body, grid=(ngrp, nk, nq), in_specs=in_specs,
        out_specs=(kspec, pl.BlockSpec((tk, HPG * dv), lambda g, j, i: (j, g))),
        out_shape=(jax.ShapeDtypeStruct((sk, qw), F32), jax.ShapeDtypeStruct((sk, ngrp * HPG * dv), BF16)),
        scratch_shapes=[pltpu.VMEM((HPG, tk, LANES), F32), pltpu.VMEM((HPG, tk, dv), F32)],
        sem=("parallel", "parallel", "arbitrary"), name=name, args=(q, k, varr, do, lse, delta), xchg=xchg)


def _xchg_copies(src_ref, out_ref, send_sems, recv_sems, local_sems, k, scatter):
    me = 4 * lax.axis_index("x") + 2 * lax.axis_index("y") + lax.axis_index("c")
    own = src_ref.at[me] if scatter else src_ref
    copies = [pltpu.make_async_copy(own, out_ref.at[me], local_sems.at[k])]
    for d in range(1, N_DEV):
        peer = (me + d) % N_DEV
        piece = src_ref.at[peer] if scatter else src_ref
        sem = (N_DEV - 1) * k + d - 1
        copies.append(pltpu.make_async_remote_copy(
            src_ref=piece, dst_ref=out_ref.at[me], send_sem=send_sems.at[sem], recv_sem=recv_sems.at[sem],
            device_id=(peer // 4, (peer // 2) % 2, peer % 2), device_id_type=pl.DeviceIdType.MESH))
    return copies


def _xchg_shapes(xchg):
    return [jax.ShapeDtypeStruct((N_DEV,) + tuple(src.shape[1:] if scatter else src.shape), src.dtype)
            for src, scatter in xchg]


def _xchg_sems(nx):
    return [pltpu.SemaphoreType.DMA(((N_DEV - 1) * nx,)), pltpu.SemaphoreType.DMA(((N_DEV - 1) * nx,)),
            pltpu.SemaphoreType.DMA((nx,))]


ANY_SPEC = pl.BlockSpec(memory_space=pl.ANY)


def _exchange_many(xchg, *, name):
    nx = len(xchg)

    def body(*refs):
        srcs, outs = refs[:nx], refs[nx:2 * nx]
        send_sems, recv_sems, local_sems = refs[2 * nx:]
        copies = [c for k, (_, scatter) in enumerate(xchg)
                  for c in _xchg_copies(srcs[k], outs[k], send_sems, recv_sems, local_sems, k, scatter)]
        for c in copies:
            c.start()
        for c in copies:
            c.wait()

    return pl.pallas_call(
        body, in_specs=[ANY_SPEC] * nx, out_specs=tuple([ANY_SPEC] * nx), out_shape=tuple(_xchg_shapes(xchg)),
        scratch_shapes=_xchg_sems(nx), name=name)(*[src for src, _ in xchg])


def _pcall(body, *, grid, in_specs, out_specs, out_shape, scratch_shapes=(), sem, name, args, xchg=None):
    single = not isinstance(out_shape, (tuple, list))
    if not xchg:
        res = pl.pallas_call(
            body, grid=grid, in_specs=list(in_specs), out_specs=out_specs, out_shape=out_shape,
            scratch_shapes=list(scratch_shapes), compiler_params=_cparams(sem), name=name)(*args)
        return res if xchg is None else (res, [])
    out_specs_l = [out_specs] if single else list(out_specs)
    out_shape_l = [out_shape] if single else list(out_shape)
    nx, n_in, n_out, n_scr = len(xchg), len(in_specs), len(out_shape_l), len(scratch_shapes)

    def wrapped(*refs):
        ins, xin = refs[:n_in], refs[n_in:n_in + nx]
        pos = n_in + nx
        outs, xout = refs[pos:pos + n_out], refs[pos + n_out:pos + n_out + nx]
        pos += n_out + nx
        scr = refs[pos:pos + n_scr]
        send_sems, recv_sems, local_sems = refs[pos + n_scr:]
        first = functools.reduce(jnp.logical_and, [pl.program_id(a) == 0 for a in range(len(grid))])
        last = functools.reduce(jnp.logical_and, [pl.program_id(a) == grid[a] - 1 for a in range(len(grid))])

        def copies():
            return [c for k, (_, scatter) in enumerate(xchg)
                    for c in _xchg_copies(xin[k], xout[k], send_sems, recv_sems, local_sems, k, scatter)]

        @pl.when(first)
        def _():
            for c in copies():
                c.start()

        body(*ins, *outs, *scr)

        @pl.when(last)
        def _():
            for c in copies():
                c.wait()

    res = pl.pallas_call(
        wrapped, grid=grid, in_specs=list(in_specs) + [ANY_SPEC] * nx,
        out_specs=tuple(out_specs_l + [ANY_SPEC] * nx), out_shape=tuple(out_shape_l + _xchg_shapes(xchg)),
        scratch_shapes=list(scratch_shapes) + _xchg_sems(nx),
        compiler_params=_cparams(("arbitrary",) * len(grid)), name=name + "_x")(*args, *[src for src, _ in xchg])
    main = res[0] if single else tuple(res[:n_out])
    return main, list(res[n_out:])


def _unshard(wg, axis):
    nd = wg.ndim - 1
    perm = list(range(1, axis + 1)) + [0] + list(range(axis + 1, nd + 1))
    t = wg.transpose(perm)
    shp = list(wg.shape[1:])
    shp[axis] *= N_DEV
    return t.reshape(shp)


def _shard_major(full, axis):
    shp = list(full.shape)
    shp[axis:axis + 1] = [N_DEV, shp[axis] // N_DEV]
    t = full.reshape(shp)
    perm = [axis] + list(range(axis)) + list(range(axis + 1, len(shp)))
    return t.transpose(perm)


BIG = ("w_in", "b_w_uq", "b_w_ukv", "m_w_kv", "w_branch", "w_out", "w_up", "w_down")
BIG_AXIS = dict(w_in=2, b_w_uq=2, b_w_ukv=2, m_w_kv=1, w_branch=3, w_out=1, w_up=2, w_down=1)
SMALL = ("g_mix", "b_gate", "a_qn", "a_kn", "a_sink", "b_qa_norm", "b_kva_norm", "b_qn", "b_kn", "c_qn", "c_kn",
         "m_g_mem", "m_qn", "m_kn", "g_mlp")
WEIGHTS = ("g_mix", "w_in", "b_gate", "a_qn", "a_kn", "a_sink", "b_qa_norm", "b_kva_norm", "b_w_uq", "b_w_ukv", "b_qn",
           "b_kn", "c_qn", "c_kn", "m_g_mem", "m_w_kv", "m_qn", "m_kn", "w_branch", "w_out", "g_mlp", "w_up", "w_down")


def _rope_tables(positions, dim):
    inv = ROPE_THETA ** (-jnp.arange(0, dim, 2, dtype=F32) / dim)
    ang = positions.astype(F32)[:, None] * inv
    return jnp.cos(ang), jnp.sin(ang)


def _lane_mask(lo, hi):
    lane = np.arange(LANES)
    return jnp.asarray(((lane >= lo) & (lane < hi)).astype(np.float32)[None, :])


def _step(x, mem, positions, w, moments_m, moments_v, loss_target):
    s_len, d_model = x.shape
    n_layers = w["g_mix"].shape[0]
    lay, orig_names, order, n_in, nw = _in_layout(d_model)
    unit = lambda name: lay[name]["off"] // lay[name]["nw"]

    def ag_src(l, names):
        return [(w[n][l].astype(BF16), False) for n in names]

    def build(gathered):
        fl = {n: _unshard(gathered[n], BIG_AXIS[n] - 1) for n in BIG}
        uq = fl["b_w_uq"].reshape(B_Q_LORA, B_HEADS, B_NOPE + B_ROPE)
        ukv = fl["b_w_ukv"].reshape(B_KV_LORA, B_HEADS, B_NOPE + B_V)
        return dict(
            w_in=_permute_cols(fl["w_in"], lay, order),
            uq=jnp.pad(uq, ((0, 0), (0, 0), (0, LANES - B_NOPE - B_ROPE))).reshape(B_Q_LORA, B_HEADS * LANES),
            ukv=jnp.concatenate([
                jnp.pad(ukv[..., :B_NOPE], ((0, 0), (0, 0), (0, LANES - B_NOPE))).reshape(B_KV_LORA, B_HEADS * LANES),
                ukv[..., B_NOPE:].reshape(B_KV_LORA, B_HEADS * B_V)], axis=-1),
            m_w_kv=fl["m_w_kv"], w_branch=fl["w_branch"], w_out=fl["w_out"], w_up=fl["w_up"], w_down=fl["w_down"])

    def rs_src(name, g):
        return (_shard_major(g, BIG_AXIS[name] - 1).astype(BF16), True)

    p64 = _pattern([64, 64])
    p128 = _pattern([128])
    pq = _pattern([64, 32, 32])
    pkr = _pattern([32, 32, 32, 32])
    cos_h, sin_h = _rope_tables(positions, A_HD)
    cos_r, sin_r = _rope_tables(positions, B_ROPE)
    ones = lambda n: jnp.ones((s_len, n), F32)
    zeros = lambda n: jnp.zeros((s_len, n), F32)
    rope64 = (jnp.tile(cos_h, (1, 4)), jnp.tile(jnp.concatenate([-sin_h, sin_h], axis=1), (1, 2)),
              jnp.asarray((np.arange(LANES) % 64 < 32).astype(np.float32)[None, :]), 32)
    rope_q = (jnp.concatenate([ones(64), cos_r, cos_r, ones(32)], axis=1),
              jnp.concatenate([zeros(64), -sin_r, sin_r, zeros(32)], axis=1), _lane_mask(64, 80), 16)
    rope_k = (jnp.concatenate([cos_r, cos_r, ones(96)], axis=1),
              jnp.concatenate([-sin_r, sin_r, zeros(96)], axis=1), _lane_mask(0, 16), 16)

    scale_b = (B_NOPE + B_ROPE) ** -0.5
    scale_m = M_HD ** -0.5
    zpad = lambda n: jnp.zeros((n,), F32)

    def gains(l):
        return dict(
            a_q=jnp.tile(w["a_qn"][l], A_HEADS), a_k=jnp.tile(w["a_kn"][l], A_KV_HEADS),
            b_q=jnp.tile(jnp.concatenate([w["b_qn"][l], zpad(32)]), B_HEADS),
            b_k=jnp.tile(jnp.concatenate([w["b_kn"][l, :B_NOPE], zpad(64)]), B_HEADS),
            b_kr=jnp.concatenate([w["b_kn"][l, B_NOPE:], zpad(96)]),
            c_q=[jnp.tile(w["c_qn"][l, g], C_HEADS) for g in range(3)],
            c_k=[jnp.tile(w["c_kn"][l, g], C_HEADS) for g in range(3)],
            m_q=jnp.tile(w["m_qn"][l], M_HEADS), m_k=jnp.tile(w["m_kn"][l], M_HEADS),
            sink=jnp.repeat(w["a_sink"][l], A_HD).reshape(1, BRANCH_W))

    ag_hosts = dict(mm_in=("w_up", "w_down"), flash_b=("w_in",),
                    mm_up=("w_branch", "w_out", "m_w_kv", "b_w_uq", "b_w_ukv"))
    saved, weights = [], []
    cur = x
    gathered = dict(zip(BIG, _exchange_many(ag_src(0, BIG), name="ag_layer0")))
    for l in range(n_layers):
        gn = gains(l)
        sv = dict(x=cur)
        wl = build(gathered)
        weights.append(wl)
        gathered = {}
        carry = lambda host: ag_src(l + 1, ag_hosts[host]) if l + 1 < n_layers else []
        h = _rmsnorm_fwd((cur, d_model, 0), w["g_mix"][l], name="f_norm_mix")
        z, got = _mm(h, wl["w_in"], name="f_mm_in", xchg=carry("mm_in"))
        gathered.update(zip(ag_hosts["mm_in"], got))
        sv.update(h=h, z=z)
        qa = _headnorm_fwd((z, 512, unit("a_q")), gn["a_q"], p64, rope64, name="f_hn_aq")
        ka = _headnorm_fwd((z, 128, unit("a_k")), gn["a_k"], p64, rope64, name="f_hn_ak")
        va = (z, 128, unit("a_v"))
        o_a, lse_a = _band_fwd(qa, ka, va, dil=1, group=A_HEADS // A_KV_HEADS, maxdist=A_WINDOW - 1,
                               sink_row=gn["sink"], name="f_band_a")
        sv.update(qa=qa, ka=ka, o_a=o_a, lse_a=lse_a)
        cq = _rmsnorm_fwd((z, B_Q_LORA, unit("b_cq")), w["b_qa_norm"][l], name="f_norm_cq")
        q_up = _mm(cq, wl["uq"], name="f_mm_uq")
        ckv = _rmsnorm_fwd((z, B_KV_LORA, unit("b_ckv")), w["b_kva_norm"][l], name="f_norm_ckv")
        kv_up = _mm(ckv, wl["ukv"], name="f_mm_ukv")
        kp = _headnorm_fwd((z, LANES, unit("b_kr")), gn["b_kr"], pkr, rope_k, name="f_hn_kr")
        qb = _headnorm_fwd((q_up, B_HEADS * LANES, 0), gn["b_q"], pq, rope_q, name="f_hn_bq")
        kb = _headnorm_fwd((kv_up, B_HEADS * LANES, 0), gn["b_k"], p64, None, addend=kp, name="f_hn_bk")
        vb = (kv_up, HPG * B_V, B_HEADS * LANES // (HPG * B_V))
        (o_b, lse_b), got = _flash_fwd(qb, kb, vb, dv=B_V, causal=True, scale=scale_b, name="f_flash_b",
                                       xchg=carry("flash_b"))
        gathered.update(zip(ag_hosts["flash_b"], got))
        sv.update(cq=cq, q_up=q_up, ckv=ckv, kv_up=kv_up, qb=qb, kb=kb, o_b=o_b, lse_b=lse_b)
        qc, kc, o_cg, lse_cg = [], [], [], []
        for g, (win, dil) in enumerate(C_PATTERNS):
            qg = _headnorm_fwd((z, 512, unit("c%d" % (3 * g))), gn["c_q"][g], p64, rope64, name="f_hn_cq%d" % g)
            kg = _headnorm_fwd((z, 512, unit("c%d" % (3 * g + 1))), gn["c_k"][g], p64, rope64, name="f_hn_ck%d" % g)
            og, lg = _band_fwd(qg, kg, (z, 512, unit("c%d" % (3 * g + 2))), dil=dil, group=1, maxdist=win // dil,
                               name="f_band_c%d" % g)
            qc.append(qg), kc.append(kg), o_cg.append(og), lse_cg.append(lg)
        o_c = _combine_fwd(o_cg, lse_cg, name="f_combine")
        sv.update(qc=qc, kc=kc, o_cg=o_cg, lse_cg=lse_cg, o_c=o_c)
        memn = _rmsnorm_fwd((mem, d_model, 0), w["m_g_mem"][l], name="f_norm_mem")
        mkv = _mm(memn, wl["m_w_kv"], name="f_mm_mkv")
        mk = _headnorm_fwd((mkv, M_HEADS * M_HD, 0), gn["m_k"], p128, None, name="f_hn_mk")
        mq = _headnorm_fwd((z, M_HEADS * M_HD, unit("m_q")), gn["m_q"], p128, None, name="f_hn_mq")
        mv = (mkv, HPG * M_HD, M_HEADS * M_HD // (HPG * M_HD))
        o_m, lse_m = _flash_fwd(mq, mk, mv, dv=M_HD, causal=False, scale=scale_m, name="f_flash_m")
        sv.update(memn=memn, mkv=mkv, mk=mk, mq=mq, o_m=o_m, lse_m=lse_m)
        outs = (o_a, o_b, o_c, o_m)
        ys = [_mm(outs[n], wl["w_branch"][n], name="f_mm_branch") for n in range(N_BRANCH)]
        t = _merge_fwd((z, N_BRANCH * d_model, 0), w["b_gate"][l], ys, name="f_merge")
        x2 = _mm(t, wl["w_out"], mode="add", extra=cur, name="f_mm_out")
        h2 = _rmsnorm_fwd((x2, d_model, 0), w["g_mlp"][l], name="f_norm_mlp")
        (u, act), got = _mm(h2, wl["w_up"], mode="relu2", name="f_mm_up", xchg=carry("mm_up"))
        gathered.update(zip(ag_hosts["mm_up"], got))
        x3 = _mm(act, wl["w_down"], mode="add", extra=x2, name="f_mm_down")
        sv.update(ys=ys, t=t, x2=x2, h2=h2, u=u, act=act)
        saved.append(sv)
        cur = x3

    dx, loss_row = _loss_head(cur, loss_target, name="loss_head")
    loss = lax.psum(loss_row[0, 0], AXES)

    recvs = {n: [None] * n_layers for n in BIG}
    gsmall = {n: [None] * n_layers for n in SMALL}
    head_sum = lambda row, nh, hd: row.reshape(nh, hd).sum(axis=0)

    def item(name, layer, g):
        return (name, layer, rs_src(name, g))

    def srcs(items):
        return [it[2] for it in items]

    def store(items, got):
        for (name, layer, _), r in zip(items, got):
            recvs[name][layer] = r

    carried = []
    for l in reversed(range(n_layers)):
        sv = saved[l]
        wl = weights[l]
        gn = gains(l)
        z = sv["z"]
        g_wdown = _mm(sv["act"], dx, ta=True, name="b_mm_wdown")
        du = _mm(dx, wl["w_down"], tb=True, mode="drelu2", extra=sv["u"], out_dtype=BF16, name="b_mm_du")
        g_wup = _mm(sv["h2"], du, ta=True, name="b_mm_wup")
        dh2 = _mm(du, wl["w_up"], tb=True, name="b_mm_dh2")
        dx2, dg = _rmsnorm_bwd(dh2, (sv["x2"], d_model, 0), w["g_mlp"][l], dres=dx, name="b_norm_mlp")
        gsmall["g_mlp"][l] = dg[0]
        g_wout = _mm(sv["t"], dx2, ta=True, name="b_mm_wout")
        dt = _mm(dx2, wl["w_out"], tb=True, name="b_mm_dt")
        dys, dgate, dbg = _merge_bwd(dt, (z, N_BRANCH * d_model, 0), w["b_gate"][l], sv["ys"], name="b_merge")
        gsmall["b_gate"][l] = dbg[0]
        outs = (sv["o_a"], sv["o_b"], sv["o_c"], sv["o_m"])
        g_wbranch = jnp.stack([_mm(outs[n], dys[n], ta=True, name="b_mm_wbranch") for n in range(N_BRANCH)])
        do_a, do_b, do_c, do_m = [_mm(dys[n], wl["w_branch"][n], tb=True, name="b_mm_dbranch")
                                  for n in range(N_BRANCH)]
        mlp_items = [item("w_down", l, g_wdown), item("w_up", l, g_wup), item("w_out", l, g_wout),
                     item("w_branch", l, g_wbranch)]
        dparts = {"gate": dgate}
        va = (z, 128, unit("a_v"))
        kw_a = dict(dil=1, group=A_HEADS // A_KV_HEADS, maxdist=A_WINDOW - 1)
        dl_a, dsink = _delta(do_a, sv["o_a"], p64, lse=sv["lse_a"], sink_row=gn["sink"], name="b_delta_a")
        gsmall["a_sink"][l] = dsink[0, ::A_HD]
        dqa = _band_dq(sv["qa"], sv["ka"], va, do_a, sv["lse_a"], dl_a, name="b_band_dq_a", **kw_a)
        dka, dva = _band_dkv(sv["qa"], sv["ka"], va, do_a, sv["lse_a"], dl_a, name="b_band_dkv_a", **kw_a)
        dparts["a_q"], dg = _headnorm_bwd(dqa, (z, 512, unit("a_q")), gn["a_q"], p64, rope64, name="b_hn_aq")
        gsmall["a_qn"][l] = head_sum(dg, A_HEADS, A_HD)
        dparts["a_k"], dg = _headnorm_bwd(dka, (z, 128, unit("a_k")), gn["a_k"], p64, rope64, name="b_hn_ak")
        gsmall["a_kn"][l] = head_sum(dg, A_KV_HEADS, A_HD)
        dparts["a_v"] = dva
        vb = (sv["kv_up"], HPG * B_V, B_HEADS * LANES // (HPG * B_V))
        dl_b, _ = _delta(do_b, sv["o_b"], p64, name="b_delta_b")
        kw_b = dict(dv=B_V, causal=True, scale=scale_b)
        dqb, got = _flash_dq(sv["qb"], sv["kb"], vb, do_b, sv["lse_b"], dl_b, name="b_flash_dq_b",
                             xchg=srcs(carried), **kw_b)
        store(carried, got)
        (dkb, dvb), got = _flash_dkv(sv["qb"], sv["kb"], vb, do_b, sv["lse_b"], dl_b, name="b_flash_dkv_b",
                                     xchg=srcs(mlp_items), **kw_b)
        store(mlp_items, got)
        d_qup, dgq = _headnorm_bwd(dqb, (sv["q_up"], B_HEADS * LANES, 0), gn["b_q"], pq, rope_q, name="b_hn_bq")
        d_kn, dgk = _headnorm_bwd(dkb, (sv["kv_up"], B_HEADS * LANES, 0), gn["b_k"], p64, None, name="b_hn_bk")
        dkp = _kp_reduce(dkb, name="b_kp_reduce")
        dparts["b_kr"], dgkr = _headnorm_bwd(dkp, (z, LANES, unit("b_kr")), gn["b_kr"], pkr, rope_k, name="b_hn_kr")
        gsmall["b_qn"][l] = head_sum(dgq, B_HEADS, LANES)[:B_NOPE + B_ROPE]
        gsmall["b_kn"][l] = jnp.concatenate([head_sum(dgk, B_HEADS, LANES)[:B_NOPE], dgkr[0, :B_ROPE]])
        d_kvup = jnp.concatenate([d_kn, dvb], axis=1)
        g_ukv = _mm(sv["ckv"], d_kvup, ta=True, name="b_mm_wukv")
        dckv = _mm(d_kvup, wl["ukv"], tb=True, name="b_mm_dckv")
        dparts["b_ckv"], dg = _rmsnorm_bwd(dckv, (z, B_KV_LORA, unit("b_ckv")), w["b_kva_norm"][l], out_dtype=BF16,
                                          name="b_norm_ckv")
        gsmall["b_kva_norm"][l] = dg[0]
        g_uq = _mm(sv["cq"], d_qup, ta=True, name="b_mm_wuq")
        dcq = _mm(d_qup, wl["uq"], tb=True, name="b_mm_dcq")
        dparts["b_cq"], dg = _rmsnorm_bwd(dcq, (z, B_Q_LORA, unit("b_cq")), w["b_qa_norm"][l], out_dtype=BF16,
                                         name="b_norm_cq")
        gsmall["b_qa_norm"][l] = dg[0]
        g_wuq = g_uq.reshape(B_Q_LORA, B_HEADS, LANES)[:, :, :B_NOPE + B_ROPE].reshape(B_Q_LORA, -1)
        g_wukv = jnp.concatenate(
            [g_ukv[:, :B_HEADS * LANES].reshape(B_KV_LORA, B_HEADS, LANES)[:, :, :B_NOPE],
             g_ukv[:, B_HEADS * LANES:].reshape(B_KV_LORA, B_HEADS, B_V)], axis=-1).reshape(B_KV_LORA, -1)
        do_cg, dl_cg = _combine_bwd(do_c, sv["o_c"], sv["lse_cg"], p64, name="b_combine")
        dcq_n, dck_n = [], []
        for g, (win, dil) in enumerate(C_PATTERNS):
            vc = (z, 512, unit("c%d" % (3 * g + 2)))
            kw_c = dict(dil=dil, group=1, maxdist=win // dil)
            args = (sv["qc"][g], sv["kc"][g], vc, do_cg[g], sv["lse_cg"][g], dl_cg[g])
            dq = _band_dq(*args, name="b_band_dq_c%d" % g, **kw_c)
            dk, dv_ = _band_dkv(*args, name="b_band_dkv_c%d" % g, **kw_c)
            dparts["c%d" % (3 * g)], dg = _headnorm_bwd(dq, (z, 512, unit("c%d" % (3 * g))), gn["c_q"][g], p64, rope64,
                                                        name="b_hn_cq%d" % g)
            dcq_n.append(head_sum(dg, C_HEADS, C_HD))
            dparts["c%d" % (3 * g + 1)], dg = _headnorm_bwd(dk, (z, 512, unit("c%d" % (3 * g + 1))), gn["c_k"][g], p64,
                                                            rope64, name="b_hn_ck%d" % g)
            dck_n.append(head_sum(dg, C_HEADS, C_HD))
            dparts["c%d" % (3 * g + 2)] = dv_
        gsmall["c_qn"][l] = jnp.stack(dcq_n)
        gsmall["c_kn"][l] = jnp.stack(dck_n)
        mv = (sv["mkv"], HPG * M_HD, M_HEADS * M_HD // (HPG * M_HD))
        dl_m, _ = _delta(do_m, sv["o_m"], p128, name="b_delta_m")
        kw_m = dict(dv=M_HD, causal=False, scale=scale_m)
        dmq = _flash_dq(sv["mq"], sv["mk"], mv, do_m, sv["lse_m"], dl_m, name="b_flash_dq_m", **kw_m)
        dmk, dmv = _flash_dkv(sv["mq"], sv["mk"], mv, do_m, sv["lse_m"], dl_m, name="b_flash_dkv_m", **kw_m)
        dparts["m_q"], dg = _headnorm_bwd(dmq, (z, M_HEADS * M_HD, unit("m_q")), gn["m_q"], p128, None, name="b_hn_mq")
        gsmall["m_qn"][l] = head_sum(dg, M_HEADS, M_HD)
        d_mk, dg = _headnorm_bwd(dmk, (sv["mkv"], M_HEADS * M_HD, 0), gn["m_k"], p128, None, name="b_hn_mk")
        gsmall["m_kn"][l] = head_sum(dg, M_HEADS, M_HD)
        d_mkv = jnp.concatenate([d_mk, dmv], axis=1)
        g_wmkv = _mm(sv["memn"], d_mkv, ta=True, name="b_mm_wmkv")
        dmemn = _mm(d_mkv, wl["m_w_kv"], tb=True, name="b_mm_dmemn")
        _, dg = _rmsnorm_bwd(dmemn, (mem, d_model, 0), w["m_g_mem"][l], name="b_norm_mem")
        gsmall["m_g_mem"][l] = dg[0]
        dz = jnp.concatenate([dparts[n] for n in order], axis=1)
        bm_items = [item("b_w_ukv", l, g_wukv), item("b_w_uq", l, g_wuq), item("m_w_kv", l, g_wmkv)]
        g_in, got = _mm(sv["h"], dz, ta=True, name="b_mm_win", xchg=srcs(bm_items))
        store(bm_items, got)
        carried = [item("w_in", l, _unpermute_cols(g_in, lay, orig_names))]
        last_items = carried if l == 0 else []
        dh, got = _mm(dz, wl["w_in"], tb=True, name="b_mm_dh", xchg=srcs(last_items))
        store(last_items, got)
        dx, dg = _rmsnorm_bwd(dh, (sv["x"], d_model, 0), w["g_mix"][l], dres=dx2, name="b_norm_mix")
        gsmall["g_mix"][l] = dg[0]

    out_g, out_d, out_m, out_v = {}, {}, {}, {}
    for name in BIG:
        shard = w[name].shape
        two = lambda a: a.reshape(-1, shard[-1])
        pieces = [r.reshape(N_DEV, -1, shard[-1]) for r in recvs[name]]
        g, dlt, mn, vn = _adam(pieces, two(w[name]), two(moments_m[name]), two(moments_v[name]), name="adam_" + name)
        out_g[name], out_d[name], out_m[name], out_v[name] = (a.reshape(shard) for a in (g, dlt, mn, vn))
    sizes = [int(np.prod(w[n].shape)) for n in SMALL]
    total = sum(sizes)
    padded = -(-total // (8 * LANES)) * (8 * LANES)
    pack = lambda d: jnp.pad(jnp.concatenate([jnp.reshape(d[n], (-1,)) for n in SMALL]), (0, padded - total)).reshape(-1, LANES)
    gpack = pack({n: jnp.stack(gsmall[n]) for n in SMALL})
    recv = _exchange_many([(gpack, False)], name="ag_small_grads")[0]
    packed = _adam([recv], pack(w), pack(moments_m), pack(moments_v), name="adam_small")
    offs = np.cumsum([0] + sizes)
    for a, out in zip(packed, (out_g, out_d, out_m, out_v)):
        flat = a.reshape(-1)
        for n, o, sz in zip(SMALL, offs[:-1], sizes):
            out[n] = flat[o:o + sz].reshape(w[n].shape)
    return loss, dx, out_g, out_d, out_m, out_v


def kernel(x, mem, positions, g_mix, w_in, b_gate, a_qn, a_kn, a_sink, b_qa_norm, b_kva_norm, b_w_uq, b_w_ukv, b_qn, b_kn, c_qn, c_kn, m_g_mem, m_w_kv, m_qn, m_kn, w_branch, w_out, g_mlp, w_up, w_down, loss_target, m_g_mix, m_w_in, m_b_gate, m_a_qn, m_a_kn, m_a_sink, m_b_qa_norm, m_b_kva_norm, m_b_w_uq, m_b_w_ukv, m_b_qn, m_b_kn, m_c_qn, m_c_kn, m_m_g_mem, m_m_w_kv, m_m_qn, m_m_kn, m_w_branch, m_w_out, m_g_mlp, m_w_up, m_w_down, v_g_mix, v_w_in, v_b_gate, v_a_qn, v_a_kn, v_a_sink, v_b_qa_norm, v_b_kva_norm, v_b_w_uq, v_b_w_ukv, v_b_qn, v_b_kn, v_c_qn, v_c_kn, v_m_g_mem, v_m_w_kv, v_m_qn, v_m_kn, v_w_branch, v_w_out, v_g_mlp, v_w_up, v_w_down):
    w = dict(g_mix=g_mix, w_in=w_in, b_gate=b_gate, a_qn=a_qn, a_kn=a_kn, a_sink=a_sink, b_qa_norm=b_qa_norm,
             b_kva_norm=b_kva_norm, b_w_uq=b_w_uq, b_w_ukv=b_w_ukv, b_qn=b_qn, b_kn=b_kn, c_qn=c_qn, c_kn=c_kn,
             m_g_mem=m_g_mem, m_w_kv=m_w_kv, m_qn=m_qn, m_kn=m_kn, w_branch=w_branch, w_out=w_out, g_mlp=g_mlp,
             w_up=w_up, w_down=w_down)
    mm_ = dict(g_mix=m_g_mix, w_in=m_w_in, b_gate=m_b_gate, a_qn=m_a_qn, a_kn=m_a_kn, a_sink=m_a_sink,
               b_qa_norm=m_b_qa_norm, b_kva_norm=m_b_kva_norm, b_w_uq=m_b_w_uq, b_w_ukv=m_b_w_ukv, b_qn=m_b_qn,
               b_kn=m_b_kn, c_qn=m_c_qn, c_kn=m_c_kn, m_g_mem=m_m_g_mem, m_w_kv=m_m_w_kv, m_qn=m_m_qn, m_kn=m_m_kn,
               w_branch=m_w_branch, w_out=m_w_out, g_mlp=m_g_mlp, w_up=m_w_up, w_down=m_w_down)
    vv = dict(g_mix=v_g_mix, w_in=v_w_in, b_gate=v_b_gate, a_qn=v_a_qn, a_kn=v_a_kn, a_sink=v_a_sink,
              b_qa_norm=v_b_qa_norm, b_kva_norm=v_b_kva_norm, b_w_uq=v_b_w_uq, b_w_ukv=v_b_w_ukv, b_qn=v_b_qn,
              b_kn=v_b_kn, c_qn=v_c_qn, c_kn=v_c_kn, m_g_mem=v_m_g_mem, m_w_kv=v_m_w_kv, m_qn=v_m_qn, m_kn=v_m_kn,
              w_branch=v_w_branch, w_out=v_w_out, g_mlp=v_g_mlp, w_up=v_w_up, w_down=v_w_down)
    loss, dx, g, dlt, mn, vn = _step(x[0], mem[0], positions[0], w, mm_, vv, loss_target[0])
    return (loss, dx[None], *[g[n] for n in WEIGHTS], *[dlt[n] for n in WEIGHTS], *[mn[n] for n in WEIGHTS],
            *[vn[n] for n in WEIGHTS])
```

```python
import functools

import numpy as np
import jax
import jax.numpy as jnp
from jax import lax
from jax.experimental import pallas as pl
from jax.experimental.pallas import tpu as pltpu

F32 = jnp.float32
BF16 = jnp.bfloat16

LANES = 128
N_DEV = 8
AXES = ("x", "y", "c")
VMEM_LIMIT = 48 * 1024 * 1024

EPS = 1e-6
NEG = -1e30
ROPE_THETA = 10000.0
BLOCK = 128

A_HEADS, A_KV_HEADS, A_HD, A_WINDOW = 8, 2, 64, 128
B_HEADS, B_Q_LORA, B_KV_LORA, B_NOPE, B_ROPE, B_V = 8, 384, 256, 64, 32, 64
C_PATTERNS = ((128, 1), (512, 4), (2048, 16))
C_HEADS, C_HD = 8, 64
M_HEADS, M_HD = 4, 128
BRANCH_W = 512
N_BRANCH = 4

ADAM_LR, ADAM_B1, ADAM_B2, ADAM_EPS, ADAM_WD, ADAM_STEP = 0.001, 0.9, 0.999, 1e-08, 0.01, 10

NT_DIMS = (((1,), (1,)), ((), ()))
TN_DIMS = (((0,), (0,)), ((), ()))
NN_DIMS = (((1,), (0,)), ((), ()))


def _pick(n, cands):
    for c in cands:
        if n % c == 0:
            return c
    return n


def _cparams(sem):
    return pltpu.CompilerParams(dimension_semantics=sem, vmem_limit_bytes=VMEM_LIMIT)


def _dot(a, b, dims=NN_DIMS):
    return lax.dot_general(a.astype(BF16), b.astype(BF16), dims, preferred_element_type=F32)


def _in_layout(d_model):
    orig = ([("a_q", 512), ("a_k", 128), ("a_v", 128), ("b_cq", B_Q_LORA), ("b_ckv", B_KV_LORA), ("b_kr", B_ROPE)]
            + [("c%d" % i, 512) for i in range(9)] + [("m_q", 512), ("gate", N_BRANCH * d_model)])
    order = ["gate", "a_q"] + ["c%d" % i for i in range(9)] + ["m_q", "b_ckv", "b_cq", "a_k", "a_v", "b_kr"]
    o_off, off = {}, 0
    for name, w in orig:
        o_off[name] = (off, w)
        off += w
    lay, noff = {}, 0
    for name in order:
        oo, w = o_off[name]
        nw = LANES if name == "b_kr" else w
        assert noff % nw == 0, (name, noff, nw)
        lay[name] = dict(orig=oo, w=w, off=noff, nw=nw)
        noff += nw
    assert noff % 512 == 0
    return lay, [n for n, _ in orig], order, off, noff


def _permute_cols(w, lay, order):
    cols = []
    for name in order:
        e = lay[name]
        p = w[..., e["orig"]:e["orig"] + e["w"]]
        if e["nw"] != e["w"]:
            p = jnp.pad(p, [(0, 0)] * (w.ndim - 1) + [(0, e["nw"] - e["w"])])
        cols.append(p)
    return jnp.concatenate(cols, axis=-1)


def _unpermute_cols(w, lay, orig_names):
    return jnp.concatenate([w[..., lay[n]["off"]:lay[n]["off"] + lay[n]["w"]] for n in orig_names], axis=-1)


def _pattern(groups):
    ids = np.repeat(np.arange(len(groups)), groups)
    assert ids.shape[0] == LANES
    gmat = (ids[:, None] == ids[None, :]).astype(np.float32)
    invn = (1.0 / np.asarray(groups, np.float32))[ids][None, :]
    return jnp.asarray(gmat, BF16), jnp.asarray(invn, F32)


def _gsum(v, gmat):
    hi = v.astype(BF16)
    r1 = v - hi.astype(F32)
    mid = r1.astype(BF16)
    lo = (r1 - mid.astype(F32)).astype(BF16)
    dot = lambda p: jnp.dot(p, gmat, preferred_element_type=F32)
    return dot(hi) + dot(mid) + dot(lo)


def _swap_half(v, lomask, half):
    return jnp.where(lomask > 0.5, pltpu.roll(v, LANES - half, 1), pltpu.roll(v, half, 1))


def _rows_spec(tr, width, off=0):
    return pl.BlockSpec((tr, width), lambda i: (i, off))


def _full_spec(shape):
    return pl.BlockSpec(shape, lambda i: (0,) * len(shape))


def _mm(a, b, *, ta=False, tb=False, out_dtype=F32, mode=None, extra=None, name, xchg=None):
    if ta:
        kd, m = a.shape
    else:
        m, kd = a.shape
    if tb:
        n, kb = b.shape
    else:
        kb, n = b.shape
    assert kd == kb, (a.shape, b.shape, ta, tb)
    tm = _pick(m, (1024, 512, 256, 128))
    tn = _pick(n, (1024, 1536, 512, 384, 256, 128))
    tk = _pick(kd, (512, 384, 256, 128))
    nk = kd // tk
    a_spec = (pl.BlockSpec((tk, tm), lambda i, j, k: (k, i)) if ta else pl.BlockSpec((tm, tk), lambda i, j, k: (i, k)))
    b_spec = (pl.BlockSpec((tn, tk), lambda i, j, k: (j, k)) if tb else pl.BlockSpec((tk, tn), lambda i, j, k: (k, j)))
    o_spec = pl.BlockSpec((tm, tn), lambda i, j, k: (i, j))
    dims = (((0 if ta else 1,), (1 if tb else 0,)), ((), ()))
    has_extra = mode in ("add", "drelu2")

    def body(*refs):
        a_ref, b_ref = refs[0], refs[1]
        e_ref = refs[2] if has_extra else None
        outs = refs[2 + has_extra:-1]
        acc = refs[-1]
        k = pl.program_id(2)

        @pl.when(k == 0)
        def _():
            acc[...] = jnp.zeros_like(acc)

        acc[...] += _dot(a_ref[...], b_ref[...], dims)

        @pl.when(k == nk - 1)
        def _():
            r = acc[...]
            if mode == "add":
                outs[0][...] = (e_ref[...] + r).astype(out_dtype)
            elif mode == "relu2":
                outs[0][...] = r
                outs[1][...] = jnp.square(jnp.maximum(r, 0.0)).astype(BF16)
            elif mode == "drelu2":
                outs[0][...] = (r * (2.0 * jnp.maximum(e_ref[...], 0.0))).astype(out_dtype)
            else:
                outs[0][...] = r.astype(out_dtype)

    in_specs = [a_spec, b_spec] + ([o_spec] if has_extra else [])
    args = [a, b] + ([extra] if has_extra else [])
    if mode == "relu2":
        out_shape = (jax.ShapeDtypeStruct((m, n), F32), jax.ShapeDtypeStruct((m, n), BF16))
        out_specs = (o_spec, o_spec)
    else:
        out_shape = jax.ShapeDtypeStruct((m, n), out_dtype)
        out_specs = o_spec
    return _pcall(
        body, grid=(m // tm, n // tn, nk), in_specs=in_specs, out_specs=out_specs, out_shape=out_shape,
        scratch_shapes=[pltpu.VMEM((tm, tn), F32)], sem=("parallel", "parallel", "arbitrary"), name=name, args=args,
        xchg=xchg)


def _rmsnorm_fwd(xv, gain, *, name, tr=512):
    x, width, off = xv
    rows = x.shape[0]
    tr = min(tr, rows)

    def body(x_ref, g_ref, o_ref):
        xf = x_ref[...]
        rstd = lax.rsqrt(jnp.mean(xf * xf, axis=-1, keepdims=True) + EPS)
        o_ref[...] = (xf * rstd * g_ref[...]).astype(BF16)

    return pl.pallas_call(
        body, grid=(rows // tr,), in_specs=[_rows_spec(tr, width, off), _full_spec((1, width))],
        out_specs=_rows_spec(tr, width), out_shape=jax.ShapeDtypeStruct((rows, width), BF16),
        compiler_params=_cparams(("parallel",)), name=name)(x, gain.reshape(1, width))


def _rmsnorm_bwd(dh, xv, gain, *, dres=None, out_dtype=F32, name, tr=512):
    x, width, off = xv
    rows = x.shape[0]
    tr = min(tr, rows)
    has_res = dres is not None

    def body(*refs):
        dh_ref, x_ref, g_ref = refs[:3]
        r_ref = refs[3] if has_res else None
        dx_ref, dg_ref = refs[3 + has_res:]
        i = pl.program_id(0)
        xf = x_ref[...]
        rstd = lax.rsqrt(jnp.mean(xf * xf, axis=-1, keepdims=True) + EPS)
        xh = xf * rstd
        d = dh_ref[...]
        dxh = d * g_ref[...]
        dx = rstd * (dxh - xh * jnp.mean(dxh * xh, axis=-1, keepdims=True))
        if has_res:
            dx = dx + r_ref[...]
        dx_ref[...] = dx.astype(out_dtype)

        @pl.when(i == 0)
        def _():
            dg_ref[...] = jnp.zeros_like(dg_ref)

        dg_ref[...] += jnp.sum(d * xh, axis=0, keepdims=True)

    in_specs = [_rows_spec(tr, width), _rows_spec(tr, width, off), _full_spec((1, width))]
    args = [dh, x, gain.reshape(1, width)]
    if has_res:
        in_specs.append(_rows_spec(tr, width))
        args.append(dres)
    return pl.pallas_call(
        body, grid=(rows // tr,), in_specs=in_specs,
        out_specs=(_rows_spec(tr, width), _full_spec((1, width))),
        out_shape=(jax.ShapeDtypeStruct((rows, width), out_dtype), jax.ShapeDtypeStruct((1, width), F32)),
        compiler_params=_cparams(("arbitrary",)), name=name)(*args)


def _headnorm_fwd(xv, gain_row, pat, rope, *, addend=None, name, tr=512):
    x, width, off = xv
    rows = x.shape[0]
    tr = min(tr, rows)
    gmat, invn = pat
    nblk = width // LANES
    has_rope = rope is not None
    has_add = addend is not None
    half = rope[3] if has_rope else 0

    def body(*refs):
        x_ref, g_ref, gm_ref, in_ref = refs[:4]
        pos = 4
        if has_rope:
            c_ref, s_ref, lo_ref = refs[pos:pos + 3]
            pos += 3
        if has_add:
            a_ref = refs[pos]
            pos += 1
        o_ref = refs[pos]
        gm = gm_ref[...]
        inv = in_ref[...]
        if has_add:
            add = pltpu.roll(a_ref[...].astype(F32), 64, 1)
        for j in range(nblk):
            sl = slice(j * LANES, (j + 1) * LANES)
            xb = x_ref[:, sl]
            rstd = lax.rsqrt(_gsum(xb * xb, gm) * inv + EPS)
            y = xb * rstd * g_ref[:, sl]
            if has_rope:
                y = y * c_ref[...] + _swap_half(y, lo_ref[...], half) * s_ref[...]
            if has_add:
                y = y + add
            o_ref[:, sl] = y.astype(BF16)

    in_specs = [_rows_spec(tr, width, off), _full_spec((1, width)), _full_spec((LANES, LANES)), _full_spec((1, LANES))]
    args = [x, gain_row.reshape(1, width), gmat, invn]
    if has_rope:
        in_specs += [_rows_spec(tr, LANES), _rows_spec(tr, LANES), _full_spec((1, LANES))]
        args += [rope[0], rope[1], rope[2]]
    if has_add:
        in_specs.append(_rows_spec(tr, LANES))
        args.append(addend)
    return pl.pallas_call(
        body, grid=(rows // tr,), in_specs=in_specs, out_specs=_rows_spec(tr, width),
        out_shape=jax.ShapeDtypeStruct((rows, width), BF16),
        compiler_params=_cparams(("parallel",)), name=name)(*args)


def _headnorm_bwd(dy, xv, gain_row, pat, rope, *, name, tr=512):
    x, width, off = xv
    rows = x.shape[0]
    tr = min(tr, rows)
    gmat, invn = pat
    nblk = width // LANES
    has_rope = rope is not None
    half = rope[3] if has_rope else 0

    def body(*refs):
        dy_ref, x_ref, g_ref, gm_ref, in_ref = refs[:5]
        pos = 5
        if has_rope:
            c_ref, s_ref, lo_ref = refs[pos:pos + 3]
            pos += 3
        dx_ref, dg_ref = refs[pos:]
        i = pl.program_id(0)
        gm = gm_ref[...]
        inv = in_ref[...]

        @pl.when(i == 0)
        def _():
            dg_ref[...] = jnp.zeros_like(dg_ref)

        for j in range(nblk):
            sl = slice(j * LANES, (j + 1) * LANES)
            d = dy_ref[:, sl]
            if has_rope:
                d = d * c_ref[...] + _swap_half(d * s_ref[...], lo_ref[...], half)
            xb = x_ref[:, sl]
            rstd = lax.rsqrt(_gsum(xb * xb, gm) * inv + EPS)
            xh = xb * rstd
            dg_ref[:, sl] += jnp.sum(d * xh, axis=0, keepdims=True)
            dxh = d * g_ref[:, sl]
            dx = rstd * (dxh - xh * (_gsum(dxh * xh, gm) * inv))
            dx_ref[:, sl] = dx.astype(BF16)

    in_specs = [_rows_spec(tr, width), _rows_spec(tr, width, off), _full_spec((1, width)),
                _full_spec((LANES, LANES)), _full_spec((1, LANES))]
    args = [dy, x, gain_row.reshape(1, width), gmat, invn]
    if has_rope:
        in_specs += [_rows_spec(tr, LANES), _rows_spec(tr, LANES), _full_spec((1, LANES))]
        args += [rope[0], rope[1], rope[2]]
    return pl.pallas_call(
        body, grid=(rows // tr,), in_specs=in_specs,
        out_specs=(_rows_spec(tr, width), _full_spec((1, width))),
        out_shape=(jax.ShapeDtypeStruct((rows, width), BF16), jax.ShapeDtypeStruct((1, width), F32)),
        compiler_params=_cparams(("arbitrary",)), name=name)(*args)


def _merge_fwd(zv, b_gate, ys, *, name, tr=256):
    z, gw, off = zv
    rows, d = ys[0].shape
    tr = min(tr, rows)

    def body(z_ref, b_ref, y0, y1, y2, y3, t_ref):
        acc = None
        for n, y_ref in enumerate((y0, y1, y2, y3)):
            sl = slice(n * d, (n + 1) * d)
            term = jax.nn.sigmoid(z_ref[:, sl] + b_ref[:, sl]) * y_ref[...]
            acc = term if acc is None else acc + term
        t_ref[...] = acc.astype(BF16)

    return pl.pallas_call(
        body, grid=(rows // tr,),
        in_specs=[_rows_spec(tr, gw, off), _full_spec((1, gw))] + [_rows_spec(tr, d)] * 4,
        out_specs=_rows_spec(tr, d), out_shape=jax.ShapeDtypeStruct((rows, d), BF16),
        compiler_params=_cparams(("parallel",)), name=name)(z, b_gate.reshape(1, gw), *ys)


def _merge_bwd(dt, zv, b_gate, ys, *, name, tr=256):
    z, gw, off = zv
    rows, d = dt.shape
    tr = min(tr, rows)

    def body(dt_ref, z_ref, b_ref, y0, y1, y2, y3, d0, d1, d2, d3, dg_ref, db_ref):
        i = pl.program_id(0)

        @pl.when(i == 0)
        def _():
            db_ref[...] = jnp.zeros_like(db_ref)

        dtv = dt_ref[...]
        for n, (y_ref, d_ref) in enumerate(((y0, d0), (y1, d1), (y2, d2), (y3, d3))):
            sl = slice(n * d, (n + 1) * d)
            gate = jax.nn.sigmoid(z_ref[:, sl] + b_ref[:, sl])
            d_ref[...] = (dtv * gate).astype(BF16)
            dpre = dtv * y_ref[...] * (gate * (1.0 - gate))
            dg_ref[:, sl] = dpre.astype(BF16)
            db_ref[:, sl] += jnp.sum(dpre, axis=0, keepdims=True)

    outs = pl.pallas_call(
        body, grid=(rows // tr,),
        in_specs=[_rows_spec(tr, d), _rows_spec(tr, gw, off), _full_spec((1, gw))] + [_rows_spec(tr, d)] * 4,
        out_specs=tuple([_rows_spec(tr, d)] * 4 + [_rows_spec(tr, gw), _full_spec((1, gw))]),
        out_shape=tuple([jax.ShapeDtypeStruct((rows, d), BF16)] * 4
                        + [jax.ShapeDtypeStruct((rows, gw), BF16), jax.ShapeDtypeStruct((1, gw), F32)]),
        compiler_params=_cparams(("arbitrary",)), name=name)(dt, z, b_gate.reshape(1, gw), *ys)
    return outs[:4], outs[4], outs[5]


def _combine_fwd(os_, lses, *, name, tr=512):
    rows, width = os_[0].shape

    def body(o0, o1, o2, l0, l1, l2, out_ref):
        la, lb, lc = l0[...], l1[...], l2[...]
        mx = jnp.maximum(jnp.maximum(la, lb), lc)
        ea, eb, ec = jnp.exp(la - mx), jnp.exp(lb - mx), jnp.exp(lc - mx)
        inv = 1.0 / (ea + eb + ec)
        out_ref[...] = (ea * o0[...] + eb * o1[...] + ec * o2[...]) * inv

    return pl.pallas_call(
        body, grid=(rows // tr,), in_specs=[_rows_spec(tr, width)] * 6, out_specs=_rows_spec(tr, width),
        out_shape=jax.ShapeDtypeStruct((rows, width), F32),
        compiler_params=_cparams(("parallel",)), name=name)(*os_, *lses)


def _combine_bwd(do, o, lses, pat, *, name, tr=512):
    rows, width = do.shape
    gmat, _ = pat
    nblk = width // LANES

    def body(do_ref, o_ref, l0, l1, l2, gm_ref, d0, d1, d2, e0, e1, e2):
        gm = gm_ref[...]
        for j in range(nblk):
            sl = slice(j * LANES, (j + 1) * LANES)
            la, lb, lc = l0[:, sl], l1[:, sl], l2[:, sl]
            mx = jnp.maximum(jnp.maximum(la, lb), lc)
            ea, eb, ec = jnp.exp(la - mx), jnp.exp(lb - mx), jnp.exp(lc - mx)
            inv = 1.0 / (ea + eb + ec)
            dov = do_ref[:, sl]
            tot = _gsum(dov * o_ref[:, sl], gm)
            for e, d_ref, e_ref in ((ea, d0, e0), (eb, d1, e1), (ec, d2, e2)):
                w = e * inv
                d_ref[:, sl] = w * dov
                e_ref[:, sl] = w * tot

    outs = pl.pallas_call(
        body, grid=(rows // tr,),
        in_specs=[_rows_spec(tr, width)] * 5 + [_full_spec((LANES, LANES))],
        out_specs=tuple([_rows_spec(tr, width)] * 6),
        out_shape=tuple([jax.ShapeDtypeStruct((rows, width), F32)] * 6),
        compiler_params=_cparams(("parallel",)), name=name)(do, o, *lses, gmat)
    return outs[:3], outs[3:]


def _delta(do, o, pat, *, lse=None, sink_row=None, name, tr=512):
    rows, width = do.shape
    gmat, _ = pat
    nblk = width // LANES
    has_sink = sink_row is not None

    def body(*refs):
        do_ref, o_ref, gm_ref = refs[:3]
        if has_sink:
            l_ref, s_ref, dl_ref, ds_ref = refs[3:]
        else:
            dl_ref = refs[3]
        gm = gm_ref[...]
        if has_sink:
            @pl.when(pl.program_id(0) == 0)
            def _():
                ds_ref[...] = jnp.zeros_like(ds_ref)
        for j in range(nblk):
            sl = slice(j * LANES, (j + 1) * LANES)
            dl = _gsum(do_ref[:, sl] * o_ref[:, sl], gm)
            dl_ref[:, sl] = dl
            if has_sink:
                ds_ref[:, sl] += jnp.sum(-jnp.exp(s_ref[:, sl] - l_ref[:, sl]) * dl, axis=0, keepdims=True)

    in_specs = [_rows_spec(tr, width), _rows_spec(tr, width), _full_spec((LANES, LANES))]
    args = [do, o, gmat]
    out_specs = [_rows_spec(tr, width)]
    out_shape = [jax.ShapeDtypeStruct((rows, width), F32)]
    if has_sink:
        in_specs += [_rows_spec(tr, width), _full_spec((1, width))]
        args += [lse, sink_row]
        out_specs.append(_full_spec((1, width)))
        out_shape.append(jax.ShapeDtypeStruct((1, width), F32))
    outs = pl.pallas_call(
        body, grid=(rows // tr,), in_specs=in_specs, out_specs=tuple(out_specs), out_shape=tuple(out_shape),
        compiler_params=_cparams(("arbitrary",)), name=name)(*args)
    return outs if has_sink else (outs[0], None)


def _kp_reduce(dk, *, name, tr=512):
    rows, width = dk.shape
    nblk = width // LANES

    def body(dk_ref, o_ref):
        acc = dk_ref[:, 0:LANES]
        for j in range(1, nblk):
            acc = acc + dk_ref[:, j * LANES:(j + 1) * LANES]
        lane = lax.broadcasted_iota(jnp.int32, acc.shape, 1)
        o_ref[...] = jnp.where(lane < B_ROPE, pltpu.roll(acc, 64, 1), 0.0)

    return pl.pallas_call(
        body, grid=(rows // tr,), in_specs=[_rows_spec(tr, width)], out_specs=_rows_spec(tr, LANES),
        out_shape=jax.ShapeDtypeStruct((rows, LANES), F32),
        compiler_params=_cparams(("parallel",)), name=name)(dk)


def _loss_head(y, target, *, name, tr=512):
    rows, d = y.shape

    def body(y_ref, t_ref, dy_ref, l_ref):
        @pl.when(pl.program_id(0) == 0)
        def _():
            l_ref[...] = jnp.zeros_like(l_ref)

        diff = y_ref[...] - t_ref[...]
        dy_ref[...] = diff * (1.0 / d)
        part = 0.5 * jnp.sum(jnp.mean(diff * diff, axis=-1, keepdims=True), axis=0, keepdims=True)
        l_ref[...] += jnp.broadcast_to(part, l_ref.shape)

    return pl.pallas_call(
        body, grid=(rows // tr,), in_specs=[_rows_spec(tr, d)] * 2,
        out_specs=(_rows_spec(tr, d), _full_spec((1, LANES))),
        out_shape=(jax.ShapeDtypeStruct((rows, d), F32), jax.ShapeDtypeStruct((1, LANES), F32)),
        compiler_params=_cparams(("arbitrary",)), name=name)(y, target)


def _adam(recvs, w, m, v, *, name):
    nl = len(recvs)
    rows_l, cols = recvs[0].shape[1:]
    assert w.shape == (nl * rows_l, cols)
    tr = _pick(rows_l, (128, 64, 32, 16, 8))
    nt = rows_l // tr
    c1 = 1.0 / (1.0 - ADAM_B1 ** ADAM_STEP)
    c2 = 1.0 / (1.0 - ADAM_B2 ** ADAM_STEP)

    def body(*refs):
        r_refs = refs[:nl]
        w_ref, m_ref, v_ref, g_out, d_out, m_out, v_out = refs[nl:]
        layer = pl.program_id(0)

        def update(r_ref):
            g = r_ref[0].astype(F32)
            for j in range(1, N_DEV):
                g = g + r_ref[j].astype(F32)
            mn = ADAM_B1 * m_ref[...] + (1.0 - ADAM_B1) * g
            vn = ADAM_B2 * v_ref[...] + (1.0 - ADAM_B2) * (g * g)
            g_out[...] = g
            m_out[...] = mn
            v_out[...] = vn
            d_out[...] = -ADAM_LR * ((mn * c1) / (jnp.sqrt(vn * c2) + ADAM_EPS) + ADAM_WD * w_ref[...])

        for k in range(nl):
            pl.when(layer == k)(functools.partial(update, r_refs[k]))

    spec = pl.BlockSpec((tr, cols), lambda l, i: (l * nt + i, 0))
    rspec = lambda k: pl.BlockSpec((N_DEV, tr, cols), lambda l, i: (0, jnp.where(l == k, i, 0), 0))
    sds = jax.ShapeDtypeStruct(w.shape, F32)
    return pl.pallas_call(
        body, grid=(nl, nt), in_specs=[rspec(k) for k in range(nl)] + [spec, spec, spec],
        out_specs=(spec,) * 4, out_shape=(sds,) * 4,
        compiler_params=_cparams(("arbitrary", "arbitrary")), name=name)(*recvs, w, m, v)


def _band_mask(i, maxdist, nrow_blocks):
    row = lax.broadcasted_iota(jnp.int32, (BLOCK, 2 * BLOCK), 0)
    col = lax.broadcasted_iota(jnp.int32, (BLOCK, 2 * BLOCK), 1)
    dist = row - col + BLOCK
    return (dist >= 0) & (dist <= maxdist) & ((i > 0) | (col >= BLOCK))


def _band_views(q, k, zv, dil):
    s = q.shape[0]
    z, kvw, voff = zv
    if dil == 1:
        return q, k, z, z.shape[1] // kvw, voff
    vd = z[:, voff * kvw:(voff + 1) * kvw].reshape(s // dil, dil * kvw)
    return q.reshape(s // dil, dil * q.shape[1]), k.reshape(s // dil, dil * k.shape[1]), vd, 1, 0


def _band_fwd(q, k, zv, *, dil, group, maxdist, sink_row=None, name):
    s, qw = q.shape
    kvw = zv[1]
    qd, kd, zd, zblocks, voff = _band_views(q, k, zv, dil)
    nb = s // dil // BLOCK
    nheads = qw // 64
    scale = 64 ** -0.5
    has_sink = sink_row is not None

    def body(*refs):
        q_ref, kp_ref, kc_ref, vp_ref, vc_ref = refs[:5]
        s_ref = refs[5] if has_sink else None
        o_ref, l_ref, sc_s, p_s = refs[5 + has_sink:]
        i = pl.program_id(1)
        mask = _band_mask(i, maxdist, nb)
        k2 = jnp.concatenate([kp_ref[...], kc_ref[...]], axis=0)
        v2 = jnp.concatenate([vp_ref[...], vc_ref[...]], axis=0).astype(BF16)
        for h in range(nheads):
            g = h // group
            sc_s[h] = _dot(q_ref[:, 64 * h:64 * h + 64], k2[:, 64 * g:64 * g + 64], NT_DIMS)
        sc = jnp.where(mask[None], sc_s[...] * scale, NEG)
        mx = jnp.max(sc, axis=2, keepdims=True)
        if has_sink:
            snk = s_ref[:, :, 0:1]
            mx = jnp.maximum(mx, snk)
        e = jnp.exp(sc - mx)
        den = jnp.sum(e, axis=2, keepdims=True)
        if has_sink:
            den = den + jnp.exp(snk - mx)
        p_s[...] = (e * (1.0 / den)).astype(BF16)
        lse = mx + jnp.log(den)
        for h in range(nheads):
            g = h // group
            o_ref[:, 64 * h:64 * h + 64] = _dot(p_s[h], v2[:, 64 * g:64 * g + 64])
            l_ref[:, 64 * h:64 * h + 64] = jnp.broadcast_to(lse[h], (BLOCK, 64))

    prev = lambda r, i: jnp.maximum(i - 1, 0)
    in_specs = [
        pl.BlockSpec((BLOCK, qw), lambda r, i: (i, r)),
        pl.BlockSpec((BLOCK, kvw), lambda r, i: (prev(r, i), r)),
        pl.BlockSpec((BLOCK, kvw), lambda r, i: (i, r)),
        pl.BlockSpec((BLOCK, kvw), lambda r, i: (prev(r, i), r * zblocks + voff)),
        pl.BlockSpec((BLOCK, kvw), lambda r, i: (i, r * zblocks + voff)),
    ]
    args = [qd, kd, kd, zd, zd]
    if has_sink:
        in_specs.append(pl.BlockSpec((nheads, 1, LANES), lambda r, i: (0, 0, 0)))
        args.append(jnp.broadcast_to(sink_row.reshape(nheads, 64)[:, 0:1, None], (nheads, 1, LANES)))
    ospec = pl.BlockSpec((BLOCK, qw), lambda r, i: (i, r))
    sds = jax.ShapeDtypeStruct((s // dil, dil * qw), F32)
    o, lse = pl.pallas_call(
        body, grid=(dil, nb), in_specs=in_specs, out_specs=(ospec, ospec), out_shape=(sds, sds),
        scratch_shapes=[pltpu.VMEM((nheads, BLOCK, 2 * BLOCK), F32), pltpu.VMEM((nheads, BLOCK, 2 * BLOCK), BF16)],
        compiler_params=_cparams(("parallel", "parallel")), name=name)(*args)
    return o.reshape(s, qw), lse.reshape(s, qw)


def _band_dq(q, k, zv, do, lse, delta, *, dil, group, maxdist, name):
    s, qw = q.shape
    kvw = zv[1]
    qd, kd, zd, zblocks, voff = _band_views(q, k, zv, dil)
    rs = lambda t: t.reshape(s // dil, dil * qw)
    nb = s // dil // BLOCK
    nheads = qw // 64
    scale = 64 ** -0.5

    def body(q_ref, kp_ref, kc_ref, vp_ref, vc_ref, do_ref, l_ref, d_ref, dq_ref, sc_s, dp_s, ds_s, st_s):
        i = pl.program_id(1)
        mask = _band_mask(i, maxdist, nb)
        k2 = jnp.concatenate([kp_ref[...], kc_ref[...]], axis=0)
        v2 = jnp.concatenate([vp_ref[...], vc_ref[...]], axis=0).astype(BF16)
        dob = do_ref[...].astype(BF16)
        for h in range(nheads):
            g = h // group
            hs = slice(64 * h, 64 * h + 64)
            sc_s[h] = _dot(q_ref[:, hs], k2[:, 64 * g:64 * g + 64], NT_DIMS)
            dp_s[h] = _dot(dob[:, hs], v2[:, 64 * g:64 * g + 64], NT_DIMS)
            st_s[0, h] = l_ref[:, 64 * h:64 * h + 1]
            st_s[1, h] = d_ref[:, 64 * h:64 * h + 1]
        p = jnp.exp(jnp.where(mask[None], sc_s[...] * scale, NEG) - st_s[0])
        ds_s[...] = (p * (dp_s[...] - st_s[1]) * scale).astype(BF16)
        for h in range(nheads):
            g = h // group
            dq_ref[:, 64 * h:64 * h + 64] = _dot(ds_s[h], k2[:, 64 * g:64 * g + 64])

    prev = lambda r, i: jnp.maximum(i - 1, 0)
    qspec = pl.BlockSpec((BLOCK, qw), lambda r, i: (i, r))
    in_specs = [
        qspec,
        pl.BlockSpec((BLOCK, kvw), lambda r, i: (prev(r, i), r)),
        pl.BlockSpec((BLOCK, kvw), lambda r, i: (i, r)),
        pl.BlockSpec((BLOCK, kvw), lambda r, i: (prev(r, i), r * zblocks + voff)),
        pl.BlockSpec((BLOCK, kvw), lambda r, i: (i, r * zblocks + voff)),
        qspec, qspec, qspec,
    ]
    dq = pl.pallas_call(
        body, grid=(dil, nb), in_specs=in_specs, out_specs=qspec,
        out_shape=jax.ShapeDtypeStruct((s // dil, dil * qw), F32),
        scratch_shapes=[pltpu.VMEM((nheads, BLOCK, 2 * BLOCK), F32), pltpu.VMEM((nheads, BLOCK, 2 * BLOCK), F32),
                        pltpu.VMEM((nheads, BLOCK, 2 * BLOCK), BF16), pltpu.VMEM((2, nheads, BLOCK, 1), F32)],
        compiler_params=_cparams(("parallel", "parallel")), name=name)(qd, kd, kd, zd, zd, rs(do), rs(lse), rs(delta))
    return dq.reshape(s, qw)


def _band_dkv(q, k, zv, do, lse, delta, *, dil, group, maxdist, name):
    s, qw = q.shape
    kvw = zv[1]
    qd, kd, zd, zblocks, voff = _band_views(q, k, zv, dil)
    rs = lambda t: t.reshape(s // dil, dil * qw)
    nb = s // dil // BLOCK
    nheads = qw // 64
    nkv = kvw // 64
    scale = 64 ** -0.5

    def body(k_ref, v_ref, q0, q1, do0, do1, l0, l1, d0, d1, dk_ref, dv_ref, sc_s, dp_s, p_s, ds_s, st_s):
        j = pl.program_id(1)
        row = lax.broadcasted_iota(jnp.int32, (2 * BLOCK, BLOCK), 0)
        col = lax.broadcasted_iota(jnp.int32, (2 * BLOCK, BLOCK), 1)
        dist = row - col
        mask = (dist >= 0) & (dist <= maxdist) & ((j < nb - 1) | (row < BLOCK))
        q2 = jnp.concatenate([q0[...], q1[...]], axis=0)
        do2 = jnp.concatenate([do0[...], do1[...]], axis=0).astype(BF16)
        vb = v_ref[...].astype(BF16)
        for h in range(nheads):
            g = h // group
            hs = slice(64 * h, 64 * h + 64)
            sc_s[h] = _dot(q2[:, hs], k_ref[:, 64 * g:64 * g + 64], NT_DIMS)
            dp_s[h] = _dot(do2[:, hs], vb[:, 64 * g:64 * g + 64], NT_DIMS)
            st_s[0, h, 0:BLOCK] = l0[:, 64 * h:64 * h + 1]
            st_s[0, h, BLOCK:2 * BLOCK] = l1[:, 64 * h:64 * h + 1]
            st_s[1, h, 0:BLOCK] = d0[:, 64 * h:64 * h + 1]
            st_s[1, h, BLOCK:2 * BLOCK] = d1[:, 64 * h:64 * h + 1]
        p = jnp.exp(jnp.where(mask[None], sc_s[...] * scale, NEG) - st_s[0])
        p_s[...] = p.astype(BF16)
        ds_s[...] = (p * (dp_s[...] - st_s[1]) * scale).astype(BF16)
        dks = [None] * nkv
        dvs = [None] * nkv
        for h in range(nheads):
            g = h // group
            hs = slice(64 * h, 64 * h + 64)
            dvh = _dot(p_s[h], do2[:, hs], TN_DIMS)
            dkh = _dot(ds_s[h], q2[:, hs], TN_DIMS)
            dks[g] = dkh if dks[g] is None else dks[g] + dkh
            dvs[g] = dvh if dvs[g] is None else dvs[g] + dvh
        for g in range(nkv):
            dk_ref[:, 64 * g:64 * g + 64] = dks[g]
            dv_ref[:, 64 * g:64 * g + 64] = dvs[g].astype(BF16)

    nxt = lambda r, j: jnp.minimum(j + 1, nb - 1)
    q0s = pl.BlockSpec((BLOCK, qw), lambda r, j: (j, r))
    q1s = pl.BlockSpec((BLOCK, qw), lambda r, j: (nxt(r, j), r))
    kspec = pl.BlockSpec((BLOCK, kvw), lambda r, j: (j, r))
    in_specs = [kspec, pl.BlockSpec((BLOCK, kvw), lambda r, j: (j, r * zblocks + voff)),
                q0s, q1s, q0s, q1s, q0s, q1s, q0s, q1s]
    dod, lsd, dld = rs(do), rs(lse), rs(delta)
    dk, dv = pl.pallas_call(
        body, grid=(dil, nb), in_specs=in_specs, out_specs=(kspec, kspec),
        out_shape=(jax.ShapeDtypeStruct((s // dil, dil * kvw), F32), jax.ShapeDtypeStruct((s // dil, dil * kvw), BF16)),
        scratch_shapes=[pltpu.VMEM((nheads, 2 * BLOCK, BLOCK), F32), pltpu.VMEM((nheads, 2 * BLOCK, BLOCK), F32),
                        pltpu.VMEM((nheads, 2 * BLOCK, BLOCK), BF16), pltpu.VMEM((nheads, 2 * BLOCK, BLOCK), BF16),
                        pltpu.VMEM((2, nheads, 2 * BLOCK, 1), F32)],
        compiler_params=_cparams(("parallel", "parallel")), name=name)(kd, zd, qd, qd, dod, dod, lsd, lsd, dld, dld)
    return dk.reshape(s, kvw), dv.reshape(s, kvw)


HPG = 2


def _flash_tiles(sq, sk):
    return _pick(sq, (512, 256, 128)), _pick(sk, (512, 256, 128))


def _diag_mask(t):
    return lax.broadcasted_iota(jnp.int32, (t, t), 1) <= lax.broadcasted_iota(jnp.int32, (t, t), 0)


def _attn_fwd(q, k, vv, *, dv, causal, scale, name, xchg=None):
    sq, qw = q.shape
    sk = k.shape[0]
    varr, vbw, voff = vv
    assert vbw == HPG * dv
    ngrp = qw // (HPG * LANES)
    tq, tk = _flash_tiles(sq, sk)
    nq, nk = sq // tq, sk // tk
    if causal:
        assert tq == tk and sq == sk

    def body(q_ref, k_ref, v_ref, o_ref, l_ref, m_s, l_s, acc_s):
        i = pl.program_id(1)
        m_s[...] = jnp.full_like(m_s, NEG)
        l_s[...] = jnp.zeros_like(l_s)
        acc_s[...] = jnp.zeros_like(acc_s)

        def step(j, masked):
            rows = pl.ds(pl.multiple_of(j * tk, tk), tk)
            if masked:
                mask = _diag_mask(tq)
            for h in range(HPG):
                hs = slice(LANES * h, LANES * (h + 1))
                sc = _dot(q_ref[:, hs], k_ref[rows, hs], NT_DIMS) * scale
                if masked:
                    sc = jnp.where(mask, sc, NEG)
                m_prev = m_s[h]
                m_new = jnp.maximum(m_prev, jnp.max(sc, axis=1, keepdims=True))
                alpha = jnp.exp(m_prev - m_new)
                p = jnp.exp(sc - m_new[:, 0:1])
                l_s[h] = alpha * l_s[h] + jnp.sum(p, axis=1, keepdims=True)
                acc_s[h] = acc_s[h] * alpha[:, 0:dv] + _dot(p, v_ref[rows, dv * h:dv * (h + 1)])
                m_s[h] = m_new

        def loop_body(j, carry):
            step(j, False)
            return carry

        if causal:
            lax.fori_loop(0, i, loop_body, 0)
            step(i, True)
        else:
            lax.fori_loop(0, nk, loop_body, 0)
        for h in range(HPG):
            o_ref[:, dv * h:dv * (h + 1)] = acc_s[h] * (1.0 / l_s[h][:, 0:dv])
            l_ref[:, LANES * h:LANES * (h + 1)] = m_s[h] + jnp.log(l_s[h])

    in_specs = [
        pl.BlockSpec((tq, HPG * LANES), lambda g, i: (i, g)),
        pl.BlockSpec((sk, HPG * LANES), lambda g, i: (0, g)),
        pl.BlockSpec((sk, vbw), lambda g, i: (0, voff + g)),
    ]
    return _pcall(
        body, grid=(ngrp, nq), in_specs=in_specs,
        out_specs=(pl.BlockSpec((tq, HPG * dv), lambda g, i: (i, g)),
                   pl.BlockSpec((tq, HPG * LANES), lambda g, i: (i, g))),
        out_shape=(jax.ShapeDtypeStruct((sq, ngrp * HPG * dv), F32), jax.ShapeDtypeStruct((sq, qw), F32)),
        scratch_shapes=[pltpu.VMEM((HPG, tq, LANES), F32), pltpu.VMEM((HPG, tq, LANES), F32),
                        pltpu.VMEM((HPG, tq, dv), F32)],
        sem=("parallel", "arbitrary"), name=name, args=(q, k, varr), xchg=xchg)


def _attn_dq(q, k, vv, do, lse, delta, *, dv, causal, scale, name, xchg=None):
    sq, qw = q.shape
    sk = k.shape[0]
    varr, vbw, voff = vv
    ngrp = qw // (HPG * LANES)
    tq, tk = _flash_tiles(sq, sk)
    nq, nk = sq // tq, sk // tk

    def body(q_ref, k_ref, v_ref, do_ref, l_ref, d_ref, dq_ref, acc_s):
        i = pl.program_id(1)
        acc_s[...] = jnp.zeros_like(acc_s)

        def step(j, masked):
            rows = pl.ds(pl.multiple_of(j * tk, tk), tk)
            if masked:
                mask = _diag_mask(tq)
            for h in range(HPG):
                hs = slice(LANES * h, LANES * (h + 1))
                kh = k_ref[rows, hs]
                sc = _dot(q_ref[:, hs], kh, NT_DIMS) * scale
                if masked:
                    sc = jnp.where(mask, sc, NEG)
                p = jnp.exp(sc - l_ref[:, LANES * h:LANES * h + 1])
                dp = _dot(do_ref[:, dv * h:dv * (h + 1)], v_ref[rows, dv * h:dv * (h + 1)], NT_DIMS)
                ds = p * (dp - d_ref[:, dv * h:dv * h + 1]) * scale
                acc_s[h] += _dot(ds, kh)

        def loop_body(j, carry):
            step(j, False)
            return carry

        if causal:
            lax.fori_loop(0, i, loop_body, 0)
            step(i, True)
        else:
            lax.fori_loop(0, nk, loop_body, 0)
        for h in range(HPG):
            dq_ref[:, LANES * h:LANES * (h + 1)] = acc_s[h]

    qspec = pl.BlockSpec((tq, HPG * LANES), lambda g, i: (i, g))
    ospec = pl.BlockSpec((tq, HPG * dv), lambda g, i: (i, g))
    in_specs = [
        qspec,
        pl.BlockSpec((sk, HPG * LANES), lambda g, i: (0, g)),
        pl.BlockSpec((sk, vbw), lambda g, i: (0, voff + g)),
        ospec, qspec, ospec,
    ]
    return _pcall(
        body, grid=(ngrp, nq), in_specs=in_specs, out_specs=qspec,
        out_shape=jax.ShapeDtypeStruct((sq, qw), F32),
        scratch_shapes=[pltpu.VMEM((HPG, tq, LANES), F32)],
        sem=("parallel", "arbitrary"), name=name, args=(q, k, varr, do, lse, delta), xchg=xchg)


def _attn_dkv(q, k, vv, do, lse, delta, *, dv, causal, scale, name, xchg=None):
    sq, qw = q.shape
    sk = k.shape[0]
    varr, vbw, voff = vv
    ngrp = qw // (HPG * LANES)
    tq, tk = _flash_tiles(sq, sk)
    nq, nk = sq // tq, sk // tk

    def body(q_ref, k_ref, v_ref, do_ref, l_ref, d_ref, dk_ref, dv_ref, dk_s, dv_s):
        j = pl.program_id(1)
        dk_s[...] = jnp.zeros_like(dk_s)
        dv_s[...] = jnp.zeros_like(dv_s)

        def step(i, masked):
            rows = pl.ds(pl.multiple_of(i * tq, tq), tq)
            if masked:
                mask = _diag_mask(tq)
            for h in range(HPG):
                qh = q_ref[rows, LANES * h:LANES * (h + 1)]
                doh = do_ref[rows, dv * h:dv * (h + 1)]
                sc = _dot(qh, k_ref[:, LANES * h:LANES * (h + 1)], NT_DIMS) * scale
                if masked:
                    sc = jnp.where(mask, sc, NEG)
                p = jnp.exp(sc - l_ref[rows, LANES * h:LANES * h + 1])
                dv_s[h] += _dot(p, doh, TN_DIMS)
                dp = _dot(doh, v_ref[:, dv * h:dv * (h + 1)], NT_DIMS)
                ds = p * (dp - d_ref[rows, dv * h:dv * h + 1]) * scale
                dk_s[h] += _dot(ds, qh, TN_DIMS)

        def loop_body(i, carry):
            step(i, False)
            return carry

        if causal:
            step(j, True)
            lax.fori_loop(j + 1, nq, loop_body, 0)
        else:
            lax.fori_loop(0, nq, loop_body, 0)
        for h in range(HPG):
            dk_ref[:, LANES * h:LANES * (h + 1)] = dk_s[h]
            dv_ref[:, dv * h:dv * (h + 1)] = dv_s[h].astype(BF16)

    qspec = pl.BlockSpec((sq, HPG * LANES), lambda g, j: (0, g))
    ospec = pl.BlockSpec((sq, HPG * dv), lambda g, j: (0, g))
    kspec = pl.BlockSpec((tk, HPG * LANES), lambda g, j: (j, g))
    in_specs = [qspec, kspec, pl.BlockSpec((tk, vbw), lambda g, j: (j, voff + g)), ospec, qspec, ospec]
    return _pcall(
        body, grid=(ngrp, nk), in_specs=in_specs,
        out_specs=(kspec, pl.BlockSpec((tk, HPG * dv), lambda g, j: (j, g))),
        out_shape=(jax.ShapeDtypeStruct((sk, qw), F32), jax.ShapeDtypeStruct((sk, ngrp * HPG * dv), BF16)),
        scratch_shapes=[pltpu.VMEM((HPG, tk, LANES), F32), pltpu.VMEM((HPG, tk, dv), F32)],
        sem=("parallel", "arbitrary"), name=name, args=(q, k, varr, do, lse, delta), xchg=xchg)


def _flash_fwd(q, k, vv, *, dv, causal, scale, name, xchg=None):
    sq, qw = q.shape
    sk = k.shape[0]
    varr, vbw, voff = vv
    assert vbw == HPG * dv
    ngrp = qw // (HPG * LANES)
    tq, tk = _flash_tiles(sq, sk)
    nq, nk = sq // tq, sk // tk
    if causal:
        assert tq == tk and sq == sk

    def body(q_ref, k_ref, v_ref, o_ref, l_ref, m_s, l_s, acc_s):
        i, j = pl.program_id(1), pl.program_id(2)

        @pl.when(j == 0)
        def _():
            m_s[...] = jnp.full_like(m_s, NEG)
            l_s[...] = jnp.zeros_like(l_s)
            acc_s[...] = jnp.zeros_like(acc_s)

        def step(masked):
            if masked:
                mask = _diag_mask(tq)
            for h in range(HPG):
                sc = _dot(q_ref[:, LANES * h:LANES * (h + 1)], k_ref[:, LANES * h:LANES * (h + 1)], NT_DIMS) * scale
                if masked:
                    sc = jnp.where(mask, sc, NEG)
                m_prev = m_s[h]
                m_new = jnp.maximum(m_prev, jnp.max(sc, axis=1, keepdims=True))
                alpha = jnp.exp(m_prev - m_new)
                p = jnp.exp(sc - m_new[:, 0:1])
                l_s[h] = alpha * l_s[h] + jnp.sum(p, axis=1, keepdims=True)
                acc_s[h] = acc_s[h] * alpha[:, 0:dv] + _dot(p, v_ref[:, dv * h:dv * (h + 1)])
                m_s[h] = m_new

        if causal:
            pl.when(j < i)(functools.partial(step, False))
            pl.when(j == i)(functools.partial(step, True))
        else:
            step(False)

        @pl.when(j == nk - 1)
        def _():
            for h in range(HPG):
                o_ref[:, dv * h:dv * (h + 1)] = acc_s[h] * (1.0 / l_s[h][:, 0:dv])
                l_ref[:, LANES * h:LANES * (h + 1)] = m_s[h] + jnp.log(l_s[h])

    kidx = (lambda g, i, j: jnp.minimum(j, i)) if causal else (lambda g, i, j: j)
    in_specs = [
        pl.BlockSpec((tq, HPG * LANES), lambda g, i, j: (i, g)),
        pl.BlockSpec((tk, HPG * LANES), lambda g, i, j: (kidx(g, i, j), g)),
        pl.BlockSpec((tk, vbw), lambda g, i, j: (kidx(g, i, j), voff + g)),
    ]
    return _pcall(
        body, grid=(ngrp, nq, nk), in_specs=in_specs,
        out_specs=(pl.BlockSpec((tq, HPG * dv), lambda g, i, j: (i, g)),
                   pl.BlockSpec((tq, HPG * LANES), lambda g, i, j: (i, g))),
        out_shape=(jax.ShapeDtypeStruct((sq, ngrp * HPG * dv), F32), jax.ShapeDtypeStruct((sq, qw), F32)),
        scratch_shapes=[pltpu.VMEM((HPG, tq, LANES), F32), pltpu.VMEM((HPG, tq, LANES), F32),
                        pltpu.VMEM((HPG, tq, dv), F32)],
        sem=("parallel", "parallel", "arbitrary"), name=name, args=(q, k, varr), xchg=xchg)


def _flash_dq(q, k, vv, do, lse, delta, *, dv, causal, scale, name, xchg=None):
    sq, qw = q.shape
    sk = k.shape[0]
    varr, vbw, voff = vv
    ngrp = qw // (HPG * LANES)
    tq, tk = _flash_tiles(sq, sk)
    nq, nk = sq // tq, sk // tk

    def body(q_ref, k_ref, v_ref, do_ref, l_ref, d_ref, dq_ref, acc_s):
        i, j = pl.program_id(1), pl.program_id(2)

        @pl.when(j == 0)
        def _():
            acc_s[...] = jnp.zeros_like(acc_s)

        def step(masked):
            if masked:
                mask = _diag_mask(tq)
            for h in range(HPG):
                kh = k_ref[:, LANES * h:LANES * (h + 1)]
                sc = _dot(q_ref[:, LANES * h:LANES * (h + 1)], kh, NT_DIMS) * scale
                if masked:
                    sc = jnp.where(mask, sc, NEG)
                p = jnp.exp(sc - l_ref[:, LANES * h:LANES * h + 1])
                dp = _dot(do_ref[:, dv * h:dv * (h + 1)], v_ref[:, dv * h:dv * (h + 1)], NT_DIMS)
                ds = p * (dp - d_ref[:, dv * h:dv * h + 1]) * scale
                acc_s[h] += _dot(ds, kh)

        if causal:
            pl.when(j < i)(functools.partial(step, False))
            pl.when(j == i)(functools.partial(step, True))
        else:
            step(False)

        @pl.when(j == nk - 1)
        def _():
            for h in range(HPG):
                dq_ref[:, LANES * h:LANES * (h + 1)] = acc_s[h]

    kidx = (lambda g, i, j: jnp.minimum(j, i)) if causal else (lambda g, i, j: j)
    qspec = pl.BlockSpec((tq, HPG * LANES), lambda g, i, j: (i, g))
    ospec = pl.BlockSpec((tq, HPG * dv), lambda g, i, j: (i, g))
    in_specs = [
        qspec,
        pl.BlockSpec((tk, HPG * LANES), lambda g, i, j: (kidx(g, i, j), g)),
        pl.BlockSpec((tk, vbw), lambda g, i, j: (kidx(g, i, j), voff + g)),
        ospec, qspec, ospec,
    ]
    return _pcall(
        body, grid=(ngrp, nq, nk), in_specs=in_specs, out_specs=qspec,
        out_shape=jax.ShapeDtypeStruct((sq, qw), F32),
        scratch_shapes=[pltpu.VMEM((HPG, tq, LANES), F32)],
        sem=("parallel", "parallel", "arbitrary"), name=name, args=(q, k, varr, do, lse, delta), xchg=xchg)


def _flash_dkv(q, k, vv, do, lse, delta, *, dv, causal, scale, name, xchg=None):
    sq, qw = q.shape
    sk = k.shape[0]
    varr, vbw, voff = vv
    ngrp = qw // (HPG * LANES)
    tq, tk = _flash_tiles(sq, sk)
    nq, nk = sq // tq, sk // tk

    def body(q_ref, k_ref, v_ref, do_ref, l_ref, d_ref, dk_ref, dv_ref, dk_s, dv_s):
        j, i = pl.program_id(1), pl.program_id(2)

        @pl.when(i == 0)
        def _():
            dk_s[...] = jnp.zeros_like(dk_s)
            dv_s[...] = jnp.zeros_like(dv_s)

        def step(masked):
            if masked:
                mask = _diag_mask(tq)
            for h in range(HPG):
                qh = q_ref[:, LANES * h:LANES * (h + 1)]
                doh = do_ref[:, dv * h:dv * (h + 1)]
                sc = _dot(qh, k_ref[:, LANES * h:LANES * (h + 1)], NT_DIMS) * scale
                if masked:
                    sc = jnp.where(mask, sc, NEG)
                p = jnp.exp(sc - l_ref[:, LANES * h:LANES * h + 1])
                dv_s[h] += _dot(p, doh, TN_DIMS)
                dp = _dot(doh, v_ref[:, dv * h:dv * (h + 1)], NT_DIMS)
                ds = p * (dp - d_ref[:, dv * h:dv * h + 1]) * scale
                dk_s[h] += _dot(ds, qh, TN_DIMS)

        if causal:
            pl.when(i > j)(functools.partial(step, False))
            pl.when(i == j)(functools.partial(step, True))
        else:
            step(False)

        @pl.when(i == nq - 1)
        def _():
            for h in range(HPG):
                dk_ref[:, LANES * h:LANES * (h + 1)] = dk_s[h]
                dv_ref[:, dv * h:dv * (h + 1)] = dv_s[h].astype(BF16)

    qidx = (lambda g, j, i: jnp.maximum(i, j)) if causal else (lambda g, j, i: i)
    qspec = pl.BlockSpec((tq, HPG * LANES), lambda g, j, i: (qidx(g, j, i), g))
    ospec = pl.BlockSpec((tq, HPG * dv), lambda g, j, i: (qidx(g, j, i), g))
    kspec = pl.BlockSpec((tk, HPG * LANES), lambda g, j, i: (j, g))
    in_specs = [qspec, kspec, pl.BlockSpec((tk, vbw), lambda g, j, i: (j, voff + g)), ospec, qspec, ospec]
    return _pcall(
        body, grid=(ngrp, nk, nq), in_specs=in_specs,
        out_specs=(kspec, pl.BlockSpec((tk, HPG * dv), lambda g, j, i: (j, g))),
        out_shape=(jax.ShapeDtypeStruct((sk, qw), F32), jax.ShapeDtypeStruct((sk, ngrp * HPG * dv), BF16)),
        scratch_shapes=[pltpu.VMEM((HPG, tk, LANES), F32), pltpu.VMEM((HPG, tk, dv), F32)],
        sem=("parallel", "parallel", "arbitrary"), name=name, args=(q, k, varr, do, lse, delta), xchg=xchg)


def _xchg_copies(src_ref, out_ref, send_sems, recv_sems, local_sems, k, scatter):
    me = 4 * lax.axis_index("x") + 2 * lax.axis_index("y") + lax.axis_index("c")
    own = src_ref.at[me] if scatter else src_ref
    copies = [pltpu.make_async_copy(own, out_ref.at[me], local_sems.at[k])]
    for d in range(1, N_DEV):
        peer = (me + d) % N_DEV
        piece = src_ref.at[peer] if scatter else src_ref
        sem = (N_DEV - 1) * k + d - 1
        copies.append(pltpu.make_async_remote_copy(
            src_ref=piece, dst_ref=out_ref.at[me], send_sem=send_sems.at[sem], recv_sem=recv_sems.at[sem],
            device_id=(peer // 4, (peer // 2) % 2, peer % 2), device_id_type=pl.DeviceIdType.MESH))
    return copies


def _xchg_shapes(xchg):
    return [jax.ShapeDtypeStruct((N_DEV,) + tuple(src.shape[1:] if scatter else src.shape), src.dtype)
            for src, scatter in xchg]


def _xchg_sems(nx):
    return [pltpu.SemaphoreType.DMA(((N_DEV - 1) * nx,)), pltpu.SemaphoreType.DMA(((N_DEV - 1) * nx,)),
            pltpu.SemaphoreType.DMA((nx,))]


ANY_SPEC = pl.BlockSpec(memory_space=pl.ANY)


def _exchange_many(xchg, *, name):
    nx = len(xchg)

    def body(*refs):
        srcs, outs = refs[:nx], refs[nx:2 * nx]
        send_sems, recv_sems, local_sems = refs[2 * nx:]
        copies = [c for k, (_, scatter) in enumerate(xchg)
                  for c in _xchg_copies(srcs[k], outs[k], send_sems, recv_sems, local_sems, k, scatter)]
        for c in copies:
            c.start()
        for c in copies:
            c.wait()

    return pl.pallas_call(
        body, in_specs=[ANY_SPEC] * nx, out_specs=tuple([ANY_SPEC] * nx), out_shape=tuple(_xchg_shapes(xchg)),
        scratch_shapes=_xchg_sems(nx), name=name)(*[src for src, _ in xchg])


def _pcall(body, *, grid, in_specs, out_specs, out_shape, scratch_shapes=(), sem, name, args, xchg=None):
    single = not isinstance(out_shape, (tuple, list))
    if not xchg:
        res = pl.pallas_call(
            body, grid=grid, in_specs=list(in_specs), out_specs=out_specs, out_shape=out_shape,
            scratch_shapes=list(scratch_shapes), compiler_params=_cparams(sem), name=name)(*args)
        return res if xchg is None else (res, [])
    out_specs_l = [out_specs] if single else list(out_specs)
    out_shape_l = [out_shape] if single else list(out_shape)
    nx, n_in, n_out, n_scr = len(xchg), len(in_specs), len(out_shape_l), len(scratch_shapes)

    def wrapped(*refs):
        ins, xin = refs[:n_in], refs[n_in:n_in + nx]
        pos = n_in + nx
        outs, xout = refs[pos:pos + n_out], refs[pos + n_out:pos + n_out + nx]
        pos += n_out + nx
        scr = refs[pos:pos + n_scr]
        send_sems, recv_sems, local_sems = refs[pos + n_scr:]
        first = functools.reduce(jnp.logical_and, [pl.program_id(a) == 0 for a in range(len(grid))])
        last = functools.reduce(jnp.logical_and, [pl.program_id(a) == grid[a] - 1 for a in range(len(grid))])

        def copies():
            return [c for k, (_, scatter) in enumerate(xchg)
                    for c in _xchg_copies(xin[k], xout[k], send_sems, recv_sems, local_sems, k, scatter)]

        @pl.when(first)
        def _():
            for c in copies():
                c.start()

        body(*ins, *outs, *scr)

        @pl.when(last)
        def _():
            for c in copies():
                c.wait()

    res = pl.pallas_call(
        wrapped, grid=grid, in_specs=list(in_specs) + [ANY_SPEC] * nx,
        out_specs=tuple(out_specs_l + [ANY_SPEC] * nx), out_shape=tuple(out_shape_l + _xchg_shapes(xchg)),
        scratch_shapes=list(scratch_shapes) + _xchg_sems(nx),
        compiler_params=_cparams(("arbitrary",) * len(grid)), name=name + "_x")(*args, *[src for src, _ in xchg])
    main = res[0] if single else tuple(res[:n_out])
    return main, list(res[n_out:])


def _unshard(wg, axis):
    nd = wg.ndim - 1
    perm = list(range(1, axis + 1)) + [0] + list(range(axis + 1, nd + 1))
    t = wg.transpose(perm)
    shp = list(wg.shape[1:])
    shp[axis] *= N_DEV
    return t.reshape(shp)


def _shard_major(full, axis):
    shp = list(full.shape)
    shp[axis:axis + 1] = [N_DEV, shp[axis] // N_DEV]
    t = full.reshape(shp)
    perm = [axis] + list(range(axis)) + list(range(axis + 1, len(shp)))
    return t.transpose(perm)


BIG = ("w_in", "b_w_uq", "b_w_ukv", "m_w_kv", "w_branch", "w_out", "w_up", "w_down")
BIG_AXIS = dict(w_in=2, b_w_uq=2, b_w_ukv=2, m_w_kv=1, w_branch=3, w_out=1, w_up=2, w_down=1)
SMALL = ("g_mix", "b_gate", "a_qn", "a_kn", "a_sink", "b_qa_norm", "b_kva_norm", "b_qn", "b_kn", "c_qn", "c_kn",
         "m_g_mem", "m_qn", "m_kn", "g_mlp")
WEIGHTS = ("g_mix", "w_in", "b_gate", "a_qn", "a_kn", "a_sink", "b_qa_norm", "b_kva_norm", "b_w_uq", "b_w_ukv", "b_qn",
           "b_kn", "c_qn", "c_kn", "m_g_mem", "m_w_kv", "m_qn", "m_kn", "w_branch", "w_out", "g_mlp", "w_up", "w_down")


def _rope_tables(positions, dim):
    inv = ROPE_THETA ** (-jnp.arange(0, dim, 2, dtype=F32) / dim)
    ang = positions.astype(F32)[:, None] * inv
    return jnp.cos(ang), jnp.sin(ang)


def _lane_mask(lo, hi):
    lane = np.arange(LANES)
    return jnp.asarray(((lane >= lo) & (lane < hi)).astype(np.float32)[None, :])


def _step(x, mem, positions, w, moments_m, moments_v, loss_target):
    s_len, d_model = x.shape
    n_layers = w["g_mix"].shape[0]
    lay, orig_names, order, n_in, nw = _in_layout(d_model)
    unit = lambda name: lay[name]["off"] // lay[name]["nw"]

    def ag_src(l, names):
        return [(w[n][l].astype(BF16), False) for n in names]

    def build(gathered):
        fl = {n: _unshard(gathered[n], BIG_AXIS[n] - 1) for n in BIG}
        uq = fl["b_w_uq"].reshape(B_Q_LORA, B_HEADS, B_NOPE + B_ROPE)
        ukv = fl["b_w_ukv"].reshape(B_KV_LORA, B_HEADS, B_NOPE + B_V)
        return dict(
            w_in=_permute_cols(fl["w_in"], lay, order),
            uq=jnp.pad(uq, ((0, 0), (0, 0), (0, LANES - B_NOPE - B_ROPE))).reshape(B_Q_LORA, B_HEADS * LANES),
            ukv=jnp.concatenate([
                jnp.pad(ukv[..., :B_NOPE], ((0, 0), (0, 0), (0, LANES - B_NOPE))).reshape(B_KV_LORA, B_HEADS * LANES),
                ukv[..., B_NOPE:].reshape(B_KV_LORA, B_HEADS * B_V)], axis=-1),
            m_w_kv=fl["m_w_kv"], w_branch=fl["w_branch"], w_out=fl["w_out"], w_up=fl["w_up"], w_down=fl["w_down"])

    def rs_src(name, g):
        return (_shard_major(g, BIG_AXIS[name] - 1).astype(BF16), True)

    p64 = _pattern([64, 64])
    p128 = _pattern([128])
    pq = _pattern([64, 32, 32])
    pkr = _pattern([32, 32, 32, 32])
    cos_h, sin_h = _rope_tables(positions, A_HD)
    cos_r, sin_r = _rope_tables(positions, B_ROPE)
    ones = lambda n: jnp.ones((s_len, n), F32)
    zeros = lambda n: jnp.zeros((s_len, n), F32)
    rope64 = (jnp.tile(cos_h, (1, 4)), jnp.tile(jnp.concatenate([-sin_h, sin_h], axis=1), (1, 2)),
              jnp.asarray((np.arange(LANES) % 64 < 32).astype(np.float32)[None, :]), 32)
    rope_q = (jnp.concatenate([ones(64), cos_r, cos_r, ones(32)], axis=1),
              jnp.concatenate([zeros(64), -sin_r, sin_r, zeros(32)], axis=1), _lane_mask(64, 80), 16)
    rope_k = (jnp.concatenate([cos_r, cos_r, ones(96)], axis=1),
              jnp.concatenate([-sin_r, sin_r, zeros(96)], axis=1), _lane_mask(0, 16), 16)

    scale_b = (B_NOPE + B_ROPE) ** -0.5
    scale_m = M_HD ** -0.5
    zpad = lambda n: jnp.zeros((n,), F32)

    def gains(l):
        return dict(
            a_q=jnp.tile(w["a_qn"][l], A_HEADS), a_k=jnp.tile(w["a_kn"][l], A_KV_HEADS),
            b_q=jnp.tile(jnp.concatenate([w["b_qn"][l], zpad(32)]), B_HEADS),
            b_k=jnp.tile(jnp.concatenate([w["b_kn"][l, :B_NOPE], zpad(64)]), B_HEADS),
            b_kr=jnp.concatenate([w["b_kn"][l, B_NOPE:], zpad(96)]),
            c_q=[jnp.tile(w["c_qn"][l, g], C_HEADS) for g in range(3)],
            c_k=[jnp.tile(w["c_kn"][l, g], C_HEADS) for g in range(3)],
            m_q=jnp.tile(w["m_qn"][l], M_HEADS), m_k=jnp.tile(w["m_kn"][l], M_HEADS),
            sink=jnp.repeat(w["a_sink"][l], A_HD).reshape(1, BRANCH_W))

    ag_hosts = dict(mm_in=("w_up", "w_down"), flash_b=("w_in",),
                    mm_up=("w_branch", "w_out", "m_w_kv", "b_w_uq", "b_w_ukv"))
    saved, weights = [], []
    cur = x
    gathered = dict(zip(BIG, _exchange_many(ag_src(0, BIG), name="ag_layer0")))
    for l in range(n_layers):
        gn = gains(l)
        sv = dict(x=cur)
        wl = build(gathered)
        weights.append(wl)
        gathered = {}
        carry = lambda host: ag_src(l + 1, ag_hosts[host]) if l + 1 < n_layers else []
        h = _rmsnorm_fwd((cur, d_model, 0), w["g_mix"][l], name="f_norm_mix")
        z, got = _mm(h, wl["w_in"], name="f_mm_in", xchg=carry("mm_in"))
        gathered.update(zip(ag_hosts["mm_in"], got))
        sv.update(h=h, z=z)
        qa = _headnorm_fwd((z, 512, unit("a_q")), gn["a_q"], p64, rope64, name="f_hn_aq")
        ka = _headnorm_fwd((z, 128, unit("a_k")), gn["a_k"], p64, rope64, name="f_hn_ak")
        va = (z, 128, unit("a_v"))
        o_a, lse_a = _band_fwd(qa, ka, va, dil=1, group=A_HEADS // A_KV_HEADS, maxdist=A_WINDOW - 1,
                               sink_row=gn["sink"], name="f_band_a")
        sv.update(qa=qa, ka=ka, o_a=o_a, lse_a=lse_a)
        cq = _rmsnorm_fwd((z, B_Q_LORA, unit("b_cq")), w["b_qa_norm"][l], name="f_norm_cq")
        q_up = _mm(cq, wl["uq"], name="f_mm_uq")
        ckv = _rmsnorm_fwd((z, B_KV_LORA, unit("b_ckv")), w["b_kva_norm"][l], name="f_norm_ckv")
        kv_up = _mm(ckv, wl["ukv"], name="f_mm_ukv")
        kp = _headnorm_fwd((z, LANES, unit("b_kr")), gn["b_kr"], pkr, rope_k, name="f_hn_kr")
        qb = _headnorm_fwd((q_up, B_HEADS * LANES, 0), gn["b_q"], pq, rope_q, name="f_hn_bq")
        kb = _headnorm_fwd((kv_up, B_HEADS * LANES, 0), gn["b_k"], p64, None, addend=kp, name="f_hn_bk")
        vb = (kv_up, HPG * B_V, B_HEADS * LANES // (HPG * B_V))
        (o_b, lse_b), got = _attn_fwd(qb, kb, vb, dv=B_V, causal=True, scale=scale_b, name="f_flash_b",
                                       xchg=carry("flash_b"))
        gathered.update(zip(ag_hosts["flash_b"], got))
        sv.update(cq=cq, q_up=q_up, ckv=ckv, kv_up=kv_up, qb=qb, kb=kb, o_b=o_b, lse_b=lse_b)
        qc, kc, o_cg, lse_cg = [], [], [], []
        for g, (win, dil) in enumerate(C_PATTERNS):
            qg = _headnorm_fwd((z, 512, unit("c%d" % (3 * g))), gn["c_q"][g], p64, rope64, name="f_hn_cq%d" % g)
            kg = _headnorm_fwd((z, 512, unit("c%d" % (3 * g + 1))), gn["c_k"][g], p64, rope64, name="f_hn_ck%d" % g)
            og, lg = _band_fwd(qg, kg, (z, 512, unit("c%d" % (3 * g + 2))), dil=dil, group=1, maxdist=win // dil,
                               name="f_band_c%d" % g)
            qc.append(qg), kc.append(kg), o_cg.append(og), lse_cg.append(lg)
        o_c = _combine_fwd(o_cg, lse_cg, name="f_combine")
        sv.update(qc=qc, kc=kc, o_cg=o_cg, lse_cg=lse_cg, o_c=o_c)
        memn = _rmsnorm_fwd((mem, d_model, 0), w["m_g_mem"][l], name="f_norm_mem")
        mkv = _mm(memn, wl["m_w_kv"], name="f_mm_mkv")
        mk = _headnorm_fwd((mkv, M_HEADS * M_HD, 0), gn["m_k"], p128, None, name="f_hn_mk")
        mq = _headnorm_fwd((z, M_HEADS * M_HD, unit("m_q")), gn["m_q"], p128, None, name="f_hn_mq")
        mv = (mkv, HPG * M_HD, M_HEADS * M_HD // (HPG * M_HD))
        o_m, lse_m = _attn_fwd(mq, mk, mv, dv=M_HD, causal=False, scale=scale_m, name="f_flash_m")
        sv.update(memn=memn, mkv=mkv, mk=mk, mq=mq, o_m=o_m, lse_m=lse_m)
        outs = (o_a, o_b, o_c, o_m)
        ys = [_mm(outs[n], wl["w_branch"][n], name="f_mm_branch") for n in range(N_BRANCH)]
        t = _merge_fwd((z, N_BRANCH * d_model, 0), w["b_gate"][l], ys, name="f_merge")
        x2 = _mm(t, wl["w_out"], mode="add", extra=cur, name="f_mm_out")
        h2 = _rmsnorm_fwd((x2, d_model, 0), w["g_mlp"][l], name="f_norm_mlp")
        (u, act), got = _mm(h2, wl["w_up"], mode="relu2", name="f_mm_up", xchg=carry("mm_up"))
        gathered.update(zip(ag_hosts["mm_up"], got))
        x3 = _mm(act, wl["w_down"], mode="add", extra=x2, name="f_mm_down")
        sv.update(ys=ys, t=t, x2=x2, h2=h2, u=u, act=act)
        saved.append(sv)
        cur = x3

    dx, loss_row = _loss_head(cur, loss_target, name="loss_head")
    loss = lax.psum(loss_row[0, 0], AXES)

    recvs = {n: [None] * n_layers for n in BIG}
    gsmall = {n: [None] * n_layers for n in SMALL}
    head_sum = lambda row, nh, hd: row.reshape(nh, hd).sum(axis=0)

    def item(name, layer, g):
        return (name, layer, rs_src(name, g))

    def srcs(items):
        return [it[2] for it in items]

    def store(items, got):
        for (name, layer, _), r in zip(items, got):
            recvs[name][layer] = r

    carried = []
    for l in reversed(range(n_layers)):
        sv = saved[l]
        wl = weights[l]
        gn = gains(l)
        z = sv["z"]
        g_wdown = _mm(sv["act"], dx, ta=True, name="b_mm_wdown")
        du = _mm(dx, wl["w_down"], tb=True, mode="drelu2", extra=sv["u"], out_dtype=BF16, name="b_mm_du")
        g_wup = _mm(sv["h2"], du, ta=True, name="b_mm_wup")
        dh2 = _mm(du, wl["w_up"], tb=True, name="b_mm_dh2")
        dx2, dg = _rmsnorm_bwd(dh2, (sv["x2"], d_model, 0), w["g_mlp"][l], dres=dx, name="b_norm_mlp")
        gsmall["g_mlp"][l] = dg[0]
        g_wout = _mm(sv["t"], dx2, ta=True, name="b_mm_wout")
        dt = _mm(dx2, wl["w_out"], tb=True, name="b_mm_dt")
        dys, dgate, dbg = _merge_bwd(dt, (z, N_BRANCH * d_model, 0), w["b_gate"][l], sv["ys"], name="b_merge")
        gsmall["b_gate"][l] = dbg[0]
        outs = (sv["o_a"], sv["o_b"], sv["o_c"], sv["o_m"])
        g_wbranch = jnp.stack([_mm(outs[n], dys[n], ta=True, name="b_mm_wbranch") for n in range(N_BRANCH)])
        do_a, do_b, do_c, do_m = [_mm(dys[n], wl["w_branch"][n], tb=True, name="b_mm_dbranch")
                                  for n in range(N_BRANCH)]
        mlp_items = [item("w_down", l, g_wdown), item("w_up", l, g_wup), item("w_out", l, g_wout),
                     item("w_branch", l, g_wbranch)]
        dparts = {"gate": dgate}
        va = (z, 128, unit("a_v"))
        kw_a = dict(dil=1, group=A_HEADS // A_KV_HEADS, maxdist=A_WINDOW - 1)
        dl_a, dsink = _delta(do_a, sv["o_a"], p64, lse=sv["lse_a"], sink_row=gn["sink"], name="b_delta_a")
        gsmall["a_sink"][l] = dsink[0, ::A_HD]
        dqa = _band_dq(sv["qa"], sv["ka"], va, do_a, sv["lse_a"], dl_a, name="b_band_dq_a", **kw_a)
        dka, dva = _band_dkv(sv["qa"], sv["ka"], va, do_a, sv["lse_a"], dl_a, name="b_band_dkv_a", **kw_a)
        dparts["a_q"], dg = _headnorm_bwd(dqa, (z, 512, unit("a_q")), gn["a_q"], p64, rope64, name="b_hn_aq")
        gsmall["a_qn"][l] = head_sum(dg, A_HEADS, A_HD)
        dparts["a_k"], dg = _headnorm_bwd(dka, (z, 128, unit("a_k")), gn["a_k"], p64, rope64, name="b_hn_ak")
        gsmall["a_kn"][l] = head_sum(dg, A_KV_HEADS, A_HD)
        dparts["a_v"] = dva
        vb = (sv["kv_up"], HPG * B_V, B_HEADS * LANES // (HPG * B_V))
        dl_b, _ = _delta(do_b, sv["o_b"], p64, name="b_delta_b")
        kw_b = dict(dv=B_V, causal=True, scale=scale_b)
        dqb, got = _attn_dq(sv["qb"], sv["kb"], vb, do_b, sv["lse_b"], dl_b, name="b_flash_dq_b",
                             xchg=srcs(carried), **kw_b)
        store(carried, got)
        (dkb, dvb), got = _attn_dkv(sv["qb"], sv["kb"], vb, do_b, sv["lse_b"], dl_b, name="b_flash_dkv_b",
                                     xchg=srcs(mlp_items), **kw_b)
        store(mlp_items, got)
        d_qup, dgq = _headnorm_bwd(dqb, (sv["q_up"], B_HEADS * LANES, 0), gn["b_q"], pq, rope_q, name="b_hn_bq")
        d_kn, dgk = _headnorm_bwd(dkb, (sv["kv_up"], B_HEADS * LANES, 0), gn["b_k"], p64, None, name="b_hn_bk")
        dkp = _kp_reduce(dkb, name="b_kp_reduce")
        dparts["b_kr"], dgkr = _headnorm_bwd(dkp, (z, LANES, unit("b_kr")), gn["b_kr"], pkr, rope_k, name="b_hn_kr")
        gsmall["b_qn"][l] = head_sum(dgq, B_HEADS, LANES)[:B_NOPE + B_ROPE]
        gsmall["b_kn"][l] = jnp.concatenate([head_sum(dgk, B_HEADS, LANES)[:B_NOPE], dgkr[0, :B_ROPE]])
        d_kvup = jnp.concatenate([d_kn, dvb], axis=1)
        g_ukv = _mm(sv["ckv"], d_kvup, ta=True, name="b_mm_wukv")
        dckv = _mm(d_kvup, wl["ukv"], tb=True, name="b_mm_dckv")
        dparts["b_ckv"], dg = _rmsnorm_bwd(dckv, (z, B_KV_LORA, unit("b_ckv")), w["b_kva_norm"][l], out_dtype=BF16,
                                          name="b_norm_ckv")
        gsmall["b_kva_norm"][l] = dg[0]
        g_uq = _mm(sv["cq"], d_qup, ta=True, name="b_mm_wuq")
        dcq = _mm(d_qup, wl["uq"], tb=True, name="b_mm_dcq")
        dparts["b_cq"], dg = _rmsnorm_bwd(dcq, (z, B_Q_LORA, unit("b_cq")), w["b_qa_norm"][l], out_dtype=BF16,
                                         name="b_norm_cq")
        gsmall["b_qa_norm"][l] = dg[0]
        g_wuq = g_uq.reshape(B_Q_LORA, B_HEADS, LANES)[:, :, :B_NOPE + B_ROPE].reshape(B_Q_LORA, -1)
        g_wukv = jnp.concatenate(
            [g_ukv[:, :B_HEADS * LANES].reshape(B_KV_LORA, B_HEADS, LANES)[:, :, :B_NOPE],
             g_ukv[:, B_HEADS * LANES:].reshape(B_KV_LORA, B_HEADS, B_V)], axis=-1).reshape(B_KV_LORA, -1)
        do_cg, dl_cg = _combine_bwd(do_c, sv["o_c"], sv["lse_cg"], p64, name="b_combine")
        dcq_n, dck_n = [], []
        for g, (win, dil) in enumerate(C_PATTERNS):
            vc = (z, 512, unit("c%d" % (3 * g + 2)))
            kw_c = dict(dil=dil, group=1, maxdist=win // dil)
            args = (sv["qc"][g], sv["kc"][g], vc, do_cg[g], sv["lse_cg"][g], dl_cg[g])
            dq = _band_dq(*args, name="b_band_dq_c%d" % g, **kw_c)
            dk, dv_ = _band_dkv(*args, name="b_band_dkv_c%d" % g, **kw_c)
            dparts["c%d" % (3 * g)], dg = _headnorm_bwd(dq, (z, 512, unit("c%d" % (3 * g))), gn["c_q"][g], p64, rope64,
                                                        name="b_hn_cq%d" % g)
            dcq_n.append(head_sum(dg, C_HEADS, C_HD))
            dparts["c%d" % (3 * g + 1)], dg = _headnorm_bwd(dk, (z, 512, unit("c%d" % (3 * g + 1))), gn["c_k"][g], p64,
                                                            rope64, name="b_hn_ck%d" % g)
            dck_n.append(head_sum(dg, C_HEADS, C_HD))
            dparts["c%d" % (3 * g + 2)] = dv_
        gsmall["c_qn"][l] = jnp.stack(dcq_n)
        gsmall["c_kn"][l] = jnp.stack(dck_n)
        mv = (sv["mkv"], HPG * M_HD, M_HEADS * M_HD // (HPG * M_HD))
        dl_m, _ = _delta(do_m, sv["o_m"], p128, name="b_delta_m")
        kw_m = dict(dv=M_HD, causal=False, scale=scale_m)
        dmq = _attn_dq(sv["mq"], sv["mk"], mv, do_m, sv["lse_m"], dl_m, name="b_flash_dq_m", **kw_m)
        dmk, dmv = _attn_dkv(sv["mq"], sv["mk"], mv, do_m, sv["lse_m"], dl_m, name="b_flash_dkv_m", **kw_m)
        dparts["m_q"], dg = _headnorm_bwd(dmq, (z, M_HEADS * M_HD, unit("m_q")), gn["m_q"], p128, None, name="b_hn_mq")
        gsmall["m_qn"][l] = head_sum(dg, M_HEADS, M_HD)
        d_mk, dg = _headnorm_bwd(dmk, (sv["mkv"], M_HEADS * M_HD, 0), gn["m_k"], p128, None, name="b_hn_mk")
        gsmall["m_kn"][l] = head_sum(dg, M_HEADS, M_HD)
        d_mkv = jnp.concatenate([d_mk, dmv], axis=1)
        g_wmkv = _mm(sv["memn"], d_mkv, ta=True, name="b_mm_wmkv")
        dmemn = _mm(d_mkv, wl["m_w_kv"], tb=True, name="b_mm_dmemn")
        _, dg = _rmsnorm_bwd(dmemn, (mem, d_model, 0), w["m_g_mem"][l], name="b_norm_mem")
        gsmall["m_g_mem"][l] = dg[0]
        dz = jnp.concatenate([dparts[n] for n in order], axis=1)
        bm_items = [item("b_w_ukv", l, g_wukv), item("b_w_uq", l, g_wuq), item("m_w_kv", l, g_wmkv)]
        g_in, got = _mm(sv["h"], dz, ta=True, name="b_mm_win", xchg=srcs(bm_items))
        store(bm_items, got)
        carried = [item("w_in", l, _unpermute_cols(g_in, lay, orig_names))]
        last_items = carried if l == 0 else []
        dh, got = _mm(dz, wl["w_in"], tb=True, name="b_mm_dh", xchg=srcs(last_items))
        store(last_items, got)
        dx, dg = _rmsnorm_bwd(dh, (sv["x"], d_model, 0), w["g_mix"][l], dres=dx2, name="b_norm_mix")
        gsmall["g_mix"][l] = dg[0]

    out_g, out_d, out_m, out_v = {}, {}, {}, {}
    for name in BIG:
        shard = w[name].shape
        two = lambda a: a.reshape(-1, shard[-1])
        pieces = [r.reshape(N_DEV, -1, shard[-1]) for r in recvs[name]]
        g, dlt, mn, vn = _adam(pieces, two(w[name]), two(moments_m[name]), two(moments_v[name]), name="adam_" + name)
        out_g[name], out_d[name], out_m[name], out_v[name] = (a.reshape(shard) for a in (g, dlt, mn, vn))
    sizes = [int(np.prod(w[n].shape)) for n in SMALL]
    total = sum(sizes)
    padded = -(-total // (8 * LANES)) * (8 * LANES)
    pack = lambda d: jnp.pad(jnp.concatenate([jnp.reshape(d[n], (-1,)) for n in SMALL]), (0, padded - total)).reshape(-1, LANES)
    gpack = pack({n: jnp.stack(gsmall[n]) for n in SMALL})
    recv = _exchange_many([(gpack, False)], name="ag_small_grads")[0]
    packed = _adam([recv], pack(w), pack(moments_m), pack(moments_v), name="adam_small")
    offs = np.cumsum([0] + sizes)
    for a, out in zip(packed, (out_g, out_d, out_m, out_v)):
        flat = a.reshape(-1)
        for n, o, sz in zip(SMALL, offs[:-1], sizes):
            out[n] = flat[o:o + sz].reshape(w[n].shape)
    return loss, dx, out_g, out_d, out_m, out_v


def kernel(x, mem, positions, g_mix, w_in, b_gate, a_qn, a_kn, a_sink, b_qa_norm, b_kva_norm, b_w_uq, b_w_ukv, b_qn, b_kn, c_qn, c_kn, m_g_mem, m_w_kv, m_qn, m_kn, w_branch, w_out, g_mlp, w_up, w_down, loss_target, m_g_mix, m_w_in, m_b_gate, m_a_qn, m_a_kn, m_a_sink, m_b_qa_norm, m_b_kva_norm, m_b_w_uq, m_b_w_ukv, m_b_qn, m_b_kn, m_c_qn, m_c_kn, m_m_g_mem, m_m_w_kv, m_m_qn, m_m_kn, m_w_branch, m_w_out, m_g_mlp, m_w_up, m_w_down, v_g_mix, v_w_in, v_b_gate, v_a_qn, v_a_kn, v_a_sink, v_b_qa_norm, v_b_kva_norm, v_b_w_uq, v_b_w_ukv, v_b_qn, v_b_kn, v_c_qn, v_c_kn, v_m_g_mem, v_m_w_kv, v_m_qn, v_m_kn, v_w_branch, v_w_out, v_g_mlp, v_w_up, v_w_down):
    w = dict(g_mix=g_mix, w_in=w_in, b_gate=b_gate, a_qn=a_qn, a_kn=a_kn, a_sink=a_sink, b_qa_norm=b_qa_norm,
             b_kva_norm=b_kva_norm, b_w_uq=b_w_uq, b_w_ukv=b_w_ukv, b_qn=b_qn, b_kn=b_kn, c_qn=c_qn, c_kn=c_kn,
             m_g_mem=m_g_mem, m_w_kv=m_w_kv, m_qn=m_qn, m_kn=m_kn, w_branch=w_branch, w_out=w_out, g_mlp=g_mlp,
             w_up=w_up, w_down=w_down)
    mm_ = dict(g_mix=m_g_mix, w_in=m_w_in, b_gate=m_b_gate, a_qn=m_a_qn, a_kn=m_a_kn, a_sink=m_a_sink,
               b_qa_norm=m_b_qa_norm, b_kva_norm=m_b_kva_norm, b_w_uq=m_b_w_uq, b_w_ukv=m_b_w_ukv, b_qn=m_b_qn,
               b_kn=m_b_kn, c_qn=m_c_qn, c_kn=m_c_kn, m_g_mem=m_m_g_mem, m_w_kv=m_m_w_kv, m_qn=m_m_qn, m_kn=m_m_kn,
               w_branch=m_w_branch, w_out=m_w_out, g_mlp=m_g_mlp, w_up=m_w_up, w_down=m_w_down)
    vv = dict(g_mix=v_g_mix, w_in=v_w_in, b_gate=v_b_gate, a_qn=v_a_qn, a_kn=v_a_kn, a_sink=v_a_sink,
              b_qa_norm=v_b_qa_norm, b_kva_norm=v_b_kva_norm, b_w_uq=v_b_w_uq, b_w_ukv=v_b_w_ukv, b_qn=v_b_qn,
              b_kn=v_b_kn, c_qn=v_c_qn, c_kn=v_c_kn, m_g_mem=v_m_g_mem, m_w_kv=v_m_w_kv, m_qn=v_m_qn, m_kn=v_m_kn,
              w_branch=v_w_branch, w_out=v_w_out, g_mlp=v_g_mlp, w_up=v_w_up, w_down=v_w_down)
    loss, dx, g, dlt, mn, vn = _step(x[0], mem[0], positions[0], w, mm_, vv, loss_target[0])
    return (loss, dx[None], *[g[n] for n in WEIGHTS], *[dlt[n] for n in WEIGHTS], *[mn[n] for n in WEIGHTS],
            *[vn[n] for n in WEIGHTS])
```

```python
import functools

import numpy as np
import jax
import jax.numpy as jnp
from jax import lax
from jax.experimental import pallas as pl
from jax.experimental.pallas import tpu as pltpu

F32 = jnp.float32
BF16 = jnp.bfloat16

LANES = 128
N_DEV = 8
AXES = ("x", "y", "c")
VMEM_LIMIT = 48 * 1024 * 1024

EPS = 1e-6
NEG = -1e30
ROPE_THETA = 10000.0
BLOCK = 128

A_HEADS, A_KV_HEADS, A_HD, A_WINDOW = 8, 2, 64, 128
B_HEADS, B_Q_LORA, B_KV_LORA, B_NOPE, B_ROPE, B_V = 8, 384, 256, 64, 32, 64
C_PATTERNS = ((128, 1), (512, 4), (2048, 16))
C_HEADS, C_HD = 8, 64
M_HEADS, M_HD = 4, 128
BRANCH_W = 512
N_BRANCH = 4

ADAM_LR, ADAM_B1, ADAM_B2, ADAM_EPS, ADAM_WD, ADAM_STEP = 0.001, 0.9, 0.999, 1e-08, 0.01, 10

NT_DIMS = (((1,), (1,)), ((), ()))
TN_DIMS = (((0,), (0,)), ((), ()))
NN_DIMS = (((1,), (0,)), ((), ()))


def _pick(n, cands):
    for c in cands:
        if n % c == 0:
            return c
    return n


def _cparams(sem):
    return pltpu.CompilerParams(dimension_semantics=sem, vmem_limit_bytes=VMEM_LIMIT)


def _dot(a, b, dims=NN_DIMS):
    return lax.dot_general(a.astype(BF16), b.astype(BF16), dims, preferred_element_type=F32)


def _in_layout(d_model):
    orig = ([("a_q", 512), ("a_k", 128), ("a_v", 128), ("b_cq", B_Q_LORA), ("b_ckv", B_KV_LORA), ("b_kr", B_ROPE)]
            + [("c%d" % i, 512) for i in range(9)] + [("m_q", 512), ("gate", N_BRANCH * d_model)])
    order = ["gate", "a_q"] + ["c%d" % i for i in range(9)] + ["m_q", "b_ckv", "b_cq", "a_k", "a_v", "b_kr"]
    o_off, off = {}, 0
    for name, w in orig:
        o_off[name] = (off, w)
        off += w
    lay, noff = {}, 0
    for name in order:
        oo, w = o_off[name]
        nw = LANES if name == "b_kr" else w
        assert noff % nw == 0, (name, noff, nw)
        lay[name] = dict(orig=oo, w=w, off=noff, nw=nw)
        noff += nw
    assert noff % 512 == 0
    return lay, [n for n, _ in orig], order, off, noff


def _permute_cols(w, lay, order):
    cols = []
    for name in order:
        e = lay[name]
        p = w[..., e["orig"]:e["orig"] + e["w"]]
        if e["nw"] != e["w"]:
            p = jnp.pad(p, [(0, 0)] * (w.ndim - 1) + [(0, e["nw"] - e["w"])])
        cols.append(p)
    return jnp.concatenate(cols, axis=-1)


def _unpermute_cols(w, lay, orig_names):
    return jnp.concatenate([w[..., lay[n]["off"]:lay[n]["off"] + lay[n]["w"]] for n in orig_names], axis=-1)


def _pattern(groups):
    ids = np.repeat(np.arange(len(groups)), groups)
    assert ids.shape[0] == LANES
    gmat = (ids[:, None] == ids[None, :]).astype(np.float32)
    invn = (1.0 / np.asarray(groups, np.float32))[ids][None, :]
    return jnp.asarray(gmat, BF16), jnp.asarray(invn, F32)


def _gsum(v, gmat):
    hi = v.astype(BF16)
    r1 = v - hi.astype(F32)
    mid = r1.astype(BF16)
    lo = (r1 - mid.astype(F32)).astype(BF16)
    dot = lambda p: jnp.dot(p, gmat, preferred_element_type=F32)
    return dot(hi) + dot(mid) + dot(lo)


def _swap_half(v, lomask, half):
    return jnp.where(lomask > 0.5, pltpu.roll(v, LANES - half, 1), pltpu.roll(v, half, 1))


def _rows_spec(tr, width, off=0):
    return pl.BlockSpec((tr, width), lambda i: (i, off))


def _layer_arg(per_layer):
    arr, l = per_layer
    nd = arr.ndim - 1
    return pl.BlockSpec((None,) + tuple(arr.shape[1:]), lambda *_: (l,) + (0,) * nd), arr


def _full_spec(shape):
    return pl.BlockSpec(shape, lambda i: (0,) * len(shape))


def _mm(a, b, *, ta=False, tb=False, out_dtype=F32, mode=None, extra=None, name, xchg=None):
    if ta:
        kd, m = a.shape
    else:
        m, kd = a.shape
    if tb:
        n, kb = b.shape
    else:
        kb, n = b.shape
    assert kd == kb, (a.shape, b.shape, ta, tb)
    tm = _pick(m, (1024, 512, 256, 128))
    tn = _pick(n, (1024, 1536, 512, 384, 256, 128))
    tk = _pick(kd, (1024, 512, 384, 256, 128))
    nk = kd // tk
    a_spec = (pl.BlockSpec((tk, tm), lambda i, j, k: (k, i)) if ta else pl.BlockSpec((tm, tk), lambda i, j, k: (i, k)))
    b_spec = (pl.BlockSpec((tn, tk), lambda i, j, k: (j, k)) if tb else pl.BlockSpec((tk, tn), lambda i, j, k: (k, j)))
    o_spec = pl.BlockSpec((tm, tn), lambda i, j, k: (i, j))
    dims = (((0 if ta else 1,), (1 if tb else 0,)), ((), ()))
    has_extra = mode in ("add", "drelu2")

    def body(*refs):
        a_ref, b_ref = refs[0], refs[1]
        e_ref = refs[2] if has_extra else None
        outs = refs[2 + has_extra:2 + has_extra + (2 if mode == "relu2" else 1)]
        k = pl.program_id(2)
        part = _dot(a_ref[...], b_ref[...], dims)

        def finish(r):
            if mode == "add":
                outs[0][...] = (e_ref[...] + r).astype(out_dtype)
            elif mode == "relu2":
                outs[0][...] = r
                outs[1][...] = jnp.square(jnp.maximum(r, 0.0)).astype(BF16)
            elif mode == "drelu2":
                outs[0][...] = (r * (2.0 * jnp.maximum(e_ref[...], 0.0))).astype(out_dtype)
            else:
                outs[0][...] = r.astype(out_dtype)

        if nk == 1:
            finish(part)
        else:
            acc = refs[-1]

            @pl.when(k == 0)
            def _():
                acc[...] = part

            @pl.when((k > 0) & (k < nk - 1))
            def _():
                acc[...] += part

            @pl.when(k == nk - 1)
            def _():
                finish(acc[...] + part)

    in_specs = [a_spec, b_spec] + ([o_spec] if has_extra else [])
    args = [a, b] + ([extra] if has_extra else [])
    if mode == "relu2":
        out_shape = (jax.ShapeDtypeStruct((m, n), F32), jax.ShapeDtypeStruct((m, n), BF16))
        out_specs = (o_spec, o_spec)
    else:
        out_shape = jax.ShapeDtypeStruct((m, n), out_dtype)
        out_specs = o_spec
    return _pcall(
        body, grid=(m // tm, n // tn, nk), in_specs=in_specs, out_specs=out_specs, out_shape=out_shape,
        scratch_shapes=[pltpu.VMEM((tm, tn), F32)] if nk > 1 else [], sem=("parallel", "parallel", "arbitrary"),
        name=name, args=args, xchg=xchg)


def _rmsnorm_fwd(xv, gain, *, name, tr=512):
    x, width, off = xv
    rows = x.shape[0]
    tr = min(tr, rows)

    def body(x_ref, g_ref, o_ref):
        xf = x_ref[...]
        rstd = lax.rsqrt(jnp.mean(xf * xf, axis=-1, keepdims=True) + EPS)
        o_ref[...] = (xf * rstd * g_ref[...]).astype(BF16)

    g_spec, g_arr = _layer_arg(gain)
    return pl.pallas_call(
        body, grid=(rows // tr,), in_specs=[_rows_spec(tr, width, off), g_spec],
        out_specs=_rows_spec(tr, width), out_shape=jax.ShapeDtypeStruct((rows, width), BF16),
        compiler_params=_cparams(("parallel",)), name=name)(x, g_arr)


def _rmsnorm_bwd(dh, xv, gain, *, dres=None, out_dtype=F32, name, tr=512):
    x, width, off = xv
    rows = x.shape[0]
    tr = min(tr, rows)
    has_res = dres is not None

    def body(*refs):
        dh_ref, x_ref, g_ref = refs[:3]
        r_ref = refs[3] if has_res else None
        dx_ref, dg_ref = refs[3 + has_res:]
        i = pl.program_id(0)
        xf = x_ref[...]
        rstd = lax.rsqrt(jnp.mean(xf * xf, axis=-1, keepdims=True) + EPS)
        xh = xf * rstd
        d = dh_ref[...]
        dxh = d * g_ref[...]
        dx = rstd * (dxh - xh * jnp.mean(dxh * xh, axis=-1, keepdims=True))
        if has_res:
            dx = dx + r_ref[...]
        dx_ref[...] = dx.astype(out_dtype)

        @pl.when(i == 0)
        def _():
            dg_ref[...] = jnp.zeros_like(dg_ref)

        dg_ref[...] += jnp.sum(d * xh, axis=0, keepdims=True)

    g_spec, g_arr = _layer_arg(gain)
    in_specs = [_rows_spec(tr, width), _rows_spec(tr, width, off), g_spec]
    args = [dh, x, g_arr]
    if has_res:
        in_specs.append(_rows_spec(tr, width))
        args.append(dres)
    return pl.pallas_call(
        body, grid=(rows // tr,), in_specs=in_specs,
        out_specs=(_rows_spec(tr, width), _full_spec((1, width))),
        out_shape=(jax.ShapeDtypeStruct((rows, width), out_dtype), jax.ShapeDtypeStruct((1, width), F32)),
        compiler_params=_cparams(("arbitrary",)), name=name)(*args)


def _headnorm_fwd(xv, gain_row, pat, rope, *, addend=None, name, tr=512):
    x, width, off = xv
    rows = x.shape[0]
    tr = min(tr, rows)
    gmat, invn = pat
    nblk = width // LANES
    has_rope = rope is not None
    has_add = addend is not None
    half = rope[3] if has_rope else 0

    def body(*refs):
        x_ref, g_ref, gm_ref, in_ref = refs[:4]
        pos = 4
        if has_rope:
            c_ref, s_ref, lo_ref = refs[pos:pos + 3]
            pos += 3
        if has_add:
            a_ref = refs[pos]
            pos += 1
        o_ref = refs[pos]
        gm = gm_ref[...]
        inv = in_ref[...]
        if has_add:
            add = pltpu.roll(a_ref[...].astype(F32), 64, 1)
        for j in range(nblk):
            sl = slice(j * LANES, (j + 1) * LANES)
            xb = x_ref[:, sl]
            rstd = lax.rsqrt(_gsum(xb * xb, gm) * inv + EPS)
            y = xb * rstd * g_ref[:, sl]
            if has_rope:
                y = y * c_ref[...] + _swap_half(y, lo_ref[...], half) * s_ref[...]
            if has_add:
                y = y + add
            o_ref[:, sl] = y.astype(BF16)

    g_spec, g_arr = _layer_arg(gain_row)
    in_specs = [_rows_spec(tr, width, off), g_spec, _full_spec((LANES, LANES)), _full_spec((1, LANES))]
    args = [x, g_arr, gmat, invn]
    if has_rope:
        in_specs += [_rows_spec(tr, LANES), _rows_spec(tr, LANES), _full_spec((1, LANES))]
        args += [rope[0], rope[1], rope[2]]
    if has_add:
        in_specs.append(_rows_spec(tr, LANES))
        args.append(addend)
    return pl.pallas_call(
        body, grid=(rows // tr,), in_specs=in_specs, out_specs=_rows_spec(tr, width),
        out_shape=jax.ShapeDtypeStruct((rows, width), BF16),
        compiler_params=_cparams(("parallel",)), name=name)(*args)


def _headnorm_bwd(dy, xv, gain_row, pat, rope, *, name, tr=512):
    x, width, off = xv
    rows = x.shape[0]
    tr = min(tr, rows)
    gmat, invn = pat
    nblk = width // LANES
    has_rope = rope is not None
    half = rope[3] if has_rope else 0

    def body(*refs):
        dy_ref, x_ref, g_ref, gm_ref, in_ref = refs[:5]
        pos = 5
        if has_rope:
            c_ref, s_ref, lo_ref = refs[pos:pos + 3]
            pos += 3
        dx_ref, dg_ref = refs[pos:]
        i = pl.program_id(0)
        gm = gm_ref[...]
        inv = in_ref[...]

        @pl.when(i == 0)
        def _():
            dg_ref[...] = jnp.zeros_like(dg_ref)

        for j in range(nblk):
            sl = slice(j * LANES, (j + 1) * LANES)
            d = dy_ref[:, sl]
            if has_rope:
                d = d * c_ref[...] + _swap_half(d * s_ref[...], lo_ref[...], half)
            xb = x_ref[:, sl]
            rstd = lax.rsqrt(_gsum(xb * xb, gm) * inv + EPS)
            xh = xb * rstd
            dg_ref[:, sl] += jnp.sum(d * xh, axis=0, keepdims=True)
            dxh = d * g_ref[:, sl]
            dx = rstd * (dxh - xh * (_gsum(dxh * xh, gm) * inv))
            dx_ref[:, sl] = dx.astype(BF16)

    g_spec, g_arr = _layer_arg(gain_row)
    in_specs = [_rows_spec(tr, width), _rows_spec(tr, width, off), g_spec,
                _full_spec((LANES, LANES)), _full_spec((1, LANES))]
    args = [dy, x, g_arr, gmat, invn]
    if has_rope:
        in_specs += [_rows_spec(tr, LANES), _rows_spec(tr, LANES), _full_spec((1, LANES))]
        args += [rope[0], rope[1], rope[2]]
    return pl.pallas_call(
        body, grid=(rows // tr,), in_specs=in_specs,
        out_specs=(_rows_spec(tr, width), _full_spec((1, width))),
        out_shape=(jax.ShapeDtypeStruct((rows, width), BF16), jax.ShapeDtypeStruct((1, width), F32)),
        compiler_params=_cparams(("arbitrary",)), name=name)(*args)


def _merge_fwd(zv, b_gate, ys, *, name, tr=256):
    z, gw, off = zv
    rows, d = ys[0].shape
    tr = min(tr, rows)

    def body(z_ref, b_ref, y0, y1, y2, y3, t_ref):
        acc = None
        for n, y_ref in enumerate((y0, y1, y2, y3)):
            sl = slice(n * d, (n + 1) * d)
            term = jax.nn.sigmoid(z_ref[:, sl] + b_ref[:, sl]) * y_ref[...]
            acc = term if acc is None else acc + term
        t_ref[...] = acc.astype(BF16)

    b_spec, b_arr = _layer_arg(b_gate)
    return pl.pallas_call(
        body, grid=(rows // tr,),
        in_specs=[_rows_spec(tr, gw, off), b_spec] + [_rows_spec(tr, d)] * 4,
        out_specs=_rows_spec(tr, d), out_shape=jax.ShapeDtypeStruct((rows, d), BF16),
        compiler_params=_cparams(("parallel",)), name=name)(z, b_arr, *ys)


def _merge_bwd(dt, zv, b_gate, ys, *, name, tr=256):
    z, gw, off = zv
    rows, d = dt.shape
    tr = min(tr, rows)

    def body(dt_ref, z_ref, b_ref, y0, y1, y2, y3, d0, d1, d2, d3, dg_ref, db_ref):
        i = pl.program_id(0)

        @pl.when(i == 0)
        def _():
            db_ref[...] = jnp.zeros_like(db_ref)

        dtv = dt_ref[...]
        for n, (y_ref, d_ref) in enumerate(((y0, d0), (y1, d1), (y2, d2), (y3, d3))):
            sl = slice(n * d, (n + 1) * d)
            gate = jax.nn.sigmoid(z_ref[:, sl] + b_ref[:, sl])
            d_ref[...] = (dtv * gate).astype(BF16)
            dpre = dtv * y_ref[...] * (gate * (1.0 - gate))
            dg_ref[:, sl] = dpre.astype(BF16)
            db_ref[:, sl] += jnp.sum(dpre, axis=0, keepdims=True)

    b_spec, b_arr = _layer_arg(b_gate)
    outs = pl.pallas_call(
        body, grid=(rows // tr,),
        in_specs=[_rows_spec(tr, d), _rows_spec(tr, gw, off), b_spec] + [_rows_spec(tr, d)] * 4,
        out_specs=tuple([_rows_spec(tr, d)] * 4 + [_rows_spec(tr, gw), _full_spec((1, gw))]),
        out_shape=tuple([jax.ShapeDtypeStruct((rows, d), BF16)] * 4
                        + [jax.ShapeDtypeStruct((rows, gw), BF16), jax.ShapeDtypeStruct((1, gw), F32)]),
        compiler_params=_cparams(("arbitrary",)), name=name)(dt, z, b_arr, *ys)
    return outs[:4], outs[4], outs[5]


def _combine_fwd(os_, lses, *, name, tr=512):
    rows, width = os_[0].shape

    def body(o0, o1, o2, l0, l1, l2, out_ref):
        la, lb, lc = l0[...], l1[...], l2[...]
        mx = jnp.maximum(jnp.maximum(la, lb), lc)
        ea, eb, ec = jnp.exp(la - mx), jnp.exp(lb - mx), jnp.exp(lc - mx)
        inv = 1.0 / (ea + eb + ec)
        out_ref[...] = (ea * o0[...] + eb * o1[...] + ec * o2[...]) * inv

    return pl.pallas_call(
        body, grid=(rows // tr,), in_specs=[_rows_spec(tr, width)] * 6, out_specs=_rows_spec(tr, width),
        out_shape=jax.ShapeDtypeStruct((rows, width), F32),
        compiler_params=_cparams(("parallel",)), name=name)(*os_, *lses)


def _combine_bwd(do, o, lses, pat, *, name, tr=512):
    rows, width = do.shape
    gmat, _ = pat
    nblk = width // LANES

    def body(do_ref, o_ref, l0, l1, l2, gm_ref, d0, d1, d2, e0, e1, e2):
        gm = gm_ref[...]
        for j in range(nblk):
            sl = slice(j * LANES, (j + 1) * LANES)
            la, lb, lc = l0[:, sl], l1[:, sl], l2[:, sl]
            mx = jnp.maximum(jnp.maximum(la, lb), lc)
            ea, eb, ec = jnp.exp(la - mx), jnp.exp(lb - mx), jnp.exp(lc - mx)
            inv = 1.0 / (ea + eb + ec)
            dov = do_ref[:, sl]
            tot = _gsum(dov * o_ref[:, sl], gm)
            for e, d_ref, e_ref in ((ea, d0, e0), (eb, d1, e1), (ec, d2, e2)):
                w = e * inv
                d_ref[:, sl] = w * dov
                e_ref[:, sl] = w * tot

    outs = pl.pallas_call(
        body, grid=(rows // tr,),
        in_specs=[_rows_spec(tr, width)] * 5 + [_full_spec((LANES, LANES))],
        out_specs=tuple([_rows_spec(tr, width)] * 6),
        out_shape=tuple([jax.ShapeDtypeStruct((rows, width), F32)] * 6),
        compiler_params=_cparams(("parallel",)), name=name)(do, o, *lses, gmat)
    return outs[:3], outs[3:]


def _delta(do, o, pat, *, lse=None, sink_row=None, name, tr=512):
    rows, width = do.shape
    gmat, _ = pat
    nblk = width // LANES
    has_sink = sink_row is not None

    def body(*refs):
        do_ref, o_ref, gm_ref = refs[:3]
        if has_sink:
            l_ref, s_ref, dl_ref, ds_ref = refs[3:]
        else:
            dl_ref = refs[3]
        gm = gm_ref[...]
        if has_sink:
            @pl.when(pl.program_id(0) == 0)
            def _():
                ds_ref[...] = jnp.zeros_like(ds_ref)
        for j in range(nblk):
            sl = slice(j * LANES, (j + 1) * LANES)
            dl = _gsum(do_ref[:, sl] * o_ref[:, sl], gm)
            dl_ref[:, sl] = dl
            if has_sink:
                ds_ref[:, sl] += jnp.sum(-jnp.exp(s_ref[:, sl] - l_ref[:, sl]) * dl, axis=0, keepdims=True)

    in_specs = [_rows_spec(tr, width), _rows_spec(tr, width), _full_spec((LANES, LANES))]
    args = [do, o, gmat]
    out_specs = [_rows_spec(tr, width)]
    out_shape = [jax.ShapeDtypeStruct((rows, width), F32)]
    if has_sink:
        s_spec, s_arr = _layer_arg(sink_row)
        in_specs += [_rows_spec(tr, width), s_spec]
        args += [lse, s_arr]
        out_specs.append(_full_spec((1, width)))
        out_shape.append(jax.ShapeDtypeStruct((1, width), F32))
    outs = pl.pallas_call(
        body, grid=(rows // tr,), in_specs=in_specs, out_specs=tuple(out_specs), out_shape=tuple(out_shape),
        compiler_params=_cparams(("arbitrary",)), name=name)(*args)
    return outs if has_sink else (outs[0], None)


def _kp_reduce(dk, *, name, tr=512):
    rows, width = dk.shape
    nblk = width // LANES

    def body(dk_ref, o_ref):
        acc = dk_ref[:, 0:LANES]
        for j in range(1, nblk):
            acc = acc + dk_ref[:, j * LANES:(j + 1) * LANES]
        lane = lax.broadcasted_iota(jnp.int32, acc.shape, 1)
        o_ref[...] = jnp.where(lane < B_ROPE, pltpu.roll(acc, 64, 1), 0.0)

    return pl.pallas_call(
        body, grid=(rows // tr,), in_specs=[_rows_spec(tr, width)], out_specs=_rows_spec(tr, LANES),
        out_shape=jax.ShapeDtypeStruct((rows, LANES), F32),
        compiler_params=_cparams(("parallel",)), name=name)(dk)


def _loss_head(y, target, *, name, tr=512):
    rows, d = y.shape

    def body(y_ref, t_ref, dy_ref, l_ref):
        @pl.when(pl.program_id(0) == 0)
        def _():
            l_ref[...] = jnp.zeros_like(l_ref)

        diff = y_ref[...] - t_ref[...]
        dy_ref[...] = diff * (1.0 / d)
        part = 0.5 * jnp.sum(jnp.mean(diff * diff, axis=-1, keepdims=True), axis=0, keepdims=True)
        l_ref[...] += jnp.broadcast_to(part, l_ref.shape)

    return pl.pallas_call(
        body, grid=(rows // tr,), in_specs=[_rows_spec(tr, d)] * 2,
        out_specs=(_rows_spec(tr, d), _full_spec((1, LANES))),
        out_shape=(jax.ShapeDtypeStruct((rows, d), F32), jax.ShapeDtypeStruct((1, LANES), F32)),
        compiler_params=_cparams(("arbitrary",)), name=name)(y, target)


def _adam(recvs, w, m, v, *, name):
    nl = len(recvs)
    rows_l, cols = recvs[0].shape[1:]
    assert w.shape == (nl * rows_l, cols)
    tr = _pick(rows_l, (128, 64, 32, 16, 8))
    nt = rows_l // tr
    c1 = 1.0 / (1.0 - ADAM_B1 ** ADAM_STEP)
    c2 = 1.0 / (1.0 - ADAM_B2 ** ADAM_STEP)

    def body(*refs):
        r_refs = refs[:nl]
        w_ref, m_ref, v_ref, g_out, d_out, m_out, v_out = refs[nl:]
        layer = pl.program_id(0)

        def update(r_ref):
            g = r_ref[0].astype(F32)
            for j in range(1, N_DEV):
                g = g + r_ref[j].astype(F32)
            mn = ADAM_B1 * m_ref[...] + (1.0 - ADAM_B1) * g
            vn = ADAM_B2 * v_ref[...] + (1.0 - ADAM_B2) * (g * g)
            g_out[...] = g
            m_out[...] = mn
            v_out[...] = vn
            d_out[...] = -ADAM_LR * ((mn * c1) / (jnp.sqrt(vn * c2) + ADAM_EPS) + ADAM_WD * w_ref[...])

        for k in range(nl):
            pl.when(layer == k)(functools.partial(update, r_refs[k]))

    spec = pl.BlockSpec((tr, cols), lambda l, i: (l * nt + i, 0))
    rspec = lambda k: pl.BlockSpec((N_DEV, tr, cols), lambda l, i: (0, jnp.where(l == k, i, 0), 0))
    sds = jax.ShapeDtypeStruct(w.shape, F32)
    return pl.pallas_call(
        body, grid=(nl, nt), in_specs=[rspec(k) for k in range(nl)] + [spec, spec, spec],
        out_specs=(spec,) * 4, out_shape=(sds,) * 4,
        compiler_params=_cparams(("arbitrary", "arbitrary")), name=name)(*recvs, w, m, v)


def _band_mask(i, maxdist, nrow_blocks):
    row = lax.broadcasted_iota(jnp.int32, (BLOCK, 2 * BLOCK), 0)
    col = lax.broadcasted_iota(jnp.int32, (BLOCK, 2 * BLOCK), 1)
    dist = row - col + BLOCK
    return (dist >= 0) & (dist <= maxdist) & ((i > 0) | (col >= BLOCK))


def _band_views(q, k, zv, dil):
    s = q.shape[0]
    z, kvw, voff = zv
    if dil == 1:
        return q, k, z, z.shape[1] // kvw, voff
    vd = z[:, voff * kvw:(voff + 1) * kvw].reshape(s // dil, dil * kvw)
    return q.reshape(s // dil, dil * q.shape[1]), k.reshape(s // dil, dil * k.shape[1]), vd, 1, 0


def _band_fwd(q, k, zv, *, dil, group, maxdist, sinks=None, name, xchg=()):
    s, qw = q.shape
    kvw = zv[1]
    qd, kd, zd, zblocks, voff = _band_views(q, k, zv, dil)
    nb = s // dil // BLOCK
    nheads = qw // 64
    scale = 64 ** -0.5
    has_sink = sinks is not None

    def body(*refs):
        q_ref, kp_ref, kc_ref, vp_ref, vc_ref = refs[:5]
        s_ref = refs[5] if has_sink else None
        o_ref, l_ref, sc_s, p_s = refs[5 + has_sink:]
        i = pl.program_id(1)
        mask = _band_mask(i, maxdist, nb)
        k2 = jnp.concatenate([kp_ref[...], kc_ref[...]], axis=0)
        v2 = jnp.concatenate([vp_ref[...], vc_ref[...]], axis=0).astype(BF16)
        for h in range(nheads):
            g = h // group
            sc_s[h] = _dot(q_ref[:, 64 * h:64 * h + 64], k2[:, 64 * g:64 * g + 64], NT_DIMS)
        sc = jnp.where(mask[None], sc_s[...] * scale, NEG)
        mx = jnp.max(sc, axis=2, keepdims=True)
        if has_sink:
            snk = s_ref[:, :, 0:1]
            mx = jnp.maximum(mx, snk)
        e = jnp.exp(sc - mx)
        den = jnp.sum(e, axis=2, keepdims=True)
        if has_sink:
            den = den + jnp.exp(snk - mx)
        p_s[...] = (e * (1.0 / den)).astype(BF16)
        lse = mx + jnp.log(den)
        for h in range(nheads):
            g = h // group
            o_ref[:, 64 * h:64 * h + 64] = _dot(p_s[h], v2[:, 64 * g:64 * g + 64])
            l_ref[:, 64 * h:64 * h + 64] = jnp.broadcast_to(lse[h], (BLOCK, 64))

    prev = lambda r, i: jnp.maximum(i - 1, 0)
    in_specs = [
        pl.BlockSpec((BLOCK, qw), lambda r, i: (i, r)),
        pl.BlockSpec((BLOCK, kvw), lambda r, i: (prev(r, i), r)),
        pl.BlockSpec((BLOCK, kvw), lambda r, i: (i, r)),
        pl.BlockSpec((BLOCK, kvw), lambda r, i: (prev(r, i), r * zblocks + voff)),
        pl.BlockSpec((BLOCK, kvw), lambda r, i: (i, r * zblocks + voff)),
    ]
    args = [qd, kd, kd, zd, zd]
    if has_sink:
        s_spec, s_arr = _layer_arg(sinks)
        in_specs.append(s_spec)
        args.append(s_arr)
    ospec = pl.BlockSpec((BLOCK, qw), lambda r, i: (i, r))
    sds = jax.ShapeDtypeStruct((s // dil, dil * qw), F32)
    (o, lse), got = _pcall(
        body, grid=(dil, nb), in_specs=in_specs, out_specs=(ospec, ospec), out_shape=(sds, sds),
        scratch_shapes=[pltpu.VMEM((nheads, BLOCK, 2 * BLOCK), F32), pltpu.VMEM((nheads, BLOCK, 2 * BLOCK), BF16)],
        sem=("parallel", "parallel"), name=name, args=args, xchg=xchg)
    return (o.reshape(s, qw), lse.reshape(s, qw)), got


def _band_dq(q, k, zv, do, lse, delta, *, dil, group, maxdist, name):
    s, qw = q.shape
    kvw = zv[1]
    qd, kd, zd, zblocks, voff = _band_views(q, k, zv, dil)
    rs = lambda t: t.reshape(s // dil, dil * qw)
    nb = s // dil // BLOCK
    nheads = qw // 64
    scale = 64 ** -0.5

    def body(q_ref, kp_ref, kc_ref, vp_ref, vc_ref, do_ref, l_ref, d_ref, dq_ref, sc_s, dp_s, ds_s, st_s):
        i = pl.program_id(1)
        mask = _band_mask(i, maxdist, nb)
        k2 = jnp.concatenate([kp_ref[...], kc_ref[...]], axis=0)
        v2 = jnp.concatenate([vp_ref[...], vc_ref[...]], axis=0).astype(BF16)
        dob = do_ref[...].astype(BF16)
        for h in range(nheads):
            g = h // group
            hs = slice(64 * h, 64 * h + 64)
            sc_s[h] = _dot(q_ref[:, hs], k2[:, 64 * g:64 * g + 64], NT_DIMS)
            dp_s[h] = _dot(dob[:, hs], v2[:, 64 * g:64 * g + 64], NT_DIMS)
            st_s[0, h] = l_ref[:, 64 * h:64 * h + 1]
            st_s[1, h] = d_ref[:, 64 * h:64 * h + 1]
        p = jnp.exp(jnp.where(mask[None], sc_s[...] * scale, NEG) - st_s[0])
        ds_s[...] = (p * (dp_s[...] - st_s[1]) * scale).astype(BF16)
        for h in range(nheads):
            g = h // group
            dq_ref[:, 64 * h:64 * h + 64] = _dot(ds_s[h], k2[:, 64 * g:64 * g + 64])

    prev = lambda r, i: jnp.maximum(i - 1, 0)
    qspec = pl.BlockSpec((BLOCK, qw), lambda r, i: (i, r))
    in_specs = [
        qspec,
        pl.BlockSpec((BLOCK, kvw), lambda r, i: (prev(r, i), r)),
        pl.BlockSpec((BLOCK, kvw), lambda r, i: (i, r)),
        pl.BlockSpec((BLOCK, kvw), lambda r, i: (prev(r, i), r * zblocks + voff)),
        pl.BlockSpec((BLOCK, kvw), lambda r, i: (i, r * zblocks + voff)),
        qspec, qspec, qspec,
    ]
    dq = pl.pallas_call(
        body, grid=(dil, nb), in_specs=in_specs, out_specs=qspec,
        out_shape=jax.ShapeDtypeStruct((s // dil, dil * qw), F32),
        scratch_shapes=[pltpu.VMEM((nheads, BLOCK, 2 * BLOCK), F32), pltpu.VMEM((nheads, BLOCK, 2 * BLOCK), F32),
                        pltpu.VMEM((nheads, BLOCK, 2 * BLOCK), BF16), pltpu.VMEM((2, nheads, BLOCK, 1), F32)],
        compiler_params=_cparams(("parallel", "parallel")), name=name)(qd, kd, kd, zd, zd, rs(do), rs(lse), rs(delta))
    return dq.reshape(s, qw)


def _band_dkv(q, k, zv, do, lse, delta, *, dil, group, maxdist, name):
    s, qw = q.shape
    kvw = zv[1]
    qd, kd, zd, zblocks, voff = _band_views(q, k, zv, dil)
    rs = lambda t: t.reshape(s // dil, dil * qw)
    nb = s // dil // BLOCK
    nheads = qw // 64
    nkv = kvw // 64
    scale = 64 ** -0.5

    def body(k_ref, v_ref, q0, q1, do0, do1, l0, l1, d0, d1, dk_ref, dv_ref, sc_s, dp_s, p_s, ds_s, st_s):
        j = pl.program_id(1)
        row = lax.broadcasted_iota(jnp.int32, (2 * BLOCK, BLOCK), 0)
        col = lax.broadcasted_iota(jnp.int32, (2 * BLOCK, BLOCK), 1)
        dist = row - col
        mask = (dist >= 0) & (dist <= maxdist) & ((j < nb - 1) | (row < BLOCK))
        q2 = jnp.concatenate([q0[...], q1[...]], axis=0)
        do2 = jnp.concatenate([do0[...], do1[...]], axis=0).astype(BF16)
        vb = v_ref[...].astype(BF16)
        for h in range(nheads):
            g = h // group
            hs = slice(64 * h, 64 * h + 64)
            sc_s[h] = _dot(q2[:, hs], k_ref[:, 64 * g:64 * g + 64], NT_DIMS)
            dp_s[h] = _dot(do2[:, hs], vb[:, 64 * g:64 * g + 64], NT_DIMS)
            st_s[0, h, 0:BLOCK] = l0[:, 64 * h:64 * h + 1]
            st_s[0, h, BLOCK:2 * BLOCK] = l1[:, 64 * h:64 * h + 1]
            st_s[1, h, 0:BLOCK] = d0[:, 64 * h:64 * h + 1]
            st_s[1, h, BLOCK:2 * BLOCK] = d1[:, 64 * h:64 * h + 1]
        p = jnp.exp(jnp.where(mask[None], sc_s[...] * scale, NEG) - st_s[0])
        p_s[...] = p.astype(BF16)
        ds_s[...] = (p * (dp_s[...] - st_s[1]) * scale).astype(BF16)
        dks = [None] * nkv
        dvs = [None] * nkv
        for h in range(nheads):
            g = h // group
            hs = slice(64 * h, 64 * h + 64)
            dvh = _dot(p_s[h], do2[:, hs], TN_DIMS)
            dkh = _dot(ds_s[h], q2[:, hs], TN_DIMS)
            dks[g] = dkh if dks[g] is None else dks[g] + dkh
            dvs[g] = dvh if dvs[g] is None else dvs[g] + dvh
        for g in range(nkv):
            dk_ref[:, 64 * g:64 * g + 64] = dks[g]
            dv_ref[:, 64 * g:64 * g + 64] = dvs[g].astype(BF16)

    nxt = lambda r, j: jnp.minimum(j + 1, nb - 1)
    q0s = pl.BlockSpec((BLOCK, qw), lambda r, j: (j, r))
    q1s = pl.BlockSpec((BLOCK, qw), lambda r, j: (nxt(r, j), r))
    kspec = pl.BlockSpec((BLOCK, kvw), lambda r, j: (j, r))
    in_specs = [kspec, pl.BlockSpec((BLOCK, kvw), lambda r, j: (j, r * zblocks + voff)),
                q0s, q1s, q0s, q1s, q0s, q1s, q0s, q1s]
    dod, lsd, dld = rs(do), rs(lse), rs(delta)
    dk, dv = pl.pallas_call(
        body, grid=(dil, nb), in_specs=in_specs, out_specs=(kspec, kspec),
        out_shape=(jax.ShapeDtypeStruct((s // dil, dil * kvw), F32), jax.ShapeDtypeStruct((s // dil, dil * kvw), BF16)),
        scratch_shapes=[pltpu.VMEM((nheads, 2 * BLOCK, BLOCK), F32), pltpu.VMEM((nheads, 2 * BLOCK, BLOCK), F32),
                        pltpu.VMEM((nheads, 2 * BLOCK, BLOCK), BF16), pltpu.VMEM((nheads, 2 * BLOCK, BLOCK), BF16),
                        pltpu.VMEM((2, nheads, 2 * BLOCK, 1), F32)],
        compiler_params=_cparams(("parallel", "parallel")), name=name)(kd, zd, qd, qd, dod, dod, lsd, lsd, dld, dld)
    return dk.reshape(s, kvw), dv.reshape(s, kvw)


HPG = 2


def _flash_tiles(sq, sk):
    return _pick(sq, (512, 256, 128)), _pick(sk, (512, 256, 128))


def _diag_mask(t):
    return lax.broadcasted_iota(jnp.int32, (t, t), 1) <= lax.broadcasted_iota(jnp.int32, (t, t), 0)


def _attn_fwd(q, k, vv, *, dv, causal, scale, name, xchg=None):
    sq, qw = q.shape
    sk = k.shape[0]
    varr, vbw, voff = vv
    assert vbw == HPG * dv
    ngrp = qw // (HPG * LANES)
    tq, tk = _flash_tiles(sq, sk)
    nq, nk = sq // tq, sk // tk
    if causal:
        assert tq == tk and sq == sk

    def body(q_ref, k_ref, v_ref, o_ref, l_ref, m_s, l_s, acc_s):
        i = pl.program_id(1)
        m_s[...] = jnp.full_like(m_s, NEG)
        l_s[...] = jnp.zeros_like(l_s)
        acc_s[...] = jnp.zeros_like(acc_s)

        def step(j, masked):
            rows = pl.ds(pl.multiple_of(j * tk, tk), tk)
            if masked:
                mask = _diag_mask(tq)
            for h in range(HPG):
                hs = slice(LANES * h, LANES * (h + 1))
                sc = _dot(q_ref[:, hs], k_ref[rows, hs], NT_DIMS) * scale
                if masked:
                    sc = jnp.where(mask, sc, NEG)
                m_prev = m_s[h]
                m_new = jnp.maximum(m_prev, jnp.max(sc, axis=1, keepdims=True))
                alpha = jnp.exp(m_prev - m_new)
                p = jnp.exp(sc - m_new[:, 0:1])
                l_s[h] = alpha * l_s[h] + jnp.sum(p, axis=1, keepdims=True)
                acc_s[h] = acc_s[h] * alpha[:, 0:dv] + _dot(p, v_ref[rows, dv * h:dv * (h + 1)])
                m_s[h] = m_new

        def loop_body(j, carry):
            step(j, False)
            return carry

        if causal:
            lax.fori_loop(0, i, loop_body, 0)
            step(i, True)
        else:
            lax.fori_loop(0, nk, loop_body, 0)
        for h in range(HPG):
            o_ref[:, dv * h:dv * (h + 1)] = acc_s[h] * (1.0 / l_s[h][:, 0:dv])
            l_ref[:, LANES * h:LANES * (h + 1)] = m_s[h] + jnp.log(l_s[h])

    in_specs = [
        pl.BlockSpec((tq, HPG * LANES), lambda g, i: (i, g)),
        pl.BlockSpec((sk, HPG * LANES), lambda g, i: (0, g)),
        pl.BlockSpec((sk, vbw), lambda g, i: (0, voff + g)),
    ]
    return _pcall(
        body, grid=(ngrp, nq), in_specs=in_specs,
        out_specs=(pl.BlockSpec((tq, HPG * dv), lambda g, i: (i, g)),
                   pl.BlockSpec((tq, HPG * LANES), lambda g, i: (i, g))),
        out_shape=(jax.ShapeDtypeStruct((sq, ngrp * HPG * dv), F32), jax.ShapeDtypeStruct((sq, qw), F32)),
        scratch_shapes=[pltpu.VMEM((HPG, tq, LANES), F32), pltpu.VMEM((HPG, tq, LANES), F32),
                        pltpu.VMEM((HPG, tq, dv), F32)],
        sem=("parallel", "arbitrary"), name=name, args=(q, k, varr), xchg=xchg)


def _attn_dq(q, k, vv, do, lse, delta, *, dv, causal, scale, name, xchg=None):
    sq, qw = q.shape
    sk = k.shape[0]
    varr, vbw, voff = vv
    ngrp = qw // (HPG * LANES)
    tq, tk = _flash_tiles(sq, sk)
    nq, nk = sq // tq, sk // tk

    def body(q_ref, k_ref, v_ref, do_ref, l_ref, d_ref, dq_ref, acc_s):
        i = pl.program_id(1)
        acc_s[...] = jnp.zeros_like(acc_s)

        def step(j, masked):
            rows = pl.ds(pl.multiple_of(j * tk, tk), tk)
            if masked:
                mask = _diag_mask(tq)
            for h in range(HPG):
                hs = slice(LANES * h, LANES * (h + 1))
                kh = k_ref[rows, hs]
                sc = _dot(q_ref[:, hs], kh, NT_DIMS) * scale
                if masked:
                    sc = jnp.where(mask, sc, NEG)
                p = jnp.exp(sc - l_ref[:, LANES * h:LANES * h + 1])
                dp = _dot(do_ref[:, dv * h:dv * (h + 1)], v_ref[rows, dv * h:dv * (h + 1)], NT_DIMS)
                ds = p * (dp - d_ref[:, dv * h:dv * h + 1]) * scale
                acc_s[h] += _dot(ds, kh)

        def loop_body(j, carry):
            step(j, False)
            return carry

        if causal:
            lax.fori_loop(0, i, loop_body, 0)
            step(i, True)
        else:
            lax.fori_loop(0, nk, loop_body, 0)
        for h in range(HPG):
            dq_ref[:, LANES * h:LANES * (h + 1)] = acc_s[h]

    qspec = pl.BlockSpec((tq, HPG * LANES), lambda g, i: (i, g))
    ospec = pl.BlockSpec((tq, HPG * dv), lambda g, i: (i, g))
    in_specs = [
        qspec,
        pl.BlockSpec((sk, HPG * LANES), lambda g, i: (0, g)),
        pl.BlockSpec((sk, vbw), lambda g, i: (0, voff + g)),
        ospec, qspec, ospec,
    ]
    return _pcall(
        body, grid=(ngrp, nq), in_specs=in_specs, out_specs=qspec,
        out_shape=jax.ShapeDtypeStruct((sq, qw), F32),
        scratch_shapes=[pltpu.VMEM((HPG, tq, LANES), F32)],
        sem=("parallel", "arbitrary"), name=name, args=(q, k, varr, do, lse, delta), xchg=xchg)


def _attn_dkv(q, k, vv, do, lse, delta, *, dv, causal, scale, name, xchg=None):
    sq, qw = q.shape
    sk = k.shape[0]
    varr, vbw, voff = vv
    ngrp = qw // (HPG * LANES)
    tq, tk = _flash_tiles(sq, sk)
    nq, nk = sq // tq, sk // tk

    def body(q_ref, k_ref, v_ref, do_ref, l_ref, d_ref, dk_ref, dv_ref, dk_s, dv_s):
        j = pl.program_id(1)
        dk_s[...] = jnp.zeros_like(dk_s)
        dv_s[...] = jnp.zeros_like(dv_s)

        def step(i, masked):
            rows = pl.ds(pl.multiple_of(i * tq, tq), tq)
            if masked:
                mask = _diag_mask(tq)
            for h in range(HPG):
                qh = q_ref[rows, LANES * h:LANES * (h + 1)]
                doh = do_ref[rows, dv * h:dv * (h + 1)]
                sc = _dot(qh, k_ref[:, LANES * h:LANES * (h + 1)], NT_DIMS) * scale
                if masked:
                    sc = jnp.where(mask, sc, NEG)
                p = jnp.exp(sc - l_ref[rows, LANES * h:LANES * h + 1])
                dv_s[h] += _dot(p, doh, TN_DIMS)
                dp = _dot(doh, v_ref[:, dv * h:dv * (h + 1)], NT_DIMS)
                ds = p * (dp - d_ref[rows, dv * h:dv * h + 1]) * scale
                dk_s[h] += _dot(ds, qh, TN_DIMS)

        def loop_body(i, carry):
            step(i, False)
            return carry

        if causal:
            step(j, True)
            lax.fori_loop(j + 1, nq, loop_body, 0)
        else:
            lax.fori_loop(0, nq, loop_body, 0)
        for h in range(HPG):
            dk_ref[:, LANES * h:LANES * (h + 1)] = dk_s[h]
            dv_ref[:, dv * h:dv * (h + 1)] = dv_s[h].astype(BF16)

    qspec = pl.BlockSpec((sq, HPG * LANES), lambda g, j: (0, g))
    ospec = pl.BlockSpec((sq, HPG * dv), lambda g, j: (0, g))
    kspec = pl.BlockSpec((tk, HPG * LANES), lambda g, j: (j, g))
    in_specs = [qspec, kspec, pl.BlockSpec((tk, vbw), lambda g, j: (j, voff + g)), ospec, qspec, ospec]
    return _pcall(
        body, grid=(ngrp, nk), in_specs=in_specs,
        out_specs=(kspec, pl.BlockSpec((tk, HPG * dv), lambda g, j: (j, g))),
        out_shape=(jax.ShapeDtypeStruct((sk, qw), F32), jax.ShapeDtypeStruct((sk, ngrp * HPG * dv), BF16)),
        scratch_shapes=[pltpu.VMEM((HPG, tk, LANES), F32), pltpu.VMEM((HPG, tk, dv), F32)],
        sem=("parallel", "arbitrary"), name=name, args=(q, k, varr, do, lse, delta), xchg=xchg)


def _xchg_copies(src_ref, out_ref, send_sems, recv_sems, local_sems, k, scatter):
    me = 4 * lax.axis_index("x") + 2 * lax.axis_index("y") + lax.axis_index("c")
    own = src_ref.at[me] if scatter else src_ref
    copies = [pltpu.make_async_copy(own, out_ref.at[me], local_sems.at[k])]
    for d in range(1, N_DEV):
        peer = (me + d) % N_DEV
        piece = src_ref.at[peer] if scatter else src_ref
        sem = (N_DEV - 1) * k + d - 1
        copies.append(pltpu.make_async_remote_copy(
            src_ref=piece, dst_ref=out_ref.at[me], send_sem=send_sems.at[sem], recv_sem=recv_sems.at[sem],
            device_id=(peer // 4, (peer // 2) % 2, peer % 2), device_id_type=pl.DeviceIdType.MESH))
    return copies


def _xchg_shapes(xchg):
    return [jax.ShapeDtypeStruct((N_DEV,) + tuple(src.shape[1:] if scatter else src.shape), src.dtype)
            for src, scatter in xchg]


def _xchg_sems(nx):
    return [pltpu.SemaphoreType.DMA(((N_DEV - 1) * nx,)), pltpu.SemaphoreType.DMA(((N_DEV - 1) * nx,)),
            pltpu.SemaphoreType.DMA((nx,))]


ANY_SPEC = pl.BlockSpec(memory_space=pl.ANY)


def _exchange_many(xchg, *, name):
    nx = len(xchg)

    def body(*refs):
        srcs, outs = refs[:nx], refs[nx:2 * nx]
        send_sems, recv_sems, local_sems = refs[2 * nx:]
        copies = [c for k, (_, scatter) in enumerate(xchg)
                  for c in _xchg_copies(srcs[k], outs[k], send_sems, recv_sems, local_sems, k, scatter)]
        for c in copies:
            c.start()
        for c in copies:
            c.wait()

    return pl.pallas_call(
        body, in_specs=[ANY_SPEC] * nx, out_specs=tuple([ANY_SPEC] * nx), out_shape=tuple(_xchg_shapes(xchg)),
        scratch_shapes=_xchg_sems(nx), name=name)(*[src for src, _ in xchg])


def _pcall(body, *, grid, in_specs, out_specs, out_shape, scratch_shapes=(), sem, name, args, xchg=None):
    single = not isinstance(out_shape, (tuple, list))
    if not xchg:
        res = pl.pallas_call(
            body, grid=grid, in_specs=list(in_specs), out_specs=out_specs, out_shape=out_shape,
            scratch_shapes=list(scratch_shapes), compiler_params=_cparams(sem), name=name)(*args)
        return res if xchg is None else (res, [])
    out_specs_l = [out_specs] if single else list(out_specs)
    out_shape_l = [out_shape] if single else list(out_shape)
    nx, n_in, n_out, n_scr = len(xchg), len(in_specs), len(out_shape_l), len(scratch_shapes)

    def wrapped(*refs):
        ins, xin = refs[:n_in], refs[n_in:n_in + nx]
        pos = n_in + nx
        outs, xout = refs[pos:pos + n_out], refs[pos + n_out:pos + n_out + nx]
        pos += n_out + nx
        scr = refs[pos:pos + n_scr]
        send_sems, recv_sems, local_sems = refs[pos + n_scr:]
        first = functools.reduce(jnp.logical_and, [pl.program_id(a) == 0 for a in range(len(grid))])
        last = functools.reduce(jnp.logical_and, [pl.program_id(a) == grid[a] - 1 for a in range(len(grid))])

        def copies():
            return [c for k, (_, scatter) in enumerate(xchg)
                    for c in _xchg_copies(xin[k], xout[k], send_sems, recv_sems, local_sems, k, scatter)]

        @pl.when(first)
        def _():
            for c in copies():
                c.start()

        body(*ins, *outs, *scr)

        @pl.when(last)
        def _():
            for c in copies():
                c.wait()

    res = pl.pallas_call(
        wrapped, grid=grid, in_specs=list(in_specs) + [ANY_SPEC] * nx,
        out_specs=tuple(out_specs_l + [ANY_SPEC] * nx), out_shape=tuple(out_shape_l + _xchg_shapes(xchg)),
        scratch_shapes=list(scratch_shapes) + _xchg_sems(nx),
        compiler_params=_cparams(("arbitrary",) * len(grid)), name=name + "_x")(*args, *[src for src, _ in xchg])
    main = res[0] if single else tuple(res[:n_out])
    return main, list(res[n_out:])


def _unshard(wg, axis):
    nd = wg.ndim - 1
    perm = list(range(1, axis + 1)) + [0] + list(range(axis + 1, nd + 1))
    t = wg.transpose(perm)
    shp = list(wg.shape[1:])
    shp[axis] *= N_DEV
    return t.reshape(shp)


def _shard_major(full, axis):
    shp = list(full.shape)
    shp[axis:axis + 1] = [N_DEV, shp[axis] // N_DEV]
    t = full.reshape(shp)
    perm = [axis] + list(range(axis)) + list(range(axis + 1, len(shp)))
    return t.transpose(perm)


BIG = ("w_in", "b_w_uq", "b_w_ukv", "m_w_kv", "w_branch", "w_out", "w_up", "w_down")
BIG_AXIS = dict(w_in=2, b_w_uq=2, b_w_ukv=2, m_w_kv=1, w_branch=3, w_out=1, w_up=2, w_down=1)
SMALL = ("g_mix", "b_gate", "a_qn", "a_kn", "a_sink", "b_qa_norm", "b_kva_norm", "b_qn", "b_kn", "c_qn", "c_kn",
         "m_g_mem", "m_qn", "m_kn", "g_mlp")
WEIGHTS = ("g_mix", "w_in", "b_gate", "a_qn", "a_kn", "a_sink", "b_qa_norm", "b_kva_norm", "b_w_uq", "b_w_ukv", "b_qn",
           "b_kn", "c_qn", "c_kn", "m_g_mem", "m_w_kv", "m_qn", "m_kn", "w_branch", "w_out", "g_mlp", "w_up", "w_down")


def _rope_tables(positions, dim):
    inv = ROPE_THETA ** (-jnp.arange(0, dim, 2, dtype=F32) / dim)
    ang = positions.astype(F32)[:, None] * inv
    return jnp.cos(ang), jnp.sin(ang)


def _lane_mask(lo, hi):
    lane = np.arange(LANES)
    return jnp.asarray(((lane >= lo) & (lane < hi)).astype(np.float32)[None, :])


def _step(x, mem, positions, w, moments_m, moments_v, loss_target):
    s_len, d_model = x.shape
    n_layers = w["g_mix"].shape[0]
    lay, orig_names, order, n_in, nw = _in_layout(d_model)
    unit = lambda name: lay[name]["off"] // lay[name]["nw"]

    def ag_src(l, names):
        return [(w[n][l].astype(BF16), False) for n in names]

    def layout(name, block):
        full = _unshard(block, BIG_AXIS[name] - 1)
        if name == "w_in":
            return dict(w_in=_permute_cols(full, lay, order))
        if name == "b_w_uq":
            uq = full.reshape(B_Q_LORA, B_HEADS, B_NOPE + B_ROPE)
            return dict(uq=jnp.pad(uq, ((0, 0), (0, 0), (0, LANES - B_NOPE - B_ROPE))).reshape(B_Q_LORA, B_HEADS * LANES))
        if name == "b_w_ukv":
            ukv = full.reshape(B_KV_LORA, B_HEADS, B_NOPE + B_V)
            return dict(ukv=jnp.concatenate([
                jnp.pad(ukv[..., :B_NOPE], ((0, 0), (0, 0), (0, LANES - B_NOPE))).reshape(B_KV_LORA, B_HEADS * LANES),
                ukv[..., B_NOPE:].reshape(B_KV_LORA, B_HEADS * B_V)], axis=-1))
        return {name: full}

    def rs_src(name, g):
        return (_shard_major(g, BIG_AXIS[name] - 1).astype(BF16), True)

    p64 = _pattern([64, 64])
    p128 = _pattern([128])
    pq = _pattern([64, 32, 32])
    pkr = _pattern([32, 32, 32, 32])
    cos_h, sin_h = _rope_tables(positions, A_HD)
    cos_r, sin_r = _rope_tables(positions, B_ROPE)
    ones = lambda n: jnp.ones((s_len, n), F32)
    zeros = lambda n: jnp.zeros((s_len, n), F32)
    rope64 = (jnp.tile(cos_h, (1, 4)), jnp.tile(jnp.concatenate([-sin_h, sin_h], axis=1), (1, 2)),
              jnp.asarray((np.arange(LANES) % 64 < 32).astype(np.float32)[None, :]), 32)
    rope_q = (jnp.concatenate([ones(64), cos_r, cos_r, ones(32)], axis=1),
              jnp.concatenate([zeros(64), -sin_r, sin_r, zeros(32)], axis=1), _lane_mask(64, 80), 16)
    rope_k = (jnp.concatenate([cos_r, cos_r, ones(96)], axis=1),
              jnp.concatenate([-sin_r, sin_r, zeros(96)], axis=1), _lane_mask(0, 16), 16)

    scale_b = (B_NOPE + B_ROPE) ** -0.5
    scale_m = M_HD ** -0.5

    n_l = n_layers
    per_head = lambda g, nh: jnp.tile(g, (1, nh))[:, None, :]
    lpad = lambda g, n: jnp.pad(g, ((0, 0), (0, n)))
    gains_all = dict(
        g_mix=w["g_mix"][:, None, :], g_mlp=w["g_mlp"][:, None, :], m_g_mem=w["m_g_mem"][:, None, :],
        b_gate=w["b_gate"][:, None, :], b_qa_norm=w["b_qa_norm"][:, None, :], b_kva_norm=w["b_kva_norm"][:, None, :],
        a_q=per_head(w["a_qn"], A_HEADS), a_k=per_head(w["a_kn"], A_KV_HEADS),
        b_q=per_head(lpad(w["b_qn"], 32), B_HEADS), b_k=per_head(lpad(w["b_kn"][:, :B_NOPE], 64), B_HEADS),
        b_kr=lpad(w["b_kn"][:, B_NOPE:], 96)[:, None, :],
        c_q=[per_head(w["c_qn"][:, g], C_HEADS) for g in range(3)],
        c_k=[per_head(w["c_kn"][:, g], C_HEADS) for g in range(3)],
        m_q=per_head(w["m_qn"], M_HEADS), m_k=per_head(w["m_kn"], M_HEADS),
        sink_row=jnp.repeat(w["a_sink"], A_HD, axis=1)[:, None, :],
        sinks=jnp.broadcast_to(w["a_sink"][:, :, None, None], (n_l, A_HEADS, 1, LANES)))

    def gains(l):
        return {k: ([(a, l) for a in v] if isinstance(v, list) else (v, l)) for k, v in gains_all.items()}

    ag_hosts = dict(mm_in=("w_up", "w_down"), flash_b=("w_in",), mm_up=("w_branch", "w_out"),
                    mm_down=("m_w_kv", "b_w_uq", "b_w_ukv"))
    first_alone = ("w_in", "b_w_uq", "b_w_ukv", "m_w_kv")
    first_hosts = dict(band_a=("w_up",), band_c0=("w_down",), band_c1=("w_branch", "w_out"))
    saved = []
    weights = [dict() for _ in range(n_layers)]

    def arrived(layer, names, blocks):
        for name, blk in zip(names, blocks):
            weights[layer].update(layout(name, blk))

    cur = x
    arrived(0, first_alone, _exchange_many(ag_src(0, first_alone), name="ag_first"))
    for l in range(n_layers):
        gn = gains(l)
        sv = dict(x=cur)
        wl = weights[l]

        def carry(host):
            return ag_src(l + 1, ag_hosts[host]) if l + 1 < n_layers else []

        def carry0(host):
            return ag_src(0, first_hosts[host]) if l == 0 else []

        h = _rmsnorm_fwd((cur, d_model, 0), gn["g_mix"], name="f_norm_mix")
        z, got = _mm(h, wl["w_in"], name="f_mm_in", xchg=carry("mm_in"))
        arrived(l + 1, ag_hosts["mm_in"], got)
        sv.update(h=h, z=z)
        qa = _headnorm_fwd((z, 512, unit("a_q")), gn["a_q"], p64, rope64, name="f_hn_aq")
        ka = _headnorm_fwd((z, 128, unit("a_k")), gn["a_k"], p64, rope64, name="f_hn_ak")
        va = (z, 128, unit("a_v"))
        (o_a, lse_a), got = _band_fwd(qa, ka, va, dil=1, group=A_HEADS // A_KV_HEADS, maxdist=A_WINDOW - 1,
                                      sinks=gn["sinks"], name="f_band_a", xchg=carry0("band_a"))
        arrived(0, first_hosts["band_a"], got)
        sv.update(qa=qa, ka=ka, o_a=o_a, lse_a=lse_a)
        cq = _rmsnorm_fwd((z, B_Q_LORA, unit("b_cq")), gn["b_qa_norm"], name="f_norm_cq")
        q_up = _mm(cq, wl["uq"], name="f_mm_uq")
        ckv = _rmsnorm_fwd((z, B_KV_LORA, unit("b_ckv")), gn["b_kva_norm"], name="f_norm_ckv")
        kv_up = _mm(ckv, wl["ukv"], name="f_mm_ukv")
        kp = _headnorm_fwd((z, LANES, unit("b_kr")), gn["b_kr"], pkr, rope_k, name="f_hn_kr")
        qb = _headnorm_fwd((q_up, B_HEADS * LANES, 0), gn["b_q"], pq, rope_q, name="f_hn_bq")
        kb = _headnorm_fwd((kv_up, B_HEADS * LANES, 0), gn["b_k"], p64, None, addend=kp, name="f_hn_bk")
        vb = (kv_up, HPG * B_V, B_HEADS * LANES // (HPG * B_V))
        (o_b, lse_b), got = _attn_fwd(qb, kb, vb, dv=B_V, causal=True, scale=scale_b, name="f_flash_b",
                                       xchg=carry("flash_b"))
        arrived(l + 1, ag_hosts["flash_b"], got)
        sv.update(cq=cq, q_up=q_up, ckv=ckv, kv_up=kv_up, qb=qb, kb=kb, o_b=o_b, lse_b=lse_b)
        qc, kc, o_cg, lse_cg = [], [], [], []
        for g, (win, dil) in enumerate(C_PATTERNS):
            qg = _headnorm_fwd((z, 512, unit("c%d" % (3 * g))), gn["c_q"][g], p64, rope64, name="f_hn_cq%d" % g)
            kg = _headnorm_fwd((z, 512, unit("c%d" % (3 * g + 1))), gn["c_k"][g], p64, rope64, name="f_hn_ck%d" % g)
            host = "band_c%d" % g
            (og, lg), got = _band_fwd(qg, kg, (z, 512, unit("c%d" % (3 * g + 2))), dil=dil, group=1,
                                      maxdist=win // dil, name="f_band_c%d" % g,
                                      xchg=carry0(host) if host in first_hosts else ())
            arrived(0, first_hosts.get(host, ()), got)
            qc.append(qg), kc.append(kg), o_cg.append(og), lse_cg.append(lg)
        o_c = _combine_fwd(o_cg, lse_cg, name="f_combine")
        sv.update(qc=qc, kc=kc, o_cg=o_cg, lse_cg=lse_cg, o_c=o_c)
        memn = _rmsnorm_fwd((mem, d_model, 0), gn["m_g_mem"], name="f_norm_mem")
        mkv = _mm(memn, wl["m_w_kv"], name="f_mm_mkv")
        mk = _headnorm_fwd((mkv, M_HEADS * M_HD, 0), gn["m_k"], p128, None, name="f_hn_mk")
        mq = _headnorm_fwd((z, M_HEADS * M_HD, unit("m_q")), gn["m_q"], p128, None, name="f_hn_mq")
        mv = (mkv, HPG * M_HD, M_HEADS * M_HD // (HPG * M_HD))
        o_m, lse_m = _attn_fwd(mq, mk, mv, dv=M_HD, causal=False, scale=scale_m, name="f_flash_m")
        sv.update(memn=memn, mkv=mkv, mk=mk, mq=mq, o_m=o_m, lse_m=lse_m)
        outs = (o_a, o_b, o_c, o_m)
        ys = [_mm(outs[n], wl["w_branch"][n], name="f_mm_branch") for n in range(N_BRANCH)]
        t = _merge_fwd((z, N_BRANCH * d_model, 0), gn["b_gate"], ys, name="f_merge")
        x2 = _mm(t, wl["w_out"], mode="add", extra=cur, name="f_mm_out")
        h2 = _rmsnorm_fwd((x2, d_model, 0), gn["g_mlp"], name="f_norm_mlp")
        (u, act), got = _mm(h2, wl["w_up"], mode="relu2", name="f_mm_up", xchg=carry("mm_up"))
        arrived(l + 1, ag_hosts["mm_up"], got)
        x3, got = _mm(act, wl["w_down"], mode="add", extra=x2, name="f_mm_down", xchg=carry("mm_down"))
        arrived(l + 1, ag_hosts["mm_down"], got)
        sv.update(ys=ys, t=t, x2=x2, h2=h2, u=u, act=act)
        saved.append(sv)
        cur = x3

    dx, loss_row = _loss_head(cur, loss_target, name="loss_head")
    loss = lax.psum(loss_row[0, 0], AXES)

    recvs = {n: [None] * n_layers for n in BIG}
    rows = {}

    def keep(key, layer, row):
        rows.setdefault(key, [None] * n_layers)[layer] = row

    def item(name, layer, g):
        return (name, layer, rs_src(name, g))

    def srcs(items):
        return [it[2] for it in items]

    def store(items, got):
        for (name, layer, _), r in zip(items, got):
            recvs[name][layer] = r

    carried = []
    for l in reversed(range(n_layers)):
        sv = saved[l]
        wl = weights[l]
        gn = gains(l)
        z = sv["z"]
        g_wdown = _mm(sv["act"], dx, ta=True, name="b_mm_wdown")
        du = _mm(dx, wl["w_down"], tb=True, mode="drelu2", extra=sv["u"], out_dtype=BF16, name="b_mm_du")
        g_wup = _mm(sv["h2"], du, ta=True, name="b_mm_wup")
        dh2 = _mm(du, wl["w_up"], tb=True, name="b_mm_dh2")
        dx2, dg = _rmsnorm_bwd(dh2, (sv["x2"], d_model, 0), gn["g_mlp"], dres=dx, name="b_norm_mlp")
        keep("g_mlp", l, dg)
        g_wout = _mm(sv["t"], dx2, ta=True, name="b_mm_wout")
        dt = _mm(dx2, wl["w_out"], tb=True, name="b_mm_dt")
        dys, dgate, dbg = _merge_bwd(dt, (z, N_BRANCH * d_model, 0), gn["b_gate"], sv["ys"], name="b_merge")
        keep("b_gate", l, dbg)
        outs = (sv["o_a"], sv["o_b"], sv["o_c"], sv["o_m"])
        g_wbranch = jnp.stack([_mm(outs[n], dys[n], ta=True, name="b_mm_wbranch") for n in range(N_BRANCH)])
        do_a, do_b, do_c, do_m = [_mm(dys[n], wl["w_branch"][n], tb=True, name="b_mm_dbranch")
                                  for n in range(N_BRANCH)]
        mlp_items = [item("w_down", l, g_wdown), item("w_up", l, g_wup)]
        proj_items = [item("w_out", l, g_wout), item("w_branch", l, g_wbranch)]
        dparts = {"gate": dgate}
        va = (z, 128, unit("a_v"))
        kw_a = dict(dil=1, group=A_HEADS // A_KV_HEADS, maxdist=A_WINDOW - 1)
        dl_a, dsink = _delta(do_a, sv["o_a"], p64, lse=sv["lse_a"], sink_row=gn["sink_row"], name="b_delta_a")
        keep("a_sink", l, dsink)
        dqa = _band_dq(sv["qa"], sv["ka"], va, do_a, sv["lse_a"], dl_a, name="b_band_dq_a", **kw_a)
        dka, dva = _band_dkv(sv["qa"], sv["ka"], va, do_a, sv["lse_a"], dl_a, name="b_band_dkv_a", **kw_a)
        dparts["a_q"], dg = _headnorm_bwd(dqa, (z, 512, unit("a_q")), gn["a_q"], p64, rope64, name="b_hn_aq")
        keep("a_qn", l, dg)
        dparts["a_k"], dg = _headnorm_bwd(dka, (z, 128, unit("a_k")), gn["a_k"], p64, rope64, name="b_hn_ak")
        keep("a_kn", l, dg)
        dparts["a_v"] = dva
        vb = (sv["kv_up"], HPG * B_V, B_HEADS * LANES // (HPG * B_V))
        dl_b, _ = _delta(do_b, sv["o_b"], p64, name="b_delta_b")
        kw_b = dict(dv=B_V, causal=True, scale=scale_b)
        dqb, got = _attn_dq(sv["qb"], sv["kb"], vb, do_b, sv["lse_b"], dl_b, name="b_flash_dq_b",
                            xchg=srcs(mlp_items), **kw_b)
        store(mlp_items, got)
        dkv_items = carried + proj_items
        (dkb, dvb), got = _attn_dkv(sv["qb"], sv["kb"], vb, do_b, sv["lse_b"], dl_b, name="b_flash_dkv_b",
                                     xchg=srcs(dkv_items), **kw_b)
        store(dkv_items, got)
        d_qup, dg = _headnorm_bwd(dqb, (sv["q_up"], B_HEADS * LANES, 0), gn["b_q"], pq, rope_q, name="b_hn_bq")
        keep("b_q", l, dg)
        d_kn, dg = _headnorm_bwd(dkb, (sv["kv_up"], B_HEADS * LANES, 0), gn["b_k"], p64, None, name="b_hn_bk")
        keep("b_k", l, dg)
        dkp = _kp_reduce(dkb, name="b_kp_reduce")
        dparts["b_kr"], dg = _headnorm_bwd(dkp, (z, LANES, unit("b_kr")), gn["b_kr"], pkr, rope_k, name="b_hn_kr")
        keep("b_kr", l, dg)
        d_kvup = jnp.concatenate([d_kn, dvb], axis=1)
        g_ukv = _mm(sv["ckv"], d_kvup, ta=True, name="b_mm_wukv")
        dckv = _mm(d_kvup, wl["ukv"], tb=True, name="b_mm_dckv")
        dparts["b_ckv"], dg = _rmsnorm_bwd(dckv, (z, B_KV_LORA, unit("b_ckv")), gn["b_kva_norm"], out_dtype=BF16,
                                          name="b_norm_ckv")
        keep("b_kva_norm", l, dg)
        g_uq = _mm(sv["cq"], d_qup, ta=True, name="b_mm_wuq")
        dcq = _mm(d_qup, wl["uq"], tb=True, name="b_mm_dcq")
        dparts["b_cq"], dg = _rmsnorm_bwd(dcq, (z, B_Q_LORA, unit("b_cq")), gn["b_qa_norm"], out_dtype=BF16,
                                         name="b_norm_cq")
        keep("b_qa_norm", l, dg)
        g_wuq = g_uq.reshape(B_Q_LORA, B_HEADS, LANES)[:, :, :B_NOPE + B_ROPE].reshape(B_Q_LORA, -1)
        g_wukv = jnp.concatenate(
            [g_ukv[:, :B_HEADS * LANES].reshape(B_KV_LORA, B_HEADS, LANES)[:, :, :B_NOPE],
             g_ukv[:, B_HEADS * LANES:].reshape(B_KV_LORA, B_HEADS, B_V)], axis=-1).reshape(B_KV_LORA, -1)
        do_cg, dl_cg = _combine_bwd(do_c, sv["o_c"], sv["lse_cg"], p64, name="b_combine")
        for g, (win, dil) in enumerate(C_PATTERNS):
            vc = (z, 512, unit("c%d" % (3 * g + 2)))
            kw_c = dict(dil=dil, group=1, maxdist=win // dil)
            args = (sv["qc"][g], sv["kc"][g], vc, do_cg[g], sv["lse_cg"][g], dl_cg[g])
            dq = _band_dq(*args, name="b_band_dq_c%d" % g, **kw_c)
            dk, dv_ = _band_dkv(*args, name="b_band_dkv_c%d" % g, **kw_c)
            dparts["c%d" % (3 * g)], dg = _headnorm_bwd(dq, (z, 512, unit("c%d" % (3 * g))), gn["c_q"][g], p64, rope64,
                                                        name="b_hn_cq%d" % g)
            keep("c_q%d" % g, l, dg)
            dparts["c%d" % (3 * g + 1)], dg = _headnorm_bwd(dk, (z, 512, unit("c%d" % (3 * g + 1))), gn["c_k"][g], p64,
                                                            rope64, name="b_hn_ck%d" % g)
            keep("c_k%d" % g, l, dg)
            dparts["c%d" % (3 * g + 2)] = dv_
        mv = (sv["mkv"], HPG * M_HD, M_HEADS * M_HD // (HPG * M_HD))
        dl_m, _ = _delta(do_m, sv["o_m"], p128, name="b_delta_m")
        kw_m = dict(dv=M_HD, causal=False, scale=scale_m)
        dmq = _attn_dq(sv["mq"], sv["mk"], mv, do_m, sv["lse_m"], dl_m, name="b_flash_dq_m", **kw_m)
        dmk, dmv = _attn_dkv(sv["mq"], sv["mk"], mv, do_m, sv["lse_m"], dl_m, name="b_flash_dkv_m", **kw_m)
        dparts["m_q"], dg = _headnorm_bwd(dmq, (z, M_HEADS * M_HD, unit("m_q")), gn["m_q"], p128, None, name="b_hn_mq")
        keep("m_qn", l, dg)
        d_mk, dg = _headnorm_bwd(dmk, (sv["mkv"], M_HEADS * M_HD, 0), gn["m_k"], p128, None, name="b_hn_mk")
        keep("m_kn", l, dg)
        d_mkv = jnp.concatenate([d_mk, dmv], axis=1)
        g_wmkv = _mm(sv["memn"], d_mkv, ta=True, name="b_mm_wmkv")
        dmemn = _mm(d_mkv, wl["m_w_kv"], tb=True, name="b_mm_dmemn")
        _, dg = _rmsnorm_bwd(dmemn, (mem, d_model, 0), gn["m_g_mem"], name="b_norm_mem")
        keep("m_g_mem", l, dg)
        dz = jnp.concatenate([dparts[n] for n in order], axis=1)
        bm_items = [item("b_w_ukv", l, g_wukv), item("b_w_uq", l, g_wuq), item("m_w_kv", l, g_wmkv)]
        g_in, got = _mm(sv["h"], dz, ta=True, name="b_mm_win", xchg=srcs(bm_items))
        store(bm_items, got)
        carried = [item("w_in", l, _unpermute_cols(g_in, lay, orig_names))]
        last_items = carried if l == 0 else []
        dh, got = _mm(dz, wl["w_in"], tb=True, name="b_mm_dh", xchg=srcs(last_items))
        store(last_items, got)
        dx, dg = _rmsnorm_bwd(dh, (sv["x"], d_model, 0), gn["g_mix"], dres=dx2, name="b_norm_mix")
        keep("g_mix", l, dg)

    cat = {k: jnp.concatenate(v, axis=0) for k, v in rows.items()}
    heads = lambda a, nh, hd: a.reshape(n_layers, nh, hd).sum(axis=1)
    gsmall = dict(
        g_mix=cat["g_mix"], b_gate=cat["b_gate"], g_mlp=cat["g_mlp"], m_g_mem=cat["m_g_mem"],
        b_qa_norm=cat["b_qa_norm"], b_kva_norm=cat["b_kva_norm"],
        a_sink=cat["a_sink"][:, ::A_HD], a_qn=heads(cat["a_qn"], A_HEADS, A_HD), a_kn=heads(cat["a_kn"], A_KV_HEADS, A_HD),
        b_qn=heads(cat["b_q"], B_HEADS, LANES)[:, :B_NOPE + B_ROPE],
        b_kn=jnp.concatenate([heads(cat["b_k"], B_HEADS, LANES)[:, :B_NOPE], cat["b_kr"][:, :B_ROPE]], axis=1),
        c_qn=jnp.stack([heads(cat["c_q%d" % g], C_HEADS, C_HD) for g in range(3)], axis=1),
        c_kn=jnp.stack([heads(cat["c_k%d" % g], C_HEADS, C_HD) for g in range(3)], axis=1),
        m_qn=heads(cat["m_qn"], M_HEADS, M_HD), m_kn=heads(cat["m_kn"], M_HEADS, M_HD))

    out_g, out_d, out_m, out_v = {}, {}, {}, {}
    for name in BIG:
        shard = w[name].shape
        two = lambda a: a.reshape(-1, shard[-1])
        pieces = [r.reshape(N_DEV, -1, shard[-1]) for r in recvs[name]]
        g, dlt, mn, vn = _adam(pieces, two(w[name]), two(moments_m[name]), two(moments_v[name]), name="adam_" + name)
        out_g[name], out_d[name], out_m[name], out_v[name] = (a.reshape(shard) for a in (g, dlt, mn, vn))
    sizes = [int(np.prod(w[n].shape)) for n in SMALL]
    total = sum(sizes)
    padded = -(-total // (8 * LANES)) * (8 * LANES)
    pack = lambda d: jnp.pad(jnp.concatenate([jnp.reshape(d[n], (-1,)) for n in SMALL]), (0, padded - total)).reshape(-1, LANES)
    gpack = pack(gsmall)
    recv = _exchange_many([(gpack, False)], name="ag_small_grads")[0]
    packed = _adam([recv], pack(w), pack(moments_m), pack(moments_v), name="adam_small")
    offs = np.cumsum([0] + sizes)
    for a, out in zip(packed, (out_g, out_d, out_m, out_v)):
        flat = a.reshape(-1)
        for n, o, sz in zip(SMALL, offs[:-1], sizes):
            out[n] = flat[o:o + sz].reshape(w[n].shape)
    return loss, dx, out_g, out_d, out_m, out_v


def kernel(x, mem, positions, g_mix, w_in, b_gate, a_qn, a_kn, a_sink, b_qa_norm, b_kva_norm, b_w_uq, b_w_ukv, b_qn, b_kn, c_qn, c_kn, m_g_mem, m_w_kv, m_qn, m_kn, w_branch, w_out, g_mlp, w_up, w_down, loss_target, m_g_mix, m_w_in, m_b_gate, m_a_qn, m_a_kn, m_a_sink, m_b_qa_norm, m_b_kva_norm, m_b_w_uq, m_b_w_ukv, m_b_qn, m_b_kn, m_c_qn, m_c_kn, m_m_g_mem, m_m_w_kv, m_m_qn, m_m_kn, m_w_branch, m_w_out, m_g_mlp, m_w_up, m_w_down, v_g_mix, v_w_in, v_b_gate, v_a_qn, v_a_kn, v_a_sink, v_b_qa_norm, v_b_kva_norm, v_b_w_uq, v_b_w_ukv, v_b_qn, v_b_kn, v_c_qn, v_c_kn, v_m_g_mem, v_m_w_kv, v_m_qn, v_m_kn, v_w_branch, v_w_out, v_g_mlp, v_w_up, v_w_down):
    w = dict(g_mix=g_mix, w_in=w_in, b_gate=b_gate, a_qn=a_qn, a_kn=a_kn, a_sink=a_sink, b_qa_norm=b_qa_norm,
             b_kva_norm=b_kva_norm, b_w_uq=b_w_uq, b_w_ukv=b_w_ukv, b_qn=b_qn, b_kn=b_kn, c_qn=c_qn, c_kn=c_kn,
             m_g_mem=m_g_mem, m_w_kv=m_w_kv, m_qn=m_qn, m_kn=m_kn, w_branch=w_branch, w_out=w_out, g_mlp=g_mlp,
             w_up=w_up, w_down=w_down)
    mm_ = dict(g_mix=m_g_mix, w_in=m_w_in, b_gate=m_b_gate, a_qn=m_a_qn, a_kn=m_a_kn, a_sink=m_a_sink,
               b_qa_norm=m_b_qa_norm, b_kva_norm=m_b_kva_norm, b_w_uq=m_b_w_uq, b_w_ukv=m_b_w_ukv, b_qn=m_b_qn,
               b_kn=m_b_kn, c_qn=m_c_qn, c_kn=m_c_kn, m_g_mem=m_m_g_mem, m_w_kv=m_m_w_kv, m_qn=m_m_qn, m_kn=m_m_kn,
               w_branch=m_w_branch, w_out=m_w_out, g_mlp=m_g_mlp, w_up=m_w_up, w_down=m_w_down)
    vv = dict(g_mix=v_g_mix, w_in=v_w_in, b_gate=v_b_gate, a_qn=v_a_qn, a_kn=v_a_kn, a_sink=v_a_sink,
              b_qa_norm=v_b_qa_norm, b_kva_norm=v_b_kva_norm, b_w_uq=v_b_w_uq, b_w_ukv=v_b_w_ukv, b_qn=v_b_qn,
              b_kn=v_b_kn, c_qn=v_c_qn, c_kn=v_c_kn, m_g_mem=v_m_g_mem, m_w_kv=v_m_w_kv, m_qn=v_m_qn, m_kn=v_m_kn,
              w_branch=v_w_branch, w_out=v_w_out, g_mlp=v_g_mlp, w_up=v_w_up, w_down=v_w_down)
    loss, dx, g, dlt, mn, vn = _step(x[0], mem[0], positions[0], w, mm_, vv, loss_target[0])
    return (loss, dx[None], *[g[n] for n in WEIGHTS], *[dlt[n] for n in WEIGHTS], *[mn[n] for n in WEIGHTS],
            *[vn[n] for n in WEIGHTS])
```

```python
import functools

import numpy as np
import jax
import jax.numpy as jnp
from jax import lax
from jax.experimental import pallas as pl
from jax.experimental.pallas import tpu as pltpu

F32 = jnp.float32
BF16 = jnp.bfloat16

LANES = 128
N_DEV = 8
AXES = ("x", "y", "c")
VMEM_LIMIT = 48 * 1024 * 1024

EPS = 1e-6
NEG = -1e30
ROPE_THETA = 10000.0
BLOCK = 128

A_HEADS, A_KV_HEADS, A_HD, A_WINDOW = 8, 2, 64, 128
B_HEADS, B_Q_LORA, B_KV_LORA, B_NOPE, B_ROPE, B_V = 8, 384, 256, 64, 32, 64
C_PATTERNS = ((128, 1), (512, 4), (2048, 16))
C_HEADS, C_HD = 8, 64
M_HEADS, M_HD = 4, 128
BRANCH_W = 512
N_BRANCH = 4

ADAM_LR, ADAM_B1, ADAM_B2, ADAM_EPS, ADAM_WD, ADAM_STEP = 0.001, 0.9, 0.999, 1e-08, 0.01, 10

NT_DIMS = (((1,), (1,)), ((), ()))
TN_DIMS = (((0,), (0,)), ((), ()))
NN_DIMS = (((1,), (0,)), ((), ()))


def _pick(n, cands):
    for c in cands:
        if n % c == 0:
            return c
    return n


def _cparams(sem):
    return pltpu.CompilerParams(dimension_semantics=sem, vmem_limit_bytes=VMEM_LIMIT)


def _dot(a, b, dims=NN_DIMS):
    return lax.dot_general(a.astype(BF16), b.astype(BF16), dims, preferred_element_type=F32)


def _in_layout(d_model):
    orig = ([("a_q", 512), ("a_k", 128), ("a_v", 128), ("b_cq", B_Q_LORA), ("b_ckv", B_KV_LORA), ("b_kr", B_ROPE)]
            + [("c%d" % i, 512) for i in range(9)] + [("m_q", 512), ("gate", N_BRANCH * d_model)])
    order = ["gate", "a_q"] + ["c%d" % i for i in range(9)] + ["m_q", "b_ckv", "b_cq", "a_k", "a_v", "b_kr"]
    o_off, off = {}, 0
    for name, w in orig:
        o_off[name] = (off, w)
        off += w
    lay, noff = {}, 0
    for name in order:
        oo, w = o_off[name]
        nw = LANES if name == "b_kr" else w
        assert noff % nw == 0, (name, noff, nw)
        lay[name] = dict(orig=oo, w=w, off=noff, nw=nw)
        noff += nw
    assert noff % 512 == 0
    return lay, [n for n, _ in orig], order, off, noff


def _permute_cols(w, lay, order):
    cols = []
    for name in order:
        e = lay[name]
        p = w[..., e["orig"]:e["orig"] + e["w"]]
        if e["nw"] != e["w"]:
            p = jnp.pad(p, [(0, 0)] * (w.ndim - 1) + [(0, e["nw"] - e["w"])])
        cols.append(p)
    return jnp.concatenate(cols, axis=-1)


def _unpermute_cols(w, lay, orig_names):
    return jnp.concatenate([w[..., lay[n]["off"]:lay[n]["off"] + lay[n]["w"]] for n in orig_names], axis=-1)


def _pattern(groups):
    ids = np.repeat(np.arange(len(groups)), groups)
    assert ids.shape[0] == LANES
    gmat = (ids[:, None] == ids[None, :]).astype(np.float32)
    invn = (1.0 / np.asarray(groups, np.float32))[ids][None, :]
    return jnp.asarray(gmat, BF16), jnp.asarray(invn, F32)


def _gsum(v, gmat):
    hi = v.astype(BF16)
    r1 = v - hi.astype(F32)
    mid = r1.astype(BF16)
    lo = (r1 - mid.astype(F32)).astype(BF16)
    dot = lambda p: jnp.dot(p, gmat, preferred_element_type=F32)
    return dot(hi) + dot(mid) + dot(lo)


def _swap_half(v, lomask, half):
    return jnp.where(lomask > 0.5, pltpu.roll(v, LANES - half, 1), pltpu.roll(v, half, 1))


def _rows_spec(tr, width, off=0):
    return pl.BlockSpec((tr, width), lambda i: (i, off))


def _undilate(src_ref, scr, dil, width):
    n = scr.shape[1] // dil
    for r in range(dil):
        for c in range(width // LANES):
            lanes = slice(r * width + c * LANES, r * width + (c + 1) * LANES)
            scr[c, pl.ds(r, n, stride=dil), :] = src_ref[:, lanes].astype(F32)


def _dilate(scr, dst_ref, dil, width):
    n = scr.shape[1] // dil
    for r in range(dil):
        for c in range(width // LANES):
            lanes = slice(r * width + c * LANES, r * width + (c + 1) * LANES)
            dst_ref[:, lanes] = scr[c, pl.ds(r, n, stride=dil), :].astype(dst_ref.dtype)


def _dil_spec(tr, width, dil):
    return pl.BlockSpec((tr // dil, dil * width), lambda i: (i, 0))


def _layer_arg(per_layer):
    arr, l = per_layer
    nd = arr.ndim - 1
    return pl.BlockSpec((None,) + tuple(arr.shape[1:]), lambda *_: (l,) + (0,) * nd), arr


def _full_spec(shape):
    return pl.BlockSpec(shape, lambda i: (0,) * len(shape))


def _mm(a, b, *, ta=False, tb=False, out_dtype=F32, mode=None, extra=None, name, xchg=None):
    if ta:
        kd, m = a.shape
    else:
        m, kd = a.shape
    if tb:
        n, kb = b.shape
    else:
        kb, n = b.shape
    assert kd == kb, (a.shape, b.shape, ta, tb)
    tm = _pick(m, (1024, 512, 256, 128))
    tn = _pick(n, (1024, 1536, 512, 384, 256, 128))
    tk = _pick(kd, (1024, 1536, 512, 384, 256, 128))
    nk = kd // tk
    a_spec = (pl.BlockSpec((tk, tm), lambda i, j, k: (k, i)) if ta else pl.BlockSpec((tm, tk), lambda i, j, k: (i, k)))
    b_spec = (pl.BlockSpec((tn, tk), lambda i, j, k: (j, k)) if tb else pl.BlockSpec((tk, tn), lambda i, j, k: (k, j)))
    o_spec = pl.BlockSpec((tm, tn), lambda i, j, k: (i, j))
    dims = (((0 if ta else 1,), (1 if tb else 0,)), ((), ()))
    has_extra = mode in ("add", "drelu2")

    def body(*refs):
        a_ref, b_ref = refs[0], refs[1]
        e_ref = refs[2] if has_extra else None
        outs = refs[2 + has_extra:2 + has_extra + (2 if mode == "relu2" else 1)]
        k = pl.program_id(2)

        def finish(r):
            if mode == "add":
                outs[0][...] = (e_ref[...] + r).astype(out_dtype)
            elif mode == "relu2":
                outs[0][...] = r
                outs[1][...] = jnp.square(jnp.maximum(r, 0.0)).astype(BF16)
            elif mode == "drelu2":
                outs[0][...] = (r * (2.0 * jnp.maximum(e_ref[...], 0.0))).astype(out_dtype)
            else:
                outs[0][...] = r.astype(out_dtype)

        if nk == 1:
            finish(_dot(a_ref[...], b_ref[...], dims))
        else:
            acc = refs[-1]

            @pl.when(k == 0)
            def _():
                acc[...] = _dot(a_ref[...], b_ref[...], dims)

            @pl.when(k > 0)
            def _():
                acc[...] += _dot(a_ref[...], b_ref[...], dims)

            @pl.when(k == nk - 1)
            def _():
                finish(acc[...])

    in_specs = [a_spec, b_spec] + ([o_spec] if has_extra else [])
    args = [a, b] + ([extra] if has_extra else [])
    if mode == "relu2":
        out_shape = (jax.ShapeDtypeStruct((m, n), F32), jax.ShapeDtypeStruct((m, n), BF16))
        out_specs = (o_spec, o_spec)
    else:
        out_shape = jax.ShapeDtypeStruct((m, n), out_dtype)
        out_specs = o_spec
    return _pcall(
        body, grid=(m // tm, n // tn, nk), in_specs=in_specs, out_specs=out_specs, out_shape=out_shape,
        scratch_shapes=[pltpu.VMEM((tm, tn), F32)] if nk > 1 else [], sem=("parallel", "parallel", "arbitrary"),
        name=name, args=args, xchg=xchg)


def _rmsnorm_fwd(xv, gain, *, name, tr=512):
    x, width, off = xv
    rows = x.shape[0]
    tr = min(tr, rows)

    def body(x_ref, g_ref, o_ref):
        xf = x_ref[...]
        rstd = lax.rsqrt(jnp.mean(xf * xf, axis=-1, keepdims=True) + EPS)
        o_ref[...] = (xf * rstd * g_ref[...]).astype(BF16)

    g_spec, g_arr = _layer_arg(gain)
    return pl.pallas_call(
        body, grid=(rows // tr,), in_specs=[_rows_spec(tr, width, off), g_spec],
        out_specs=_rows_spec(tr, width), out_shape=jax.ShapeDtypeStruct((rows, width), BF16),
        compiler_params=_cparams(("parallel",)), name=name)(x, g_arr)


def _rmsnorm_bwd(dh, xv, gain, *, dres=None, out_dtype=F32, name, tr=512):
    x, width, off = xv
    rows = x.shape[0]
    tr = min(tr, rows)
    has_res = dres is not None

    def body(*refs):
        dh_ref, x_ref, g_ref = refs[:3]
        r_ref = refs[3] if has_res else None
        dx_ref, dg_ref = refs[3 + has_res:]
        i = pl.program_id(0)
        xf = x_ref[...]
        rstd = lax.rsqrt(jnp.mean(xf * xf, axis=-1, keepdims=True) + EPS)
        xh = xf * rstd
        d = dh_ref[...]
        dxh = d * g_ref[...]
        dx = rstd * (dxh - xh * jnp.mean(dxh * xh, axis=-1, keepdims=True))
        if has_res:
            dx = dx + r_ref[...]
        dx_ref[...] = dx.astype(out_dtype)

        @pl.when(i == 0)
        def _():
            dg_ref[...] = jnp.zeros_like(dg_ref)

        dg_ref[...] += jnp.sum(d * xh, axis=0, keepdims=True)

    g_spec, g_arr = _layer_arg(gain)
    in_specs = [_rows_spec(tr, width), _rows_spec(tr, width, off), g_spec]
    args = [dh, x, g_arr]
    if has_res:
        in_specs.append(_rows_spec(tr, width))
        args.append(dres)
    return pl.pallas_call(
        body, grid=(rows // tr,), in_specs=in_specs,
        out_specs=(_rows_spec(tr, width), _full_spec((1, width))),
        out_shape=(jax.ShapeDtypeStruct((rows, width), out_dtype), jax.ShapeDtypeStruct((1, width), F32)),
        compiler_params=_cparams(("arbitrary",)), name=name)(*args)


def _headnorm_fwd(xv, gain_row, pat, rope, *, addend=None, dil=1, name, tr=512):
    x, width, off = xv
    rows = x.shape[0]
    tr = min(tr, rows)
    gmat, invn = pat
    nblk = width // LANES
    has_rope = rope is not None
    has_add = addend is not None
    half = rope[3] if has_rope else 0

    def body(*refs):
        x_ref, g_ref, gm_ref, in_ref = refs[:4]
        pos = 4
        if has_rope:
            c_ref, s_ref, lo_ref = refs[pos:pos + 3]
            pos += 3
        if has_add:
            a_ref = refs[pos]
            pos += 1
        o_ref = refs[pos]
        scr = refs[pos + 1] if dil > 1 else None
        gm = gm_ref[...]
        inv = in_ref[...]
        if has_add:
            add = pltpu.roll(a_ref[...].astype(F32), 64, 1)
        for j in range(nblk):
            sl = slice(j * LANES, (j + 1) * LANES)
            xb = x_ref[:, sl]
            rstd = lax.rsqrt(_gsum(xb * xb, gm) * inv + EPS)
            y = xb * rstd * g_ref[:, sl]
            if has_rope:
                y = y * c_ref[...] + _swap_half(y, lo_ref[...], half) * s_ref[...]
            if has_add:
                y = y + add
            if dil > 1:
                scr[j] = y
            else:
                o_ref[:, sl] = y.astype(BF16)
        if dil > 1:
            _dilate(scr, o_ref, dil, width)

    g_spec, g_arr = _layer_arg(gain_row)
    in_specs = [_rows_spec(tr, width, off), g_spec, _full_spec((LANES, LANES)), _full_spec((1, LANES))]
    args = [x, g_arr, gmat, invn]
    if has_rope:
        in_specs += [_rows_spec(tr, LANES), _rows_spec(tr, LANES), _full_spec((1, LANES))]
        args += [rope[0], rope[1], rope[2]]
    if has_add:
        in_specs.append(_rows_spec(tr, LANES))
        args.append(addend)
    return pl.pallas_call(
        body, grid=(rows // tr,), in_specs=in_specs,
        out_specs=_dil_spec(tr, width, dil) if dil > 1 else _rows_spec(tr, width),
        out_shape=jax.ShapeDtypeStruct((rows // dil, dil * width), BF16),
        scratch_shapes=[pltpu.VMEM((nblk, tr, LANES), F32)] if dil > 1 else [],
        compiler_params=_cparams(("parallel",)), name=name)(*args)


def _headnorm_bwd(dy, xv, gain_row, pat, rope, *, dil=1, name, tr=512):
    x, width, off = xv
    rows = x.shape[0]
    tr = min(tr, rows)
    gmat, invn = pat
    nblk = width // LANES
    has_rope = rope is not None
    half = rope[3] if has_rope else 0

    def body(*refs):
        dy_ref, x_ref, g_ref, gm_ref, in_ref = refs[:5]
        pos = 5
        if has_rope:
            c_ref, s_ref, lo_ref = refs[pos:pos + 3]
            pos += 3
        dx_ref, dg_ref = refs[pos:pos + 2]
        i = pl.program_id(0)
        gm = gm_ref[...]
        inv = in_ref[...]
        if dil > 1:
            scr = refs[pos + 2]
            _undilate(dy_ref, scr, dil, width)

        @pl.when(i == 0)
        def _():
            dg_ref[...] = jnp.zeros_like(dg_ref)

        for j in range(nblk):
            sl = slice(j * LANES, (j + 1) * LANES)
            d = scr[j] if dil > 1 else dy_ref[:, sl]
            if has_rope:
                d = d * c_ref[...] + _swap_half(d * s_ref[...], lo_ref[...], half)
            xb = x_ref[:, sl]
            rstd = lax.rsqrt(_gsum(xb * xb, gm) * inv + EPS)
            xh = xb * rstd
            dg_ref[:, sl] += jnp.sum(d * xh, axis=0, keepdims=True)
            dxh = d * g_ref[:, sl]
            dx = rstd * (dxh - xh * (_gsum(dxh * xh, gm) * inv))
            dx_ref[:, sl] = dx.astype(BF16)

    g_spec, g_arr = _layer_arg(gain_row)
    in_specs = [_dil_spec(tr, width, dil) if dil > 1 else _rows_spec(tr, width), _rows_spec(tr, width, off), g_spec,
                _full_spec((LANES, LANES)), _full_spec((1, LANES))]
    args = [dy, x, g_arr, gmat, invn]
    if has_rope:
        in_specs += [_rows_spec(tr, LANES), _rows_spec(tr, LANES), _full_spec((1, LANES))]
        args += [rope[0], rope[1], rope[2]]
    return pl.pallas_call(
        body, grid=(rows // tr,), in_specs=in_specs,
        out_specs=(_rows_spec(tr, width), _full_spec((1, width))),
        out_shape=(jax.ShapeDtypeStruct((rows, width), BF16), jax.ShapeDtypeStruct((1, width), F32)),
        scratch_shapes=[pltpu.VMEM((nblk, tr, LANES), F32)] if dil > 1 else [],
        compiler_params=_cparams(("arbitrary",)), name=name)(*args)


def _merge_fwd(zv, b_gate, ys, *, name, tr=256):
    z, gw, off = zv
    rows, d = ys[0].shape
    tr = min(tr, rows)

    def body(z_ref, b_ref, y0, y1, y2, y3, t_ref):
        acc = None
        for n, y_ref in enumerate((y0, y1, y2, y3)):
            sl = slice(n * d, (n + 1) * d)
            term = jax.nn.sigmoid(z_ref[:, sl] + b_ref[:, sl]) * y_ref[...]
            acc = term if acc is None else acc + term
        t_ref[...] = acc.astype(BF16)

    b_spec, b_arr = _layer_arg(b_gate)
    return pl.pallas_call(
        body, grid=(rows // tr,),
        in_specs=[_rows_spec(tr, gw, off), b_spec] + [_rows_spec(tr, d)] * 4,
        out_specs=_rows_spec(tr, d), out_shape=jax.ShapeDtypeStruct((rows, d), BF16),
        compiler_params=_cparams(("parallel",)), name=name)(z, b_arr, *ys)


def _merge_bwd(dt, zv, b_gate, ys, *, name, tr=256):
    z, gw, off = zv
    rows, d = dt.shape
    tr = min(tr, rows)

    def body(dt_ref, z_ref, b_ref, y0, y1, y2, y3, d0, d1, d2, d3, dg_ref, db_ref):
        i = pl.program_id(0)

        @pl.when(i == 0)
        def _():
            db_ref[...] = jnp.zeros_like(db_ref)

        dtv = dt_ref[...]
        for n, (y_ref, d_ref) in enumerate(((y0, d0), (y1, d1), (y2, d2), (y3, d3))):
            sl = slice(n * d, (n + 1) * d)
            gate = jax.nn.sigmoid(z_ref[:, sl] + b_ref[:, sl])
            d_ref[...] = (dtv * gate).astype(BF16)
            dpre = dtv * y_ref[...] * (gate * (1.0 - gate))
            dg_ref[:, sl] = dpre.astype(BF16)
            db_ref[:, sl] += jnp.sum(dpre, axis=0, keepdims=True)

    b_spec, b_arr = _layer_arg(b_gate)
    outs = pl.pallas_call(
        body, grid=(rows // tr,),
        in_specs=[_rows_spec(tr, d), _rows_spec(tr, gw, off), b_spec] + [_rows_spec(tr, d)] * 4,
        out_specs=tuple([_rows_spec(tr, d)] * 4 + [_rows_spec(tr, gw), _full_spec((1, gw))]),
        out_shape=tuple([jax.ShapeDtypeStruct((rows, d), BF16)] * 4
                        + [jax.ShapeDtypeStruct((rows, gw), BF16), jax.ShapeDtypeStruct((1, gw), F32)]),
        compiler_params=_cparams(("arbitrary",)), name=name)(dt, z, b_arr, *ys)
    return outs[:4], outs[4], outs[5]


def _group_weights(la, lb, lc):
    mx = jnp.maximum(jnp.maximum(la, lb), lc)
    ea, eb, ec = jnp.exp(la - mx), jnp.exp(lb - mx), jnp.exp(lc - mx)
    inv = 1.0 / (ea + eb + ec)
    return ea * inv, eb * inv, ec * inv


def _combine_fwd(os_, lses, dils, *, rows, width, name, tr=512):
    nblk = width // LANES
    ndil = sum(d > 1 for d in dils)

    def body(*refs):
        o_refs, l_refs, out_ref = refs[0:3], refs[3:6], refs[6]
        scr = list(refs[7:])
        o_tok, l_tok = [], []
        for g, d in enumerate(dils):
            if d > 1:
                so, sl_ = scr.pop(0), scr.pop(0)
                _undilate(o_refs[g], so, d, width)
                _undilate(l_refs[g], sl_, d, width)
                o_tok.append(lambda j, s=so: s[j])
                l_tok.append(lambda j, s=sl_: s[j])
            else:
                o_tok.append(lambda j, r=o_refs[g]: r[:, j * LANES:(j + 1) * LANES])
                l_tok.append(lambda j, r=l_refs[g]: r[:, j * LANES:(j + 1) * LANES])
        for j in range(nblk):
            wa, wb, wc = _group_weights(l_tok[0](j), l_tok[1](j), l_tok[2](j))
            out_ref[:, j * LANES:(j + 1) * LANES] = wa * o_tok[0](j) + wb * o_tok[1](j) + wc * o_tok[2](j)

    spec = lambda d: _dil_spec(tr, width, d) if d > 1 else _rows_spec(tr, width)
    return pl.pallas_call(
        body, grid=(rows // tr,), in_specs=[spec(d) for d in dils] * 2, out_specs=_rows_spec(tr, width),
        out_shape=jax.ShapeDtypeStruct((rows, width), F32),
        scratch_shapes=[pltpu.VMEM((nblk, tr, LANES), F32)] * (2 * ndil),
        compiler_params=_cparams(("parallel",)), name=name)(*os_, *lses)


def _combine_bwd(do, o, lses, dils, pat, *, name, tr=512):
    rows, width = do.shape
    gmat, _ = pat
    nblk = width // LANES
    ndil = sum(d > 1 for d in dils)

    def body(*refs):
        do_ref, o_ref = refs[0:2]
        l_refs, gm_ref = refs[2:5], refs[5]
        d_refs, e_refs = refs[6:9], refs[9:12]
        scr = list(refs[12:])
        gm = gm_ref[...]
        l_tok, d_scr, e_scr = [], [], []
        for g, d in enumerate(dils):
            if d > 1:
                sl_, sd, se = scr.pop(0), scr.pop(0), scr.pop(0)
                _undilate(l_refs[g], sl_, d, width)
                l_tok.append(lambda j, s=sl_: s[j])
                d_scr.append(sd), e_scr.append(se)
            else:
                l_tok.append(lambda j, r=l_refs[g]: r[:, j * LANES:(j + 1) * LANES])
                d_scr.append(None), e_scr.append(None)
        for j in range(nblk):
            sl = slice(j * LANES, (j + 1) * LANES)
            ws = _group_weights(l_tok[0](j), l_tok[1](j), l_tok[2](j))
            dov = do_ref[:, sl]
            tot = _gsum(dov * o_ref[:, sl], gm)
            for g, wgt in enumerate(ws):
                if dils[g] > 1:
                    d_scr[g][j] = wgt * dov
                    e_scr[g][j] = wgt * tot
                else:
                    d_refs[g][:, sl] = wgt * dov
                    e_refs[g][:, sl] = wgt * tot
        for g, d in enumerate(dils):
            if d > 1:
                _dilate(d_scr[g], d_refs[g], d, width)
                _dilate(e_scr[g], e_refs[g], d, width)

    spec = lambda d: _dil_spec(tr, width, d) if d > 1 else _rows_spec(tr, width)
    sds = lambda d: jax.ShapeDtypeStruct((rows // d, d * width), F32)
    outs = pl.pallas_call(
        body, grid=(rows // tr,),
        in_specs=[_rows_spec(tr, width)] * 2 + [spec(d) for d in dils] + [_full_spec((LANES, LANES))],
        out_specs=tuple([spec(d) for d in dils] * 2),
        out_shape=tuple([sds(d) for d in dils] * 2),
        scratch_shapes=[pltpu.VMEM((nblk, tr, LANES), F32)] * (3 * ndil),
        compiler_params=_cparams(("parallel",)), name=name)(do, o, *lses, gmat)
    return outs[:3], outs[3:]


def _delta(do, o, pat, *, lse=None, sink_row=None, name, tr=512):
    rows, width = do.shape
    gmat, _ = pat
    nblk = width // LANES
    has_sink = sink_row is not None

    def body(*refs):
        do_ref, o_ref, gm_ref = refs[:3]
        if has_sink:
            l_ref, s_ref, dl_ref, ds_ref = refs[3:]
        else:
            dl_ref = refs[3]
        gm = gm_ref[...]
        if has_sink:
            @pl.when(pl.program_id(0) == 0)
            def _():
                ds_ref[...] = jnp.zeros_like(ds_ref)
        for j in range(nblk):
            sl = slice(j * LANES, (j + 1) * LANES)
            dl = _gsum(do_ref[:, sl] * o_ref[:, sl], gm)
            dl_ref[:, sl] = dl
            if has_sink:
                ds_ref[:, sl] += jnp.sum(-jnp.exp(s_ref[:, sl] - l_ref[:, sl]) * dl, axis=0, keepdims=True)

    in_specs = [_rows_spec(tr, width), _rows_spec(tr, width), _full_spec((LANES, LANES))]
    args = [do, o, gmat]
    out_specs = [_rows_spec(tr, width)]
    out_shape = [jax.ShapeDtypeStruct((rows, width), F32)]
    if has_sink:
        s_spec, s_arr = _layer_arg(sink_row)
        in_specs += [_rows_spec(tr, width), s_spec]
        args += [lse, s_arr]
        out_specs.append(_full_spec((1, width)))
        out_shape.append(jax.ShapeDtypeStruct((1, width), F32))
    outs = pl.pallas_call(
        body, grid=(rows // tr,), in_specs=in_specs, out_specs=tuple(out_specs), out_shape=tuple(out_shape),
        compiler_params=_cparams(("arbitrary",)), name=name)(*args)
    return outs if has_sink else (outs[0], None)


def _kp_reduce(dk, *, name, tr=512):
    rows, width = dk.shape
    nblk = width // LANES

    def body(dk_ref, o_ref):
        acc = dk_ref[:, 0:LANES]
        for j in range(1, nblk):
            acc = acc + dk_ref[:, j * LANES:(j + 1) * LANES]
        lane = lax.broadcasted_iota(jnp.int32, acc.shape, 1)
        o_ref[...] = jnp.where(lane < B_ROPE, pltpu.roll(acc, 64, 1), 0.0)

    return pl.pallas_call(
        body, grid=(rows // tr,), in_specs=[_rows_spec(tr, width)], out_specs=_rows_spec(tr, LANES),
        out_shape=jax.ShapeDtypeStruct((rows, LANES), F32),
        compiler_params=_cparams(("parallel",)), name=name)(dk)


def _loss_head(y, target, *, name, tr=512):
    rows, d = y.shape

    def body(y_ref, t_ref, dy_ref, l_ref):
        @pl.when(pl.program_id(0) == 0)
        def _():
            l_ref[...] = jnp.zeros_like(l_ref)

        diff = y_ref[...] - t_ref[...]
        dy_ref[...] = diff * (1.0 / d)
        part = 0.5 * jnp.sum(jnp.mean(diff * diff, axis=-1, keepdims=True), axis=0, keepdims=True)
        l_ref[...] += jnp.broadcast_to(part, l_ref.shape)

    return pl.pallas_call(
        body, grid=(rows // tr,), in_specs=[_rows_spec(tr, d)] * 2,
        out_specs=(_rows_spec(tr, d), _full_spec((1, LANES))),
        out_shape=(jax.ShapeDtypeStruct((rows, d), F32), jax.ShapeDtypeStruct((1, LANES), F32)),
        compiler_params=_cparams(("arbitrary",)), name=name)(y, target)


def _adam(recvs, w, m, v, *, name):
    nl = len(recvs)
    rows_l, cols = recvs[0].shape[1:]
    assert w.shape == (nl * rows_l, cols)
    tr = _pick(rows_l, (128, 64, 32, 16, 8))
    nt = rows_l // tr
    c1 = 1.0 / (1.0 - ADAM_B1 ** ADAM_STEP)
    c2 = 1.0 / (1.0 - ADAM_B2 ** ADAM_STEP)

    def body(*refs):
        r_refs = refs[:nl]
        w_ref, m_ref, v_ref, g_out, d_out, m_out, v_out = refs[nl:]
        layer = pl.program_id(0)

        def update(r_ref):
            g = r_ref[0].astype(F32)
            for j in range(1, N_DEV):
                g = g + r_ref[j].astype(F32)
            mn = ADAM_B1 * m_ref[...] + (1.0 - ADAM_B1) * g
            vn = ADAM_B2 * v_ref[...] + (1.0 - ADAM_B2) * (g * g)
            g_out[...] = g
            m_out[...] = mn
            v_out[...] = vn
            d_out[...] = -ADAM_LR * ((mn * c1) / (jnp.sqrt(vn * c2) + ADAM_EPS) + ADAM_WD * w_ref[...])

        for k in range(nl):
            pl.when(layer == k)(functools.partial(update, r_refs[k]))

    spec = pl.BlockSpec((tr, cols), lambda l, i: (l * nt + i, 0))
    rspec = lambda k: pl.BlockSpec((N_DEV, tr, cols), lambda l, i: (0, jnp.where(l == k, i, 0), 0))
    sds = jax.ShapeDtypeStruct(w.shape, F32)
    return pl.pallas_call(
        body, grid=(nl, nt), in_specs=[rspec(k) for k in range(nl)] + [spec, spec, spec],
        out_specs=(spec,) * 4, out_shape=(sds,) * 4,
        compiler_params=_cparams(("arbitrary", "arbitrary")), name=name)(*recvs, w, m, v)


def _band_mask(i, maxdist, nrow_blocks):
    row = lax.broadcasted_iota(jnp.int32, (BLOCK, 2 * BLOCK), 0)
    col = lax.broadcasted_iota(jnp.int32, (BLOCK, 2 * BLOCK), 1)
    dist = row - col + BLOCK
    return (dist >= 0) & (dist <= maxdist) & ((i > 0) | (col >= BLOCK))


def _band_views(q, k, zv, dil):
    z, kvw, voff = zv
    if dil == 1:
        return q, k, z, z.shape[1] // kvw, voff
    vd = z[:, voff * kvw:(voff + 1) * kvw].reshape(z.shape[0] // dil, dil * kvw)
    return q, k, vd, 1, 0


def _band_fwd(q, k, zv, *, dil, group, maxdist, sinks=None, name, xchg=()):
    s, qw = q.shape[0] * dil, q.shape[1] // dil
    kvw = zv[1]
    qd, kd, zd, zblocks, voff = _band_views(q, k, zv, dil)
    nb = s // dil // BLOCK
    nheads = qw // 64
    scale = 64 ** -0.5
    has_sink = sinks is not None

    def body(*refs):
        q_ref, kp_ref, kc_ref, vp_ref, vc_ref = refs[:5]
        s_ref = refs[5] if has_sink else None
        o_ref, l_ref, sc_s, p_s = refs[5 + has_sink:]
        i = pl.program_id(1)
        mask = _band_mask(i, maxdist, nb)
        k2 = jnp.concatenate([kp_ref[...], kc_ref[...]], axis=0)
        v2 = jnp.concatenate([vp_ref[...], vc_ref[...]], axis=0).astype(BF16)
        for h in range(nheads):
            g = h // group
            sc_s[h] = _dot(q_ref[:, 64 * h:64 * h + 64], k2[:, 64 * g:64 * g + 64], NT_DIMS)
        sc = jnp.where(mask[None], sc_s[...] * scale, NEG)
        mx = jnp.max(sc, axis=2, keepdims=True)
        if has_sink:
            snk = s_ref[:, :, 0:1]
            mx = jnp.maximum(mx, snk)
        e = jnp.exp(sc - mx)
        den = jnp.sum(e, axis=2, keepdims=True)
        if has_sink:
            den = den + jnp.exp(snk - mx)
        p_s[...] = (e * (1.0 / den)).astype(BF16)
        lse = mx + jnp.log(den)
        for h in range(nheads):
            g = h // group
            o_ref[:, 64 * h:64 * h + 64] = _dot(p_s[h], v2[:, 64 * g:64 * g + 64])
            l_ref[:, 64 * h:64 * h + 64] = jnp.broadcast_to(lse[h], (BLOCK, 64))

    prev = lambda r, i: jnp.maximum(i - 1, 0)
    in_specs = [
        pl.BlockSpec((BLOCK, qw), lambda r, i: (i, r)),
        pl.BlockSpec((BLOCK, kvw), lambda r, i: (prev(r, i), r)),
        pl.BlockSpec((BLOCK, kvw), lambda r, i: (i, r)),
        pl.BlockSpec((BLOCK, kvw), lambda r, i: (prev(r, i), r * zblocks + voff)),
        pl.BlockSpec((BLOCK, kvw), lambda r, i: (i, r * zblocks + voff)),
    ]
    args = [qd, kd, kd, zd, zd]
    if has_sink:
        s_spec, s_arr = _layer_arg(sinks)
        in_specs.append(s_spec)
        args.append(s_arr)
    ospec = pl.BlockSpec((BLOCK, qw), lambda r, i: (i, r))
    sds = jax.ShapeDtypeStruct((s // dil, dil * qw), F32)
    (o, lse), got = _pcall(
        body, grid=(dil, nb), in_specs=in_specs, out_specs=(ospec, ospec), out_shape=(sds, sds),
        scratch_shapes=[pltpu.VMEM((nheads, BLOCK, 2 * BLOCK), F32), pltpu.VMEM((nheads, BLOCK, 2 * BLOCK), BF16)],
        sem=("parallel", "parallel"), name=name, args=args, xchg=xchg)
    return (o, lse), got


def _band_dq(q, k, zv, do, lse, delta, *, dil, group, maxdist, name):
    s, qw = q.shape[0] * dil, q.shape[1] // dil
    kvw = zv[1]
    qd, kd, zd, zblocks, voff = _band_views(q, k, zv, dil)
    nb = s // dil // BLOCK
    nheads = qw // 64
    scale = 64 ** -0.5

    def body(q_ref, kp_ref, kc_ref, vp_ref, vc_ref, do_ref, l_ref, d_ref, dq_ref, sc_s, dp_s, ds_s, st_s):
        i = pl.program_id(1)
        mask = _band_mask(i, maxdist, nb)
        k2 = jnp.concatenate([kp_ref[...], kc_ref[...]], axis=0)
        v2 = jnp.concatenate([vp_ref[...], vc_ref[...]], axis=0).astype(BF16)
        dob = do_ref[...].astype(BF16)
        for h in range(nheads):
            g = h // group
            hs = slice(64 * h, 64 * h + 64)
            sc_s[h] = _dot(q_ref[:, hs], k2[:, 64 * g:64 * g + 64], NT_DIMS)
            dp_s[h] = _dot(dob[:, hs], v2[:, 64 * g:64 * g + 64], NT_DIMS)
            st_s[0, h] = l_ref[:, 64 * h:64 * h + 1]
            st_s[1, h] = d_ref[:, 64 * h:64 * h + 1]
        p = jnp.exp(jnp.where(mask[None], sc_s[...] * scale, NEG) - st_s[0])
        ds_s[...] = (p * (dp_s[...] - st_s[1]) * scale).astype(BF16)
        for h in range(nheads):
            g = h // group
            dq_ref[:, 64 * h:64 * h + 64] = _dot(ds_s[h], k2[:, 64 * g:64 * g + 64])

    prev = lambda r, i: jnp.maximum(i - 1, 0)
    qspec = pl.BlockSpec((BLOCK, qw), lambda r, i: (i, r))
    in_specs = [
        qspec,
        pl.BlockSpec((BLOCK, kvw), lambda r, i: (prev(r, i), r)),
        pl.BlockSpec((BLOCK, kvw), lambda r, i: (i, r)),
        pl.BlockSpec((BLOCK, kvw), lambda r, i: (prev(r, i), r * zblocks + voff)),
        pl.BlockSpec((BLOCK, kvw), lambda r, i: (i, r * zblocks + voff)),
        qspec, qspec, qspec,
    ]
    dq = pl.pallas_call(
        body, grid=(dil, nb), in_specs=in_specs, out_specs=qspec,
        out_shape=jax.ShapeDtypeStruct((s // dil, dil * qw), F32),
        scratch_shapes=[pltpu.VMEM((nheads, BLOCK, 2 * BLOCK), F32), pltpu.VMEM((nheads, BLOCK, 2 * BLOCK), F32),
                        pltpu.VMEM((nheads, BLOCK, 2 * BLOCK), BF16), pltpu.VMEM((2, nheads, BLOCK, 1), F32)],
        compiler_params=_cparams(("parallel", "parallel")), name=name)(qd, kd, kd, zd, zd, do, lse, delta)
    return dq


def _band_dkv(q, k, zv, do, lse, delta, *, dil, group, maxdist, name):
    s, qw = q.shape[0] * dil, q.shape[1] // dil
    kvw = zv[1]
    qd, kd, zd, zblocks, voff = _band_views(q, k, zv, dil)
    nb = s // dil // BLOCK
    nheads = qw // 64
    nkv = kvw // 64
    scale = 64 ** -0.5

    def body(k_ref, v_ref, q0, q1, do0, do1, l0, l1, d0, d1, dk_ref, dv_ref, sc_s, dp_s, p_s, ds_s, st_s):
        j = pl.program_id(1)
        row = lax.broadcasted_iota(jnp.int32, (2 * BLOCK, BLOCK), 0)
        col = lax.broadcasted_iota(jnp.int32, (2 * BLOCK, BLOCK), 1)
        dist = row - col
        mask = (dist >= 0) & (dist <= maxdist) & ((j < nb - 1) | (row < BLOCK))
        q2 = jnp.concatenate([q0[...], q1[...]], axis=0)
        do2 = jnp.concatenate([do0[...], do1[...]], axis=0).astype(BF16)
        vb = v_ref[...].astype(BF16)
        for h in range(nheads):
            g = h // group
            hs = slice(64 * h, 64 * h + 64)
            sc_s[h] = _dot(q2[:, hs], k_ref[:, 64 * g:64 * g + 64], NT_DIMS)
            dp_s[h] = _dot(do2[:, hs], vb[:, 64 * g:64 * g + 64], NT_DIMS)
            st_s[0, h, 0:BLOCK] = l0[:, 64 * h:64 * h + 1]
            st_s[0, h, BLOCK:2 * BLOCK] = l1[:, 64 * h:64 * h + 1]
            st_s[1, h, 0:BLOCK] = d0[:, 64 * h:64 * h + 1]
            st_s[1, h, BLOCK:2 * BLOCK] = d1[:, 64 * h:64 * h + 1]
        p = jnp.exp(jnp.where(mask[None], sc_s[...] * scale, NEG) - st_s[0])
        p_s[...] = p.astype(BF16)
        ds_s[...] = (p * (dp_s[...] - st_s[1]) * scale).astype(BF16)
        dks = [None] * nkv
        dvs = [None] * nkv
        for h in range(nheads):
            g = h // group
            hs = slice(64 * h, 64 * h + 64)
            dvh = _dot(p_s[h], do2[:, hs], TN_DIMS)
            dkh = _dot(ds_s[h], q2[:, hs], TN_DIMS)
            dks[g] = dkh if dks[g] is None else dks[g] + dkh
            dvs[g] = dvh if dvs[g] is None else dvs[g] + dvh
        for g in range(nkv):
            dk_ref[:, 64 * g:64 * g + 64] = dks[g]
            dv_ref[:, 64 * g:64 * g + 64] = dvs[g].astype(BF16)

    nxt = lambda r, j: jnp.minimum(j + 1, nb - 1)
    q0s = pl.BlockSpec((BLOCK, qw), lambda r, j: (j, r))
    q1s = pl.BlockSpec((BLOCK, qw), lambda r, j: (nxt(r, j), r))
    kspec = pl.BlockSpec((BLOCK, kvw), lambda r, j: (j, r))
    in_specs = [kspec, pl.BlockSpec((BLOCK, kvw), lambda r, j: (j, r * zblocks + voff)),
                q0s, q1s, q0s, q1s, q0s, q1s, q0s, q1s]
    dod, lsd, dld = do, lse, delta
    dk, dv = pl.pallas_call(
        body, grid=(dil, nb), in_specs=in_specs, out_specs=(kspec, kspec),
        out_shape=(jax.ShapeDtypeStruct((s // dil, dil * kvw), F32), jax.ShapeDtypeStruct((s // dil, dil * kvw), BF16)),
        scratch_shapes=[pltpu.VMEM((nheads, 2 * BLOCK, BLOCK), F32), pltpu.VMEM((nheads, 2 * BLOCK, BLOCK), F32),
                        pltpu.VMEM((nheads, 2 * BLOCK, BLOCK), BF16), pltpu.VMEM((nheads, 2 * BLOCK, BLOCK), BF16),
                        pltpu.VMEM((2, nheads, 2 * BLOCK, 1), F32)],
        compiler_params=_cparams(("parallel", "parallel")), name=name)(kd, zd, qd, qd, dod, dod, lsd, lsd, dld, dld)
    return dk, dv.reshape(s, kvw)


HPG = 2


def _flash_tiles(sq, sk):
    return _pick(sq, (512, 256, 128)), _pick(sk, (512, 256, 128))


def _diag_mask(t):
    return lax.broadcasted_iota(jnp.int32, (t, t), 1) <= lax.broadcasted_iota(jnp.int32, (t, t), 0)


def _attn_fwd(q, k, vv, *, dv, causal, scale, name, xchg=None):
    sq, qw = q.shape
    sk = k.shape[0]
    varr, vbw, voff = vv
    assert vbw == HPG * dv
    ngrp = qw // (HPG * LANES)
    tq, tk = _flash_tiles(sq, sk)
    nq, nk = sq // tq, sk // tk
    if causal:
        assert tq == tk and sq == sk

    def body(q_ref, k_ref, v_ref, o_ref, l_ref, m_s, l_s, acc_s):
        i = pl.program_id(1)
        m_s[...] = jnp.full_like(m_s, NEG)
        l_s[...] = jnp.zeros_like(l_s)
        acc_s[...] = jnp.zeros_like(acc_s)

        def step(j, masked):
            rows = pl.ds(pl.multiple_of(j * tk, tk), tk)
            if masked:
                mask = _diag_mask(tq)
            for h in range(HPG):
                hs = slice(LANES * h, LANES * (h + 1))
                sc = _dot(q_ref[:, hs], k_ref[rows, hs], NT_DIMS) * scale
                if masked:
                    sc = jnp.where(mask, sc, NEG)
                m_prev = m_s[h]
                m_new = jnp.maximum(m_prev, jnp.max(sc, axis=1, keepdims=True))
                alpha = jnp.exp(m_prev - m_new)
                p = jnp.exp(sc - m_new[:, 0:1])
                l_s[h] = alpha * l_s[h] + jnp.sum(p, axis=1, keepdims=True)
                acc_s[h] = acc_s[h] * alpha[:, 0:dv] + _dot(p, v_ref[rows, dv * h:dv * (h + 1)])
                m_s[h] = m_new

        def loop_body(j, carry):
            step(j, False)
            return carry

        if causal:
            lax.fori_loop(0, i, loop_body, 0)
            step(i, True)
        else:
            lax.fori_loop(0, nk, loop_body, 0)
        for h in range(HPG):
            o_ref[:, dv * h:dv * (h + 1)] = acc_s[h] * (1.0 / l_s[h][:, 0:dv])
            l_ref[:, LANES * h:LANES * (h + 1)] = m_s[h] + jnp.log(l_s[h])

    in_specs = [
        pl.BlockSpec((tq, HPG * LANES), lambda g, i: (i, g)),
        pl.BlockSpec((sk, HPG * LANES), lambda g, i: (0, g)),
        pl.BlockSpec((sk, vbw), lambda g, i: (0, voff + g)),
    ]
    return _pcall(
        body, grid=(ngrp, nq), in_specs=in_specs,
        out_specs=(pl.BlockSpec((tq, HPG * dv), lambda g, i: (i, g)),
                   pl.BlockSpec((tq, HPG * LANES), lambda g, i: (i, g))),
        out_shape=(jax.ShapeDtypeStruct((sq, ngrp * HPG * dv), F32), jax.ShapeDtypeStruct((sq, qw), F32)),
        scratch_shapes=[pltpu.VMEM((HPG, tq, LANES), F32), pltpu.VMEM((HPG, tq, LANES), F32),
                        pltpu.VMEM((HPG, tq, dv), F32)],
        sem=("parallel", "arbitrary"), name=name, args=(q, k, varr), xchg=xchg)


def _attn_dq(q, k, vv, do, lse, delta, *, dv, causal, scale, name, xchg=None):
    sq, qw = q.shape
    sk = k.shape[0]
    varr, vbw, voff = vv
    ngrp = qw // (HPG * LANES)
    tq, tk = _flash_tiles(sq, sk)
    nq, nk = sq // tq, sk // tk

    def body(q_ref, k_ref, v_ref, do_ref, l_ref, d_ref, dq_ref, acc_s):
        i = pl.program_id(1)
        acc_s[...] = jnp.zeros_like(acc_s)

        def step(j, masked):
            rows = pl.ds(pl.multiple_of(j * tk, tk), tk)
            if masked:
                mask = _diag_mask(tq)
            for h in range(HPG):
                hs = slice(LANES * h, LANES * (h + 1))
                kh = k_ref[rows, hs]
                sc = _dot(q_ref[:, hs], kh, NT_DIMS) * scale
                if masked:
                    sc = jnp.where(mask, sc, NEG)
                p = jnp.exp(sc - l_ref[:, LANES * h:LANES * h + 1])
                dp = _dot(do_ref[:, dv * h:dv * (h + 1)], v_ref[rows, dv * h:dv * (h + 1)], NT_DIMS)
                ds = p * (dp - d_ref[:, dv * h:dv * h + 1]) * scale
                acc_s[h] += _dot(ds, kh)

        def loop_body(j, carry):
            step(j, False)
            return carry

        if causal:
            lax.fori_loop(0, i, loop_body, 0)
            step(i, True)
        else:
            lax.fori_loop(0, nk, loop_body, 0)
        for h in range(HPG):
            dq_ref[:, LANES * h:LANES * (h + 1)] = acc_s[h]

    qspec = pl.BlockSpec((tq, HPG * LANES), lambda g, i: (i, g))
    ospec = pl.BlockSpec((tq, HPG * dv), lambda g, i: (i, g))
    in_specs = [
        qspec,
        pl.BlockSpec((sk, HPG * LANES), lambda g, i: (0, g)),
        pl.BlockSpec((sk, vbw), lambda g, i: (0, voff + g)),
        ospec, qspec, ospec,
    ]
    return _pcall(
        body, grid=(ngrp, nq), in_specs=in_specs, out_specs=qspec,
        out_shape=jax.ShapeDtypeStruct((sq, qw), F32),
        scratch_shapes=[pltpu.VMEM((HPG, tq, LANES), F32)],
        sem=("parallel", "arbitrary"), name=name, args=(q, k, varr, do, lse, delta), xchg=xchg)


def _attn_dkv(q, k, vv, do, lse, delta, *, dv, causal, scale, name, xchg=None):
    sq, qw = q.shape
    sk = k.shape[0]
    varr, vbw, voff = vv
    ngrp = qw // (HPG * LANES)
    tq, tk = _flash_tiles(sq, sk)
    nq, nk = sq // tq, sk // tk

    def body(q_ref, k_ref, v_ref, do_ref, l_ref, d_ref, dk_ref, dv_ref, dk_s, dv_s):
        j = pl.program_id(1)
        dk_s[...] = jnp.zeros_like(dk_s)
        dv_s[...] = jnp.zeros_like(dv_s)

        def step(i, masked):
            rows = pl.ds(pl.multiple_of(i * tq, tq), tq)
            if masked:
                mask = _diag_mask(tq)
            for h in range(HPG):
                qh = q_ref[rows, LANES * h:LANES * (h + 1)]
                doh = do_ref[rows, dv * h:dv * (h + 1)]
                sc = _dot(qh, k_ref[:, LANES * h:LANES * (h + 1)], NT_DIMS) * scale
                if masked:
                    sc = jnp.where(mask, sc, NEG)
                p = jnp.exp(sc - l_ref[rows, LANES * h:LANES * h + 1])
                dv_s[h] += _dot(p, doh, TN_DIMS)
                dp = _dot(doh, v_ref[:, dv * h:dv * (h + 1)], NT_DIMS)
                ds = p * (dp - d_ref[rows, dv * h:dv * h + 1]) * scale
                dk_s[h] += _dot(ds, qh, TN_DIMS)

        def loop_body(i, carry):
            step(i, False)
            return carry

        if causal:
            step(j, True)
            lax.fori_loop(j + 1, nq, loop_body, 0)
        else:
            lax.fori_loop(0, nq, loop_body, 0)
        for h in range(HPG):
            dk_ref[:, LANES * h:LANES * (h + 1)] = dk_s[h]
            dv_ref[:, dv * h:dv * (h + 1)] = dv_s[h].astype(BF16)

    qspec = pl.BlockSpec((sq, HPG * LANES), lambda g, j: (0, g))
    ospec = pl.BlockSpec((sq, HPG * dv), lambda g, j: (0, g))
    kspec = pl.BlockSpec((tk, HPG * LANES), lambda g, j: (j, g))
    in_specs = [qspec, kspec, pl.BlockSpec((tk, vbw), lambda g, j: (j, voff + g)), ospec, qspec, ospec]
    return _pcall(
        body, grid=(ngrp, nk), in_specs=in_specs,
        out_specs=(kspec, pl.BlockSpec((tk, HPG * dv), lambda g, j: (j, g))),
        out_shape=(jax.ShapeDtypeStruct((sk, qw), F32), jax.ShapeDtypeStruct((sk, ngrp * HPG * dv), BF16)),
        scratch_shapes=[pltpu.VMEM((HPG, tk, LANES), F32), pltpu.VMEM((HPG, tk, dv), F32)],
        sem=("parallel", "arbitrary"), name=name, args=(q, k, varr, do, lse, delta), xchg=xchg)


def _xchg_copies(src_ref, out_ref, send_sems, recv_sems, local_sems, k, scatter):
    me = 4 * lax.axis_index("x") + 2 * lax.axis_index("y") + lax.axis_index("c")
    own = src_ref.at[me] if scatter else src_ref
    copies = [pltpu.make_async_copy(own, out_ref.at[me], local_sems.at[k])]
    for d in range(1, N_DEV):
        peer = (me + d) % N_DEV
        piece = src_ref.at[peer] if scatter else src_ref
        sem = (N_DEV - 1) * k + d - 1
        copies.append(pltpu.make_async_remote_copy(
            src_ref=piece, dst_ref=out_ref.at[me], send_sem=send_sems.at[sem], recv_sem=recv_sems.at[sem],
            device_id=(peer // 4, (peer // 2) % 2, peer % 2), device_id_type=pl.DeviceIdType.MESH))
    return copies


def _xchg_shapes(xchg):
    return [jax.ShapeDtypeStruct((N_DEV,) + tuple(src.shape[1:] if scatter else src.shape), src.dtype)
            for src, scatter in xchg]


def _xchg_sems(nx):
    return [pltpu.SemaphoreType.DMA(((N_DEV - 1) * nx,)), pltpu.SemaphoreType.DMA(((N_DEV - 1) * nx,)),
            pltpu.SemaphoreType.DMA((nx,))]


ANY_SPEC = pl.BlockSpec(memory_space=pl.ANY)


def _exchange_many(xchg, *, name):
    nx = len(xchg)

    def body(*refs):
        srcs, outs = refs[:nx], refs[nx:2 * nx]
        send_sems, recv_sems, local_sems = refs[2 * nx:]
        copies = [c for k, (_, scatter) in enumerate(xchg)
                  for c in _xchg_copies(srcs[k], outs[k], send_sems, recv_sems, local_sems, k, scatter)]
        for c in copies:
            c.start()
        for c in copies:
            c.wait()

    return pl.pallas_call(
        body, in_specs=[ANY_SPEC] * nx, out_specs=tuple([ANY_SPEC] * nx), out_shape=tuple(_xchg_shapes(xchg)),
        scratch_shapes=_xchg_sems(nx), name=name)(*[src for src, _ in xchg])


def _pcall(body, *, grid, in_specs, out_specs, out_shape, scratch_shapes=(), sem, name, args, xchg=None):
    single = not isinstance(out_shape, (tuple, list))
    if not xchg:
        res = pl.pallas_call(
            body, grid=grid, in_specs=list(in_specs), out_specs=out_specs, out_shape=out_shape,
            scratch_shapes=list(scratch_shapes), compiler_params=_cparams(sem), name=name)(*args)
        return res if xchg is None else (res, [])
    out_specs_l = [out_specs] if single else list(out_specs)
    out_shape_l = [out_shape] if single else list(out_shape)
    nx, n_in, n_out, n_scr = len(xchg), len(in_specs), len(out_shape_l), len(scratch_shapes)

    def wrapped(*refs):
        ins, xin = refs[:n_in], refs[n_in:n_in + nx]
        pos = n_in + nx
        outs, xout = refs[pos:pos + n_out], refs[pos + n_out:pos + n_out + nx]
        pos += n_out + nx
        scr = refs[pos:pos + n_scr]
        send_sems, recv_sems, local_sems = refs[pos + n_scr:]
        first = functools.reduce(jnp.logical_and, [pl.program_id(a) == 0 for a in range(len(grid))])
        last = functools.reduce(jnp.logical_and, [pl.program_id(a) == grid[a] - 1 for a in range(len(grid))])

        def copies():
            return [c for k, (_, scatter) in enumerate(xchg)
                    for c in _xchg_copies(xin[k], xout[k], send_sems, recv_sems, local_sems, k, scatter)]

        @pl.when(first)
        def _():
            for c in copies():
                c.start()

        body(*ins, *outs, *scr)

        @pl.when(last)
        def _():
            for c in copies():
                c.wait()

    res = pl.pallas_call(
        wrapped, grid=grid, in_specs=list(in_specs) + [ANY_SPEC] * nx,
        out_specs=tuple(out_specs_l + [ANY_SPEC] * nx), out_shape=tuple(out_shape_l + _xchg_shapes(xchg)),
        scratch_shapes=list(scratch_shapes) + _xchg_sems(nx),
        compiler_params=_cparams(("arbitrary",) * len(grid)), name=name + "_x")(*args, *[src for src, _ in xchg])
    main = res[0] if single else tuple(res[:n_out])
    return main, list(res[n_out:])


def _unshard(wg, axis):
    nd = wg.ndim - 1
    perm = list(range(1, axis + 1)) + [0] + list(range(axis + 1, nd + 1))
    t = wg.transpose(perm)
    shp = list(wg.shape[1:])
    shp[axis] *= N_DEV
    return t.reshape(shp)


def _shard_major(full, axis):
    shp = list(full.shape)
    shp[axis:axis + 1] = [N_DEV, shp[axis] // N_DEV]
    t = full.reshape(shp)
    perm = [axis] + list(range(axis)) + list(range(axis + 1, len(shp)))
    return t.transpose(perm)


BIG = ("w_in", "b_w_uq", "b_w_ukv", "m_w_kv", "w_branch", "w_out", "w_up", "w_down")
BIG_AXIS = dict(w_in=2, b_w_uq=2, b_w_ukv=2, m_w_kv=1, w_branch=3, w_out=1, w_up=2, w_down=1)
SMALL = ("g_mix", "b_gate", "a_qn", "a_kn", "a_sink", "b_qa_norm", "b_kva_norm", "b_qn", "b_kn", "c_qn", "c_kn",
         "m_g_mem", "m_qn", "m_kn", "g_mlp")
WEIGHTS = ("g_mix", "w_in", "b_gate", "a_qn", "a_kn", "a_sink", "b_qa_norm", "b_kva_norm", "b_w_uq", "b_w_ukv", "b_qn",
           "b_kn", "c_qn", "c_kn", "m_g_mem", "m_w_kv", "m_qn", "m_kn", "w_branch", "w_out", "g_mlp", "w_up", "w_down")


def _rope_tables(positions, dim):
    inv = ROPE_THETA ** (-jnp.arange(0, dim, 2, dtype=F32) / dim)
    ang = positions.astype(F32)[:, None] * inv
    return jnp.cos(ang), jnp.sin(ang)


def _lane_mask(lo, hi):
    lane = np.arange(LANES)
    return jnp.asarray(((lane >= lo) & (lane < hi)).astype(np.float32)[None, :])


def _step(x, mem, positions, w, moments_m, moments_v, loss_target):
    s_len, d_model = x.shape
    n_layers = w["g_mix"].shape[0]
    lay, orig_names, order, n_in, nw = _in_layout(d_model)
    unit = lambda name: lay[name]["off"] // lay[name]["nw"]

    def ag_src(l, names):
        return [(w[n][l].astype(BF16), False) for n in names]

    def layout(name, block):
        full = _unshard(block, BIG_AXIS[name] - 1)
        if name == "w_in":
            return dict(w_in=_permute_cols(full, lay, order))
        if name == "b_w_uq":
            uq = full.reshape(B_Q_LORA, B_HEADS, B_NOPE + B_ROPE)
            return dict(uq=jnp.pad(uq, ((0, 0), (0, 0), (0, LANES - B_NOPE - B_ROPE))).reshape(B_Q_LORA, B_HEADS * LANES))
        if name == "b_w_ukv":
            ukv = full.reshape(B_KV_LORA, B_HEADS, B_NOPE + B_V)
            return dict(ukv=jnp.concatenate([
                jnp.pad(ukv[..., :B_NOPE], ((0, 0), (0, 0), (0, LANES - B_NOPE))).reshape(B_KV_LORA, B_HEADS * LANES),
                ukv[..., B_NOPE:].reshape(B_KV_LORA, B_HEADS * B_V)], axis=-1))
        return {name: full}

    def rs_src(name, g):
        return (_shard_major(g, BIG_AXIS[name] - 1).astype(BF16), True)

    p64 = _pattern([64, 64])
    p128 = _pattern([128])
    pq = _pattern([64, 32, 32])
    pkr = _pattern([32, 32, 32, 32])
    cos_h, sin_h = _rope_tables(positions, A_HD)
    cos_r, sin_r = _rope_tables(positions, B_ROPE)
    ones = lambda n: jnp.ones((s_len, n), F32)
    zeros = lambda n: jnp.zeros((s_len, n), F32)
    rope64 = (jnp.tile(cos_h, (1, 4)), jnp.tile(jnp.concatenate([-sin_h, sin_h], axis=1), (1, 2)),
              jnp.asarray((np.arange(LANES) % 64 < 32).astype(np.float32)[None, :]), 32)
    rope_q = (jnp.concatenate([ones(64), cos_r, cos_r, ones(32)], axis=1),
              jnp.concatenate([zeros(64), -sin_r, sin_r, zeros(32)], axis=1), _lane_mask(64, 80), 16)
    rope_k = (jnp.concatenate([cos_r, cos_r, ones(96)], axis=1),
              jnp.concatenate([-sin_r, sin_r, zeros(96)], axis=1), _lane_mask(0, 16), 16)

    scale_b = (B_NOPE + B_ROPE) ** -0.5
    scale_m = M_HD ** -0.5

    n_l = n_layers
    per_head = lambda g, nh: jnp.tile(g, (1, nh))[:, None, :]
    lpad = lambda g, n: jnp.pad(g, ((0, 0), (0, n)))
    gains_all = dict(
        g_mix=w["g_mix"][:, None, :], g_mlp=w["g_mlp"][:, None, :], m_g_mem=w["m_g_mem"][:, None, :],
        b_gate=w["b_gate"][:, None, :], b_qa_norm=w["b_qa_norm"][:, None, :], b_kva_norm=w["b_kva_norm"][:, None, :],
        a_q=per_head(w["a_qn"], A_HEADS), a_k=per_head(w["a_kn"], A_KV_HEADS),
        b_q=per_head(lpad(w["b_qn"], 32), B_HEADS), b_k=per_head(lpad(w["b_kn"][:, :B_NOPE], 64), B_HEADS),
        b_kr=lpad(w["b_kn"][:, B_NOPE:], 96)[:, None, :],
        c_q=[per_head(w["c_qn"][:, g], C_HEADS) for g in range(3)],
        c_k=[per_head(w["c_kn"][:, g], C_HEADS) for g in range(3)],
        m_q=per_head(w["m_qn"], M_HEADS), m_k=per_head(w["m_kn"], M_HEADS),
        sink_row=jnp.repeat(w["a_sink"], A_HD, axis=1)[:, None, :],
        sinks=jnp.broadcast_to(w["a_sink"][:, :, None, None], (n_l, A_HEADS, 1, LANES)))

    def gains(l):
        return {k: ([(a, l) for a in v] if isinstance(v, list) else (v, l)) for k, v in gains_all.items()}

    ag_hosts = dict(mm_in=("w_up",), band_a=("w_down",), flash_b=("w_in",), mm_up=("w_branch", "w_out"),
                    mm_down=("m_w_kv", "b_w_uq", "b_w_ukv"))
    first_alone = ("w_in", "b_w_uq", "b_w_ukv", "m_w_kv")
    first_hosts = dict(band_c0=("w_up",), band_c1=("w_down",), band_c2=("w_branch", "w_out"))
    saved = []
    weights = [dict() for _ in range(n_layers)]

    def arrived(layer, names, blocks):
        for name, blk in zip(names, blocks):
            weights[layer].update(layout(name, blk))

    cur = x
    arrived(0, first_alone, _exchange_many(ag_src(0, first_alone), name="ag_first"))
    for l in range(n_layers):
        gn = gains(l)
        sv = dict(x=cur)
        wl = weights[l]

        def carry(host):
            return ag_src(l + 1, ag_hosts[host]) if l + 1 < n_layers else []

        def carry0(host):
            return ag_src(0, first_hosts[host]) if l == 0 else []

        h = _rmsnorm_fwd((cur, d_model, 0), gn["g_mix"], name="f_norm_mix")
        z, got = _mm(h, wl["w_in"], name="f_mm_in", xchg=carry("mm_in"))
        arrived(l + 1, ag_hosts["mm_in"], got)
        sv.update(h=h, z=z)
        qa = _headnorm_fwd((z, 512, unit("a_q")), gn["a_q"], p64, rope64, name="f_hn_aq")
        ka = _headnorm_fwd((z, 128, unit("a_k")), gn["a_k"], p64, rope64, name="f_hn_ak")
        va = (z, 128, unit("a_v"))
        (o_a, lse_a), got = _band_fwd(qa, ka, va, dil=1, group=A_HEADS // A_KV_HEADS, maxdist=A_WINDOW - 1,
                                      sinks=gn["sinks"], name="f_band_a", xchg=carry("band_a"))
        arrived(l + 1, ag_hosts["band_a"], got)
        sv.update(qa=qa, ka=ka, o_a=o_a, lse_a=lse_a)
        cq = _rmsnorm_fwd((z, B_Q_LORA, unit("b_cq")), gn["b_qa_norm"], name="f_norm_cq")
        q_up = _mm(cq, wl["uq"], name="f_mm_uq")
        ckv = _rmsnorm_fwd((z, B_KV_LORA, unit("b_ckv")), gn["b_kva_norm"], name="f_norm_ckv")
        kv_up = _mm(ckv, wl["ukv"], name="f_mm_ukv")
        kp = _headnorm_fwd((z, LANES, unit("b_kr")), gn["b_kr"], pkr, rope_k, name="f_hn_kr")
        qb = _headnorm_fwd((q_up, B_HEADS * LANES, 0), gn["b_q"], pq, rope_q, name="f_hn_bq")
        kb = _headnorm_fwd((kv_up, B_HEADS * LANES, 0), gn["b_k"], p64, None, addend=kp, name="f_hn_bk")
        vb = (kv_up, HPG * B_V, B_HEADS * LANES // (HPG * B_V))
        (o_b, lse_b), got = _attn_fwd(qb, kb, vb, dv=B_V, causal=True, scale=scale_b, name="f_flash_b",
                                       xchg=carry("flash_b"))
        arrived(l + 1, ag_hosts["flash_b"], got)
        sv.update(cq=cq, q_up=q_up, ckv=ckv, kv_up=kv_up, qb=qb, kb=kb, o_b=o_b, lse_b=lse_b)
        qc, kc, o_cg, lse_cg = [], [], [], []
        for g, (win, dil) in enumerate(C_PATTERNS):
            qg = _headnorm_fwd((z, 512, unit("c%d" % (3 * g))), gn["c_q"][g], p64, rope64, dil=dil,
                               name="f_hn_cq%d" % g)
            kg = _headnorm_fwd((z, 512, unit("c%d" % (3 * g + 1))), gn["c_k"][g], p64, rope64, dil=dil,
                               name="f_hn_ck%d" % g)
            host = "band_c%d" % g
            (og, lg), got = _band_fwd(qg, kg, (z, 512, unit("c%d" % (3 * g + 2))), dil=dil, group=1,
                                      maxdist=win // dil, name="f_band_c%d" % g,
                                      xchg=carry0(host) if host in first_hosts else ())
            arrived(0, first_hosts.get(host, ()), got)
            qc.append(qg), kc.append(kg), o_cg.append(og), lse_cg.append(lg)
        c_dils = [dil for _, dil in C_PATTERNS]
        o_c = _combine_fwd(o_cg, lse_cg, c_dils, rows=s_len, width=BRANCH_W, name="f_combine")
        sv.update(qc=qc, kc=kc, o_cg=o_cg, lse_cg=lse_cg, o_c=o_c)
        memn = _rmsnorm_fwd((mem, d_model, 0), gn["m_g_mem"], name="f_norm_mem")
        mkv = _mm(memn, wl["m_w_kv"], name="f_mm_mkv")
        mk = _headnorm_fwd((mkv, M_HEADS * M_HD, 0), gn["m_k"], p128, None, name="f_hn_mk")
        mq = _headnorm_fwd((z, M_HEADS * M_HD, unit("m_q")), gn["m_q"], p128, None, name="f_hn_mq")
        mv = (mkv, HPG * M_HD, M_HEADS * M_HD // (HPG * M_HD))
        o_m, lse_m = _attn_fwd(mq, mk, mv, dv=M_HD, causal=False, scale=scale_m, name="f_flash_m")
        sv.update(memn=memn, mkv=mkv, mk=mk, mq=mq, o_m=o_m, lse_m=lse_m)
        outs = (o_a, o_b, o_c, o_m)
        ys = [_mm(outs[n], wl["w_branch"][n], name="f_mm_branch") for n in range(N_BRANCH)]
        t = _merge_fwd((z, N_BRANCH * d_model, 0), gn["b_gate"], ys, name="f_merge")
        x2 = _mm(t, wl["w_out"], mode="add", extra=cur, name="f_mm_out")
        h2 = _rmsnorm_fwd((x2, d_model, 0), gn["g_mlp"], name="f_norm_mlp")
        (u, act), got = _mm(h2, wl["w_up"], mode="relu2", name="f_mm_up", xchg=carry("mm_up"))
        arrived(l + 1, ag_hosts["mm_up"], got)
        x3, got = _mm(act, wl["w_down"], mode="add", extra=x2, name="f_mm_down", xchg=carry("mm_down"))
        arrived(l + 1, ag_hosts["mm_down"], got)
        sv.update(ys=ys, t=t, x2=x2, h2=h2, u=u, act=act)
        saved.append(sv)
        cur = x3

    dx, loss_row = _loss_head(cur, loss_target, name="loss_head")
    loss = lax.psum(loss_row[0, 0], AXES)

    recvs = {n: [None] * n_layers for n in BIG}
    rows = {}

    def keep(key, layer, row):
        rows.setdefault(key, [None] * n_layers)[layer] = row

    def item(name, layer, g):
        return (name, layer, rs_src(name, g))

    def srcs(items):
        return [it[2] for it in items]

    def store(items, got):
        for (name, layer, _), r in zip(items, got):
            recvs[name][layer] = r

    carried = []
    for l in reversed(range(n_layers)):
        sv = saved[l]
        wl = weights[l]
        gn = gains(l)
        z = sv["z"]
        g_wdown = _mm(sv["act"], dx, ta=True, name="b_mm_wdown")
        du = _mm(dx, wl["w_down"], tb=True, mode="drelu2", extra=sv["u"], out_dtype=BF16, name="b_mm_du")
        g_wup = _mm(sv["h2"], du, ta=True, name="b_mm_wup")
        dh2 = _mm(du, wl["w_up"], tb=True, name="b_mm_dh2")
        dx2, dg = _rmsnorm_bwd(dh2, (sv["x2"], d_model, 0), gn["g_mlp"], dres=dx, name="b_norm_mlp")
        keep("g_mlp", l, dg)
        g_wout = _mm(sv["t"], dx2, ta=True, name="b_mm_wout")
        dt = _mm(dx2, wl["w_out"], tb=True, name="b_mm_dt")
        dys, dgate, dbg = _merge_bwd(dt, (z, N_BRANCH * d_model, 0), gn["b_gate"], sv["ys"], name="b_merge")
        keep("b_gate", l, dbg)
        outs = (sv["o_a"], sv["o_b"], sv["o_c"], sv["o_m"])
        g_wbranch = jnp.stack([_mm(outs[n], dys[n], ta=True, name="b_mm_wbranch") for n in range(N_BRANCH)])
        do_a, do_b, do_c, do_m = [_mm(dys[n], wl["w_branch"][n], tb=True, name="b_mm_dbranch")
                                  for n in range(N_BRANCH)]
        mlp_items = [item("w_down", l, g_wdown), item("w_up", l, g_wup)]
        proj_items = [item("w_out", l, g_wout), item("w_branch", l, g_wbranch)]
        dparts = {"gate": dgate}
        va = (z, 128, unit("a_v"))
        kw_a = dict(dil=1, group=A_HEADS // A_KV_HEADS, maxdist=A_WINDOW - 1)
        dl_a, dsink = _delta(do_a, sv["o_a"], p64, lse=sv["lse_a"], sink_row=gn["sink_row"], name="b_delta_a")
        keep("a_sink", l, dsink)
        dqa = _band_dq(sv["qa"], sv["ka"], va, do_a, sv["lse_a"], dl_a, name="b_band_dq_a", **kw_a)
        dka, dva = _band_dkv(sv["qa"], sv["ka"], va, do_a, sv["lse_a"], dl_a, name="b_band_dkv_a", **kw_a)
        dparts["a_q"], dg = _headnorm_bwd(dqa, (z, 512, unit("a_q")), gn["a_q"], p64, rope64, name="b_hn_aq")
        keep("a_qn", l, dg)
        dparts["a_k"], dg = _headnorm_bwd(dka, (z, 128, unit("a_k")), gn["a_k"], p64, rope64, name="b_hn_ak")
        keep("a_kn", l, dg)
        dparts["a_v"] = dva
        vb = (sv["kv_up"], HPG * B_V, B_HEADS * LANES // (HPG * B_V))
        dl_b, _ = _delta(do_b, sv["o_b"], p64, name="b_delta_b")
        kw_b = dict(dv=B_V, causal=True, scale=scale_b)
        dqb, got = _attn_dq(sv["qb"], sv["kb"], vb, do_b, sv["lse_b"], dl_b, name="b_flash_dq_b",
                            xchg=srcs(mlp_items), **kw_b)
        store(mlp_items, got)
        dkv_items = carried + proj_items
        (dkb, dvb), got = _attn_dkv(sv["qb"], sv["kb"], vb, do_b, sv["lse_b"], dl_b, name="b_flash_dkv_b",
                                     xchg=srcs(dkv_items), **kw_b)
        store(dkv_items, got)
        d_qup, dg = _headnorm_bwd(dqb, (sv["q_up"], B_HEADS * LANES, 0), gn["b_q"], pq, rope_q, name="b_hn_bq")
        keep("b_q", l, dg)
        d_kn, dg = _headnorm_bwd(dkb, (sv["kv_up"], B_HEADS * LANES, 0), gn["b_k"], p64, None, name="b_hn_bk")
        keep("b_k", l, dg)
        dkp = _kp_reduce(dkb, name="b_kp_reduce")
        dparts["b_kr"], dg = _headnorm_bwd(dkp, (z, LANES, unit("b_kr")), gn["b_kr"], pkr, rope_k, name="b_hn_kr")
        keep("b_kr", l, dg)
        d_kvup = jnp.concatenate([d_kn, dvb], axis=1)
        g_ukv = _mm(sv["ckv"], d_kvup, ta=True, name="b_mm_wukv")
        dckv = _mm(d_kvup, wl["ukv"], tb=True, name="b_mm_dckv")
        dparts["b_ckv"], dg = _rmsnorm_bwd(dckv, (z, B_KV_LORA, unit("b_ckv")), gn["b_kva_norm"], out_dtype=BF16,
                                          name="b_norm_ckv")
        keep("b_kva_norm", l, dg)
        g_uq = _mm(sv["cq"], d_qup, ta=True, name="b_mm_wuq")
        dcq = _mm(d_qup, wl["uq"], tb=True, name="b_mm_dcq")
        dparts["b_cq"], dg = _rmsnorm_bwd(dcq, (z, B_Q_LORA, unit("b_cq")), gn["b_qa_norm"], out_dtype=BF16,
                                         name="b_norm_cq")
        keep("b_qa_norm", l, dg)
        g_wuq = g_uq.reshape(B_Q_LORA, B_HEADS, LANES)[:, :, :B_NOPE + B_ROPE].reshape(B_Q_LORA, -1)
        g_wukv = jnp.concatenate(
            [g_ukv[:, :B_HEADS * LANES].reshape(B_KV_LORA, B_HEADS, LANES)[:, :, :B_NOPE],
             g_ukv[:, B_HEADS * LANES:].reshape(B_KV_LORA, B_HEADS, B_V)], axis=-1).reshape(B_KV_LORA, -1)
        do_cg, dl_cg = _combine_bwd(do_c, sv["o_c"], sv["lse_cg"], [dil for _, dil in C_PATTERNS], p64,
                                    name="b_combine")
        for g, (win, dil) in enumerate(C_PATTERNS):
            vc = (z, 512, unit("c%d" % (3 * g + 2)))
            kw_c = dict(dil=dil, group=1, maxdist=win // dil)
            args = (sv["qc"][g], sv["kc"][g], vc, do_cg[g], sv["lse_cg"][g], dl_cg[g])
            dq = _band_dq(*args, name="b_band_dq_c%d" % g, **kw_c)
            dk, dv_ = _band_dkv(*args, name="b_band_dkv_c%d" % g, **kw_c)
            dparts["c%d" % (3 * g)], dg = _headnorm_bwd(dq, (z, 512, unit("c%d" % (3 * g))), gn["c_q"][g], p64, rope64,
                                                        dil=dil, name="b_hn_cq%d" % g)
            keep("c_q%d" % g, l, dg)
            dparts["c%d" % (3 * g + 1)], dg = _headnorm_bwd(dk, (z, 512, unit("c%d" % (3 * g + 1))), gn["c_k"][g], p64,
                                                            rope64, dil=dil, name="b_hn_ck%d" % g)
            keep("c_k%d" % g, l, dg)
            dparts["c%d" % (3 * g + 2)] = dv_
        mv = (sv["mkv"], HPG * M_HD, M_HEADS * M_HD // (HPG * M_HD))
        dl_m, _ = _delta(do_m, sv["o_m"], p128, name="b_delta_m")
        kw_m = dict(dv=M_HD, causal=False, scale=scale_m)
        dmq = _attn_dq(sv["mq"], sv["mk"], mv, do_m, sv["lse_m"], dl_m, name="b_flash_dq_m", **kw_m)
        dmk, dmv = _attn_dkv(sv["mq"], sv["mk"], mv, do_m, sv["lse_m"], dl_m, name="b_flash_dkv_m", **kw_m)
        dparts["m_q"], dg = _headnorm_bwd(dmq, (z, M_HEADS * M_HD, unit("m_q")), gn["m_q"], p128, None, name="b_hn_mq")
        keep("m_qn", l, dg)
        d_mk, dg = _headnorm_bwd(dmk, (sv["mkv"], M_HEADS * M_HD, 0), gn["m_k"], p128, None, name="b_hn_mk")
        keep("m_kn", l, dg)
        d_mkv = jnp.concatenate([d_mk, dmv], axis=1)
        g_wmkv = _mm(sv["memn"], d_mkv, ta=True, name="b_mm_wmkv")
        dmemn = _mm(d_mkv, wl["m_w_kv"], tb=True, name="b_mm_dmemn")
        _, dg = _rmsnorm_bwd(dmemn, (mem, d_model, 0), gn["m_g_mem"], name="b_norm_mem")
        keep("m_g_mem", l, dg)
        dz = jnp.concatenate([dparts[n] for n in order], axis=1)
        bm_items = [item("b_w_ukv", l, g_wukv), item("b_w_uq", l, g_wuq), item("m_w_kv", l, g_wmkv)]
        g_in, got = _mm(sv["h"], dz, ta=True, name="b_mm_win", xchg=srcs(bm_items))
        store(bm_items, got)
        carried = [item("w_in", l, _unpermute_cols(g_in, lay, orig_names))]
        last_items = carried if l == 0 else []
        dh, got = _mm(dz, wl["w_in"], tb=True, name="b_mm_dh", xchg=srcs(last_items))
        store(last_items, got)
        dx, dg = _rmsnorm_bwd(dh, (sv["x"], d_model, 0), gn["g_mix"], dres=dx2, name="b_norm_mix")
        keep("g_mix", l, dg)

    cat = {k: jnp.concatenate(v, axis=0) for k, v in rows.items()}
    heads = lambda a, nh, hd: a.reshape(n_layers, nh, hd).sum(axis=1)
    gsmall = dict(
        g_mix=cat["g_mix"], b_gate=cat["b_gate"], g_mlp=cat["g_mlp"], m_g_mem=cat["m_g_mem"],
        b_qa_norm=cat["b_qa_norm"], b_kva_norm=cat["b_kva_norm"],
        a_sink=cat["a_sink"][:, ::A_HD], a_qn=heads(cat["a_qn"], A_HEADS, A_HD), a_kn=heads(cat["a_kn"], A_KV_HEADS, A_HD),
        b_qn=heads(cat["b_q"], B_HEADS, LANES)[:, :B_NOPE + B_ROPE],
        b_kn=jnp.concatenate([heads(cat["b_k"], B_HEADS, LANES)[:, :B_NOPE], cat["b_kr"][:, :B_ROPE]], axis=1),
        c_qn=jnp.stack([heads(cat["c_q%d" % g], C_HEADS, C_HD) for g in range(3)], axis=1),
        c_kn=jnp.stack([heads(cat["c_k%d" % g], C_HEADS, C_HD) for g in range(3)], axis=1),
        m_qn=heads(cat["m_qn"], M_HEADS, M_HD), m_kn=heads(cat["m_kn"], M_HEADS, M_HD))

    out_g, out_d, out_m, out_v = {}, {}, {}, {}
    for name in BIG:
        shard = w[name].shape
        two = lambda a: a.reshape(-1, shard[-1])
        pieces = [r.reshape(N_DEV, -1, shard[-1]) for r in recvs[name]]
        g, dlt, mn, vn = _adam(pieces, two(w[name]), two(moments_m[name]), two(moments_v[name]), name="adam_" + name)
        out_g[name], out_d[name], out_m[name], out_v[name] = (a.reshape(shard) for a in (g, dlt, mn, vn))
    sizes = [int(np.prod(w[n].shape)) for n in SMALL]
    total = sum(sizes)
    padded = -(-total // (8 * LANES)) * (8 * LANES)
    pack = lambda d: jnp.pad(jnp.concatenate([jnp.reshape(d[n], (-1,)) for n in SMALL]), (0, padded - total)).reshape(-1, LANES)
    gpack = pack(gsmall)
    recv = _exchange_many([(gpack, False)], name="ag_small_grads")[0]
    packed = _adam([recv], pack(w), pack(moments_m), pack(moments_v), name="adam_small")
    offs = np.cumsum([0] + sizes)
    for a, out in zip(packed, (out_g, out_d, out_m, out_v)):
        flat = a.reshape(-1)
        for n, o, sz in zip(SMALL, offs[:-1], sizes):
            out[n] = flat[o:o + sz].reshape(w[n].shape)
    return loss, dx, out_g, out_d, out_m, out_v


def kernel(x, mem, positions, g_mix, w_in, b_gate, a_qn, a_kn, a_sink, b_qa_norm, b_kva_norm, b_w_uq, b_w_ukv, b_qn, b_kn, c_qn, c_kn, m_g_mem, m_w_kv, m_qn, m_kn, w_branch, w_out, g_mlp, w_up, w_down, loss_target, m_g_mix, m_w_in, m_b_gate, m_a_qn, m_a_kn, m_a_sink, m_b_qa_norm, m_b_kva_norm, m_b_w_uq, m_b_w_ukv, m_b_qn, m_b_kn, m_c_qn, m_c_kn, m_m_g_mem, m_m_w_kv, m_m_qn, m_m_kn, m_w_branch, m_w_out, m_g_mlp, m_w_up, m_w_down, v_g_mix, v_w_in, v_b_gate, v_a_qn, v_a_kn, v_a_sink, v_b_qa_norm, v_b_kva_norm, v_b_w_uq, v_b_w_ukv, v_b_qn, v_b_kn, v_c_qn, v_c_kn, v_m_g_mem, v_m_w_kv, v_m_qn, v_m_kn, v_w_branch, v_w_out, v_g_mlp, v_w_up, v_w_down):
    w = dict(g_mix=g_mix, w_in=w_in, b_gate=b_gate, a_qn=a_qn, a_kn=a_kn, a_sink=a_sink, b_qa_norm=b_qa_norm,
             b_kva_norm=b_kva_norm, b_w_uq=b_w_uq, b_w_ukv=b_w_ukv, b_qn=b_qn, b_kn=b_kn, c_qn=c_qn, c_kn=c_kn,
             m_g_mem=m_g_mem, m_w_kv=m_w_kv, m_qn=m_qn, m_kn=m_kn, w_branch=w_branch, w_out=w_out, g_mlp=g_mlp,
             w_up=w_up, w_down=w_down)
    mm_ = dict(g_mix=m_g_mix, w_in=m_w_in, b_gate=m_b_gate, a_qn=m_a_qn, a_kn=m_a_kn, a_sink=m_a_sink,
               b_qa_norm=m_b_qa_norm, b_kva_norm=m_b_kva_norm, b_w_uq=m_b_w_uq, b_w_ukv=m_b_w_ukv, b_qn=m_b_qn,
               b_kn=m_b_kn, c_qn=m_c_qn, c_kn=m_c_kn, m_g_mem=m_m_g_mem, m_w_kv=m_m_w_kv, m_qn=m_m_qn, m_kn=m_m_kn,
               w_branch=m_w_branch, w_out=m_w_out, g_mlp=m_g_mlp, w_up=m_w_up, w_down=m_w_down)
    vv = dict(g_mix=v_g_mix, w_in=v_w_in, b_gate=v_b_gate, a_qn=v_a_qn, a_kn=v_a_kn, a_sink=v_a_sink,
              b_qa_norm=v_b_qa_norm, b_kva_norm=v_b_kva_norm, b_w_uq=v_b_w_uq, b_w_ukv=v_b_w_ukv, b_qn=v_b_qn,
              b_kn=v_b_kn, c_qn=v_c_qn, c_kn=v_c_kn, m_g_mem=v_m_g_mem, m_w_kv=v_m_w_kv, m_qn=v_m_qn, m_kn=v_m_kn,
              w_branch=v_w_branch, w_out=v_w_out, g_mlp=v_g_mlp, w_up=v_w_up, w_down=v_w_down)
    loss, dx, g, dlt, mn, vn = _step(x[0], mem[0], positions[0], w, mm_, vv, loss_target[0])
    return (loss, dx[None], *[g[n] for n in WEIGHTS], *[dlt[n] for n in WEIGHTS], *[mn[n] for n in WEIGHTS],
            *[vn[n] for n in WEIGHTS])
```

```python
import functools

import numpy as np
import jax
import jax.numpy as jnp
from jax import lax
from jax.experimental import pallas as pl
from jax.experimental.pallas import tpu as pltpu

F32 = jnp.float32
BF16 = jnp.bfloat16

LANES = 128
N_DEV = 8
AXES = ("x", "y", "c")
VMEM_LIMIT = 48 * 1024 * 1024

EPS = 1e-6
NEG = -1e30
ROPE_THETA = 10000.0
BLOCK = 128

A_HEADS, A_KV_HEADS, A_HD, A_WINDOW = 8, 2, 64, 128
B_HEADS, B_Q_LORA, B_KV_LORA, B_NOPE, B_ROPE, B_V = 8, 384, 256, 64, 32, 64
C_PATTERNS = ((128, 1), (512, 4), (2048, 16))
C_HEADS, C_HD = 8, 64
M_HEADS, M_HD = 4, 128
BRANCH_W = 512
N_BRANCH = 4

ADAM_LR, ADAM_B1, ADAM_B2, ADAM_EPS, ADAM_WD, ADAM_STEP = 0.001, 0.9, 0.999, 1e-08, 0.01, 10

NT_DIMS = (((1,), (1,)), ((), ()))
TN_DIMS = (((0,), (0,)), ((), ()))
NN_DIMS = (((1,), (0,)), ((), ()))


def _pick(n, cands):
    for c in cands:
        if n % c == 0:
            return c
    return n


def _cparams(sem):
    return pltpu.CompilerParams(dimension_semantics=sem, vmem_limit_bytes=VMEM_LIMIT)


def _dot(a, b, dims=NN_DIMS):
    return lax.dot_general(a.astype(BF16), b.astype(BF16), dims, preferred_element_type=F32)


def _in_layout(d_model):
    orig = ([("a_q", 512), ("a_k", 128), ("a_v", 128), ("b_cq", B_Q_LORA), ("b_ckv", B_KV_LORA), ("b_kr", B_ROPE)]
            + [("c%d" % i, 512) for i in range(9)] + [("m_q", 512), ("gate", N_BRANCH * d_model)])
    order = ["gate", "a_q"] + ["c%d" % i for i in range(9)] + ["m_q", "b_ckv", "b_cq", "a_k", "a_v", "b_kr"]
    o_off, off = {}, 0
    for name, w in orig:
        o_off[name] = (off, w)
        off += w
    lay, noff = {}, 0
    for name in order:
        oo, w = o_off[name]
        nw = LANES if name == "b_kr" else w
        assert noff % nw == 0, (name, noff, nw)
        lay[name] = dict(orig=oo, w=w, off=noff, nw=nw)
        noff += nw
    assert noff % 512 == 0
    return lay, [n for n, _ in orig], order, off, noff


def _permute_cols(w, lay, order):
    cols = []
    for name in order:
        e = lay[name]
        p = w[..., e["orig"]:e["orig"] + e["w"]]
        if e["nw"] != e["w"]:
            p = jnp.pad(p, [(0, 0)] * (w.ndim - 1) + [(0, e["nw"] - e["w"])])
        cols.append(p)
    return jnp.concatenate(cols, axis=-1)


def _unpermute_cols(w, lay, orig_names):
    return jnp.concatenate([w[..., lay[n]["off"]:lay[n]["off"] + lay[n]["w"]] for n in orig_names], axis=-1)


def _pattern(groups):
    ids = np.repeat(np.arange(len(groups)), groups)
    assert ids.shape[0] == LANES
    gmat = (ids[:, None] == ids[None, :]).astype(np.float32)
    invn = (1.0 / np.asarray(groups, np.float32))[ids][None, :]
    return jnp.asarray(gmat, BF16), jnp.asarray(invn, F32)


def _gsum(v, gmat):
    hi = v.astype(BF16)
    r1 = v - hi.astype(F32)
    mid = r1.astype(BF16)
    lo = (r1 - mid.astype(F32)).astype(BF16)
    dot = lambda p: jnp.dot(p, gmat, preferred_element_type=F32)
    return dot(hi) + dot(mid) + dot(lo)


def _swap_half(v, lomask, half):
    return jnp.where(lomask > 0.5, pltpu.roll(v, LANES - half, 1), pltpu.roll(v, half, 1))


def _rows_spec(tr, width, off=0):
    return pl.BlockSpec((tr, width), lambda i: (i, off))


def _undilate(src_ref, scr, dil, width):
    n = scr.shape[1] // dil
    for r in range(dil):
        for c in range(width // LANES):
            lanes = slice(r * width + c * LANES, r * width + (c + 1) * LANES)
            scr[c, pl.ds(r, n, stride=dil), :] = src_ref[:, lanes].astype(F32)


def _dilate(scr, dst_ref, dil, width):
    n = scr.shape[1] // dil
    for r in range(dil):
        for c in range(width // LANES):
            lanes = slice(r * width + c * LANES, r * width + (c + 1) * LANES)
            dst_ref[:, lanes] = scr[c, pl.ds(r, n, stride=dil), :].astype(dst_ref.dtype)


def _dil_spec(tr, width, dil):
    return pl.BlockSpec((tr // dil, dil * width), lambda i: (i, 0))


def _layer_arg(per_layer):
    arr, l = per_layer
    nd = arr.ndim - 1
    return pl.BlockSpec((None,) + tuple(arr.shape[1:]), lambda *_: (l,) + (0,) * nd), arr


def _full_spec(shape):
    return pl.BlockSpec(shape, lambda i: (0,) * len(shape))


def _mm(a, b, *, ta=False, tb=False, out_dtype=F32, mode=None, extra=None, name, xchg=None):
    if ta:
        kd, m = a.shape
    else:
        m, kd = a.shape
    if tb:
        n, kb = b.shape
    else:
        kb, n = b.shape
    assert kd == kb, (a.shape, b.shape, ta, tb)
    tm = _pick(m, (1024, 512, 256, 128))
    tn = _pick(n, (1024, 1536, 512, 384, 256, 128))
    tk = _pick(kd, (1024, 1536, 512, 384, 256, 128))
    nk = kd // tk
    a_spec = (pl.BlockSpec((tk, tm), lambda i, j, k: (k, i)) if ta else pl.BlockSpec((tm, tk), lambda i, j, k: (i, k)))
    b_spec = (pl.BlockSpec((tn, tk), lambda i, j, k: (j, k)) if tb else pl.BlockSpec((tk, tn), lambda i, j, k: (k, j)))
    o_spec = pl.BlockSpec((tm, tn), lambda i, j, k: (i, j))
    dims = (((0 if ta else 1,), (1 if tb else 0,)), ((), ()))
    has_extra = mode in ("add", "drelu2")

    def body(*refs):
        a_ref, b_ref = refs[0], refs[1]
        e_ref = refs[2] if has_extra else None
        outs = refs[2 + has_extra:2 + has_extra + (2 if mode == "relu2" else 1)]
        k = pl.program_id(2)

        def finish(r):
            if mode == "add":
                outs[0][...] = (e_ref[...] + r).astype(out_dtype)
            elif mode == "relu2":
                outs[0][...] = r
                outs[1][...] = jnp.square(jnp.maximum(r, 0.0)).astype(BF16)
            elif mode == "drelu2":
                outs[0][...] = (r * (2.0 * jnp.maximum(e_ref[...], 0.0))).astype(out_dtype)
            else:
                outs[0][...] = r.astype(out_dtype)

        if nk == 1:
            finish(_dot(a_ref[...], b_ref[...], dims))
        else:
            acc = refs[-1]

            @pl.when(k == 0)
            def _():
                acc[...] = _dot(a_ref[...], b_ref[...], dims)

            @pl.when(k > 0)
            def _():
                acc[...] += _dot(a_ref[...], b_ref[...], dims)

            @pl.when(k == nk - 1)
            def _():
                finish(acc[...])

    in_specs = [a_spec, b_spec] + ([o_spec] if has_extra else [])
    args = [a, b] + ([extra] if has_extra else [])
    if mode == "relu2":
        out_shape = (jax.ShapeDtypeStruct((m, n), F32), jax.ShapeDtypeStruct((m, n), BF16))
        out_specs = (o_spec, o_spec)
    else:
        out_shape = jax.ShapeDtypeStruct((m, n), out_dtype)
        out_specs = o_spec
    return _pcall(
        body, grid=(m // tm, n // tn, nk), in_specs=in_specs, out_specs=out_specs, out_shape=out_shape,
        scratch_shapes=[pltpu.VMEM((tm, tn), F32)] if nk > 1 else [], sem=("parallel", "parallel", "arbitrary"),
        name=name, args=args, xchg=xchg)


def _rmsnorm_fwd(xv, gain, *, name, tr=512):
    x, width, off = xv
    rows = x.shape[0]
    tr = min(tr, rows)

    def body(x_ref, g_ref, o_ref):
        xf = x_ref[...]
        rstd = lax.rsqrt(jnp.mean(xf * xf, axis=-1, keepdims=True) + EPS)
        o_ref[...] = (xf * rstd * g_ref[...]).astype(BF16)

    g_spec, g_arr = _layer_arg(gain)
    return pl.pallas_call(
        body, grid=(rows // tr,), in_specs=[_rows_spec(tr, width, off), g_spec],
        out_specs=_rows_spec(tr, width), out_shape=jax.ShapeDtypeStruct((rows, width), BF16),
        compiler_params=_cparams(("parallel",)), name=name)(x, g_arr)


def _slab_out(slab, rows, in_specs, args, out_index):
    prev, total = slab
    aliases = {}
    if prev is not None:
        in_specs.append(pl.BlockSpec(memory_space=pl.ANY))
        args.append(prev)
        aliases = {len(args) - 1: out_index}
    return jax.ShapeDtypeStruct((rows, total), BF16), aliases, int(prev is not None)


def _rmsnorm_bwd(dh, xv, gain, *, dres=None, out_dtype=F32, slab=None, name, tr=512):
    x, width, off = xv
    rows = x.shape[0]
    tr = min(tr, rows)
    has_res = dres is not None
    has_prev = int(slab is not None and slab[0] is not None)

    def body(*refs):
        dh_ref, x_ref, g_ref = refs[:3]
        r_ref = refs[3] if has_res else None
        dx_ref, dg_ref = refs[3 + has_res + has_prev:]
        i = pl.program_id(0)
        xf = x_ref[...]
        rstd = lax.rsqrt(jnp.mean(xf * xf, axis=-1, keepdims=True) + EPS)
        xh = xf * rstd
        d = dh_ref[...]
        dxh = d * g_ref[...]
        dx = rstd * (dxh - xh * jnp.mean(dxh * xh, axis=-1, keepdims=True))
        if has_res:
            dx = dx + r_ref[...]
        dx_ref[...] = dx.astype(out_dtype)

        @pl.when(i == 0)
        def _():
            dg_ref[...] = jnp.zeros_like(dg_ref)

        dg_ref[...] += jnp.sum(d * xh, axis=0, keepdims=True)

    g_spec, g_arr = _layer_arg(gain)
    in_specs = [_rows_spec(tr, width), _rows_spec(tr, width, off), g_spec]
    args = [dh, x, g_arr]
    if has_res:
        in_specs.append(_rows_spec(tr, width))
        args.append(dres)
    dx_spec, dx_shape, aliases = _rows_spec(tr, width), jax.ShapeDtypeStruct((rows, width), out_dtype), {}
    if slab is not None:
        assert out_dtype == BF16
        dx_spec = _rows_spec(tr, width, off)
        dx_shape, aliases, _ = _slab_out(slab, rows, in_specs, args, 0)
    return pl.pallas_call(
        body, grid=(rows // tr,), in_specs=in_specs,
        out_specs=(dx_spec, _full_spec((1, width))),
        out_shape=(dx_shape, jax.ShapeDtypeStruct((1, width), F32)), input_output_aliases=aliases,
        compiler_params=_cparams(("arbitrary",)), name=name)(*args)


def _headnorm_fwd(xv, gain_row, pat, rope, *, addend=None, dil=1, name, tr=512):
    x, width, off = xv
    rows = x.shape[0]
    tr = min(tr, rows)
    gmat, invn = pat
    nblk = width // LANES
    has_rope = rope is not None
    has_add = addend is not None
    half = rope[3] if has_rope else 0

    def body(*refs):
        x_ref, g_ref, gm_ref, in_ref = refs[:4]
        pos = 4
        if has_rope:
            c_ref, s_ref, lo_ref = refs[pos:pos + 3]
            pos += 3
        if has_add:
            a_ref = refs[pos]
            pos += 1
        o_ref = refs[pos]
        scr = refs[pos + 1] if dil > 1 else None
        gm = gm_ref[...]
        inv = in_ref[...]
        if has_add:
            add = pltpu.roll(a_ref[...].astype(F32), 64, 1)
        for j in range(nblk):
            sl = slice(j * LANES, (j + 1) * LANES)
            xb = x_ref[:, sl]
            rstd = lax.rsqrt(_gsum(xb * xb, gm) * inv + EPS)
            y = xb * rstd * g_ref[:, sl]
            if has_rope:
                y = y * c_ref[...] + _swap_half(y, lo_ref[...], half) * s_ref[...]
            if has_add:
                y = y + add
            if dil > 1:
                scr[j] = y
            else:
                o_ref[:, sl] = y.astype(BF16)
        if dil > 1:
            _dilate(scr, o_ref, dil, width)

    g_spec, g_arr = _layer_arg(gain_row)
    in_specs = [_rows_spec(tr, width, off), g_spec, _full_spec((LANES, LANES)), _full_spec((1, LANES))]
    args = [x, g_arr, gmat, invn]
    if has_rope:
        in_specs += [_rows_spec(tr, LANES), _rows_spec(tr, LANES), _full_spec((1, LANES))]
        args += [rope[0], rope[1], rope[2]]
    if has_add:
        in_specs.append(_rows_spec(tr, LANES))
        args.append(addend)
    return pl.pallas_call(
        body, grid=(rows // tr,), in_specs=in_specs,
        out_specs=_dil_spec(tr, width, dil) if dil > 1 else _rows_spec(tr, width),
        out_shape=jax.ShapeDtypeStruct((rows // dil, dil * width), BF16),
        scratch_shapes=[pltpu.VMEM((nblk, tr, LANES), F32)] if dil > 1 else [],
        compiler_params=_cparams(("parallel",)), name=name)(*args)


def _headnorm_bwd(dy, xv, gain_row, pat, rope, *, dil=1, slab=None, name, tr=512):
    x, width, off = xv
    rows = x.shape[0]
    tr = min(tr, rows)
    gmat, invn = pat
    nblk = width // LANES
    has_rope = rope is not None
    has_prev = int(slab is not None and slab[0] is not None)
    half = rope[3] if has_rope else 0

    def body(*refs):
        dy_ref, x_ref, g_ref, gm_ref, in_ref = refs[:5]
        pos = 5
        if has_rope:
            c_ref, s_ref, lo_ref = refs[pos:pos + 3]
            pos += 3
        pos += has_prev
        dx_ref, dg_ref = refs[pos:pos + 2]
        i = pl.program_id(0)
        gm = gm_ref[...]
        inv = in_ref[...]
        if dil > 1:
            scr = refs[pos + 2]
            _undilate(dy_ref, scr, dil, width)

        @pl.when(i == 0)
        def _():
            dg_ref[...] = jnp.zeros_like(dg_ref)

        for j in range(nblk):
            sl = slice(j * LANES, (j + 1) * LANES)
            d = scr[j] if dil > 1 else dy_ref[:, sl]
            if has_rope:
                d = d * c_ref[...] + _swap_half(d * s_ref[...], lo_ref[...], half)
            xb = x_ref[:, sl]
            rstd = lax.rsqrt(_gsum(xb * xb, gm) * inv + EPS)
            xh = xb * rstd
            dg_ref[:, sl] += jnp.sum(d * xh, axis=0, keepdims=True)
            dxh = d * g_ref[:, sl]
            dx = rstd * (dxh - xh * (_gsum(dxh * xh, gm) * inv))
            dx_ref[:, sl] = dx.astype(BF16)

    g_spec, g_arr = _layer_arg(gain_row)
    in_specs = [_dil_spec(tr, width, dil) if dil > 1 else _rows_spec(tr, width), _rows_spec(tr, width, off), g_spec,
                _full_spec((LANES, LANES)), _full_spec((1, LANES))]
    args = [dy, x, g_arr, gmat, invn]
    if has_rope:
        in_specs += [_rows_spec(tr, LANES), _rows_spec(tr, LANES), _full_spec((1, LANES))]
        args += [rope[0], rope[1], rope[2]]
    dx_spec, dx_shape, aliases = _rows_spec(tr, width), jax.ShapeDtypeStruct((rows, width), BF16), {}
    if slab is not None:
        dx_spec = _rows_spec(tr, width, off)
        dx_shape, aliases, _ = _slab_out(slab, rows, in_specs, args, 0)
    return pl.pallas_call(
        body, grid=(rows // tr,), in_specs=in_specs,
        out_specs=(dx_spec, _full_spec((1, width))),
        out_shape=(dx_shape, jax.ShapeDtypeStruct((1, width), F32)), input_output_aliases=aliases,
        scratch_shapes=[pltpu.VMEM((nblk, tr, LANES), F32)] if dil > 1 else [],
        compiler_params=_cparams(("arbitrary",)), name=name)(*args)


def _merge_fwd(zv, b_gate, ys, *, name, tr=256):
    z, gw, off = zv
    rows, d = ys[0].shape
    tr = min(tr, rows)

    def body(z_ref, b_ref, y0, y1, y2, y3, t_ref):
        acc = None
        for n, y_ref in enumerate((y0, y1, y2, y3)):
            sl = slice(n * d, (n + 1) * d)
            term = jax.nn.sigmoid(z_ref[:, sl] + b_ref[:, sl]) * y_ref[...]
            acc = term if acc is None else acc + term
        t_ref[...] = acc.astype(BF16)

    b_spec, b_arr = _layer_arg(b_gate)
    return pl.pallas_call(
        body, grid=(rows // tr,),
        in_specs=[_rows_spec(tr, gw, off), b_spec] + [_rows_spec(tr, d)] * 4,
        out_specs=_rows_spec(tr, d), out_shape=jax.ShapeDtypeStruct((rows, d), BF16),
        compiler_params=_cparams(("parallel",)), name=name)(z, b_arr, *ys)


def _merge_bwd(dt, zv, b_gate, ys, *, slab_width, name, tr=256):
    z, gw, off = zv
    rows, d = dt.shape
    tr = min(tr, rows)

    def body(dt_ref, z_ref, b_ref, y0, y1, y2, y3, d0, d1, d2, d3, dg_ref, db_ref):
        i = pl.program_id(0)

        @pl.when(i == 0)
        def _():
            db_ref[...] = jnp.zeros_like(db_ref)

        dtv = dt_ref[...]
        for n, (y_ref, d_ref) in enumerate(((y0, d0), (y1, d1), (y2, d2), (y3, d3))):
            sl = slice(n * d, (n + 1) * d)
            gate = jax.nn.sigmoid(z_ref[:, sl] + b_ref[:, sl])
            d_ref[...] = (dtv * gate).astype(BF16)
            dpre = dtv * y_ref[...] * (gate * (1.0 - gate))
            dg_ref[:, sl] = dpre.astype(BF16)
            db_ref[:, sl] += jnp.sum(dpre, axis=0, keepdims=True)

    b_spec, b_arr = _layer_arg(b_gate)
    outs = pl.pallas_call(
        body, grid=(rows // tr,),
        in_specs=[_rows_spec(tr, d), _rows_spec(tr, gw, off), b_spec] + [_rows_spec(tr, d)] * 4,
        out_specs=tuple([_rows_spec(tr, d)] * 4 + [_rows_spec(tr, gw, off), _full_spec((1, gw))]),
        out_shape=tuple([jax.ShapeDtypeStruct((rows, d), BF16)] * 4
                        + [jax.ShapeDtypeStruct((rows, slab_width), BF16), jax.ShapeDtypeStruct((1, gw), F32)]),
        compiler_params=_cparams(("arbitrary",)), name=name)(dt, z, b_arr, *ys)
    return outs[:4], outs[4], outs[5]


def _group_weights(la, lb, lc):
    mx = jnp.maximum(jnp.maximum(la, lb), lc)
    ea, eb, ec = jnp.exp(la - mx), jnp.exp(lb - mx), jnp.exp(lc - mx)
    inv = 1.0 / (ea + eb + ec)
    return ea * inv, eb * inv, ec * inv


def _combine_fwd(os_, lses, dils, *, rows, width, name, tr=512):
    nblk = width // LANES
    ndil = sum(d > 1 for d in dils)

    def body(*refs):
        o_refs, l_refs, out_ref = refs[0:3], refs[3:6], refs[6]
        scr = list(refs[7:])
        o_tok, l_tok = [], []
        for g, d in enumerate(dils):
            if d > 1:
                so, sl_ = scr.pop(0), scr.pop(0)
                _undilate(o_refs[g], so, d, width)
                _undilate(l_refs[g], sl_, d, width)
                o_tok.append(lambda j, s=so: s[j])
                l_tok.append(lambda j, s=sl_: s[j])
            else:
                o_tok.append(lambda j, r=o_refs[g]: r[:, j * LANES:(j + 1) * LANES])
                l_tok.append(lambda j, r=l_refs[g]: r[:, j * LANES:(j + 1) * LANES])
        for j in range(nblk):
            wa, wb, wc = _group_weights(l_tok[0](j), l_tok[1](j), l_tok[2](j))
            out_ref[:, j * LANES:(j + 1) * LANES] = wa * o_tok[0](j) + wb * o_tok[1](j) + wc * o_tok[2](j)

    spec = lambda d: _dil_spec(tr, width, d) if d > 1 else _rows_spec(tr, width)
    return pl.pallas_call(
        body, grid=(rows // tr,), in_specs=[spec(d) for d in dils] * 2, out_specs=_rows_spec(tr, width),
        out_shape=jax.ShapeDtypeStruct((rows, width), F32),
        scratch_shapes=[pltpu.VMEM((nblk, tr, LANES), F32)] * (2 * ndil),
        compiler_params=_cparams(("parallel",)), name=name)(*os_, *lses)


def _combine_bwd(do, o, lses, dils, pat, *, name, tr=512):
    rows, width = do.shape
    gmat, _ = pat
    nblk = width // LANES
    ndil = sum(d > 1 for d in dils)

    def body(*refs):
        do_ref, o_ref = refs[0:2]
        l_refs, gm_ref = refs[2:5], refs[5]
        d_refs, e_refs = refs[6:9], refs[9:12]
        scr = list(refs[12:])
        gm = gm_ref[...]
        l_tok, d_scr, e_scr = [], [], []
        for g, d in enumerate(dils):
            if d > 1:
                sl_, sd, se = scr.pop(0), scr.pop(0), scr.pop(0)
                _undilate(l_refs[g], sl_, d, width)
                l_tok.append(lambda j, s=sl_: s[j])
                d_scr.append(sd), e_scr.append(se)
            else:
                l_tok.append(lambda j, r=l_refs[g]: r[:, j * LANES:(j + 1) * LANES])
                d_scr.append(None), e_scr.append(None)
        for j in range(nblk):
            sl = slice(j * LANES, (j + 1) * LANES)
            ws = _group_weights(l_tok[0](j), l_tok[1](j), l_tok[2](j))
            dov = do_ref[:, sl]
            tot = _gsum(dov * o_ref[:, sl], gm)
            for g, wgt in enumerate(ws):
                if dils[g] > 1:
                    d_scr[g][j] = wgt * dov
                    e_scr[g][j] = wgt * tot
                else:
                    d_refs[g][:, sl] = wgt * dov
                    e_refs[g][:, sl] = wgt * tot
        for g, d in enumerate(dils):
            if d > 1:
                _dilate(d_scr[g], d_refs[g], d, width)
                _dilate(e_scr[g], e_refs[g], d, width)

    spec = lambda d: _dil_spec(tr, width, d) if d > 1 else _rows_spec(tr, width)
    sds = lambda d: jax.ShapeDtypeStruct((rows // d, d * width), F32)
    outs = pl.pallas_call(
        body, grid=(rows // tr,),
        in_specs=[_rows_spec(tr, width)] * 2 + [spec(d) for d in dils] + [_full_spec((LANES, LANES))],
        out_specs=tuple([spec(d) for d in dils] * 2),
        out_shape=tuple([sds(d) for d in dils] * 2),
        scratch_shapes=[pltpu.VMEM((nblk, tr, LANES), F32)] * (3 * ndil),
        compiler_params=_cparams(("parallel",)), name=name)(do, o, *lses, gmat)
    return outs[:3], outs[3:]


def _delta(do, o, pat, *, lse=None, sink_row=None, name, tr=512):
    rows, width = do.shape
    gmat, _ = pat
    nblk = width // LANES
    has_sink = sink_row is not None

    def body(*refs):
        do_ref, o_ref, gm_ref = refs[:3]
        if has_sink:
            l_ref, s_ref, dl_ref, ds_ref = refs[3:]
        else:
            dl_ref = refs[3]
        gm = gm_ref[...]
        if has_sink:
            @pl.when(pl.program_id(0) == 0)
            def _():
                ds_ref[...] = jnp.zeros_like(ds_ref)
        for j in range(nblk):
            sl = slice(j * LANES, (j + 1) * LANES)
            dl = _gsum(do_ref[:, sl] * o_ref[:, sl], gm)
            dl_ref[:, sl] = dl
            if has_sink:
                ds_ref[:, sl] += jnp.sum(-jnp.exp(s_ref[:, sl] - l_ref[:, sl]) * dl, axis=0, keepdims=True)

    in_specs = [_rows_spec(tr, width), _rows_spec(tr, width), _full_spec((LANES, LANES))]
    args = [do, o, gmat]
    out_specs = [_rows_spec(tr, width)]
    out_shape = [jax.ShapeDtypeStruct((rows, width), F32)]
    if has_sink:
        s_spec, s_arr = _layer_arg(sink_row)
        in_specs += [_rows_spec(tr, width), s_spec]
        args += [lse, s_arr]
        out_specs.append(_full_spec((1, width)))
        out_shape.append(jax.ShapeDtypeStruct((1, width), F32))
    outs = pl.pallas_call(
        body, grid=(rows // tr,), in_specs=in_specs, out_specs=tuple(out_specs), out_shape=tuple(out_shape),
        compiler_params=_cparams(("arbitrary",)), name=name)(*args)
    return outs if has_sink else (outs[0], None)


def _kp_reduce(dk, *, name, tr=512):
    rows, width = dk.shape
    nblk = width // LANES

    def body(dk_ref, o_ref):
        acc = dk_ref[:, 0:LANES]
        for j in range(1, nblk):
            acc = acc + dk_ref[:, j * LANES:(j + 1) * LANES]
        lane = lax.broadcasted_iota(jnp.int32, acc.shape, 1)
        o_ref[...] = jnp.where(lane < B_ROPE, pltpu.roll(acc, 64, 1), 0.0)

    return pl.pallas_call(
        body, grid=(rows // tr,), in_specs=[_rows_spec(tr, width)], out_specs=_rows_spec(tr, LANES),
        out_shape=jax.ShapeDtypeStruct((rows, LANES), F32),
        compiler_params=_cparams(("parallel",)), name=name)(dk)


def _loss_head(y, target, *, name, tr=512):
    rows, d = y.shape

    def body(y_ref, t_ref, dy_ref, l_ref):
        @pl.when(pl.program_id(0) == 0)
        def _():
            l_ref[...] = jnp.zeros_like(l_ref)

        diff = y_ref[...] - t_ref[...]
        dy_ref[...] = diff * (1.0 / d)
        part = 0.5 * jnp.sum(jnp.mean(diff * diff, axis=-1, keepdims=True), axis=0, keepdims=True)
        l_ref[...] += jnp.broadcast_to(part, l_ref.shape)

    return pl.pallas_call(
        body, grid=(rows // tr,), in_specs=[_rows_spec(tr, d)] * 2,
        out_specs=(_rows_spec(tr, d), _full_spec((1, LANES))),
        out_shape=(jax.ShapeDtypeStruct((rows, d), F32), jax.ShapeDtypeStruct((1, LANES), F32)),
        compiler_params=_cparams(("arbitrary",)), name=name)(y, target)


def _adam(recvs, w, m, v, *, name):
    nl = len(recvs)
    rows_l, cols = recvs[0].shape[1:]
    assert w.shape == (nl * rows_l, cols)
    tr = _pick(rows_l, (128, 64, 32, 16, 8))
    nt = rows_l // tr
    c1 = 1.0 / (1.0 - ADAM_B1 ** ADAM_STEP)
    c2 = 1.0 / (1.0 - ADAM_B2 ** ADAM_STEP)

    def body(*refs):
        r_refs = refs[:nl]
        w_ref, m_ref, v_ref, g_out, d_out, m_out, v_out = refs[nl:]
        layer = pl.program_id(0)

        def update(r_ref):
            g = r_ref[0].astype(F32)
            for j in range(1, N_DEV):
                g = g + r_ref[j].astype(F32)
            mn = ADAM_B1 * m_ref[...] + (1.0 - ADAM_B1) * g
            vn = ADAM_B2 * v_ref[...] + (1.0 - ADAM_B2) * (g * g)
            g_out[...] = g
            m_out[...] = mn
            v_out[...] = vn
            d_out[...] = -ADAM_LR * ((mn * c1) / (jnp.sqrt(vn * c2) + ADAM_EPS) + ADAM_WD * w_ref[...])

        for k in range(nl):
            pl.when(layer == k)(functools.partial(update, r_refs[k]))

    spec = pl.BlockSpec((tr, cols), lambda l, i: (l * nt + i, 0))
    rspec = lambda k: pl.BlockSpec((N_DEV, tr, cols), lambda l, i: (0, jnp.where(l == k, i, 0), 0))
    sds = jax.ShapeDtypeStruct(w.shape, F32)
    return pl.pallas_call(
        body, grid=(nl, nt), in_specs=[rspec(k) for k in range(nl)] + [spec, spec, spec],
        out_specs=(spec,) * 4, out_shape=(sds,) * 4,
        compiler_params=_cparams(("arbitrary", "arbitrary")), name=name)(*recvs, w, m, v)


def _band_mask(i, maxdist, nrow_blocks):
    row = lax.broadcasted_iota(jnp.int32, (BLOCK, 2 * BLOCK), 0)
    col = lax.broadcasted_iota(jnp.int32, (BLOCK, 2 * BLOCK), 1)
    dist = row - col + BLOCK
    return (dist >= 0) & (dist <= maxdist) & ((i > 0) | (col >= BLOCK))


def _band_views(q, k, zv, dil):
    z, kvw, voff = zv
    if dil == 1:
        return q, k, z, z.shape[1] // kvw, voff
    vd = z[:, voff * kvw:(voff + 1) * kvw].reshape(z.shape[0] // dil, dil * kvw)
    return q, k, vd, 1, 0


def _band_fwd(q, k, zv, *, dil, group, maxdist, sinks=None, name, xchg=()):
    s, qw = q.shape[0] * dil, q.shape[1] // dil
    kvw = zv[1]
    qd, kd, zd, zblocks, voff = _band_views(q, k, zv, dil)
    nb = s // dil // BLOCK
    nheads = qw // 64
    scale = 64 ** -0.5
    has_sink = sinks is not None

    def body(*refs):
        q_ref, kp_ref, kc_ref, vp_ref, vc_ref = refs[:5]
        s_ref = refs[5] if has_sink else None
        o_ref, l_ref, sc_s, p_s = refs[5 + has_sink:]
        i = pl.program_id(1)
        mask = _band_mask(i, maxdist, nb)
        k2 = jnp.concatenate([kp_ref[...], kc_ref[...]], axis=0)
        v2 = jnp.concatenate([vp_ref[...], vc_ref[...]], axis=0).astype(BF16)
        for h in range(nheads):
            g = h // group
            sc_s[h] = _dot(q_ref[:, 64 * h:64 * h + 64], k2[:, 64 * g:64 * g + 64], NT_DIMS)
        sc = jnp.where(mask[None], sc_s[...] * scale, NEG)
        mx = jnp.max(sc, axis=2, keepdims=True)
        if has_sink:
            snk = s_ref[:, :, 0:1]
            mx = jnp.maximum(mx, snk)
        e = jnp.exp(sc - mx)
        den = jnp.sum(e, axis=2, keepdims=True)
        if has_sink:
            den = den + jnp.exp(snk - mx)
        p_s[...] = (e * (1.0 / den)).astype(BF16)
        lse = mx + jnp.log(den)
        for h in range(nheads):
            g = h // group
            o_ref[:, 64 * h:64 * h + 64] = _dot(p_s[h], v2[:, 64 * g:64 * g + 64])
            l_ref[:, 64 * h:64 * h + 64] = jnp.broadcast_to(lse[h], (BLOCK, 64))

    prev = lambda r, i: jnp.maximum(i - 1, 0)
    in_specs = [
        pl.BlockSpec((BLOCK, qw), lambda r, i: (i, r)),
        pl.BlockSpec((BLOCK, kvw), lambda r, i: (prev(r, i), r)),
        pl.BlockSpec((BLOCK, kvw), lambda r, i: (i, r)),
        pl.BlockSpec((BLOCK, kvw), lambda r, i: (prev(r, i), r * zblocks + voff)),
        pl.BlockSpec((BLOCK, kvw), lambda r, i: (i, r * zblocks + voff)),
    ]
    args = [qd, kd, kd, zd, zd]
    if has_sink:
        s_spec, s_arr = _layer_arg(sinks)
        in_specs.append(s_spec)
        args.append(s_arr)
    ospec = pl.BlockSpec((BLOCK, qw), lambda r, i: (i, r))
    sds = jax.ShapeDtypeStruct((s // dil, dil * qw), F32)
    (o, lse), got = _pcall(
        body, grid=(dil, nb), in_specs=in_specs, out_specs=(ospec, ospec), out_shape=(sds, sds),
        scratch_shapes=[pltpu.VMEM((nheads, BLOCK, 2 * BLOCK), F32), pltpu.VMEM((nheads, BLOCK, 2 * BLOCK), BF16)],
        sem=("parallel", "parallel"), name=name, args=args, xchg=xchg)
    return (o, lse), got


def _band_dq(q, k, zv, do, lse, delta, *, dil, group, maxdist, name):
    s, qw = q.shape[0] * dil, q.shape[1] // dil
    kvw = zv[1]
    qd, kd, zd, zblocks, voff = _band_views(q, k, zv, dil)
    nb = s // dil // BLOCK
    nheads = qw // 64
    scale = 64 ** -0.5

    def body(q_ref, kp_ref, kc_ref, vp_ref, vc_ref, do_ref, l_ref, d_ref, dq_ref, sc_s, dp_s, ds_s, st_s):
        i = pl.program_id(1)
        mask = _band_mask(i, maxdist, nb)
        k2 = jnp.concatenate([kp_ref[...], kc_ref[...]], axis=0)
        v2 = jnp.concatenate([vp_ref[...], vc_ref[...]], axis=0).astype(BF16)
        dob = do_ref[...].astype(BF16)
        for h in range(nheads):
            g = h // group
            hs = slice(64 * h, 64 * h + 64)
            sc_s[h] = _dot(q_ref[:, hs], k2[:, 64 * g:64 * g + 64], NT_DIMS)
            dp_s[h] = _dot(dob[:, hs], v2[:, 64 * g:64 * g + 64], NT_DIMS)
            st_s[0, h] = l_ref[:, 64 * h:64 * h + 1]
            st_s[1, h] = d_ref[:, 64 * h:64 * h + 1]
        p = jnp.exp(jnp.where(mask[None], sc_s[...] * scale, NEG) - st_s[0])
        ds_s[...] = (p * (dp_s[...] - st_s[1]) * scale).astype(BF16)
        for h in range(nheads):
            g = h // group
            dq_ref[:, 64 * h:64 * h + 64] = _dot(ds_s[h], k2[:, 64 * g:64 * g + 64])

    prev = lambda r, i: jnp.maximum(i - 1, 0)
    qspec = pl.BlockSpec((BLOCK, qw), lambda r, i: (i, r))
    in_specs = [
        qspec,
        pl.BlockSpec((BLOCK, kvw), lambda r, i: (prev(r, i), r)),
        pl.BlockSpec((BLOCK, kvw), lambda r, i: (i, r)),
        pl.BlockSpec((BLOCK, kvw), lambda r, i: (prev(r, i), r * zblocks + voff)),
        pl.BlockSpec((BLOCK, kvw), lambda r, i: (i, r * zblocks + voff)),
        qspec, qspec, qspec,
    ]
    dq = pl.pallas_call(
        body, grid=(dil, nb), in_specs=in_specs, out_specs=qspec,
        out_shape=jax.ShapeDtypeStruct((s // dil, dil * qw), F32),
        scratch_shapes=[pltpu.VMEM((nheads, BLOCK, 2 * BLOCK), F32), pltpu.VMEM((nheads, BLOCK, 2 * BLOCK), F32),
                        pltpu.VMEM((nheads, BLOCK, 2 * BLOCK), BF16), pltpu.VMEM((2, nheads, BLOCK, 1), F32)],
        compiler_params=_cparams(("parallel", "parallel")), name=name)(qd, kd, kd, zd, zd, do, lse, delta)
    return dq


def _band_dkv(q, k, zv, do, lse, delta, *, dil, group, maxdist, slab=None, name):
    s, qw = q.shape[0] * dil, q.shape[1] // dil
    kvw = zv[1]
    qd, kd, zd, zblocks, voff = _band_views(q, k, zv, dil)
    nb = s // dil // BLOCK
    nheads = qw // 64
    nkv = kvw // 64
    scale = 64 ** -0.5

    def body(k_ref, v_ref, q0, q1, do0, do1, l0, l1, d0, d1, dk_ref, dv_ref, sc_s, dp_s, p_s, ds_s, st_s):
        j = pl.program_id(1)
        row = lax.broadcasted_iota(jnp.int32, (2 * BLOCK, BLOCK), 0)
        col = lax.broadcasted_iota(jnp.int32, (2 * BLOCK, BLOCK), 1)
        dist = row - col
        mask = (dist >= 0) & (dist <= maxdist) & ((j < nb - 1) | (row < BLOCK))
        q2 = jnp.concatenate([q0[...], q1[...]], axis=0)
        do2 = jnp.concatenate([do0[...], do1[...]], axis=0).astype(BF16)
        vb = v_ref[...].astype(BF16)
        for h in range(nheads):
            g = h // group
            hs = slice(64 * h, 64 * h + 64)
            sc_s[h] = _dot(q2[:, hs], k_ref[:, 64 * g:64 * g + 64], NT_DIMS)
            dp_s[h] = _dot(do2[:, hs], vb[:, 64 * g:64 * g + 64], NT_DIMS)
            st_s[0, h, 0:BLOCK] = l0[:, 64 * h:64 * h + 1]
            st_s[0, h, BLOCK:2 * BLOCK] = l1[:, 64 * h:64 * h + 1]
            st_s[1, h, 0:BLOCK] = d0[:, 64 * h:64 * h + 1]
            st_s[1, h, BLOCK:2 * BLOCK] = d1[:, 64 * h:64 * h + 1]
        p = jnp.exp(jnp.where(mask[None], sc_s[...] * scale, NEG) - st_s[0])
        p_s[...] = p.astype(BF16)
        ds_s[...] = (p * (dp_s[...] - st_s[1]) * scale).astype(BF16)
        dks = [None] * nkv
        dvs = [None] * nkv
        for h in range(nheads):
            g = h // group
            hs = slice(64 * h, 64 * h + 64)
            dvh = _dot(p_s[h], do2[:, hs], TN_DIMS)
            dkh = _dot(ds_s[h], q2[:, hs], TN_DIMS)
            dks[g] = dkh if dks[g] is None else dks[g] + dkh
            dvs[g] = dvh if dvs[g] is None else dvs[g] + dvh
        for g in range(nkv):
            dk_ref[:, 64 * g:64 * g + 64] = dks[g]
            dv_ref[:, 64 * g:64 * g + 64] = dvs[g].astype(BF16)

    nxt = lambda r, j: jnp.minimum(j + 1, nb - 1)
    q0s = pl.BlockSpec((BLOCK, qw), lambda r, j: (j, r))
    q1s = pl.BlockSpec((BLOCK, qw), lambda r, j: (nxt(r, j), r))
    kspec = pl.BlockSpec((BLOCK, kvw), lambda r, j: (j, r))
    in_specs = [kspec, pl.BlockSpec((BLOCK, kvw), lambda r, j: (j, r * zblocks + voff)),
                q0s, q1s, q0s, q1s, q0s, q1s, q0s, q1s]
    args = [kd, zd, qd, qd, do, do, lse, lse, delta, delta]
    n_in = len(args)
    dv_spec, dv_shape, aliases, has_prev = kspec, jax.ShapeDtypeStruct((s // dil, dil * kvw), BF16), {}, 0
    if slab is not None:
        assert dil == 1
        dv_spec = in_specs[1]
        dv_shape, aliases, has_prev = _slab_out(slab, s, in_specs, args, 1)

    def body_skip_slab(*refs):
        return body(*refs[:n_in], *refs[n_in + has_prev:])

    dk, dv = pl.pallas_call(
        body_skip_slab, grid=(dil, nb), in_specs=in_specs, out_specs=(kspec, dv_spec),
        out_shape=(jax.ShapeDtypeStruct((s // dil, dil * kvw), F32), dv_shape), input_output_aliases=aliases,
        scratch_shapes=[pltpu.VMEM((nheads, 2 * BLOCK, BLOCK), F32), pltpu.VMEM((nheads, 2 * BLOCK, BLOCK), F32),
                        pltpu.VMEM((nheads, 2 * BLOCK, BLOCK), BF16), pltpu.VMEM((nheads, 2 * BLOCK, BLOCK), BF16),
                        pltpu.VMEM((2, nheads, 2 * BLOCK, 1), F32)],
        compiler_params=_cparams(("parallel", "parallel")), name=name)(*args)
    return dk, (dv if slab is not None else dv.reshape(s, kvw))


HPG = 2


def _flash_tiles(sq, sk):
    return _pick(sq, (512, 256, 128)), _pick(sk, (512, 256, 128))


def _diag_mask(t):
    return lax.broadcasted_iota(jnp.int32, (t, t), 1) <= lax.broadcasted_iota(jnp.int32, (t, t), 0)


def _attn_fwd(q, k, vv, *, dv, causal, scale, name, xchg=None):
    sq, qw = q.shape
    sk = k.shape[0]
    varr, vbw, voff = vv
    assert vbw == HPG * dv
    ngrp = qw // (HPG * LANES)
    tq, tk = _flash_tiles(sq, sk)
    nq, nk = sq // tq, sk // tk
    if causal:
        assert tq == tk and sq == sk

    def body(q_ref, k_ref, v_ref, o_ref, l_ref, m_s, l_s, acc_s):
        i = pl.program_id(1)
        m_s[...] = jnp.full_like(m_s, NEG)
        l_s[...] = jnp.zeros_like(l_s)
        acc_s[...] = jnp.zeros_like(acc_s)

        def step(j, masked):
            rows = pl.ds(pl.multiple_of(j * tk, tk), tk)
            if masked:
                mask = _diag_mask(tq)
            for h in range(HPG):
                hs = slice(LANES * h, LANES * (h + 1))
                sc = _dot(q_ref[:, hs], k_ref[rows, hs], NT_DIMS) * scale
                if masked:
                    sc = jnp.where(mask, sc, NEG)
                m_prev = m_s[h]
                m_new = jnp.maximum(m_prev, jnp.max(sc, axis=1, keepdims=True))
                alpha = jnp.exp(m_prev - m_new)
                p = jnp.exp(sc - m_new[:, 0:1])
                l_s[h] = alpha * l_s[h] + jnp.sum(p, axis=1, keepdims=True)
                acc_s[h] = acc_s[h] * alpha[:, 0:dv] + _dot(p, v_ref[rows, dv * h:dv * (h + 1)])
                m_s[h] = m_new

        def loop_body(j, carry):
            step(j, False)
            return carry

        if causal:
            lax.fori_loop(0, i, loop_body, 0)
            step(i, True)
        else:
            lax.fori_loop(0, nk, loop_body, 0)
        for h in range(HPG):
            o_ref[:, dv * h:dv * (h + 1)] = acc_s[h] * (1.0 / l_s[h][:, 0:dv])
            l_ref[:, LANES * h:LANES * (h + 1)] = m_s[h] + jnp.log(l_s[h])

    in_specs = [
        pl.BlockSpec((tq, HPG * LANES), lambda g, i: (i, g)),
        pl.BlockSpec((sk, HPG * LANES), lambda g, i: (0, g)),
        pl.BlockSpec((sk, vbw), lambda g, i: (0, voff + g)),
    ]
    return _pcall(
        body, grid=(ngrp, nq), in_specs=in_specs,
        out_specs=(pl.BlockSpec((tq, HPG * dv), lambda g, i: (i, g)),
                   pl.BlockSpec((tq, HPG * LANES), lambda g, i: (i, g))),
        out_shape=(jax.ShapeDtypeStruct((sq, ngrp * HPG * dv), F32), jax.ShapeDtypeStruct((sq, qw), F32)),
        scratch_shapes=[pltpu.VMEM((HPG, tq, LANES), F32), pltpu.VMEM((HPG, tq, LANES), F32),
                        pltpu.VMEM((HPG, tq, dv), F32)],
        sem=("parallel", "arbitrary"), name=name, args=(q, k, varr), xchg=xchg)


def _attn_dq(q, k, vv, do, lse, delta, *, dv, causal, scale, name, xchg=None):
    sq, qw = q.shape
    sk = k.shape[0]
    varr, vbw, voff = vv
    ngrp = qw // (HPG * LANES)
    tq, tk = _flash_tiles(sq, sk)
    nq, nk = sq // tq, sk // tk

    def body(q_ref, k_ref, v_ref, do_ref, l_ref, d_ref, dq_ref, acc_s):
        i = pl.program_id(1)
        acc_s[...] = jnp.zeros_like(acc_s)

        def step(j, masked):
            rows = pl.ds(pl.multiple_of(j * tk, tk), tk)
            if masked:
                mask = _diag_mask(tq)
            for h in range(HPG):
                hs = slice(LANES * h, LANES * (h + 1))
                kh = k_ref[rows, hs]
                sc = _dot(q_ref[:, hs], kh, NT_DIMS) * scale
                if masked:
                    sc = jnp.where(mask, sc, NEG)
                p = jnp.exp(sc - l_ref[:, LANES * h:LANES * h + 1])
                dp = _dot(do_ref[:, dv * h:dv * (h + 1)], v_ref[rows, dv * h:dv * (h + 1)], NT_DIMS)
                ds = p * (dp - d_ref[:, dv * h:dv * h + 1]) * scale
                acc_s[h] += _dot(ds, kh)

        def loop_body(j, carry):
            step(j, False)
            return carry

        if causal:
            lax.fori_loop(0, i, loop_body, 0)
            step(i, True)
        else:
            lax.fori_loop(0, nk, loop_body, 0)
        for h in range(HPG):
            dq_ref[:, LANES * h:LANES * (h + 1)] = acc_s[h]

    qspec = pl.BlockSpec((tq, HPG * LANES), lambda g, i: (i, g))
    ospec = pl.BlockSpec((tq, HPG * dv), lambda g, i: (i, g))
    in_specs = [
        qspec,
        pl.BlockSpec((sk, HPG * LANES), lambda g, i: (0, g)),
        pl.BlockSpec((sk, vbw), lambda g, i: (0, voff + g)),
        ospec, qspec, ospec,
    ]
    return _pcall(
        body, grid=(ngrp, nq), in_specs=in_specs, out_specs=qspec,
        out_shape=jax.ShapeDtypeStruct((sq, qw), F32),
        scratch_shapes=[pltpu.VMEM((HPG, tq, LANES), F32)],
        sem=("parallel", "arbitrary"), name=name, args=(q, k, varr, do, lse, delta), xchg=xchg)


def _attn_dkv(q, k, vv, do, lse, delta, *, dv, causal, scale, name, xchg=None):
    sq, qw = q.shape
    sk = k.shape[0]
    varr, vbw, voff = vv
    ngrp = qw // (HPG * LANES)
    tq, tk = _flash_tiles(sq, sk)
    nq, nk = sq // tq, sk // tk

    def body(q_ref, k_ref, v_ref, do_ref, l_ref, d_ref, dk_ref, dv_ref, dk_s, dv_s):
        j = pl.program_id(1)
        dk_s[...] = jnp.zeros_like(dk_s)
        dv_s[...] = jnp.zeros_like(dv_s)

        def step(i, masked):
            rows = pl.ds(pl.multiple_of(i * tq, tq), tq)
            if masked:
                mask = _diag_mask(tq)
            for h in range(HPG):
                qh = q_ref[rows, LANES * h:LANES * (h + 1)]
                doh = do_ref[rows, dv * h:dv * (h + 1)]
                sc = _dot(qh, k_ref[:, LANES * h:LANES * (h + 1)], NT_DIMS) * scale
                if masked:
                    sc = jnp.where(mask, sc, NEG)
                p = jnp.exp(sc - l_ref[rows, LANES * h:LANES * h + 1])
                dv_s[h] += _dot(p, doh, TN_DIMS)
                dp = _dot(doh, v_ref[:, dv * h:dv * (h + 1)], NT_DIMS)
                ds = p * (dp - d_ref[rows, dv * h:dv * h + 1]) * scale
                dk_s[h] += _dot(ds, qh, TN_DIMS)

        def loop_body(i, carry):
            step(i, False)
            return carry

        if causal:
            step(j, True)
            lax.fori_loop(j + 1, nq, loop_body, 0)
        else:
            lax.fori_loop(0, nq, loop_body, 0)
        for h in range(HPG):
            dk_ref[:, LANES * h:LANES * (h + 1)] = dk_s[h]
            dv_ref[:, dv * h:dv * (h + 1)] = dv_s[h].astype(BF16)

    qspec = pl.BlockSpec((sq, HPG * LANES), lambda g, j: (0, g))
    ospec = pl.BlockSpec((sq, HPG * dv), lambda g, j: (0, g))
    kspec = pl.BlockSpec((tk, HPG * LANES), lambda g, j: (j, g))
    in_specs = [qspec, kspec, pl.BlockSpec((tk, vbw), lambda g, j: (j, voff + g)), ospec, qspec, ospec]
    return _pcall(
        body, grid=(ngrp, nk), in_specs=in_specs,
        out_specs=(kspec, pl.BlockSpec((tk, HPG * dv), lambda g, j: (j, g))),
        out_shape=(jax.ShapeDtypeStruct((sk, qw), F32), jax.ShapeDtypeStruct((sk, ngrp * HPG * dv), BF16)),
        scratch_shapes=[pltpu.VMEM((HPG, tk, LANES), F32), pltpu.VMEM((HPG, tk, dv), F32)],
        sem=("parallel", "arbitrary"), name=name, args=(q, k, varr, do, lse, delta), xchg=xchg)


def _xchg_copies(src_ref, out_ref, send_sems, recv_sems, local_sems, k, scatter):
    me = 4 * lax.axis_index("x") + 2 * lax.axis_index("y") + lax.axis_index("c")
    own = src_ref.at[me] if scatter else src_ref
    copies = [pltpu.make_async_copy(own, out_ref.at[me], local_sems.at[k])]
    for d in range(1, N_DEV):
        peer = (me + d) % N_DEV
        piece = src_ref.at[peer] if scatter else src_ref
        sem = (N_DEV - 1) * k + d - 1
        copies.append(pltpu.make_async_remote_copy(
            src_ref=piece, dst_ref=out_ref.at[me], send_sem=send_sems.at[sem], recv_sem=recv_sems.at[sem],
            device_id=(peer // 4, (peer // 2) % 2, peer % 2), device_id_type=pl.DeviceIdType.MESH))
    return copies


def _xchg_shapes(xchg):
    return [jax.ShapeDtypeStruct((N_DEV,) + tuple(src.shape[1:] if scatter else src.shape), src.dtype)
            for src, scatter in xchg]


def _xchg_sems(nx):
    return [pltpu.SemaphoreType.DMA(((N_DEV - 1) * nx,)), pltpu.SemaphoreType.DMA(((N_DEV - 1) * nx,)),
            pltpu.SemaphoreType.DMA((nx,))]


ANY_SPEC = pl.BlockSpec(memory_space=pl.ANY)


def _exchange_many(xchg, *, name):
    nx = len(xchg)

    def body(*refs):
        srcs, outs = refs[:nx], refs[nx:2 * nx]
        send_sems, recv_sems, local_sems = refs[2 * nx:]
        copies = [c for k, (_, scatter) in enumerate(xchg)
                  for c in _xchg_copies(srcs[k], outs[k], send_sems, recv_sems, local_sems, k, scatter)]
        for c in copies:
            c.start()
        for c in copies:
            c.wait()

    return pl.pallas_call(
        body, in_specs=[ANY_SPEC] * nx, out_specs=tuple([ANY_SPEC] * nx), out_shape=tuple(_xchg_shapes(xchg)),
        scratch_shapes=_xchg_sems(nx), name=name)(*[src for src, _ in xchg])


def _pcall(body, *, grid, in_specs, out_specs, out_shape, scratch_shapes=(), sem, name, args, xchg=None):
    single = not isinstance(out_shape, (tuple, list))
    if not xchg:
        res = pl.pallas_call(
            body, grid=grid, in_specs=list(in_specs), out_specs=out_specs, out_shape=out_shape,
            scratch_shapes=list(scratch_shapes), compiler_params=_cparams(sem), name=name)(*args)
        return res if xchg is None else (res, [])
    out_specs_l = [out_specs] if single else list(out_specs)
    out_shape_l = [out_shape] if single else list(out_shape)
    nx, n_in, n_out, n_scr = len(xchg), len(in_specs), len(out_shape_l), len(scratch_shapes)

    def wrapped(*refs):
        ins, xin = refs[:n_in], refs[n_in:n_in + nx]
        pos = n_in + nx
        outs, xout = refs[pos:pos + n_out], refs[pos + n_out:pos + n_out + nx]
        pos += n_out + nx
        scr = refs[pos:pos + n_scr]
        send_sems, recv_sems, local_sems = refs[pos + n_scr:]
        first = functools.reduce(jnp.logical_and, [pl.program_id(a) == 0 for a in range(len(grid))])
        last = functools.reduce(jnp.logical_and, [pl.program_id(a) == grid[a] - 1 for a in range(len(grid))])

        def copies():
            return [c for k, (_, scatter) in enumerate(xchg)
                    for c in _xchg_copies(xin[k], xout[k], send_sems, recv_sems, local_sems, k, scatter)]

        @pl.when(first)
        def _():
            for c in copies():
                c.start()

        body(*ins, *outs, *scr)

        @pl.when(last)
        def _():
            for c in copies():
                c.wait()

    res = pl.pallas_call(
        wrapped, grid=grid, in_specs=list(in_specs) + [ANY_SPEC] * nx,
        out_specs=tuple(out_specs_l + [ANY_SPEC] * nx), out_shape=tuple(out_shape_l + _xchg_shapes(xchg)),
        scratch_shapes=list(scratch_shapes) + _xchg_sems(nx),
        compiler_params=_cparams(("arbitrary",) * len(grid)), name=name + "_x")(*args, *[src for src, _ in xchg])
    main = res[0] if single else tuple(res[:n_out])
    return main, list(res[n_out:])


def _unshard(wg, axis):
    nd = wg.ndim - 1
    perm = list(range(1, axis + 1)) + [0] + list(range(axis + 1, nd + 1))
    t = wg.transpose(perm)
    shp = list(wg.shape[1:])
    shp[axis] *= N_DEV
    return t.reshape(shp)


def _shard_major(full, axis):
    shp = list(full.shape)
    shp[axis:axis + 1] = [N_DEV, shp[axis] // N_DEV]
    t = full.reshape(shp)
    perm = [axis] + list(range(axis)) + list(range(axis + 1, len(shp)))
    return t.transpose(perm)


BIG = ("w_in", "b_w_uq", "b_w_ukv", "m_w_kv", "w_branch", "w_out", "w_up", "w_down")
BIG_AXIS = dict(w_in=2, b_w_uq=2, b_w_ukv=2, m_w_kv=1, w_branch=3, w_out=1, w_up=2, w_down=1)
SMALL = ("g_mix", "b_gate", "a_qn", "a_kn", "a_sink", "b_qa_norm", "b_kva_norm", "b_qn", "b_kn", "c_qn", "c_kn",
         "m_g_mem", "m_qn", "m_kn", "g_mlp")
WEIGHTS = ("g_mix", "w_in", "b_gate", "a_qn", "a_kn", "a_sink", "b_qa_norm", "b_kva_norm", "b_w_uq", "b_w_ukv", "b_qn",
           "b_kn", "c_qn", "c_kn", "m_g_mem", "m_w_kv", "m_qn", "m_kn", "w_branch", "w_out", "g_mlp", "w_up", "w_down")


def _rope_tables(positions, dim):
    inv = ROPE_THETA ** (-jnp.arange(0, dim, 2, dtype=F32) / dim)
    ang = positions.astype(F32)[:, None] * inv
    return jnp.cos(ang), jnp.sin(ang)


def _lane_mask(lo, hi):
    lane = np.arange(LANES)
    return jnp.asarray(((lane >= lo) & (lane < hi)).astype(np.float32)[None, :])


def _step(x, mem, positions, w, moments_m, moments_v, loss_target):
    s_len, d_model = x.shape
    n_layers = w["g_mix"].shape[0]
    lay, orig_names, order, n_in, nw = _in_layout(d_model)
    unit = lambda name: lay[name]["off"] // lay[name]["nw"]

    def ag_src(l, names):
        return [(w[n][l].astype(BF16), False) for n in names]

    def layout(name, block):
        full = _unshard(block, BIG_AXIS[name] - 1)
        if name == "w_in":
            return dict(w_in=_permute_cols(full, lay, order))
        if name == "b_w_uq":
            uq = full.reshape(B_Q_LORA, B_HEADS, B_NOPE + B_ROPE)
            return dict(uq=jnp.pad(uq, ((0, 0), (0, 0), (0, LANES - B_NOPE - B_ROPE))).reshape(B_Q_LORA, B_HEADS * LANES))
        if name == "b_w_ukv":
            ukv = full.reshape(B_KV_LORA, B_HEADS, B_NOPE + B_V)
            return dict(ukv=jnp.concatenate([
                jnp.pad(ukv[..., :B_NOPE], ((0, 0), (0, 0), (0, LANES - B_NOPE))).reshape(B_KV_LORA, B_HEADS * LANES),
                ukv[..., B_NOPE:].reshape(B_KV_LORA, B_HEADS * B_V)], axis=-1))
        return {name: full}

    def rs_src(name, g):
        return (_shard_major(g, BIG_AXIS[name] - 1).astype(BF16), True)

    p64 = _pattern([64, 64])
    p128 = _pattern([128])
    pq = _pattern([64, 32, 32])
    pkr = _pattern([32, 32, 32, 32])
    cos_h, sin_h = _rope_tables(positions, A_HD)
    cos_r, sin_r = _rope_tables(positions, B_ROPE)
    ones = lambda n: jnp.ones((s_len, n), F32)
    zeros = lambda n: jnp.zeros((s_len, n), F32)
    rope64 = (jnp.tile(cos_h, (1, 4)), jnp.tile(jnp.concatenate([-sin_h, sin_h], axis=1), (1, 2)),
              jnp.asarray((np.arange(LANES) % 64 < 32).astype(np.float32)[None, :]), 32)
    rope_q = (jnp.concatenate([ones(64), cos_r, cos_r, ones(32)], axis=1),
              jnp.concatenate([zeros(64), -sin_r, sin_r, zeros(32)], axis=1), _lane_mask(64, 80), 16)
    rope_k = (jnp.concatenate([cos_r, cos_r, ones(96)], axis=1),
              jnp.concatenate([-sin_r, sin_r, zeros(96)], axis=1), _lane_mask(0, 16), 16)

    scale_b = (B_NOPE + B_ROPE) ** -0.5
    scale_m = M_HD ** -0.5

    n_l = n_layers
    per_head = lambda g, nh: jnp.tile(g, (1, nh))[:, None, :]
    lpad = lambda g, n: jnp.pad(g, ((0, 0), (0, n)))
    gains_all = dict(
        g_mix=w["g_mix"][:, None, :], g_mlp=w["g_mlp"][:, None, :], m_g_mem=w["m_g_mem"][:, None, :],
        b_gate=w["b_gate"][:, None, :], b_qa_norm=w["b_qa_norm"][:, None, :], b_kva_norm=w["b_kva_norm"][:, None, :],
        a_q=per_head(w["a_qn"], A_HEADS), a_k=per_head(w["a_kn"], A_KV_HEADS),
        b_q=per_head(lpad(w["b_qn"], 32), B_HEADS), b_k=per_head(lpad(w["b_kn"][:, :B_NOPE], 64), B_HEADS),
        b_kr=lpad(w["b_kn"][:, B_NOPE:], 96)[:, None, :],
        c_q=[per_head(w["c_qn"][:, g], C_HEADS) for g in range(3)],
        c_k=[per_head(w["c_kn"][:, g], C_HEADS) for g in range(3)],
        m_q=per_head(w["m_qn"], M_HEADS), m_k=per_head(w["m_kn"], M_HEADS),
        sink_row=jnp.repeat(w["a_sink"], A_HD, axis=1)[:, None, :],
        sinks=jnp.broadcast_to(w["a_sink"][:, :, None, None], (n_l, A_HEADS, 1, LANES)))

    def gains(l):
        return {k: ([(a, l) for a in v] if isinstance(v, list) else (v, l)) for k, v in gains_all.items()}

    ag_hosts = dict(mm_in=("w_up",), band_a=("w_down",), flash_b=("w_in",), mm_up=("w_branch", "w_out"),
                    mm_down=("m_w_kv", "b_w_uq", "b_w_ukv"))
    first_alone = ("w_in", "b_w_uq", "b_w_ukv", "m_w_kv")
    first_hosts = dict(band_c0=("w_up",), band_c1=("w_down",), band_c2=("w_branch", "w_out"))
    saved = []
    weights = [dict() for _ in range(n_layers)]

    def arrived(layer, names, blocks):
        for name, blk in zip(names, blocks):
            weights[layer].update(layout(name, blk))

    cur = x
    arrived(0, first_alone, _exchange_many(ag_src(0, first_alone), name="ag_first"))
    for l in range(n_layers):
        gn = gains(l)
        sv = dict(x=cur)
        wl = weights[l]

        def carry(host):
            return ag_src(l + 1, ag_hosts[host]) if l + 1 < n_layers else []

        def carry0(host):
            return ag_src(0, first_hosts[host]) if l == 0 else []

        h = _rmsnorm_fwd((cur, d_model, 0), gn["g_mix"], name="f_norm_mix")
        z, got = _mm(h, wl["w_in"], name="f_mm_in", xchg=carry("mm_in"))
        arrived(l + 1, ag_hosts["mm_in"], got)
        sv.update(h=h, z=z)
        qa = _headnorm_fwd((z, 512, unit("a_q")), gn["a_q"], p64, rope64, name="f_hn_aq")
        ka = _headnorm_fwd((z, 128, unit("a_k")), gn["a_k"], p64, rope64, name="f_hn_ak")
        va = (z, 128, unit("a_v"))
        (o_a, lse_a), got = _band_fwd(qa, ka, va, dil=1, group=A_HEADS // A_KV_HEADS, maxdist=A_WINDOW - 1,
                                      sinks=gn["sinks"], name="f_band_a", xchg=carry("band_a"))
        arrived(l + 1, ag_hosts["band_a"], got)
        sv.update(qa=qa, ka=ka, o_a=o_a, lse_a=lse_a)
        cq = _rmsnorm_fwd((z, B_Q_LORA, unit("b_cq")), gn["b_qa_norm"], name="f_norm_cq")
        q_up = _mm(cq, wl["uq"], name="f_mm_uq")
        ckv = _rmsnorm_fwd((z, B_KV_LORA, unit("b_ckv")), gn["b_kva_norm"], name="f_norm_ckv")
        kv_up = _mm(ckv, wl["ukv"], name="f_mm_ukv")
        kp = _headnorm_fwd((z, LANES, unit("b_kr")), gn["b_kr"], pkr, rope_k, name="f_hn_kr")
        qb = _headnorm_fwd((q_up, B_HEADS * LANES, 0), gn["b_q"], pq, rope_q, name="f_hn_bq")
        kb = _headnorm_fwd((kv_up, B_HEADS * LANES, 0), gn["b_k"], p64, None, addend=kp, name="f_hn_bk")
        vb = (kv_up, HPG * B_V, B_HEADS * LANES // (HPG * B_V))
        (o_b, lse_b), got = _attn_fwd(qb, kb, vb, dv=B_V, causal=True, scale=scale_b, name="f_flash_b",
                                       xchg=carry("flash_b"))
        arrived(l + 1, ag_hosts["flash_b"], got)
        sv.update(cq=cq, q_up=q_up, ckv=ckv, kv_up=kv_up, qb=qb, kb=kb, o_b=o_b, lse_b=lse_b)
        qc, kc, o_cg, lse_cg = [], [], [], []
        for g, (win, dil) in enumerate(C_PATTERNS):
            qg = _headnorm_fwd((z, 512, unit("c%d" % (3 * g))), gn["c_q"][g], p64, rope64, dil=dil,
                               name="f_hn_cq%d" % g)
            kg = _headnorm_fwd((z, 512, unit("c%d" % (3 * g + 1))), gn["c_k"][g], p64, rope64, dil=dil,
                               name="f_hn_ck%d" % g)
            host = "band_c%d" % g
            (og, lg), got = _band_fwd(qg, kg, (z, 512, unit("c%d" % (3 * g + 2))), dil=dil, group=1,
                                      maxdist=win // dil, name="f_band_c%d" % g,
                                      xchg=carry0(host) if host in first_hosts else ())
            arrived(0, first_hosts.get(host, ()), got)
            qc.append(qg), kc.append(kg), o_cg.append(og), lse_cg.append(lg)
        c_dils = [dil for _, dil in C_PATTERNS]
        o_c = _combine_fwd(o_cg, lse_cg, c_dils, rows=s_len, width=BRANCH_W, name="f_combine")
        sv.update(qc=qc, kc=kc, o_cg=o_cg, lse_cg=lse_cg, o_c=o_c)
        memn = _rmsnorm_fwd((mem, d_model, 0), gn["m_g_mem"], name="f_norm_mem")
        mkv = _mm(memn, wl["m_w_kv"], name="f_mm_mkv")
        mk = _headnorm_fwd((mkv, M_HEADS * M_HD, 0), gn["m_k"], p128, None, name="f_hn_mk")
        mq = _headnorm_fwd((z, M_HEADS * M_HD, unit("m_q")), gn["m_q"], p128, None, name="f_hn_mq")
        mv = (mkv, HPG * M_HD, M_HEADS * M_HD // (HPG * M_HD))
        o_m, lse_m = _attn_fwd(mq, mk, mv, dv=M_HD, causal=False, scale=scale_m, name="f_flash_m")
        sv.update(memn=memn, mkv=mkv, mk=mk, mq=mq, o_m=o_m, lse_m=lse_m)
        outs = (o_a, o_b, o_c, o_m)
        ys = [_mm(outs[n], wl["w_branch"][n], name="f_mm_branch") for n in range(N_BRANCH)]
        t = _merge_fwd((z, N_BRANCH * d_model, 0), gn["b_gate"], ys, name="f_merge")
        x2 = _mm(t, wl["w_out"], mode="add", extra=cur, name="f_mm_out")
        h2 = _rmsnorm_fwd((x2, d_model, 0), gn["g_mlp"], name="f_norm_mlp")
        (u, act), got = _mm(h2, wl["w_up"], mode="relu2", name="f_mm_up", xchg=carry("mm_up"))
        arrived(l + 1, ag_hosts["mm_up"], got)
        x3, got = _mm(act, wl["w_down"], mode="add", extra=x2, name="f_mm_down", xchg=carry("mm_down"))
        arrived(l + 1, ag_hosts["mm_down"], got)
        sv.update(ys=ys, t=t, x2=x2, h2=h2, u=u, act=act)
        saved.append(sv)
        cur = x3

    dx, loss_row = _loss_head(cur, loss_target, name="loss_head")
    loss = lax.psum(loss_row[0, 0], AXES)

    recvs = {n: [None] * n_layers for n in BIG}
    rows = {}

    def keep(key, layer, row):
        rows.setdefault(key, [None] * n_layers)[layer] = row

    def item(name, layer, g):
        return (name, layer, rs_src(name, g))

    def srcs(items):
        return [it[2] for it in items]

    def store(items, got):
        for (name, layer, _), r in zip(items, got):
            recvs[name][layer] = r

    carried = []
    for l in reversed(range(n_layers)):
        sv = saved[l]
        wl = weights[l]
        gn = gains(l)
        z = sv["z"]
        g_wdown = _mm(sv["act"], dx, ta=True, name="b_mm_wdown")
        du = _mm(dx, wl["w_down"], tb=True, mode="drelu2", extra=sv["u"], out_dtype=BF16, name="b_mm_du")
        g_wup = _mm(sv["h2"], du, ta=True, name="b_mm_wup")
        dh2 = _mm(du, wl["w_up"], tb=True, name="b_mm_dh2")
        dx2, dg = _rmsnorm_bwd(dh2, (sv["x2"], d_model, 0), gn["g_mlp"], dres=dx, name="b_norm_mlp")
        keep("g_mlp", l, dg)
        g_wout = _mm(sv["t"], dx2, ta=True, name="b_mm_wout")
        dt = _mm(dx2, wl["w_out"], tb=True, name="b_mm_dt")
        dys, dz, dbg = _merge_bwd(dt, (z, N_BRANCH * d_model, 0), gn["b_gate"], sv["ys"], slab_width=nw,
                                  name="b_merge")
        keep("b_gate", l, dbg)
        outs = (sv["o_a"], sv["o_b"], sv["o_c"], sv["o_m"])
        g_wbranch = jnp.stack([_mm(outs[n], dys[n], ta=True, name="b_mm_wbranch") for n in range(N_BRANCH)])
        do_a, do_b, do_c, do_m = [_mm(dys[n], wl["w_branch"][n], tb=True, name="b_mm_dbranch")
                                  for n in range(N_BRANCH)]
        mlp_items = [item("w_down", l, g_wdown), item("w_up", l, g_wup)]
        proj_items = [item("w_out", l, g_wout), item("w_branch", l, g_wbranch)]
        va = (z, 128, unit("a_v"))
        kw_a = dict(dil=1, group=A_HEADS // A_KV_HEADS, maxdist=A_WINDOW - 1)
        dl_a, dsink = _delta(do_a, sv["o_a"], p64, lse=sv["lse_a"], sink_row=gn["sink_row"], name="b_delta_a")
        keep("a_sink", l, dsink)
        dqa = _band_dq(sv["qa"], sv["ka"], va, do_a, sv["lse_a"], dl_a, name="b_band_dq_a", **kw_a)
        dka, dz = _band_dkv(sv["qa"], sv["ka"], va, do_a, sv["lse_a"], dl_a, slab=(dz, nw), name="b_band_dkv_a",
                            **kw_a)
        dz, dg = _headnorm_bwd(dqa, (z, 512, unit("a_q")), gn["a_q"], p64, rope64, slab=(dz, nw), name="b_hn_aq")
        keep("a_qn", l, dg)
        dz, dg = _headnorm_bwd(dka, (z, 128, unit("a_k")), gn["a_k"], p64, rope64, slab=(dz, nw), name="b_hn_ak")
        keep("a_kn", l, dg)
        vb = (sv["kv_up"], HPG * B_V, B_HEADS * LANES // (HPG * B_V))
        dl_b, _ = _delta(do_b, sv["o_b"], p64, name="b_delta_b")
        kw_b = dict(dv=B_V, causal=True, scale=scale_b)
        dqb, got = _attn_dq(sv["qb"], sv["kb"], vb, do_b, sv["lse_b"], dl_b, name="b_flash_dq_b",
                            xchg=srcs(mlp_items), **kw_b)
        store(mlp_items, got)
        dkv_items = carried + proj_items
        (dkb, dvb), got = _attn_dkv(sv["qb"], sv["kb"], vb, do_b, sv["lse_b"], dl_b, name="b_flash_dkv_b",
                                     xchg=srcs(dkv_items), **kw_b)
        store(dkv_items, got)
        d_qup, dg = _headnorm_bwd(dqb, (sv["q_up"], B_HEADS * LANES, 0), gn["b_q"], pq, rope_q, name="b_hn_bq")
        keep("b_q", l, dg)
        d_kn, dg = _headnorm_bwd(dkb, (sv["kv_up"], B_HEADS * LANES, 0), gn["b_k"], p64, None, name="b_hn_bk")
        keep("b_k", l, dg)
        dkp = _kp_reduce(dkb, name="b_kp_reduce")
        dz, dg = _headnorm_bwd(dkp, (z, LANES, unit("b_kr")), gn["b_kr"], pkr, rope_k, slab=(dz, nw), name="b_hn_kr")
        keep("b_kr", l, dg)
        d_kvup = jnp.concatenate([d_kn, dvb], axis=1)
        g_ukv = _mm(sv["ckv"], d_kvup, ta=True, name="b_mm_wukv")
        dckv = _mm(d_kvup, wl["ukv"], tb=True, name="b_mm_dckv")
        dz, dg = _rmsnorm_bwd(dckv, (z, B_KV_LORA, unit("b_ckv")), gn["b_kva_norm"], out_dtype=BF16, slab=(dz, nw),
                              name="b_norm_ckv")
        keep("b_kva_norm", l, dg)
        g_uq = _mm(sv["cq"], d_qup, ta=True, name="b_mm_wuq")
        dcq = _mm(d_qup, wl["uq"], tb=True, name="b_mm_dcq")
        dz, dg = _rmsnorm_bwd(dcq, (z, B_Q_LORA, unit("b_cq")), gn["b_qa_norm"], out_dtype=BF16, slab=(dz, nw),
                              name="b_norm_cq")
        keep("b_qa_norm", l, dg)
        g_wuq = g_uq.reshape(B_Q_LORA, B_HEADS, LANES)[:, :, :B_NOPE + B_ROPE].reshape(B_Q_LORA, -1)
        g_wukv = jnp.concatenate(
            [g_ukv[:, :B_HEADS * LANES].reshape(B_KV_LORA, B_HEADS, LANES)[:, :, :B_NOPE],
             g_ukv[:, B_HEADS * LANES:].reshape(B_KV_LORA, B_HEADS, B_V)], axis=-1).reshape(B_KV_LORA, -1)
        do_cg, dl_cg = _combine_bwd(do_c, sv["o_c"], sv["lse_cg"], [dil for _, dil in C_PATTERNS], p64,
                                    name="b_combine")
        for g, (win, dil) in enumerate(C_PATTERNS):
            vc = (z, 512, unit("c%d" % (3 * g + 2)))
            kw_c = dict(dil=dil, group=1, maxdist=win // dil)
            args = (sv["qc"][g], sv["kc"][g], vc, do_cg[g], sv["lse_cg"][g], dl_cg[g])
            dq = _band_dq(*args, name="b_band_dq_c%d" % g, **kw_c)
            if dil == 1:
                dk, dz = _band_dkv(*args, slab=(dz, nw), name="b_band_dkv_c%d" % g, **kw_c)
            else:
                dk, dv_ = _band_dkv(*args, name="b_band_dkv_c%d" % g, **kw_c)
                vcol = lay["c%d" % (3 * g + 2)]["off"]
                dz = lax.dynamic_update_slice(dz, dv_, (0, vcol))
            dz, dg = _headnorm_bwd(dq, (z, 512, unit("c%d" % (3 * g))), gn["c_q"][g], p64, rope64, dil=dil,
                                   slab=(dz, nw), name="b_hn_cq%d" % g)
            keep("c_q%d" % g, l, dg)
            dz, dg = _headnorm_bwd(dk, (z, 512, unit("c%d" % (3 * g + 1))), gn["c_k"][g], p64, rope64, dil=dil,
                                   slab=(dz, nw), name="b_hn_ck%d" % g)
            keep("c_k%d" % g, l, dg)
        mv = (sv["mkv"], HPG * M_HD, M_HEADS * M_HD // (HPG * M_HD))
        dl_m, _ = _delta(do_m, sv["o_m"], p128, name="b_delta_m")
        kw_m = dict(dv=M_HD, causal=False, scale=scale_m)
        dmq = _attn_dq(sv["mq"], sv["mk"], mv, do_m, sv["lse_m"], dl_m, name="b_flash_dq_m", **kw_m)
        dmk, dmv = _attn_dkv(sv["mq"], sv["mk"], mv, do_m, sv["lse_m"], dl_m, name="b_flash_dkv_m", **kw_m)
        dz, dg = _headnorm_bwd(dmq, (z, M_HEADS * M_HD, unit("m_q")), gn["m_q"], p128, None, slab=(dz, nw),
                               name="b_hn_mq")
        keep("m_qn", l, dg)
        d_mk, dg = _headnorm_bwd(dmk, (sv["mkv"], M_HEADS * M_HD, 0), gn["m_k"], p128, None, name="b_hn_mk")
        keep("m_kn", l, dg)
        d_mkv = jnp.concatenate([d_mk, dmv], axis=1)
        g_wmkv = _mm(sv["memn"], d_mkv, ta=True, name="b_mm_wmkv")
        dmemn = _mm(d_mkv, wl["m_w_kv"], tb=True, name="b_mm_dmemn")
        _, dg = _rmsnorm_bwd(dmemn, (mem, d_model, 0), gn["m_g_mem"], name="b_norm_mem")
        keep("m_g_mem", l, dg)
        bm_items = [item("b_w_ukv", l, g_wukv), item("b_w_uq", l, g_wuq), item("m_w_kv", l, g_wmkv)]
        g_in, got = _mm(sv["h"], dz, ta=True, name="b_mm_win", xchg=srcs(bm_items))
        store(bm_items, got)
        carried = [item("w_in", l, _unpermute_cols(g_in, lay, orig_names))]
        last_items = carried if l == 0 else []
        dh, got = _mm(dz, wl["w_in"], tb=True, name="b_mm_dh", xchg=srcs(last_items))
        store(last_items, got)
        dx, dg = _rmsnorm_bwd(dh, (sv["x"], d_model, 0), gn["g_mix"], dres=dx2, name="b_norm_mix")
        keep("g_mix", l, dg)

    cat = {k: jnp.concatenate(v, axis=0) for k, v in rows.items()}
    heads = lambda a, nh, hd: a.reshape(n_layers, nh, hd).sum(axis=1)
    gsmall = dict(
        g_mix=cat["g_mix"], b_gate=cat["b_gate"], g_mlp=cat["g_mlp"], m_g_mem=cat["m_g_mem"],
        b_qa_norm=cat["b_qa_norm"], b_kva_norm=cat["b_kva_norm"],
        a_sink=cat["a_sink"][:, ::A_HD], a_qn=heads(cat["a_qn"], A_HEADS, A_HD), a_kn=heads(cat["a_kn"], A_KV_HEADS, A_HD),
        b_qn=heads(cat["b_q"], B_HEADS, LANES)[:, :B_NOPE + B_ROPE],
        b_kn=jnp.concatenate([heads(cat["b_k"], B_HEADS, LANES)[:, :B_NOPE], cat["b_kr"][:, :B_ROPE]], axis=1),
        c_qn=jnp.stack([heads(cat["c_q%d" % g], C_HEADS, C_HD) for g in range(3)], axis=1),
        c_kn=jnp.stack([heads(cat["c_k%d" % g], C_HEADS, C_HD) for g in range(3)], axis=1),
        m_qn=heads(cat["m_qn"], M_HEADS, M_HD), m_kn=heads(cat["m_kn"], M_HEADS, M_HD))

    out_g, out_d, out_m, out_v = {}, {}, {}, {}
    for name in BIG:
        shard = w[name].shape
        two = lambda a: a.reshape(-1, shard[-1])
        pieces = [r.reshape(N_DEV, -1, shard[-1]) for r in recvs[name]]
        g, dlt, mn, vn = _adam(pieces, two(w[name]), two(moments_m[name]), two(moments_v[name]), name="adam_" + name)
        out_g[name], out_d[name], out_m[name], out_v[name] = (a.reshape(shard) for a in (g, dlt, mn, vn))
    sizes = [int(np.prod(w[n].shape)) for n in SMALL]
    total = sum(sizes)
    padded = -(-total // (8 * LANES)) * (8 * LANES)
    pack = lambda d: jnp.pad(jnp.concatenate([jnp.reshape(d[n], (-1,)) for n in SMALL]), (0, padded - total)).reshape(-1, LANES)
    gpack = pack(gsmall)
    recv = _exchange_many([(gpack, False)], name="ag_small_grads")[0]
    packed = _adam([recv], pack(w), pack(moments_m), pack(moments_v), name="adam_small")
    offs = np.cumsum([0] + sizes)
    for a, out in zip(packed, (out_g, out_d, out_m, out_v)):
        flat = a.reshape(-1)
        for n, o, sz in zip(SMALL, offs[:-1], sizes):
            out[n] = flat[o:o + sz].reshape(w[n].shape)
    return loss, dx, out_g, out_d, out_m, out_v


def kernel(x, mem, positions, g_mix, w_in, b_gate, a_qn, a_kn, a_sink, b_qa_norm, b_kva_norm, b_w_uq, b_w_ukv, b_qn, b_kn, c_qn, c_kn, m_g_mem, m_w_kv, m_qn, m_kn, w_branch, w_out, g_mlp, w_up, w_down, loss_target, m_g_mix, m_w_in, m_b_gate, m_a_qn, m_a_kn, m_a_sink, m_b_qa_norm, m_b_kva_norm, m_b_w_uq, m_b_w_ukv, m_b_qn, m_b_kn, m_c_qn, m_c_kn, m_m_g_mem, m_m_w_kv, m_m_qn, m_m_kn, m_w_branch, m_w_out, m_g_mlp, m_w_up, m_w_down, v_g_mix, v_w_in, v_b_gate, v_a_qn, v_a_kn, v_a_sink, v_b_qa_norm, v_b_kva_norm, v_b_w_uq, v_b_w_ukv, v_b_qn, v_b_kn, v_c_qn, v_c_kn, v_m_g_mem, v_m_w_kv, v_m_qn, v_m_kn, v_w_branch, v_w_out, v_g_mlp, v_w_up, v_w_down):
    w = dict(g_mix=g_mix, w_in=w_in, b_gate=b_gate, a_qn=a_qn, a_kn=a_kn, a_sink=a_sink, b_qa_norm=b_qa_norm,
             b_kva_norm=b_kva_norm, b_w_uq=b_w_uq, b_w_ukv=b_w_ukv, b_qn=b_qn, b_kn=b_kn, c_qn=c_qn, c_kn=c_kn,
             m_g_mem=m_g_mem, m_w_kv=m_w_kv, m_qn=m_qn, m_kn=m_kn, w_branch=w_branch, w_out=w_out, g_mlp=g_mlp,
             w_up=w_up, w_down=w_down)
    mm_ = dict(g_mix=m_g_mix, w_in=m_w_in, b_gate=m_b_gate, a_qn=m_a_qn, a_kn=m_a_kn, a_sink=m_a_sink,
               b_qa_norm=m_b_qa_norm, b_kva_norm=m_b_kva_norm, b_w_uq=m_b_w_uq, b_w_ukv=m_b_w_ukv, b_qn=m_b_qn,
               b_kn=m_b_kn, c_qn=m_c_qn, c_kn=m_c_kn, m_g_mem=m_m_g_mem, m_w_kv=m_m_w_kv, m_qn=m_m_qn, m_kn=m_m_kn,
               w_branch=m_w_branch, w_out=m_w_out, g_mlp=m_g_mlp, w_up=m_w_up, w_down=m_w_down)
    vv = dict(g_mix=v_g_mix, w_in=v_w_in, b_gate=v_b_gate, a_qn=v_a_qn, a_kn=v_a_kn, a_sink=v_a_sink,
              b_qa_norm=v_b_qa_norm, b_kva_norm=v_b_kva_norm, b_w_uq=v_b_w_uq, b_w_ukv=v_b_w_ukv, b_qn=v_b_qn,
              b_kn=v_b_kn, c_qn=v_c_qn, c_kn=v_c_kn, m_g_mem=v_m_g_mem, m_w_kv=v_m_w_kv, m_qn=v_m_qn, m_kn=v_m_kn,
              w_branch=v_w_branch, w_out=v_w_out, g_mlp=v_g_mlp, w_up=v_w_up, w_down=v_w_down)
    loss, dx, g, dlt, mn, vn = _step(x[0], mem[0], positions[0], w, mm_, vv, loss_target[0])
    return (loss, dx[None], *[g[n] for n in WEIGHTS], *[dlt[n] for n in WEIGHTS], *[mn[n] for n in WEIGHTS],
            *[vn[n] for n in WEIGHTS])
```

```python
import functools

import numpy as np
import jax
import jax.numpy as jnp
from jax import lax
from jax.experimental import pallas as pl
from jax.experimental.pallas import tpu as pltpu

F32 = jnp.float32
BF16 = jnp.bfloat16

LANES = 128
N_DEV = 8
AXES = ("x", "y", "c")
VMEM_LIMIT = 48 * 1024 * 1024

EPS = 1e-6
NEG = -1e30
ROPE_THETA = 10000.0
BLOCK = 128

A_HEADS, A_KV_HEADS, A_HD, A_WINDOW = 8, 2, 64, 128
B_HEADS, B_Q_LORA, B_KV_LORA, B_NOPE, B_ROPE, B_V = 8, 384, 256, 64, 32, 64
C_PATTERNS = ((128, 1), (512, 4), (2048, 16))
C_HEADS, C_HD = 8, 64
M_HEADS, M_HD = 4, 128
BRANCH_W = 512
N_BRANCH = 4

ADAM_LR, ADAM_B1, ADAM_B2, ADAM_EPS, ADAM_WD, ADAM_STEP = 0.001, 0.9, 0.999, 1e-08, 0.01, 10

NT_DIMS = (((1,), (1,)), ((), ()))
TN_DIMS = (((0,), (0,)), ((), ()))
NN_DIMS = (((1,), (0,)), ((), ()))


def _pick(n, cands):
    for c in cands:
        if n % c == 0:
            return c
    return n


def _cparams(sem):
    return pltpu.CompilerParams(dimension_semantics=sem, vmem_limit_bytes=VMEM_LIMIT)


def _dot(a, b, dims=NN_DIMS):
    return lax.dot_general(a.astype(BF16), b.astype(BF16), dims, preferred_element_type=F32)


def _in_layout(d_model):
    orig = ([("a_q", 512), ("a_k", 128), ("a_v", 128), ("b_cq", B_Q_LORA), ("b_ckv", B_KV_LORA), ("b_kr", B_ROPE)]
            + [("c%d" % i, 512) for i in range(9)] + [("m_q", 512), ("gate", N_BRANCH * d_model)])
    order = ["gate", "a_q"] + ["c%d" % i for i in range(9)] + ["m_q", "b_ckv", "b_cq", "a_k", "a_v", "b_kr"]
    o_off, off = {}, 0
    for name, w in orig:
        o_off[name] = (off, w)
        off += w
    lay, noff = {}, 0
    for name in order:
        oo, w = o_off[name]
        nw = LANES if name == "b_kr" else w
        assert noff % nw == 0, (name, noff, nw)
        lay[name] = dict(orig=oo, w=w, off=noff, nw=nw)
        noff += nw
    assert noff % 512 == 0
    return lay, [n for n, _ in orig], order, off, noff


def _permute_cols(w, lay, order):
    cols = []
    for name in order:
        e = lay[name]
        p = w[..., e["orig"]:e["orig"] + e["w"]]
        if e["nw"] != e["w"]:
            p = jnp.pad(p, [(0, 0)] * (w.ndim - 1) + [(0, e["nw"] - e["w"])])
        cols.append(p)
    return jnp.concatenate(cols, axis=-1)


def _unpermute_cols(w, lay, orig_names):
    return jnp.concatenate([w[..., lay[n]["off"]:lay[n]["off"] + lay[n]["w"]] for n in orig_names], axis=-1)


def _pattern(groups):
    ids = np.repeat(np.arange(len(groups)), groups)
    assert ids.shape[0] == LANES
    gmat = (ids[:, None] == ids[None, :]).astype(np.float32)
    invn = (1.0 / np.asarray(groups, np.float32))[ids][None, :]
    return jnp.asarray(gmat, BF16), jnp.asarray(invn, F32)


def _gsum(v, gmat):
    hi = v.astype(BF16)
    r1 = v - hi.astype(F32)
    mid = r1.astype(BF16)
    lo = (r1 - mid.astype(F32)).astype(BF16)
    dot = lambda p: jnp.dot(p, gmat, preferred_element_type=F32)
    return dot(hi) + dot(mid) + dot(lo)


def _swap_half(v, lomask, half):
    return jnp.where(lomask > 0.5, pltpu.roll(v, LANES - half, 1), pltpu.roll(v, half, 1))


def _rows_spec(tr, width, off=0):
    return pl.BlockSpec((tr, width), lambda i: (i, off))


def _undilate(src_ref, scr, dil, width):
    n = scr.shape[1] // dil
    for r in range(dil):
        for c in range(width // LANES):
            lanes = slice(r * width + c * LANES, r * width + (c + 1) * LANES)
            scr[c, pl.ds(r, n, stride=dil), :] = src_ref[:, lanes].astype(F32)


def _dilate(scr, dst_ref, dil, width):
    n = scr.shape[1] // dil
    for r in range(dil):
        for c in range(width // LANES):
            lanes = slice(r * width + c * LANES, r * width + (c + 1) * LANES)
            dst_ref[:, lanes] = scr[c, pl.ds(r, n, stride=dil), :].astype(dst_ref.dtype)


def _dil_spec(tr, width, dil):
    return pl.BlockSpec((tr // dil, dil * width), lambda i: (i, 0))


def _layer_arg(per_layer):
    arr, l = per_layer
    nd = arr.ndim - 1
    return pl.BlockSpec((None,) + tuple(arr.shape[1:]), lambda *_: (l,) + (0,) * nd), arr


def _full_spec(shape):
    return pl.BlockSpec(shape, lambda i: (0,) * len(shape))


def _mm(a, b, *, ta=False, tb=False, out_dtype=F32, mode=None, extra=None, name, xchg=None):
    if ta:
        kd, m = a.shape
    else:
        m, kd = a.shape
    if tb:
        n, kb = b.shape
    else:
        kb, n = b.shape
    assert kd == kb, (a.shape, b.shape, ta, tb)
    tm = _pick(m, (1024, 512, 256, 128))
    tn = _pick(n, (1024, 1536, 512, 384, 256, 128))
    tk = _pick(kd, (1024, 1536, 512, 384, 256, 128))
    nk = kd // tk
    a_spec = (pl.BlockSpec((tk, tm), lambda i, j, k: (k, i)) if ta else pl.BlockSpec((tm, tk), lambda i, j, k: (i, k)))
    b_spec = (pl.BlockSpec((tn, tk), lambda i, j, k: (j, k)) if tb else pl.BlockSpec((tk, tn), lambda i, j, k: (k, j)))
    o_spec = pl.BlockSpec((tm, tn), lambda i, j, k: (i, j))
    dims = (((0 if ta else 1,), (1 if tb else 0,)), ((), ()))
    has_extra = mode in ("add", "drelu2")

    def body(*refs):
        a_ref, b_ref = refs[0], refs[1]
        e_ref = refs[2] if has_extra else None
        outs = refs[2 + has_extra:2 + has_extra + (2 if mode == "relu2" else 1)]
        k = pl.program_id(2)

        def finish(r):
            if mode == "add":
                outs[0][...] = (e_ref[...] + r).astype(out_dtype)
            elif mode == "relu2":
                outs[0][...] = r
                outs[1][...] = jnp.square(jnp.maximum(r, 0.0)).astype(BF16)
            elif mode == "drelu2":
                outs[0][...] = (r * (2.0 * jnp.maximum(e_ref[...], 0.0))).astype(out_dtype)
            else:
                outs[0][...] = r.astype(out_dtype)

        if nk == 1:
            finish(_dot(a_ref[...], b_ref[...], dims))
        else:
            acc = refs[-1]

            @pl.when(k == 0)
            def _():
                acc[...] = _dot(a_ref[...], b_ref[...], dims)

            @pl.when(k > 0)
            def _():
                acc[...] += _dot(a_ref[...], b_ref[...], dims)

            @pl.when(k == nk - 1)
            def _():
                finish(acc[...])

    in_specs = [a_spec, b_spec] + ([o_spec] if has_extra else [])
    args = [a, b] + ([extra] if has_extra else [])
    if mode == "relu2":
        out_shape = (jax.ShapeDtypeStruct((m, n), F32), jax.ShapeDtypeStruct((m, n), BF16))
        out_specs = (o_spec, o_spec)
    else:
        out_shape = jax.ShapeDtypeStruct((m, n), out_dtype)
        out_specs = o_spec
    return _pcall(
        body, grid=(m // tm, n // tn, nk), in_specs=in_specs, out_specs=out_specs, out_shape=out_shape,
        scratch_shapes=[pltpu.VMEM((tm, tn), F32)] if nk > 1 else [], sem=("parallel", "parallel", "arbitrary"),
        name=name, args=args, xchg=xchg)


def _rmsnorm_fwd(xv, gain, *, name, tr=512):
    x, width, off = xv
    rows = x.shape[0]
    tr = min(tr, rows)

    def body(x_ref, g_ref, o_ref):
        xf = x_ref[...]
        rstd = lax.rsqrt(jnp.mean(xf * xf, axis=-1, keepdims=True) + EPS)
        o_ref[...] = (xf * rstd * g_ref[...]).astype(BF16)

    g_spec, g_arr = _layer_arg(gain)
    return pl.pallas_call(
        body, grid=(rows // tr,), in_specs=[_rows_spec(tr, width, off), g_spec],
        out_specs=_rows_spec(tr, width), out_shape=jax.ShapeDtypeStruct((rows, width), BF16),
        compiler_params=_cparams(("parallel",)), name=name)(x, g_arr)


def _slab_out(slab, rows, in_specs, args, out_index):
    prev, total = slab
    aliases = {}
    if prev is not None:
        in_specs.append(pl.BlockSpec(memory_space=pl.ANY))
        args.append(prev)
        aliases = {len(args) - 1: out_index}
    return jax.ShapeDtypeStruct((rows, total), BF16), aliases, int(prev is not None)


def _rmsnorm_bwd(dh, xv, gain, *, dres=None, out_dtype=F32, slab=None, name, tr=512):
    x, width, off = xv
    rows = x.shape[0]
    tr = min(tr, rows)
    has_res = dres is not None
    has_prev = int(slab is not None and slab[0] is not None)

    def body(*refs):
        dh_ref, x_ref, g_ref = refs[:3]
        r_ref = refs[3] if has_res else None
        dx_ref, dg_ref = refs[3 + has_res + has_prev:]
        i = pl.program_id(0)
        xf = x_ref[...]
        rstd = lax.rsqrt(jnp.mean(xf * xf, axis=-1, keepdims=True) + EPS)
        xh = xf * rstd
        d = dh_ref[...]
        dxh = d * g_ref[...]
        dx = rstd * (dxh - xh * jnp.mean(dxh * xh, axis=-1, keepdims=True))
        if has_res:
            dx = dx + r_ref[...]
        dx_ref[...] = dx.astype(out_dtype)

        @pl.when(i == 0)
        def _():
            dg_ref[...] = jnp.zeros_like(dg_ref)

        dg_ref[...] += jnp.sum(d * xh, axis=0, keepdims=True)

    g_spec, g_arr = _layer_arg(gain)
    in_specs = [_rows_spec(tr, width), _rows_spec(tr, width, off), g_spec]
    args = [dh, x, g_arr]
    if has_res:
        in_specs.append(_rows_spec(tr, width))
        args.append(dres)
    dx_spec, dx_shape, aliases = _rows_spec(tr, width), jax.ShapeDtypeStruct((rows, width), out_dtype), {}
    if slab is not None:
        assert out_dtype == BF16
        dx_spec = _rows_spec(tr, width, off)
        dx_shape, aliases, _ = _slab_out(slab, rows, in_specs, args, 0)
    return pl.pallas_call(
        body, grid=(rows // tr,), in_specs=in_specs,
        out_specs=(dx_spec, _full_spec((1, width))),
        out_shape=(dx_shape, jax.ShapeDtypeStruct((1, width), F32)), input_output_aliases=aliases,
        compiler_params=_cparams(("arbitrary",)), name=name)(*args)


def _headnorm_fwd(xv, gain_row, pat, rope, *, addend=None, dil=1, name, tr=512):
    x, width, off = xv
    rows = x.shape[0]
    tr = min(tr, rows)
    gmat, invn = pat
    nblk = width // LANES
    has_rope = rope is not None
    has_add = addend is not None
    half = rope[3] if has_rope else 0

    def body(*refs):
        x_ref, g_ref, gm_ref, in_ref = refs[:4]
        pos = 4
        if has_rope:
            c_ref, s_ref, lo_ref = refs[pos:pos + 3]
            pos += 3
        if has_add:
            a_ref = refs[pos]
            pos += 1
        o_ref = refs[pos]
        scr = refs[pos + 1] if dil > 1 else None
        gm = gm_ref[...]
        inv = in_ref[...]
        if has_add:
            add = pltpu.roll(a_ref[...].astype(F32), 64, 1)
        for j in range(nblk):
            sl = slice(j * LANES, (j + 1) * LANES)
            xb = x_ref[:, sl]
            rstd = lax.rsqrt(_gsum(xb * xb, gm) * inv + EPS)
            y = xb * rstd * g_ref[:, sl]
            if has_rope:
                y = y * c_ref[...] + _swap_half(y, lo_ref[...], half) * s_ref[...]
            if has_add:
                y = y + add
            if dil > 1:
                scr[j] = y
            else:
                o_ref[:, sl] = y.astype(BF16)
        if dil > 1:
            _dilate(scr, o_ref, dil, width)

    g_spec, g_arr = _layer_arg(gain_row)
    in_specs = [_rows_spec(tr, width, off), g_spec, _full_spec((LANES, LANES)), _full_spec((1, LANES))]
    args = [x, g_arr, gmat, invn]
    if has_rope:
        in_specs += [_rows_spec(tr, LANES), _rows_spec(tr, LANES), _full_spec((1, LANES))]
        args += [rope[0], rope[1], rope[2]]
    if has_add:
        in_specs.append(_rows_spec(tr, LANES))
        args.append(addend)
    return pl.pallas_call(
        body, grid=(rows // tr,), in_specs=in_specs,
        out_specs=_dil_spec(tr, width, dil) if dil > 1 else _rows_spec(tr, width),
        out_shape=jax.ShapeDtypeStruct((rows // dil, dil * width), BF16),
        scratch_shapes=[pltpu.VMEM((nblk, tr, LANES), F32)] if dil > 1 else [],
        compiler_params=_cparams(("parallel",)), name=name)(*args)


def _headnorm_bwd(dy, xv, gain_row, pat, rope, *, dil=1, slab=None, name, tr=512):
    x, width, off = xv
    rows = x.shape[0]
    tr = min(tr, rows)
    gmat, invn = pat
    nblk = width // LANES
    has_rope = rope is not None
    has_prev = int(slab is not None and slab[0] is not None)
    half = rope[3] if has_rope else 0

    def body(*refs):
        dy_ref, x_ref, g_ref, gm_ref, in_ref = refs[:5]
        pos = 5
        if has_rope:
            c_ref, s_ref, lo_ref = refs[pos:pos + 3]
            pos += 3
        pos += has_prev
        dx_ref, dg_ref = refs[pos:pos + 2]
        i = pl.program_id(0)
        gm = gm_ref[...]
        inv = in_ref[...]
        if dil > 1:
            scr = refs[pos + 2]
            _undilate(dy_ref, scr, dil, width)

        @pl.when(i == 0)
        def _():
            dg_ref[...] = jnp.zeros_like(dg_ref)

        for j in range(nblk):
            sl = slice(j * LANES, (j + 1) * LANES)
            d = scr[j] if dil > 1 else dy_ref[:, sl]
            if has_rope:
                d = d * c_ref[...] + _swap_half(d * s_ref[...], lo_ref[...], half)
            xb = x_ref[:, sl]
            rstd = lax.rsqrt(_gsum(xb * xb, gm) * inv + EPS)
            xh = xb * rstd
            dg_ref[:, sl] += jnp.sum(d * xh, axis=0, keepdims=True)
            dxh = d * g_ref[:, sl]
            dx = rstd * (dxh - xh * (_gsum(dxh * xh, gm) * inv))
            dx_ref[:, sl] = dx.astype(BF16)

    g_spec, g_arr = _layer_arg(gain_row)
    in_specs = [_dil_spec(tr, width, dil) if dil > 1 else _rows_spec(tr, width), _rows_spec(tr, width, off), g_spec,
                _full_spec((LANES, LANES)), _full_spec((1, LANES))]
    args = [dy, x, g_arr, gmat, invn]
    if has_rope:
        in_specs += [_rows_spec(tr, LANES), _rows_spec(tr, LANES), _full_spec((1, LANES))]
        args += [rope[0], rope[1], rope[2]]
    dx_spec, dx_shape, aliases = _rows_spec(tr, width), jax.ShapeDtypeStruct((rows, width), BF16), {}
    if slab is not None:
        dx_spec = _rows_spec(tr, width, off)
        dx_shape, aliases, _ = _slab_out(slab, rows, in_specs, args, 0)
    return pl.pallas_call(
        body, grid=(rows // tr,), in_specs=in_specs,
        out_specs=(dx_spec, _full_spec((1, width))),
        out_shape=(dx_shape, jax.ShapeDtypeStruct((1, width), F32)), input_output_aliases=aliases,
        scratch_shapes=[pltpu.VMEM((nblk, tr, LANES), F32)] if dil > 1 else [],
        compiler_params=_cparams(("arbitrary",)), name=name)(*args)


def _merge_fwd(zv, b_gate, ys, *, name, tr=256):
    z, gw, off = zv
    rows, d = ys[0].shape
    tr = min(tr, rows)

    def body(z_ref, b_ref, y0, y1, y2, y3, t_ref):
        acc = None
        for n, y_ref in enumerate((y0, y1, y2, y3)):
            sl = slice(n * d, (n + 1) * d)
            term = jax.nn.sigmoid(z_ref[:, sl] + b_ref[:, sl]) * y_ref[...]
            acc = term if acc is None else acc + term
        t_ref[...] = acc.astype(BF16)

    b_spec, b_arr = _layer_arg(b_gate)
    return pl.pallas_call(
        body, grid=(rows // tr,),
        in_specs=[_rows_spec(tr, gw, off), b_spec] + [_rows_spec(tr, d)] * 4,
        out_specs=_rows_spec(tr, d), out_shape=jax.ShapeDtypeStruct((rows, d), BF16),
        compiler_params=_cparams(("parallel",)), name=name)(z, b_arr, *ys)


def _merge_bwd(dt, zv, b_gate, ys, *, slab_width, name, tr=256):
    z, gw, off = zv
    rows, d = dt.shape
    tr = min(tr, rows)

    def body(dt_ref, z_ref, b_ref, y0, y1, y2, y3, d0, d1, d2, d3, dg_ref, db_ref):
        i = pl.program_id(0)

        @pl.when(i == 0)
        def _():
            db_ref[...] = jnp.zeros_like(db_ref)

        dtv = dt_ref[...]
        for n, (y_ref, d_ref) in enumerate(((y0, d0), (y1, d1), (y2, d2), (y3, d3))):
            sl = slice(n * d, (n + 1) * d)
            gate = jax.nn.sigmoid(z_ref[:, sl] + b_ref[:, sl])
            d_ref[...] = (dtv * gate).astype(BF16)
            dpre = dtv * y_ref[...] * (gate * (1.0 - gate))
            dg_ref[:, sl] = dpre.astype(BF16)
            db_ref[:, sl] += jnp.sum(dpre, axis=0, keepdims=True)

    b_spec, b_arr = _layer_arg(b_gate)
    outs = pl.pallas_call(
        body, grid=(rows // tr,),
        in_specs=[_rows_spec(tr, d), _rows_spec(tr, gw, off), b_spec] + [_rows_spec(tr, d)] * 4,
        out_specs=tuple([_rows_spec(tr, d)] * 4 + [_rows_spec(tr, gw, off), _full_spec((1, gw))]),
        out_shape=tuple([jax.ShapeDtypeStruct((rows, d), BF16)] * 4
                        + [jax.ShapeDtypeStruct((rows, slab_width), BF16), jax.ShapeDtypeStruct((1, gw), F32)]),
        compiler_params=_cparams(("arbitrary",)), name=name)(dt, z, b_arr, *ys)
    return outs[:4], outs[4], outs[5]


def _group_weights(la, lb, lc):
    mx = jnp.maximum(jnp.maximum(la, lb), lc)
    ea, eb, ec = jnp.exp(la - mx), jnp.exp(lb - mx), jnp.exp(lc - mx)
    inv = 1.0 / (ea + eb + ec)
    return ea * inv, eb * inv, ec * inv


def _combine_fwd(os_, lses, dils, *, rows, width, name, tr=512):
    nblk = width // LANES
    ndil = sum(d > 1 for d in dils)

    def body(*refs):
        o_refs, l_refs, out_ref = refs[0:3], refs[3:6], refs[6]
        scr = list(refs[7:])
        o_tok, l_tok = [], []
        for g, d in enumerate(dils):
            if d > 1:
                so, sl_ = scr.pop(0), scr.pop(0)
                _undilate(o_refs[g], so, d, width)
                _undilate(l_refs[g], sl_, d, width)
                o_tok.append(lambda j, s=so: s[j])
                l_tok.append(lambda j, s=sl_: s[j])
            else:
                o_tok.append(lambda j, r=o_refs[g]: r[:, j * LANES:(j + 1) * LANES])
                l_tok.append(lambda j, r=l_refs[g]: r[:, j * LANES:(j + 1) * LANES])
        for j in range(nblk):
            wa, wb, wc = _group_weights(l_tok[0](j), l_tok[1](j), l_tok[2](j))
            out_ref[:, j * LANES:(j + 1) * LANES] = wa * o_tok[0](j) + wb * o_tok[1](j) + wc * o_tok[2](j)

    spec = lambda d: _dil_spec(tr, width, d) if d > 1 else _rows_spec(tr, width)
    return pl.pallas_call(
        body, grid=(rows // tr,), in_specs=[spec(d) for d in dils] * 2, out_specs=_rows_spec(tr, width),
        out_shape=jax.ShapeDtypeStruct((rows, width), F32),
        scratch_shapes=[pltpu.VMEM((nblk, tr, LANES), F32)] * (2 * ndil),
        compiler_params=_cparams(("parallel",)), name=name)(*os_, *lses)


def _combine_bwd(do, o, lses, dils, pat, *, name, tr=512):
    rows, width = do.shape
    gmat, _ = pat
    nblk = width // LANES
    ndil = sum(d > 1 for d in dils)

    def body(*refs):
        do_ref, o_ref = refs[0:2]
        l_refs, gm_ref = refs[2:5], refs[5]
        d_refs, e_refs = refs[6:9], refs[9:12]
        scr = list(refs[12:])
        gm = gm_ref[...]
        l_tok, d_scr, e_scr = [], [], []
        for g, d in enumerate(dils):
            if d > 1:
                sl_, sd, se = scr.pop(0), scr.pop(0), scr.pop(0)
                _undilate(l_refs[g], sl_, d, width)
                l_tok.append(lambda j, s=sl_: s[j])
                d_scr.append(sd), e_scr.append(se)
            else:
                l_tok.append(lambda j, r=l_refs[g]: r[:, j * LANES:(j + 1) * LANES])
                d_scr.append(None), e_scr.append(None)
        for j in range(nblk):
            sl = slice(j * LANES, (j + 1) * LANES)
            ws = _group_weights(l_tok[0](j), l_tok[1](j), l_tok[2](j))
            dov = do_ref[:, sl]
            tot = _gsum(dov * o_ref[:, sl], gm)
            for g, wgt in enumerate(ws):
                if dils[g] > 1:
                    d_scr[g][j] = wgt * dov
                    e_scr[g][j] = wgt * tot
                else:
                    d_refs[g][:, sl] = wgt * dov
                    e_refs[g][:, sl] = wgt * tot
        for g, d in enumerate(dils):
            if d > 1:
                _dilate(d_scr[g], d_refs[g], d, width)
                _dilate(e_scr[g], e_refs[g], d, width)

    spec = lambda d: _dil_spec(tr, width, d) if d > 1 else _rows_spec(tr, width)
    sds = lambda d: jax.ShapeDtypeStruct((rows // d, d * width), F32)
    outs = pl.pallas_call(
        body, grid=(rows // tr,),
        in_specs=[_rows_spec(tr, width)] * 2 + [spec(d) for d in dils] + [_full_spec((LANES, LANES))],
        out_specs=tuple([spec(d) for d in dils] * 2),
        out_shape=tuple([sds(d) for d in dils] * 2),
        scratch_shapes=[pltpu.VMEM((nblk, tr, LANES), F32)] * (3 * ndil),
        compiler_params=_cparams(("parallel",)), name=name)(do, o, *lses, gmat)
    return outs[:3], outs[3:]


def _delta(do, o, pat, *, lse=None, sink_row=None, name, tr=512):
    rows, width = do.shape
    gmat, _ = pat
    nblk = width // LANES
    has_sink = sink_row is not None

    def body(*refs):
        do_ref, o_ref, gm_ref = refs[:3]
        if has_sink:
            l_ref, s_ref, dl_ref, ds_ref = refs[3:]
        else:
            dl_ref = refs[3]
        gm = gm_ref[...]
        if has_sink:
            @pl.when(pl.program_id(0) == 0)
            def _():
                ds_ref[...] = jnp.zeros_like(ds_ref)
        for j in range(nblk):
            sl = slice(j * LANES, (j + 1) * LANES)
            dl = _gsum(do_ref[:, sl] * o_ref[:, sl], gm)
            dl_ref[:, sl] = dl
            if has_sink:
                ds_ref[:, sl] += jnp.sum(-jnp.exp(s_ref[:, sl] - l_ref[:, sl]) * dl, axis=0, keepdims=True)

    in_specs = [_rows_spec(tr, width), _rows_spec(tr, width), _full_spec((LANES, LANES))]
    args = [do, o, gmat]
    out_specs = [_rows_spec(tr, width)]
    out_shape = [jax.ShapeDtypeStruct((rows, width), F32)]
    if has_sink:
        s_spec, s_arr = _layer_arg(sink_row)
        in_specs += [_rows_spec(tr, width), s_spec]
        args += [lse, s_arr]
        out_specs.append(_full_spec((1, width)))
        out_shape.append(jax.ShapeDtypeStruct((1, width), F32))
    outs = pl.pallas_call(
        body, grid=(rows // tr,), in_specs=in_specs, out_specs=tuple(out_specs), out_shape=tuple(out_shape),
        compiler_params=_cparams(("arbitrary",)), name=name)(*args)
    return outs if has_sink else (outs[0], None)


def _kp_reduce(dk, *, name, tr=512):
    rows, width = dk.shape
    nblk = width // LANES

    def body(dk_ref, o_ref):
        acc = dk_ref[:, 0:LANES]
        for j in range(1, nblk):
            acc = acc + dk_ref[:, j * LANES:(j + 1) * LANES]
        lane = lax.broadcasted_iota(jnp.int32, acc.shape, 1)
        o_ref[...] = jnp.where(lane < B_ROPE, pltpu.roll(acc, 64, 1), 0.0)

    return pl.pallas_call(
        body, grid=(rows // tr,), in_specs=[_rows_spec(tr, width)], out_specs=_rows_spec(tr, LANES),
        out_shape=jax.ShapeDtypeStruct((rows, LANES), F32),
        compiler_params=_cparams(("parallel",)), name=name)(dk)


def _loss_head(y, target, *, name, tr=512):
    rows, d = y.shape

    def body(y_ref, t_ref, dy_ref, l_ref):
        @pl.when(pl.program_id(0) == 0)
        def _():
            l_ref[...] = jnp.zeros_like(l_ref)

        diff = y_ref[...] - t_ref[...]
        dy_ref[...] = diff * (1.0 / d)
        part = 0.5 * jnp.sum(jnp.mean(diff * diff, axis=-1, keepdims=True), axis=0, keepdims=True)
        l_ref[...] += jnp.broadcast_to(part, l_ref.shape)

    return pl.pallas_call(
        body, grid=(rows // tr,), in_specs=[_rows_spec(tr, d)] * 2,
        out_specs=(_rows_spec(tr, d), _full_spec((1, LANES))),
        out_shape=(jax.ShapeDtypeStruct((rows, d), F32), jax.ShapeDtypeStruct((1, LANES), F32)),
        compiler_params=_cparams(("arbitrary",)), name=name)(y, target)


def _adam(recvs, w, m, v, *, name):
    nl = len(recvs)
    rows_l, cols = recvs[0].shape[1:]
    assert w.shape == (nl * rows_l, cols)
    tr = _pick(rows_l, (128, 64, 32, 16, 8))
    nt = rows_l // tr
    c1 = 1.0 / (1.0 - ADAM_B1 ** ADAM_STEP)
    c2 = 1.0 / (1.0 - ADAM_B2 ** ADAM_STEP)

    def body(*refs):
        r_refs = refs[:nl]
        w_ref, m_ref, v_ref, g_out, d_out, m_out, v_out = refs[nl:]
        layer = pl.program_id(0)

        def update(r_ref):
            g = r_ref[0].astype(F32)
            for j in range(1, N_DEV):
                g = g + r_ref[j].astype(F32)
            mn = ADAM_B1 * m_ref[...] + (1.0 - ADAM_B1) * g
            vn = ADAM_B2 * v_ref[...] + (1.0 - ADAM_B2) * (g * g)
            g_out[...] = g
            m_out[...] = mn
            v_out[...] = vn
            d_out[...] = -ADAM_LR * ((mn * c1) / (jnp.sqrt(vn * c2) + ADAM_EPS) + ADAM_WD * w_ref[...])

        for k in range(nl):
            pl.when(layer == k)(functools.partial(update, r_refs[k]))

    spec = pl.BlockSpec((tr, cols), lambda l, i: (l * nt + i, 0))
    rspec = lambda k: pl.BlockSpec((N_DEV, tr, cols), lambda l, i: (0, jnp.where(l == k, i, 0), 0))
    sds = jax.ShapeDtypeStruct(w.shape, F32)
    return pl.pallas_call(
        body, grid=(nl, nt), in_specs=[rspec(k) for k in range(nl)] + [spec, spec, spec],
        out_specs=(spec,) * 4, out_shape=(sds,) * 4,
        compiler_params=_cparams(("arbitrary", "arbitrary")), name=name)(*recvs, w, m, v)


def _band_mask(i, maxdist, nrow_blocks):
    row = lax.broadcasted_iota(jnp.int32, (BLOCK, 2 * BLOCK), 0)
    col = lax.broadcasted_iota(jnp.int32, (BLOCK, 2 * BLOCK), 1)
    dist = row - col + BLOCK
    return (dist >= 0) & (dist <= maxdist) & ((i > 0) | (col >= BLOCK))


def _band_views(q, k, zv, dil):
    z, kvw, voff = zv
    if dil == 1:
        return q, k, z, z.shape[1] // kvw, voff
    vd = z[:, voff * kvw:(voff + 1) * kvw].reshape(z.shape[0] // dil, dil * kvw)
    return q, k, vd, 1, 0


def _band_fwd(q, k, zv, *, dil, group, maxdist, sinks=None, name, xchg=()):
    s, qw = q.shape[0] * dil, q.shape[1] // dil
    kvw = zv[1]
    qd, kd, zd, zblocks, voff = _band_views(q, k, zv, dil)
    nb = s // dil // BLOCK
    nheads = qw // 64
    scale = 64 ** -0.5
    has_sink = sinks is not None

    def body(*refs):
        q_ref, kp_ref, kc_ref, vp_ref, vc_ref = refs[:5]
        s_ref = refs[5] if has_sink else None
        o_ref, l_ref, sc_s, p_s = refs[5 + has_sink:]
        i = pl.program_id(1)
        mask = _band_mask(i, maxdist, nb)
        k2 = jnp.concatenate([kp_ref[...], kc_ref[...]], axis=0)
        v2 = jnp.concatenate([vp_ref[...], vc_ref[...]], axis=0).astype(BF16)
        for h in range(nheads):
            g = h // group
            sc_s[h] = _dot(q_ref[:, 64 * h:64 * h + 64], k2[:, 64 * g:64 * g + 64], NT_DIMS)
        sc = jnp.where(mask[None], sc_s[...] * scale, NEG)
        mx = jnp.max(sc, axis=2, keepdims=True)
        if has_sink:
            snk = s_ref[:, :, 0:1]
            mx = jnp.maximum(mx, snk)
        e = jnp.exp(sc - mx)
        den = jnp.sum(e, axis=2, keepdims=True)
        if has_sink:
            den = den + jnp.exp(snk - mx)
        p_s[...] = (e * (1.0 / den)).astype(BF16)
        lse = mx + jnp.log(den)
        for h in range(nheads):
            g = h // group
            o_ref[:, 64 * h:64 * h + 64] = _dot(p_s[h], v2[:, 64 * g:64 * g + 64])
            l_ref[:, 64 * h:64 * h + 64] = jnp.broadcast_to(lse[h], (BLOCK, 64))

    prev = lambda r, i: jnp.maximum(i - 1, 0)
    in_specs = [
        pl.BlockSpec((BLOCK, qw), lambda r, i: (i, r)),
        pl.BlockSpec((BLOCK, kvw), lambda r, i: (prev(r, i), r)),
        pl.BlockSpec((BLOCK, kvw), lambda r, i: (i, r)),
        pl.BlockSpec((BLOCK, kvw), lambda r, i: (prev(r, i), r * zblocks + voff)),
        pl.BlockSpec((BLOCK, kvw), lambda r, i: (i, r * zblocks + voff)),
    ]
    args = [qd, kd, kd, zd, zd]
    if has_sink:
        s_spec, s_arr = _layer_arg(sinks)
        in_specs.append(s_spec)
        args.append(s_arr)
    ospec = pl.BlockSpec((BLOCK, qw), lambda r, i: (i, r))
    sds = jax.ShapeDtypeStruct((s // dil, dil * qw), F32)
    (o, lse), got = _pcall(
        body, grid=(dil, nb), in_specs=in_specs, out_specs=(ospec, ospec), out_shape=(sds, sds),
        scratch_shapes=[pltpu.VMEM((nheads, BLOCK, 2 * BLOCK), F32), pltpu.VMEM((nheads, BLOCK, 2 * BLOCK), BF16)],
        sem=("parallel", "parallel"), name=name, args=args, xchg=xchg)
    return (o, lse), got


def _band_dq(q, k, zv, do, lse, delta, *, dil, group, maxdist, name):
    s, qw = q.shape[0] * dil, q.shape[1] // dil
    kvw = zv[1]
    qd, kd, zd, zblocks, voff = _band_views(q, k, zv, dil)
    nb = s // dil // BLOCK
    nheads = qw // 64
    scale = 64 ** -0.5

    def body(q_ref, kp_ref, kc_ref, vp_ref, vc_ref, do_ref, l_ref, d_ref, dq_ref, sc_s, dp_s, ds_s, st_s):
        i = pl.program_id(1)
        mask = _band_mask(i, maxdist, nb)
        k2 = jnp.concatenate([kp_ref[...], kc_ref[...]], axis=0)
        v2 = jnp.concatenate([vp_ref[...], vc_ref[...]], axis=0).astype(BF16)
        dob = do_ref[...].astype(BF16)
        for h in range(nheads):
            g = h // group
            hs = slice(64 * h, 64 * h + 64)
            sc_s[h] = _dot(q_ref[:, hs], k2[:, 64 * g:64 * g + 64], NT_DIMS)
            dp_s[h] = _dot(dob[:, hs], v2[:, 64 * g:64 * g + 64], NT_DIMS)
            st_s[0, h] = l_ref[:, 64 * h:64 * h + 1]
            st_s[1, h] = d_ref[:, 64 * h:64 * h + 1]
        p = jnp.exp(jnp.where(mask[None], sc_s[...] * scale, NEG) - st_s[0])
        ds_s[...] = (p * (dp_s[...] - st_s[1]) * scale).astype(BF16)
        for h in range(nheads):
            g = h // group
            dq_ref[:, 64 * h:64 * h + 64] = _dot(ds_s[h], k2[:, 64 * g:64 * g + 64])

    prev = lambda r, i: jnp.maximum(i - 1, 0)
    qspec = pl.BlockSpec((BLOCK, qw), lambda r, i: (i, r))
    in_specs = [
        qspec,
        pl.BlockSpec((BLOCK, kvw), lambda r, i: (prev(r, i), r)),
        pl.BlockSpec((BLOCK, kvw), lambda r, i: (i, r)),
        pl.BlockSpec((BLOCK, kvw), lambda r, i: (prev(r, i), r * zblocks + voff)),
        pl.BlockSpec((BLOCK, kvw), lambda r, i: (i, r * zblocks + voff)),
        qspec, qspec, qspec,
    ]
    dq = pl.pallas_call(
        body, grid=(dil, nb), in_specs=in_specs, out_specs=qspec,
        out_shape=jax.ShapeDtypeStruct((s // dil, dil * qw), F32),
        scratch_shapes=[pltpu.VMEM((nheads, BLOCK, 2 * BLOCK), F32), pltpu.VMEM((nheads, BLOCK, 2 * BLOCK), F32),
                        pltpu.VMEM((nheads, BLOCK, 2 * BLOCK), BF16), pltpu.VMEM((2, nheads, BLOCK, 1), F32)],
        compiler_params=_cparams(("parallel", "parallel")), name=name)(qd, kd, kd, zd, zd, do, lse, delta)
    return dq


def _band_dkv(q, k, zv, do, lse, delta, *, dil, group, maxdist, slab=None, name):
    s, qw = q.shape[0] * dil, q.shape[1] // dil
    kvw = zv[1]
    qd, kd, zd, zblocks, voff = _band_views(q, k, zv, dil)
    nb = s // dil // BLOCK
    nheads = qw // 64
    nkv = kvw // 64
    scale = 64 ** -0.5

    def body(k_ref, v_ref, q0, q1, do0, do1, l0, l1, d0, d1, dk_ref, dv_ref, sc_s, dp_s, p_s, ds_s, st_s):
        j = pl.program_id(1)
        row = lax.broadcasted_iota(jnp.int32, (2 * BLOCK, BLOCK), 0)
        col = lax.broadcasted_iota(jnp.int32, (2 * BLOCK, BLOCK), 1)
        dist = row - col
        mask = (dist >= 0) & (dist <= maxdist) & ((j < nb - 1) | (row < BLOCK))
        q2 = jnp.concatenate([q0[...], q1[...]], axis=0)
        do2 = jnp.concatenate([do0[...], do1[...]], axis=0).astype(BF16)
        vb = v_ref[...].astype(BF16)
        for h in range(nheads):
            g = h // group
            hs = slice(64 * h, 64 * h + 64)
            sc_s[h] = _dot(q2[:, hs], k_ref[:, 64 * g:64 * g + 64], NT_DIMS)
            dp_s[h] = _dot(do2[:, hs], vb[:, 64 * g:64 * g + 64], NT_DIMS)
            st_s[0, h, 0:BLOCK] = l0[:, 64 * h:64 * h + 1]
            st_s[0, h, BLOCK:2 * BLOCK] = l1[:, 64 * h:64 * h + 1]
            st_s[1, h, 0:BLOCK] = d0[:, 64 * h:64 * h + 1]
            st_s[1, h, BLOCK:2 * BLOCK] = d1[:, 64 * h:64 * h + 1]
        p = jnp.exp(jnp.where(mask[None], sc_s[...] * scale, NEG) - st_s[0])
        p_s[...] = p.astype(BF16)
        ds_s[...] = (p * (dp_s[...] - st_s[1]) * scale).astype(BF16)
        dks = [None] * nkv
        dvs = [None] * nkv
        for h in range(nheads):
            g = h // group
            hs = slice(64 * h, 64 * h + 64)
            dvh = _dot(p_s[h], do2[:, hs], TN_DIMS)
            dkh = _dot(ds_s[h], q2[:, hs], TN_DIMS)
            dks[g] = dkh if dks[g] is None else dks[g] + dkh
            dvs[g] = dvh if dvs[g] is None else dvs[g] + dvh
        for g in range(nkv):
            dk_ref[:, 64 * g:64 * g + 64] = dks[g]
            dv_ref[:, 64 * g:64 * g + 64] = dvs[g].astype(BF16)

    nxt = lambda r, j: jnp.minimum(j + 1, nb - 1)
    q0s = pl.BlockSpec((BLOCK, qw), lambda r, j: (j, r))
    q1s = pl.BlockSpec((BLOCK, qw), lambda r, j: (nxt(r, j), r))
    kspec = pl.BlockSpec((BLOCK, kvw), lambda r, j: (j, r))
    in_specs = [kspec, pl.BlockSpec((BLOCK, kvw), lambda r, j: (j, r * zblocks + voff)),
                q0s, q1s, q0s, q1s, q0s, q1s, q0s, q1s]
    args = [kd, zd, qd, qd, do, do, lse, lse, delta, delta]
    n_in = len(args)
    dv_spec, dv_shape, aliases, has_prev = kspec, jax.ShapeDtypeStruct((s // dil, dil * kvw), BF16), {}, 0
    if slab is not None:
        assert dil == 1
        dv_spec = in_specs[1]
        dv_shape, aliases, has_prev = _slab_out(slab, s, in_specs, args, 1)

    def body_skip_slab(*refs):
        return body(*refs[:n_in], *refs[n_in + has_prev:])

    dk, dv = pl.pallas_call(
        body_skip_slab, grid=(dil, nb), in_specs=in_specs, out_specs=(kspec, dv_spec),
        out_shape=(jax.ShapeDtypeStruct((s // dil, dil * kvw), F32), dv_shape), input_output_aliases=aliases,
        scratch_shapes=[pltpu.VMEM((nheads, 2 * BLOCK, BLOCK), F32), pltpu.VMEM((nheads, 2 * BLOCK, BLOCK), F32),
                        pltpu.VMEM((nheads, 2 * BLOCK, BLOCK), BF16), pltpu.VMEM((nheads, 2 * BLOCK, BLOCK), BF16),
                        pltpu.VMEM((2, nheads, 2 * BLOCK, 1), F32)],
        compiler_params=_cparams(("parallel", "parallel")), name=name)(*args)
    return dk, (dv if slab is not None else dv.reshape(s, kvw))


HPG = 2


def _flash_tiles(sq, sk):
    return _pick(sq, (512, 256, 128)), _pick(sk, (512, 256, 128))


def _diag_mask(t):
    return lax.broadcasted_iota(jnp.int32, (t, t), 1) <= lax.broadcasted_iota(jnp.int32, (t, t), 0)


def _attn_fwd(q, k, vv, *, dv, causal, scale, name, xchg=None):
    sq, qw = q.shape
    sk = k.shape[0]
    varr, vbw, voff = vv
    assert vbw == HPG * dv
    ngrp = qw // (HPG * LANES)
    tq, tk = _flash_tiles(sq, sk)
    nq, nk = sq // tq, sk // tk
    if causal:
        assert tq == tk and sq == sk

    def body(q_ref, k_ref, v_ref, o_ref, l_ref, m_s, l_s, acc_s):
        i = pl.program_id(1)
        m_s[...] = jnp.full_like(m_s, NEG)
        l_s[...] = jnp.zeros_like(l_s)
        acc_s[...] = jnp.zeros_like(acc_s)

        def step(j, masked):
            rows = pl.ds(pl.multiple_of(j * tk, tk), tk)
            if masked:
                mask = _diag_mask(tq)
            for h in range(HPG):
                hs = slice(LANES * h, LANES * (h + 1))
                sc = _dot(q_ref[:, hs], k_ref[rows, hs], NT_DIMS) * scale
                if masked:
                    sc = jnp.where(mask, sc, NEG)
                m_prev = m_s[h]
                m_new = jnp.maximum(m_prev, jnp.max(sc, axis=1, keepdims=True))
                alpha = jnp.exp(m_prev - m_new)
                p = jnp.exp(sc - m_new[:, 0:1])
                l_s[h] = alpha * l_s[h] + jnp.sum(p, axis=1, keepdims=True)
                acc_s[h] = acc_s[h] * alpha[:, 0:dv] + _dot(p, v_ref[rows, dv * h:dv * (h + 1)])
                m_s[h] = m_new

        def loop_body(j, carry):
            step(j, False)
            return carry

        if causal:
            lax.fori_loop(0, i, loop_body, 0)
            step(i, True)
        else:
            lax.fori_loop(0, nk, loop_body, 0)
        for h in range(HPG):
            o_ref[:, dv * h:dv * (h + 1)] = acc_s[h] * (1.0 / l_s[h][:, 0:dv])
            l_ref[:, LANES * h:LANES * (h + 1)] = m_s[h] + jnp.log(l_s[h])

    in_specs = [
        pl.BlockSpec((tq, HPG * LANES), lambda g, i: (i, g)),
        pl.BlockSpec((sk, HPG * LANES), lambda g, i: (0, g)),
        pl.BlockSpec((sk, vbw), lambda g, i: (0, voff + g)),
    ]
    return _pcall(
        body, grid=(ngrp, nq), in_specs=in_specs,
        out_specs=(pl.BlockSpec((tq, HPG * dv), lambda g, i: (i, g)),
                   pl.BlockSpec((tq, HPG * LANES), lambda g, i: (i, g))),
        out_shape=(jax.ShapeDtypeStruct((sq, ngrp * HPG * dv), F32), jax.ShapeDtypeStruct((sq, qw), F32)),
        scratch_shapes=[pltpu.VMEM((HPG, tq, LANES), F32), pltpu.VMEM((HPG, tq, LANES), F32),
                        pltpu.VMEM((HPG, tq, dv), F32)],
        sem=("parallel", "arbitrary"), name=name, args=(q, k, varr), xchg=xchg)


def _attn_dq(q, k, vv, do, lse, delta, *, dv, causal, scale, name, xchg=None):
    sq, qw = q.shape
    sk = k.shape[0]
    varr, vbw, voff = vv
    ngrp = qw // (HPG * LANES)
    tq, tk = _flash_tiles(sq, sk)
    nq, nk = sq // tq, sk // tk

    def body(q_ref, k_ref, v_ref, do_ref, l_ref, d_ref, dq_ref, acc_s):
        i = pl.program_id(1)
        acc_s[...] = jnp.zeros_like(acc_s)

        def step(j, masked):
            rows = pl.ds(pl.multiple_of(j * tk, tk), tk)
            if masked:
                mask = _diag_mask(tq)
            for h in range(HPG):
                hs = slice(LANES * h, LANES * (h + 1))
                kh = k_ref[rows, hs]
                sc = _dot(q_ref[:, hs], kh, NT_DIMS) * scale
                if masked:
                    sc = jnp.where(mask, sc, NEG)
                p = jnp.exp(sc - l_ref[:, LANES * h:LANES * h + 1])
                dp = _dot(do_ref[:, dv * h:dv * (h + 1)], v_ref[rows, dv * h:dv * (h + 1)], NT_DIMS)
                ds = p * (dp - d_ref[:, dv * h:dv * h + 1]) * scale
                acc_s[h] += _dot(ds, kh)

        def loop_body(j, carry):
            step(j, False)
            return carry

        if causal:
            lax.fori_loop(0, i, loop_body, 0)
            step(i, True)
        else:
            lax.fori_loop(0, nk, loop_body, 0)
        for h in range(HPG):
            dq_ref[:, LANES * h:LANES * (h + 1)] = acc_s[h]

    qspec = pl.BlockSpec((tq, HPG * LANES), lambda g, i: (i, g))
    ospec = pl.BlockSpec((tq, HPG * dv), lambda g, i: (i, g))
    in_specs = [
        qspec,
        pl.BlockSpec((sk, HPG * LANES), lambda g, i: (0, g)),
        pl.BlockSpec((sk, vbw), lambda g, i: (0, voff + g)),
        ospec, qspec, ospec,
    ]
    return _pcall(
        body, grid=(ngrp, nq), in_specs=in_specs, out_specs=qspec,
        out_shape=jax.ShapeDtypeStruct((sq, qw), F32),
        scratch_shapes=[pltpu.VMEM((HPG, tq, LANES), F32)],
        sem=("parallel", "arbitrary"), name=name, args=(q, k, varr, do, lse, delta), xchg=xchg)


def _attn_dkv(q, k, vv, do, lse, delta, *, dv, causal, scale, name, xchg=None):
    sq, qw = q.shape
    sk = k.shape[0]
    varr, vbw, voff = vv
    ngrp = qw // (HPG * LANES)
    tq, tk = _flash_tiles(sq, sk)
    nq, nk = sq // tq, sk // tk

    def body(q_ref, k_ref, v_ref, do_ref, l_ref, d_ref, dk_ref, dv_ref, dk_s, dv_s):
        j = pl.program_id(1)
        dk_s[...] = jnp.zeros_like(dk_s)
        dv_s[...] = jnp.zeros_like(dv_s)

        def step(i, masked):
            rows = pl.ds(pl.multiple_of(i * tq, tq), tq)
            if masked:
                mask = _diag_mask(tq)
            for h in range(HPG):
                qh = q_ref[rows, LANES * h:LANES * (h + 1)]
                doh = do_ref[rows, dv * h:dv * (h + 1)]
                sc = _dot(qh, k_ref[:, LANES * h:LANES * (h + 1)], NT_DIMS) * scale
                if masked:
                    sc = jnp.where(mask, sc, NEG)
                p = jnp.exp(sc - l_ref[rows, LANES * h:LANES * h + 1])
                dv_s[h] += _dot(p, doh, TN_DIMS)
                dp = _dot(doh, v_ref[:, dv * h:dv * (h + 1)], NT_DIMS)
                ds = p * (dp - d_ref[rows, dv * h:dv * h + 1]) * scale
                dk_s[h] += _dot(ds, qh, TN_DIMS)

        def loop_body(i, carry):
            step(i, False)
            return carry

        if causal:
            step(j, True)
            lax.fori_loop(j + 1, nq, loop_body, 0)
        else:
            lax.fori_loop(0, nq, loop_body, 0)
        for h in range(HPG):
            dk_ref[:, LANES * h:LANES * (h + 1)] = dk_s[h]
            dv_ref[:, dv * h:dv * (h + 1)] = dv_s[h].astype(BF16)

    qspec = pl.BlockSpec((sq, HPG * LANES), lambda g, j: (0, g))
    ospec = pl.BlockSpec((sq, HPG * dv), lambda g, j: (0, g))
    kspec = pl.BlockSpec((tk, HPG * LANES), lambda g, j: (j, g))
    in_specs = [qspec, kspec, pl.BlockSpec((tk, vbw), lambda g, j: (j, voff + g)), ospec, qspec, ospec]
    return _pcall(
        body, grid=(ngrp, nk), in_specs=in_specs,
        out_specs=(kspec, pl.BlockSpec((tk, HPG * dv), lambda g, j: (j, g))),
        out_shape=(jax.ShapeDtypeStruct((sk, qw), F32), jax.ShapeDtypeStruct((sk, ngrp * HPG * dv), BF16)),
        scratch_shapes=[pltpu.VMEM((HPG, tk, LANES), F32), pltpu.VMEM((HPG, tk, dv), F32)],
        sem=("parallel", "arbitrary"), name=name, args=(q, k, varr, do, lse, delta), xchg=xchg)


def _xchg_copies(src_ref, out_ref, send_sems, recv_sems, local_sems, k, scatter):
    me = 4 * lax.axis_index("x") + 2 * lax.axis_index("y") + lax.axis_index("c")
    own = src_ref.at[me] if scatter else src_ref
    copies = [pltpu.make_async_copy(own, out_ref.at[me], local_sems.at[k])]
    for d in range(1, N_DEV):
        peer = (me + d) % N_DEV
        piece = src_ref.at[peer] if scatter else src_ref
        sem = (N_DEV - 1) * k + d - 1
        copies.append(pltpu.make_async_remote_copy(
            src_ref=piece, dst_ref=out_ref.at[me], send_sem=send_sems.at[sem], recv_sem=recv_sems.at[sem],
            device_id=(peer // 4, (peer // 2) % 2, peer % 2), device_id_type=pl.DeviceIdType.MESH))
    return copies


def _xchg_shapes(xchg):
    return [jax.ShapeDtypeStruct((N_DEV,) + tuple(src.shape[1:] if scatter else src.shape), src.dtype)
            for src, scatter in xchg]


def _xchg_sems(nx):
    return [pltpu.SemaphoreType.DMA(((N_DEV - 1) * nx,)), pltpu.SemaphoreType.DMA(((N_DEV - 1) * nx,)),
            pltpu.SemaphoreType.DMA((nx,))]


ANY_SPEC = pl.BlockSpec(memory_space=pl.ANY)


def _exchange_many(xchg, *, name):
    nx = len(xchg)

    def body(*refs):
        srcs, outs = refs[:nx], refs[nx:2 * nx]
        send_sems, recv_sems, local_sems = refs[2 * nx:]
        copies = [c for k, (_, scatter) in enumerate(xchg)
                  for c in _xchg_copies(srcs[k], outs[k], send_sems, recv_sems, local_sems, k, scatter)]
        for c in copies:
            c.start()
        for c in copies:
            c.wait()

    return pl.pallas_call(
        body, in_specs=[ANY_SPEC] * nx, out_specs=tuple([ANY_SPEC] * nx), out_shape=tuple(_xchg_shapes(xchg)),
        scratch_shapes=_xchg_sems(nx), name=name)(*[src for src, _ in xchg])


def _pcall(body, *, grid, in_specs, out_specs, out_shape, scratch_shapes=(), sem, name, args, xchg=None):
    single = not isinstance(out_shape, (tuple, list))
    if not xchg:
        res = pl.pallas_call(
            body, grid=grid, in_specs=list(in_specs), out_specs=out_specs, out_shape=out_shape,
            scratch_shapes=list(scratch_shapes), compiler_params=_cparams(sem), name=name)(*args)
        return res if xchg is None else (res, [])
    out_specs_l = [out_specs] if single else list(out_specs)
    out_shape_l = [out_shape] if single else list(out_shape)
    nx, n_in, n_out, n_scr = len(xchg), len(in_specs), len(out_shape_l), len(scratch_shapes)

    def wrapped(*refs):
        ins, xin = refs[:n_in], refs[n_in:n_in + nx]
        pos = n_in + nx
        outs, xout = refs[pos:pos + n_out], refs[pos + n_out:pos + n_out + nx]
        pos += n_out + nx
        scr = refs[pos:pos + n_scr]
        send_sems, recv_sems, local_sems = refs[pos + n_scr:]
        first = functools.reduce(jnp.logical_and, [pl.program_id(a) == 0 for a in range(len(grid))])
        last = functools.reduce(jnp.logical_and, [pl.program_id(a) == grid[a] - 1 for a in range(len(grid))])

        def copies():
            return [c for k, (_, scatter) in enumerate(xchg)
                    for c in _xchg_copies(xin[k], xout[k], send_sems, recv_sems, local_sems, k, scatter)]

        @pl.when(first)
        def _():
            for c in copies():
                c.start()

        body(*ins, *outs, *scr)

        @pl.when(last)
        def _():
            for c in copies():
                c.wait()

    res = pl.pallas_call(
        wrapped, grid=grid, in_specs=list(in_specs) + [ANY_SPEC] * nx,
        out_specs=tuple(out_specs_l + [ANY_SPEC] * nx), out_shape=tuple(out_shape_l + _xchg_shapes(xchg)),
        scratch_shapes=list(scratch_shapes) + _xchg_sems(nx),
        compiler_params=_cparams(("arbitrary",) * len(grid)), name=name + "_x")(*args, *[src for src, _ in xchg])
    main = res[0] if single else tuple(res[:n_out])
    return main, list(res[n_out:])


def _unshard(wg, axis):
    nd = wg.ndim - 1
    perm = list(range(1, axis + 1)) + [0] + list(range(axis + 1, nd + 1))
    t = wg.transpose(perm)
    shp = list(wg.shape[1:])
    shp[axis] *= N_DEV
    return t.reshape(shp)


def _shard_major(full, axis):
    shp = list(full.shape)
    shp[axis:axis + 1] = [N_DEV, shp[axis] // N_DEV]
    t = full.reshape(shp)
    perm = [axis] + list(range(axis)) + list(range(axis + 1, len(shp)))
    return t.transpose(perm)


BIG = ("w_in", "b_w_uq", "b_w_ukv", "m_w_kv", "w_branch", "w_out", "w_up", "w_down")
BIG_AXIS = dict(w_in=2, b_w_uq=2, b_w_ukv=2, m_w_kv=1, w_branch=3, w_out=1, w_up=2, w_down=1)
SMALL = ("g_mix", "b_gate", "a_qn", "a_kn", "a_sink", "b_qa_norm", "b_kva_norm", "b_qn", "b_kn", "c_qn", "c_kn",
         "m_g_mem", "m_qn", "m_kn", "g_mlp")
WEIGHTS = ("g_mix", "w_in", "b_gate", "a_qn", "a_kn", "a_sink", "b_qa_norm", "b_kva_norm", "b_w_uq", "b_w_ukv", "b_qn",
           "b_kn", "c_qn", "c_kn", "m_g_mem", "m_w_kv", "m_qn", "m_kn", "w_branch", "w_out", "g_mlp", "w_up", "w_down")


def _rope_tables(positions, dim):
    inv = ROPE_THETA ** (-jnp.arange(0, dim, 2, dtype=F32) / dim)
    ang = positions.astype(F32)[:, None] * inv
    return jnp.cos(ang), jnp.sin(ang)


def _lane_mask(lo, hi):
    lane = np.arange(LANES)
    return jnp.asarray(((lane >= lo) & (lane < hi)).astype(np.float32)[None, :])


def _step(x, mem, positions, w, moments_m, moments_v, loss_target):
    s_len, d_model = x.shape
    n_layers = w["g_mix"].shape[0]
    lay, orig_names, order, n_in, nw = _in_layout(d_model)
    unit = lambda name: lay[name]["off"] // lay[name]["nw"]

    def ag_src(l, names):
        return [(w[n][l].astype(BF16), False) for n in names]

    def layout(name, block):
        full = _unshard(block, BIG_AXIS[name] - 1)
        if name == "w_in":
            return dict(w_in=_permute_cols(full, lay, order))
        if name == "b_w_uq":
            uq = full.reshape(B_Q_LORA, B_HEADS, B_NOPE + B_ROPE)
            return dict(uq=jnp.pad(uq, ((0, 0), (0, 0), (0, LANES - B_NOPE - B_ROPE))).reshape(B_Q_LORA, B_HEADS * LANES))
        if name == "b_w_ukv":
            ukv = full.reshape(B_KV_LORA, B_HEADS, B_NOPE + B_V)
            return dict(ukv=jnp.concatenate([
                jnp.pad(ukv[..., :B_NOPE], ((0, 0), (0, 0), (0, LANES - B_NOPE))).reshape(B_KV_LORA, B_HEADS * LANES),
                ukv[..., B_NOPE:].reshape(B_KV_LORA, B_HEADS * B_V)], axis=-1))
        return {name: full}

    def rs_src(name, g):
        return (_shard_major(g, BIG_AXIS[name] - 1).astype(BF16), True)

    p64 = _pattern([64, 64])
    p128 = _pattern([128])
    pq = _pattern([64, 32, 32])
    pkr = _pattern([32, 32, 32, 32])
    cos_h, sin_h = _rope_tables(positions, A_HD)
    cos_r, sin_r = _rope_tables(positions, B_ROPE)
    ones = lambda n: jnp.ones((s_len, n), F32)
    zeros = lambda n: jnp.zeros((s_len, n), F32)
    rope64 = (jnp.tile(cos_h, (1, 4)), jnp.tile(jnp.concatenate([-sin_h, sin_h], axis=1), (1, 2)),
              jnp.asarray((np.arange(LANES) % 64 < 32).astype(np.float32)[None, :]), 32)
    rope_q = (jnp.concatenate([ones(64), cos_r, cos_r, ones(32)], axis=1),
              jnp.concatenate([zeros(64), -sin_r, sin_r, zeros(32)], axis=1), _lane_mask(64, 80), 16)
    rope_k = (jnp.concatenate([cos_r, cos_r, ones(96)], axis=1),
              jnp.concatenate([-sin_r, sin_r, zeros(96)], axis=1), _lane_mask(0, 16), 16)

    scale_b = (B_NOPE + B_ROPE) ** -0.5
    scale_m = M_HD ** -0.5

    n_l = n_layers
    per_head = lambda g, nh: jnp.tile(g, (1, nh))[:, None, :]
    lpad = lambda g, n: jnp.pad(g, ((0, 0), (0, n)))
    gains_all = dict(
        g_mix=w["g_mix"][:, None, :], g_mlp=w["g_mlp"][:, None, :], m_g_mem=w["m_g_mem"][:, None, :],
        b_gate=w["b_gate"][:, None, :], b_qa_norm=w["b_qa_norm"][:, None, :], b_kva_norm=w["b_kva_norm"][:, None, :],
        a_q=per_head(w["a_qn"], A_HEADS), a_k=per_head(w["a_kn"], A_KV_HEADS),
        b_q=per_head(lpad(w["b_qn"], 32), B_HEADS), b_k=per_head(lpad(w["b_kn"][:, :B_NOPE], 64), B_HEADS),
        b_kr=lpad(w["b_kn"][:, B_NOPE:], 96)[:, None, :],
        c_q=[per_head(w["c_qn"][:, g], C_HEADS) for g in range(3)],
        c_k=[per_head(w["c_kn"][:, g], C_HEADS) for g in range(3)],
        m_q=per_head(w["m_qn"], M_HEADS), m_k=per_head(w["m_kn"], M_HEADS),
        sink_row=jnp.repeat(w["a_sink"], A_HD, axis=1)[:, None, :],
        sinks=jnp.broadcast_to(w["a_sink"][:, :, None, None], (n_l, A_HEADS, 1, LANES)))

    def gains(l):
        return {k: ([(a, l) for a in v] if isinstance(v, list) else (v, l)) for k, v in gains_all.items()}

    ag_hosts = dict(mm_in=("w_up",), band_a=("w_down",), flash_b=("w_in",), mm_up=("w_branch", "w_out"),
                    mm_down=("m_w_kv", "b_w_uq", "b_w_ukv"))
    first_alone = ("w_in",)
    first_hosts = dict(mm_in=("b_w_uq", "b_w_ukv"), band_c0=("w_up",), band_c1=("w_down",),
                       band_c2=("w_branch", "w_out", "m_w_kv"))
    saved = []
    weights = [dict() for _ in range(n_layers)]

    def arrived(layer, names, blocks):
        for name, blk in zip(names, blocks):
            weights[layer].update(layout(name, blk))

    cur = x
    arrived(0, first_alone, _exchange_many(ag_src(0, first_alone), name="ag_first"))
    for l in range(n_layers):
        gn = gains(l)
        sv = dict(x=cur)
        wl = weights[l]

        def carry(host):
            return ag_src(l + 1, ag_hosts[host]) if l + 1 < n_layers else []

        def carry0(host):
            return ag_src(0, first_hosts[host]) if l == 0 else []

        h = _rmsnorm_fwd((cur, d_model, 0), gn["g_mix"], name="f_norm_mix")
        nxt_w = carry("mm_in")
        z, got = _mm(h, wl["w_in"], name="f_mm_in", xchg=nxt_w + carry0("mm_in"))
        arrived(l + 1, ag_hosts["mm_in"], got[:len(nxt_w)])
        arrived(0, first_hosts["mm_in"], got[len(nxt_w):])
        sv.update(h=h, z=z)
        qa = _headnorm_fwd((z, 512, unit("a_q")), gn["a_q"], p64, rope64, name="f_hn_aq")
        ka = _headnorm_fwd((z, 128, unit("a_k")), gn["a_k"], p64, rope64, name="f_hn_ak")
        va = (z, 128, unit("a_v"))
        (o_a, lse_a), got = _band_fwd(qa, ka, va, dil=1, group=A_HEADS // A_KV_HEADS, maxdist=A_WINDOW - 1,
                                      sinks=gn["sinks"], name="f_band_a", xchg=carry("band_a"))
        arrived(l + 1, ag_hosts["band_a"], got)
        sv.update(qa=qa, ka=ka, o_a=o_a, lse_a=lse_a)
        cq = _rmsnorm_fwd((z, B_Q_LORA, unit("b_cq")), gn["b_qa_norm"], name="f_norm_cq")
        q_up = _mm(cq, wl["uq"], name="f_mm_uq")
        ckv = _rmsnorm_fwd((z, B_KV_LORA, unit("b_ckv")), gn["b_kva_norm"], name="f_norm_ckv")
        kv_up = _mm(ckv, wl["ukv"], name="f_mm_ukv")
        kp = _headnorm_fwd((z, LANES, unit("b_kr")), gn["b_kr"], pkr, rope_k, name="f_hn_kr")
        qb = _headnorm_fwd((q_up, B_HEADS * LANES, 0), gn["b_q"], pq, rope_q, name="f_hn_bq")
        kb = _headnorm_fwd((kv_up, B_HEADS * LANES, 0), gn["b_k"], p64, None, addend=kp, name="f_hn_bk")
        vb = (kv_up, HPG * B_V, B_HEADS * LANES // (HPG * B_V))
        (o_b, lse_b), got = _attn_fwd(qb, kb, vb, dv=B_V, causal=True, scale=scale_b, name="f_flash_b",
                                       xchg=carry("flash_b"))
        arrived(l + 1, ag_hosts["flash_b"], got)
        sv.update(cq=cq, q_up=q_up, ckv=ckv, kv_up=kv_up, qb=qb, kb=kb, o_b=o_b, lse_b=lse_b)
        qc, kc, o_cg, lse_cg = [], [], [], []
        for g, (win, dil) in enumerate(C_PATTERNS):
            qg = _headnorm_fwd((z, 512, unit("c%d" % (3 * g))), gn["c_q"][g], p64, rope64, dil=dil,
                               name="f_hn_cq%d" % g)
            kg = _headnorm_fwd((z, 512, unit("c%d" % (3 * g + 1))), gn["c_k"][g], p64, rope64, dil=dil,
                               name="f_hn_ck%d" % g)
            host = "band_c%d" % g
            (og, lg), got = _band_fwd(qg, kg, (z, 512, unit("c%d" % (3 * g + 2))), dil=dil, group=1,
                                      maxdist=win // dil, name="f_band_c%d" % g,
                                      xchg=carry0(host) if host in first_hosts else ())
            arrived(0, first_hosts.get(host, ()), got)
            qc.append(qg), kc.append(kg), o_cg.append(og), lse_cg.append(lg)
        c_dils = [dil for _, dil in C_PATTERNS]
        o_c = _combine_fwd(o_cg, lse_cg, c_dils, rows=s_len, width=BRANCH_W, name="f_combine")
        sv.update(qc=qc, kc=kc, o_cg=o_cg, lse_cg=lse_cg, o_c=o_c)
        memn = _rmsnorm_fwd((mem, d_model, 0), gn["m_g_mem"], name="f_norm_mem")
        mkv = _mm(memn, wl["m_w_kv"], name="f_mm_mkv")
        mk = _headnorm_fwd((mkv, M_HEADS * M_HD, 0), gn["m_k"], p128, None, name="f_hn_mk")
        mq = _headnorm_fwd((z, M_HEADS * M_HD, unit("m_q")), gn["m_q"], p128, None, name="f_hn_mq")
        mv = (mkv, HPG * M_HD, M_HEADS * M_HD // (HPG * M_HD))
        o_m, lse_m = _attn_fwd(mq, mk, mv, dv=M_HD, causal=False, scale=scale_m, name="f_flash_m")
        sv.update(memn=memn, mkv=mkv, mk=mk, mq=mq, o_m=o_m, lse_m=lse_m)
        outs = (o_a, o_b, o_c, o_m)
        ys = [_mm(outs[n], wl["w_branch"][n], name="f_mm_branch") for n in range(N_BRANCH)]
        t = _merge_fwd((z, N_BRANCH * d_model, 0), gn["b_gate"], ys, name="f_merge")
        x2 = _mm(t, wl["w_out"], mode="add", extra=cur, name="f_mm_out")
        h2 = _rmsnorm_fwd((x2, d_model, 0), gn["g_mlp"], name="f_norm_mlp")
        (u, act), got = _mm(h2, wl["w_up"], mode="relu2", name="f_mm_up", xchg=carry("mm_up"))
        arrived(l + 1, ag_hosts["mm_up"], got)
        x3, got = _mm(act, wl["w_down"], mode="add", extra=x2, name="f_mm_down", xchg=carry("mm_down"))
        arrived(l + 1, ag_hosts["mm_down"], got)
        sv.update(ys=ys, t=t, x2=x2, h2=h2, u=u, act=act)
        saved.append(sv)
        cur = x3

    dx, loss_row = _loss_head(cur, loss_target, name="loss_head")
    loss = lax.psum(loss_row[0, 0], AXES)

    recvs = {n: [None] * n_layers for n in BIG}
    rows = {}

    def keep(key, layer, row):
        rows.setdefault(key, [None] * n_layers)[layer] = row

    def item(name, layer, g):
        return (name, layer, rs_src(name, g))

    def srcs(items):
        return [it[2] for it in items]

    def store(items, got):
        for (name, layer, _), r in zip(items, got):
            recvs[name][layer] = r

    carried = []
    for l in reversed(range(n_layers)):
        sv = saved[l]
        wl = weights[l]
        gn = gains(l)
        z = sv["z"]
        g_wdown = _mm(sv["act"], dx, ta=True, name="b_mm_wdown")
        du = _mm(dx, wl["w_down"], tb=True, mode="drelu2", extra=sv["u"], out_dtype=BF16, name="b_mm_du")
        g_wup = _mm(sv["h2"], du, ta=True, name="b_mm_wup")
        dh2 = _mm(du, wl["w_up"], tb=True, name="b_mm_dh2")
        dx2, dg = _rmsnorm_bwd(dh2, (sv["x2"], d_model, 0), gn["g_mlp"], dres=dx, name="b_norm_mlp")
        keep("g_mlp", l, dg)
        g_wout = _mm(sv["t"], dx2, ta=True, name="b_mm_wout")
        dt = _mm(dx2, wl["w_out"], tb=True, name="b_mm_dt")
        dys, dz, dbg = _merge_bwd(dt, (z, N_BRANCH * d_model, 0), gn["b_gate"], sv["ys"], slab_width=nw,
                                  name="b_merge")
        keep("b_gate", l, dbg)
        outs = (sv["o_a"], sv["o_b"], sv["o_c"], sv["o_m"])
        g_wbranch = jnp.stack([_mm(outs[n], dys[n], ta=True, name="b_mm_wbranch") for n in range(N_BRANCH)])
        do_a, do_b, do_c, do_m = [_mm(dys[n], wl["w_branch"][n], tb=True, name="b_mm_dbranch")
                                  for n in range(N_BRANCH)]
        mlp_items = [item("w_down", l, g_wdown), item("w_up", l, g_wup)]
        proj_items = [item("w_out", l, g_wout), item("w_branch", l, g_wbranch)]
        va = (z, 128, unit("a_v"))
        kw_a = dict(dil=1, group=A_HEADS // A_KV_HEADS, maxdist=A_WINDOW - 1)
        dl_a, dsink = _delta(do_a, sv["o_a"], p64, lse=sv["lse_a"], sink_row=gn["sink_row"], name="b_delta_a")
        keep("a_sink", l, dsink)
        dqa = _band_dq(sv["qa"], sv["ka"], va, do_a, sv["lse_a"], dl_a, name="b_band_dq_a", **kw_a)
        dka, dz = _band_dkv(sv["qa"], sv["ka"], va, do_a, sv["lse_a"], dl_a, slab=(dz, nw), name="b_band_dkv_a",
                            **kw_a)
        dz, dg = _headnorm_bwd(dqa, (z, 512, unit("a_q")), gn["a_q"], p64, rope64, slab=(dz, nw), name="b_hn_aq")
        keep("a_qn", l, dg)
        dz, dg = _headnorm_bwd(dka, (z, 128, unit("a_k")), gn["a_k"], p64, rope64, slab=(dz, nw), name="b_hn_ak")
        keep("a_kn", l, dg)
        vb = (sv["kv_up"], HPG * B_V, B_HEADS * LANES // (HPG * B_V))
        dl_b, _ = _delta(do_b, sv["o_b"], p64, name="b_delta_b")
        kw_b = dict(dv=B_V, causal=True, scale=scale_b)
        dqb, got = _attn_dq(sv["qb"], sv["kb"], vb, do_b, sv["lse_b"], dl_b, name="b_flash_dq_b",
                            xchg=srcs(mlp_items), **kw_b)
        store(mlp_items, got)
        dkv_items = carried + proj_items
        (dkb, dvb), got = _attn_dkv(sv["qb"], sv["kb"], vb, do_b, sv["lse_b"], dl_b, name="b_flash_dkv_b",
                                     xchg=srcs(dkv_items), **kw_b)
        store(dkv_items, got)
        d_qup, dg = _headnorm_bwd(dqb, (sv["q_up"], B_HEADS * LANES, 0), gn["b_q"], pq, rope_q, name="b_hn_bq")
        keep("b_q", l, dg)
        d_kn, dg = _headnorm_bwd(dkb, (sv["kv_up"], B_HEADS * LANES, 0), gn["b_k"], p64, None, name="b_hn_bk")
        keep("b_k", l, dg)
        dkp = _kp_reduce(dkb, name="b_kp_reduce")
        dz, dg = _headnorm_bwd(dkp, (z, LANES, unit("b_kr")), gn["b_kr"], pkr, rope_k, slab=(dz, nw), name="b_hn_kr")
        keep("b_kr", l, dg)
        d_kvup = jnp.concatenate([d_kn, dvb], axis=1)
        g_ukv = _mm(sv["ckv"], d_kvup, ta=True, name="b_mm_wukv")
        dckv = _mm(d_kvup, wl["ukv"], tb=True, name="b_mm_dckv")
        dz, dg = _rmsnorm_bwd(dckv, (z, B_KV_LORA, unit("b_ckv")), gn["b_kva_norm"], out_dtype=BF16, slab=(dz, nw),
                              name="b_norm_ckv")
        keep("b_kva_norm", l, dg)
        g_uq = _mm(sv["cq"], d_qup, ta=True, name="b_mm_wuq")
        dcq = _mm(d_qup, wl["uq"], tb=True, name="b_mm_dcq")
        dz, dg = _rmsnorm_bwd(dcq, (z, B_Q_LORA, unit("b_cq")), gn["b_qa_norm"], out_dtype=BF16, slab=(dz, nw),
                              name="b_norm_cq")
        keep("b_qa_norm", l, dg)
        g_wuq = g_uq.reshape(B_Q_LORA, B_HEADS, LANES)[:, :, :B_NOPE + B_ROPE].reshape(B_Q_LORA, -1)
        g_wukv = jnp.concatenate(
            [g_ukv[:, :B_HEADS * LANES].reshape(B_KV_LORA, B_HEADS, LANES)[:, :, :B_NOPE],
             g_ukv[:, B_HEADS * LANES:].reshape(B_KV_LORA, B_HEADS, B_V)], axis=-1).reshape(B_KV_LORA, -1)
        do_cg, dl_cg = _combine_bwd(do_c, sv["o_c"], sv["lse_cg"], [dil for _, dil in C_PATTERNS], p64,
                                    name="b_combine")
        for g, (win, dil) in enumerate(C_PATTERNS):
            vc = (z, 512, unit("c%d" % (3 * g + 2)))
            kw_c = dict(dil=dil, group=1, maxdist=win // dil)
            args = (sv["qc"][g], sv["kc"][g], vc, do_cg[g], sv["lse_cg"][g], dl_cg[g])
            dq = _band_dq(*args, name="b_band_dq_c%d" % g, **kw_c)
            if dil == 1:
                dk, dz = _band_dkv(*args, slab=(dz, nw), name="b_band_dkv_c%d" % g, **kw_c)
            else:
                dk, dv_ = _band_dkv(*args, name="b_band_dkv_c%d" % g, **kw_c)
                vcol = lay["c%d" % (3 * g + 2)]["off"]
                dz = lax.dynamic_update_slice(dz, dv_, (0, vcol))
            dz, dg = _headnorm_bwd(dq, (z, 512, unit("c%d" % (3 * g))), gn["c_q"][g], p64, rope64, dil=dil,
                                   slab=(dz, nw), name="b_hn_cq%d" % g)
            keep("c_q%d" % g, l, dg)
            dz, dg = _headnorm_bwd(dk, (z, 512, unit("c%d" % (3 * g + 1))), gn["c_k"][g], p64, rope64, dil=dil,
                                   slab=(dz, nw), name="b_hn_ck%d" % g)
            keep("c_k%d" % g, l, dg)
        mv = (sv["mkv"], HPG * M_HD, M_HEADS * M_HD // (HPG * M_HD))
        dl_m, _ = _delta(do_m, sv["o_m"], p128, name="b_delta_m")
        kw_m = dict(dv=M_HD, causal=False, scale=scale_m)
        dmq = _attn_dq(sv["mq"], sv["mk"], mv, do_m, sv["lse_m"], dl_m, name="b_flash_dq_m", **kw_m)
        dmk, dmv = _attn_dkv(sv["mq"], sv["mk"], mv, do_m, sv["lse_m"], dl_m, name="b_flash_dkv_m", **kw_m)
        dz, dg = _headnorm_bwd(dmq, (z, M_HEADS * M_HD, unit("m_q")), gn["m_q"], p128, None, slab=(dz, nw),
                               name="b_hn_mq")
        keep("m_qn", l, dg)
        d_mk, dg = _headnorm_bwd(dmk, (sv["mkv"], M_HEADS * M_HD, 0), gn["m_k"], p128, None, name="b_hn_mk")
        keep("m_kn", l, dg)
        d_mkv = jnp.concatenate([d_mk, dmv], axis=1)
        g_wmkv = _mm(sv["memn"], d_mkv, ta=True, name="b_mm_wmkv")
        dmemn = _mm(d_mkv, wl["m_w_kv"], tb=True, name="b_mm_dmemn")
        _, dg = _rmsnorm_bwd(dmemn, (mem, d_model, 0), gn["m_g_mem"], name="b_norm_mem")
        keep("m_g_mem", l, dg)
        bm_items = [item("b_w_ukv", l, g_wukv), item("b_w_uq", l, g_wuq), item("m_w_kv", l, g_wmkv)]
        g_in, got = _mm(sv["h"], dz, ta=True, name="b_mm_win", xchg=srcs(bm_items))
        store(bm_items, got)
        carried = [item("w_in", l, _unpermute_cols(g_in, lay, orig_names))]
        last_items = carried if l == 0 else []
        dh, got = _mm(dz, wl["w_in"], tb=True, name="b_mm_dh", xchg=srcs(last_items))
        store(last_items, got)
        dx, dg = _rmsnorm_bwd(dh, (sv["x"], d_model, 0), gn["g_mix"], dres=dx2, name="b_norm_mix")
        keep("g_mix", l, dg)

    cat = {k: jnp.concatenate(v, axis=0) for k, v in rows.items()}
    heads = lambda a, nh, hd: a.reshape(n_layers, nh, hd).sum(axis=1)
    gsmall = dict(
        g_mix=cat["g_mix"], b_gate=cat["b_gate"], g_mlp=cat["g_mlp"], m_g_mem=cat["m_g_mem"],
        b_qa_norm=cat["b_qa_norm"], b_kva_norm=cat["b_kva_norm"],
        a_sink=cat["a_sink"][:, ::A_HD], a_qn=heads(cat["a_qn"], A_HEADS, A_HD), a_kn=heads(cat["a_kn"], A_KV_HEADS, A_HD),
        b_qn=heads(cat["b_q"], B_HEADS, LANES)[:, :B_NOPE + B_ROPE],
        b_kn=jnp.concatenate([heads(cat["b_k"], B_HEADS, LANES)[:, :B_NOPE], cat["b_kr"][:, :B_ROPE]], axis=1),
        c_qn=jnp.stack([heads(cat["c_q%d" % g], C_HEADS, C_HD) for g in range(3)], axis=1),
        c_kn=jnp.stack([heads(cat["c_k%d" % g], C_HEADS, C_HD) for g in range(3)], axis=1),
        m_qn=heads(cat["m_qn"], M_HEADS, M_HD), m_kn=heads(cat["m_kn"], M_HEADS, M_HD))

    out_g, out_d, out_m, out_v = {}, {}, {}, {}
    for name in BIG:
        shard = w[name].shape
        two = lambda a: a.reshape(-1, shard[-1])
        pieces = [r.reshape(N_DEV, -1, shard[-1]) for r in recvs[name]]
        g, dlt, mn, vn = _adam(pieces, two(w[name]), two(moments_m[name]), two(moments_v[name]), name="adam_" + name)
        out_g[name], out_d[name], out_m[name], out_v[name] = (a.reshape(shard) for a in (g, dlt, mn, vn))
    sizes = [int(np.prod(w[n].shape)) for n in SMALL]
    total = sum(sizes)
    padded = -(-total // (8 * LANES)) * (8 * LANES)
    pack = lambda d: jnp.pad(jnp.concatenate([jnp.reshape(d[n], (-1,)) for n in SMALL]), (0, padded - total)).reshape(-1, LANES)
    gpack = pack(gsmall)
    recv = _exchange_many([(gpack, False)], name="ag_small_grads")[0]
    packed = _adam([recv], pack(w), pack(moments_m), pack(moments_v), name="adam_small")
    offs = np.cumsum([0] + sizes)
    for a, out in zip(packed, (out_g, out_d, out_m, out_v)):
        flat = a.reshape(-1)
        for n, o, sz in zip(SMALL, offs[:-1], sizes):
            out[n] = flat[o:o + sz].reshape(w[n].shape)
    return loss, dx, out_g, out_d, out_m, out_v


def kernel(x, mem, positions, g_mix, w_in, b_gate, a_qn, a_kn, a_sink, b_qa_norm, b_kva_norm, b_w_uq, b_w_ukv, b_qn, b_kn, c_qn, c_kn, m_g_mem, m_w_kv, m_qn, m_kn, w_branch, w_out, g_mlp, w_up, w_down, loss_target, m_g_mix, m_w_in, m_b_gate, m_a_qn, m_a_kn, m_a_sink, m_b_qa_norm, m_b_kva_norm, m_b_w_uq, m_b_w_ukv, m_b_qn, m_b_kn, m_c_qn, m_c_kn, m_m_g_mem, m_m_w_kv, m_m_qn, m_m_kn, m_w_branch, m_w_out, m_g_mlp, m_w_up, m_w_down, v_g_mix, v_w_in, v_b_gate, v_a_qn, v_a_kn, v_a_sink, v_b_qa_norm, v_b_kva_norm, v_b_w_uq, v_b_w_ukv, v_b_qn, v_b_kn, v_c_qn, v_c_kn, v_m_g_mem, v_m_w_kv, v_m_qn, v_m_kn, v_w_branch, v_w_out, v_g_mlp, v_w_up, v_w_down):
    w = dict(g_mix=g_mix, w_in=w_in, b_gate=b_gate, a_qn=a_qn, a_kn=a_kn, a_sink=a_sink, b_qa_norm=b_qa_norm,
             b_kva_norm=b_kva_norm, b_w_uq=b_w_uq, b_w_ukv=b_w_ukv, b_qn=b_qn, b_kn=b_kn, c_qn=c_qn, c_kn=c_kn,
             m_g_mem=m_g_mem, m_w_kv=m_w_kv, m_qn=m_qn, m_kn=m_kn, w_branch=w_branch, w_out=w_out, g_mlp=g_mlp,
             w_up=w_up, w_down=w_down)
    mm_ = dict(g_mix=m_g_mix, w_in=m_w_in, b_gate=m_b_gate, a_qn=m_a_qn, a_kn=m_a_kn, a_sink=m_a_sink,
               b_qa_norm=m_b_qa_norm, b_kva_norm=m_b_kva_norm, b_w_uq=m_b_w_uq, b_w_ukv=m_b_w_ukv, b_qn=m_b_qn,
               b_kn=m_b_kn, c_qn=m_c_qn, c_kn=m_c_kn, m_g_mem=m_m_g_mem, m_w_kv=m_m_w_kv, m_qn=m_m_qn, m_kn=m_m_kn,
               w_branch=m_w_branch, w_out=m_w_out, g_mlp=m_g_mlp, w_up=m_w_up, w_down=m_w_down)
    vv = dict(g_mix=v_g_mix, w_in=v_w_in, b_gate=v_b_gate, a_qn=v_a_qn, a_kn=v_a_kn, a_sink=v_a_sink,
              b_qa_norm=v_b_qa_norm, b_kva_norm=v_b_kva_norm, b_w_uq=v_b_w_uq, b_w_ukv=v_b_w_ukv, b_qn=v_b_qn,
              b_kn=v_b_kn, c_qn=v_c_qn, c_kn=v_c_kn, m_g_mem=v_m_g_mem, m_w_kv=v_m_w_kv, m_qn=v_m_qn, m_kn=v_m_kn,
              w_branch=v_w_branch, w_out=v_w_out, g_mlp=v_g_mlp, w_up=v_w_up, w_down=v_w_down)
    loss, dx, g, dlt, mn, vn = _step(x[0], mem[0], positions[0], w, mm_, vv, loss_target[0])
    return (loss, dx[None], *[g[n] for n in WEIGHTS], *[dlt[n] for n in WEIGHTS], *[mn[n] for n in WEIGHTS],
            *[vn[n] for n in WEIGHTS])
```

```python
import functools

import numpy as np
import jax
import jax.numpy as jnp
from jax import lax
from jax.experimental import pallas as pl
from jax.experimental.pallas import tpu as pltpu

F32 = jnp.float32
BF16 = jnp.bfloat16

LANES = 128
N_DEV = 8
AXES = ("x", "y", "c")
VMEM_LIMIT = 48 * 1024 * 1024

EPS = 1e-6
NEG = -1e30
ROPE_THETA = 10000.0
BLOCK = 128

A_HEADS, A_KV_HEADS, A_HD, A_WINDOW = 8, 2, 64, 128
B_HEADS, B_Q_LORA, B_KV_LORA, B_NOPE, B_ROPE, B_V = 8, 384, 256, 64, 32, 64
C_PATTERNS = ((128, 1), (512, 4), (2048, 16))
C_HEADS, C_HD = 8, 64
M_HEADS, M_HD = 4, 128
BRANCH_W = 512
N_BRANCH = 4

ADAM_LR, ADAM_B1, ADAM_B2, ADAM_EPS, ADAM_WD, ADAM_STEP = 0.001, 0.9, 0.999, 1e-08, 0.01, 10

NT_DIMS = (((1,), (1,)), ((), ()))
TN_DIMS = (((0,), (0,)), ((), ()))
NN_DIMS = (((1,), (0,)), ((), ()))


def _pick(n, cands):
    for c in cands:
        if n % c == 0:
            return c
    return n


def _cparams(sem):
    return pltpu.CompilerParams(dimension_semantics=sem, vmem_limit_bytes=VMEM_LIMIT)


def _dot(a, b, dims=NN_DIMS):
    return lax.dot_general(a.astype(BF16), b.astype(BF16), dims, preferred_element_type=F32)


def _in_layout(d_model):
    orig = ([("a_q", 512), ("a_k", 128), ("a_v", 128), ("b_cq", B_Q_LORA), ("b_ckv", B_KV_LORA), ("b_kr", B_ROPE)]
            + [("c%d" % i, 512) for i in range(9)] + [("m_q", 512), ("gate", N_BRANCH * d_model)])
    order = ["gate", "a_q"] + ["c%d" % i for i in range(9)] + ["m_q", "b_ckv", "b_cq", "a_k", "a_v", "b_kr"]
    o_off, off = {}, 0
    for name, w in orig:
        o_off[name] = (off, w)
        off += w
    lay, noff = {}, 0
    for name in order:
        oo, w = o_off[name]
        nw = LANES if name == "b_kr" else w
        assert noff % nw == 0, (name, noff, nw)
        lay[name] = dict(orig=oo, w=w, off=noff, nw=nw)
        noff += nw
    assert noff % 512 == 0
    return lay, [n for n, _ in orig], order, off, noff


def _permute_cols(w, lay, order):
    cols = []
    for name in order:
        e = lay[name]
        p = w[..., e["orig"]:e["orig"] + e["w"]]
        if e["nw"] != e["w"]:
            p = jnp.pad(p, [(0, 0)] * (w.ndim - 1) + [(0, e["nw"] - e["w"])])
        cols.append(p)
    return jnp.concatenate(cols, axis=-1)


def _unpermute_cols(w, lay, orig_names):
    return jnp.concatenate([w[..., lay[n]["off"]:lay[n]["off"] + lay[n]["w"]] for n in orig_names], axis=-1)


def _pattern(groups):
    ids = np.repeat(np.arange(len(groups)), groups)
    assert ids.shape[0] == LANES
    gmat = (ids[:, None] == ids[None, :]).astype(np.float32)
    invn = (1.0 / np.asarray(groups, np.float32))[ids][None, :]
    return jnp.asarray(gmat, BF16), jnp.asarray(invn, F32)


def _gsum(v, gmat):
    hi = v.astype(BF16)
    r1 = v - hi.astype(F32)
    mid = r1.astype(BF16)
    lo = (r1 - mid.astype(F32)).astype(BF16)
    dot = lambda p: jnp.dot(p, gmat, preferred_element_type=F32)
    return dot(hi) + dot(mid) + dot(lo)


def _swap_half(v, lomask, half):
    return jnp.where(lomask > 0.5, pltpu.roll(v, LANES - half, 1), pltpu.roll(v, half, 1))


def _rows_spec(tr, width, off=0):
    return pl.BlockSpec((tr, width), lambda i: (i, off))


def _undilate(src_ref, scr, dil, width):
    n = scr.shape[1] // dil
    for r in range(dil):
        for c in range(width // LANES):
            lanes = slice(r * width + c * LANES, r * width + (c + 1) * LANES)
            scr[c, pl.ds(r, n, stride=dil), :] = src_ref[:, lanes].astype(F32)


def _dilate(scr, dst_ref, dil, width):
    n = scr.shape[1] // dil
    for r in range(dil):
        for c in range(width // LANES):
            lanes = slice(r * width + c * LANES, r * width + (c + 1) * LANES)
            dst_ref[:, lanes] = scr[c, pl.ds(r, n, stride=dil), :].astype(dst_ref.dtype)


def _dil_spec(tr, width, dil):
    return pl.BlockSpec((tr // dil, dil * width), lambda i: (i, 0))


def _layer_arg(per_layer):
    arr, l = per_layer
    nd = arr.ndim - 1
    return pl.BlockSpec((None,) + tuple(arr.shape[1:]), lambda *_: (l,) + (0,) * nd), arr


def _full_spec(shape):
    return pl.BlockSpec(shape, lambda i: (0,) * len(shape))


def _mm(a, b, *, ta=False, tb=False, out_dtype=F32, mode=None, extra=None, name, xchg=None):
    if ta:
        kd, m = a.shape
    else:
        m, kd = a.shape
    if tb:
        n, kb = b.shape
    else:
        kb, n = b.shape
    assert kd == kb, (a.shape, b.shape, ta, tb)
    tm = _pick(m, (1024, 512, 256, 128))
    tn = _pick(n, (1024, 1536, 512, 384, 256, 128))
    tk = _pick(kd, (1024, 1536, 512, 384, 256, 128))
    nk = kd // tk
    a_spec = (pl.BlockSpec((tk, tm), lambda i, j, k: (k, i)) if ta else pl.BlockSpec((tm, tk), lambda i, j, k: (i, k)))
    b_spec = (pl.BlockSpec((tn, tk), lambda i, j, k: (j, k)) if tb else pl.BlockSpec((tk, tn), lambda i, j, k: (k, j)))
    o_spec = pl.BlockSpec((tm, tn), lambda i, j, k: (i, j))
    dims = (((0 if ta else 1,), (1 if tb else 0,)), ((), ()))
    has_extra = mode in ("add", "drelu2")

    def body(*refs):
        a_ref, b_ref = refs[0], refs[1]
        e_ref = refs[2] if has_extra else None
        outs = refs[2 + has_extra:2 + has_extra + (2 if mode == "relu2" else 1)]
        k = pl.program_id(2)

        def finish(r):
            if mode == "add":
                outs[0][...] = (e_ref[...] + r).astype(out_dtype)
            elif mode == "relu2":
                outs[0][...] = r
                outs[1][...] = jnp.square(jnp.maximum(r, 0.0)).astype(BF16)
            elif mode == "drelu2":
                outs[0][...] = (r * (2.0 * jnp.maximum(e_ref[...], 0.0))).astype(out_dtype)
            else:
                outs[0][...] = r.astype(out_dtype)

        if nk == 1:
            finish(_dot(a_ref[...], b_ref[...], dims))
        else:
            acc = refs[-1]

            @pl.when(k == 0)
            def _():
                acc[...] = _dot(a_ref[...], b_ref[...], dims)

            @pl.when(k > 0)
            def _():
                acc[...] += _dot(a_ref[...], b_ref[...], dims)

            @pl.when(k == nk - 1)
            def _():
                finish(acc[...])

    in_specs = [a_spec, b_spec] + ([o_spec] if has_extra else [])
    args = [a, b] + ([extra] if has_extra else [])
    if mode == "relu2":
        out_shape = (jax.ShapeDtypeStruct((m, n), F32), jax.ShapeDtypeStruct((m, n), BF16))
        out_specs = (o_spec, o_spec)
    else:
        out_shape = jax.ShapeDtypeStruct((m, n), out_dtype)
        out_specs = o_spec
    return _pcall(
        body, grid=(m // tm, n // tn, nk), in_specs=in_specs, out_specs=out_specs, out_shape=out_shape,
        scratch_shapes=[pltpu.VMEM((tm, tn), F32)] if nk > 1 else [], sem=("parallel", "parallel", "arbitrary"),
        name=name, args=args, xchg=xchg)


def _rmsnorm_fwd(xv, gain, *, name, tr=512):
    x, width, off = xv
    rows = x.shape[0]
    tr = min(tr, rows)

    def body(x_ref, g_ref, o_ref):
        xf = x_ref[...]
        rstd = lax.rsqrt(jnp.mean(xf * xf, axis=-1, keepdims=True) + EPS)
        o_ref[...] = (xf * rstd * g_ref[...]).astype(BF16)

    g_spec, g_arr = _layer_arg(gain)
    return pl.pallas_call(
        body, grid=(rows // tr,), in_specs=[_rows_spec(tr, width, off), g_spec],
        out_specs=_rows_spec(tr, width), out_shape=jax.ShapeDtypeStruct((rows, width), BF16),
        compiler_params=_cparams(("parallel",)), name=name)(x, g_arr)


def _slab_out(slab, rows, in_specs, args, out_index):
    prev, total = slab
    aliases = {}
    if prev is not None:
        in_specs.append(pl.BlockSpec(memory_space=pl.ANY))
        args.append(prev)
        aliases = {len(args) - 1: out_index}
    return jax.ShapeDtypeStruct((rows, total), BF16), aliases, int(prev is not None)


def _rmsnorm_bwd(dh, xv, gain, *, dres=None, out_dtype=F32, slab=None, name, tr=512):
    x, width, off = xv
    rows = x.shape[0]
    tr = min(tr, rows)
    has_res = dres is not None
    has_prev = int(slab is not None and slab[0] is not None)

    def body(*refs):
        dh_ref, x_ref, g_ref = refs[:3]
        r_ref = refs[3] if has_res else None
        dx_ref, dg_ref = refs[3 + has_res + has_prev:]
        i = pl.program_id(0)
        xf = x_ref[...]
        rstd = lax.rsqrt(jnp.mean(xf * xf, axis=-1, keepdims=True) + EPS)
        xh = xf * rstd
        d = dh_ref[...]
        dxh = d * g_ref[...]
        dx = rstd * (dxh - xh * jnp.mean(dxh * xh, axis=-1, keepdims=True))
        if has_res:
            dx = dx + r_ref[...]
        dx_ref[...] = dx.astype(out_dtype)

        @pl.when(i == 0)
        def _():
            dg_ref[...] = jnp.zeros_like(dg_ref)

        dg_ref[...] += jnp.sum(d * xh, axis=0, keepdims=True)

    g_spec, g_arr = _layer_arg(gain)
    in_specs = [_rows_spec(tr, width), _rows_spec(tr, width, off), g_spec]
    args = [dh, x, g_arr]
    if has_res:
        in_specs.append(_rows_spec(tr, width))
        args.append(dres)
    dx_spec, dx_shape, aliases = _rows_spec(tr, width), jax.ShapeDtypeStruct((rows, width), out_dtype), {}
    if slab is not None:
        assert out_dtype == BF16
        dx_spec = _rows_spec(tr, width, off)
        dx_shape, aliases, _ = _slab_out(slab, rows, in_specs, args, 0)
    return pl.pallas_call(
        body, grid=(rows // tr,), in_specs=in_specs,
        out_specs=(dx_spec, _full_spec((1, width))),
        out_shape=(dx_shape, jax.ShapeDtypeStruct((1, width), F32)), input_output_aliases=aliases,
        compiler_params=_cparams(("arbitrary",)), name=name)(*args)


def _headnorm_fwd(xv, gain_row, pat, rope, *, addend=None, dil=1, name, tr=512):
    x, width, off = xv
    rows = x.shape[0]
    tr = min(tr, rows)
    gmat, invn = pat
    nblk = width // LANES
    has_rope = rope is not None
    has_add = addend is not None
    half = rope[3] if has_rope else 0

    def body(*refs):
        x_ref, g_ref, gm_ref, in_ref = refs[:4]
        pos = 4
        if has_rope:
            c_ref, s_ref, lo_ref = refs[pos:pos + 3]
            pos += 3
        if has_add:
            a_ref = refs[pos]
            pos += 1
        o_ref = refs[pos]
        scr = refs[pos + 1] if dil > 1 else None
        gm = gm_ref[...]
        inv = in_ref[...]
        if has_add:
            add = pltpu.roll(a_ref[...].astype(F32), 64, 1)
        for j in range(nblk):
            sl = slice(j * LANES, (j + 1) * LANES)
            xb = x_ref[:, sl]
            rstd = lax.rsqrt(_gsum(xb * xb, gm) * inv + EPS)
            y = xb * rstd * g_ref[:, sl]
            if has_rope:
                y = y * c_ref[...] + _swap_half(y, lo_ref[...], half) * s_ref[...]
            if has_add:
                y = y + add
            if dil > 1:
                scr[j] = y
            else:
                o_ref[:, sl] = y.astype(BF16)
        if dil > 1:
            _dilate(scr, o_ref, dil, width)

    g_spec, g_arr = _layer_arg(gain_row)
    in_specs = [_rows_spec(tr, width, off), g_spec, _full_spec((LANES, LANES)), _full_spec((1, LANES))]
    args = [x, g_arr, gmat, invn]
    if has_rope:
        in_specs += [_rows_spec(tr, LANES), _rows_spec(tr, LANES), _full_spec((1, LANES))]
        args += [rope[0], rope[1], rope[2]]
    if has_add:
        in_specs.append(_rows_spec(tr, LANES))
        args.append(addend)
    return pl.pallas_call(
        body, grid=(rows // tr,), in_specs=in_specs,
        out_specs=_dil_spec(tr, width, dil) if dil > 1 else _rows_spec(tr, width),
        out_shape=jax.ShapeDtypeStruct((rows // dil, dil * width), BF16),
        scratch_shapes=[pltpu.VMEM((nblk, tr, LANES), F32)] if dil > 1 else [],
        compiler_params=_cparams(("parallel",)), name=name)(*args)


def _headnorm_bwd(dy, xv, gain_row, pat, rope, *, dil=1, slab=None, name, tr=512):
    x, width, off = xv
    rows = x.shape[0]
    tr = min(tr, rows)
    gmat, invn = pat
    nblk = width // LANES
    has_rope = rope is not None
    has_prev = int(slab is not None and slab[0] is not None)
    half = rope[3] if has_rope else 0

    def body(*refs):
        dy_ref, x_ref, g_ref, gm_ref, in_ref = refs[:5]
        pos = 5
        if has_rope:
            c_ref, s_ref, lo_ref = refs[pos:pos + 3]
            pos += 3
        pos += has_prev
        dx_ref, dg_ref = refs[pos:pos + 2]
        i = pl.program_id(0)
        gm = gm_ref[...]
        inv = in_ref[...]
        if dil > 1:
            scr = refs[pos + 2]
            _undilate(dy_ref, scr, dil, width)

        @pl.when(i == 0)
        def _():
            dg_ref[...] = jnp.zeros_like(dg_ref)

        for j in range(nblk):
            sl = slice(j * LANES, (j + 1) * LANES)
            d = scr[j] if dil > 1 else dy_ref[:, sl]
            if has_rope:
                d = d * c_ref[...] + _swap_half(d * s_ref[...], lo_ref[...], half)
            xb = x_ref[:, sl]
            rstd = lax.rsqrt(_gsum(xb * xb, gm) * inv + EPS)
            xh = xb * rstd
            dg_ref[:, sl] += jnp.sum(d * xh, axis=0, keepdims=True)
            dxh = d * g_ref[:, sl]
            dx = rstd * (dxh - xh * (_gsum(dxh * xh, gm) * inv))
            dx_ref[:, sl] = dx.astype(BF16)

    g_spec, g_arr = _layer_arg(gain_row)
    in_specs = [_dil_spec(tr, width, dil) if dil > 1 else _rows_spec(tr, width), _rows_spec(tr, width, off), g_spec,
                _full_spec((LANES, LANES)), _full_spec((1, LANES))]
    args = [dy, x, g_arr, gmat, invn]
    if has_rope:
        in_specs += [_rows_spec(tr, LANES), _rows_spec(tr, LANES), _full_spec((1, LANES))]
        args += [rope[0], rope[1], rope[2]]
    dx_spec, dx_shape, aliases = _rows_spec(tr, width), jax.ShapeDtypeStruct((rows, width), BF16), {}
    if slab is not None:
        dx_spec = _rows_spec(tr, width, off)
        dx_shape, aliases, _ = _slab_out(slab, rows, in_specs, args, 0)
    return pl.pallas_call(
        body, grid=(rows // tr,), in_specs=in_specs,
        out_specs=(dx_spec, _full_spec((1, width))),
        out_shape=(dx_shape, jax.ShapeDtypeStruct((1, width), F32)), input_output_aliases=aliases,
        scratch_shapes=[pltpu.VMEM((nblk, tr, LANES), F32)] if dil > 1 else [],
        compiler_params=_cparams(("arbitrary",)), name=name)(*args)


def _merge_fwd(zv, b_gate, ys, *, name, tr=256):
    z, gw, off = zv
    rows, d = ys[0].shape
    tr = min(tr, rows)

    def body(z_ref, b_ref, y0, y1, y2, y3, t_ref):
        acc = None
        for n, y_ref in enumerate((y0, y1, y2, y3)):
            sl = slice(n * d, (n + 1) * d)
            term = jax.nn.sigmoid(z_ref[:, sl] + b_ref[:, sl]) * y_ref[...]
            acc = term if acc is None else acc + term
        t_ref[...] = acc.astype(BF16)

    b_spec, b_arr = _layer_arg(b_gate)
    return pl.pallas_call(
        body, grid=(rows // tr,),
        in_specs=[_rows_spec(tr, gw, off), b_spec] + [_rows_spec(tr, d)] * 4,
        out_specs=_rows_spec(tr, d), out_shape=jax.ShapeDtypeStruct((rows, d), BF16),
        compiler_params=_cparams(("parallel",)), name=name)(z, b_arr, *ys)


def _merge_bwd(dt, zv, b_gate, ys, *, slab_width, name, tr=256):
    z, gw, off = zv
    rows, d = dt.shape
    tr = min(tr, rows)

    def body(dt_ref, z_ref, b_ref, y0, y1, y2, y3, d0, d1, d2, d3, dg_ref, db_ref):
        i = pl.program_id(0)

        @pl.when(i == 0)
        def _():
            db_ref[...] = jnp.zeros_like(db_ref)

        dtv = dt_ref[...]
        for n, (y_ref, d_ref) in enumerate(((y0, d0), (y1, d1), (y2, d2), (y3, d3))):
            sl = slice(n * d, (n + 1) * d)
            gate = jax.nn.sigmoid(z_ref[:, sl] + b_ref[:, sl])
            d_ref[...] = (dtv * gate).astype(BF16)
            dpre = dtv * y_ref[...] * (gate * (1.0 - gate))
            dg_ref[:, sl] = dpre.astype(BF16)
            db_ref[:, sl] += jnp.sum(dpre, axis=0, keepdims=True)

    b_spec, b_arr = _layer_arg(b_gate)
    outs = pl.pallas_call(
        body, grid=(rows // tr,),
        in_specs=[_rows_spec(tr, d), _rows_spec(tr, gw, off), b_spec] + [_rows_spec(tr, d)] * 4,
        out_specs=tuple([_rows_spec(tr, d)] * 4 + [_rows_spec(tr, gw, off), _full_spec((1, gw))]),
        out_shape=tuple([jax.ShapeDtypeStruct((rows, d), BF16)] * 4
                        + [jax.ShapeDtypeStruct((rows, slab_width), BF16), jax.ShapeDtypeStruct((1, gw), F32)]),
        compiler_params=_cparams(("arbitrary",)), name=name)(dt, z, b_arr, *ys)
    return outs[:4], outs[4], outs[5]


def _group_weights(la, lb, lc):
    mx = jnp.maximum(jnp.maximum(la, lb), lc)
    ea, eb, ec = jnp.exp(la - mx), jnp.exp(lb - mx), jnp.exp(lc - mx)
    inv = 1.0 / (ea + eb + ec)
    return ea * inv, eb * inv, ec * inv


def _combine_fwd(os_, lses, dils, *, rows, width, name, tr=512):
    nblk = width // LANES
    ndil = sum(d > 1 for d in dils)

    def body(*refs):
        o_refs, l_refs, out_ref = refs[0:3], refs[3:6], refs[6]
        scr = list(refs[7:])
        o_tok, l_tok = [], []
        for g, d in enumerate(dils):
            if d > 1:
                so, sl_ = scr.pop(0), scr.pop(0)
                _undilate(o_refs[g], so, d, width)
                _undilate(l_refs[g], sl_, d, width)
                o_tok.append(lambda j, s=so: s[j])
                l_tok.append(lambda j, s=sl_: s[j])
            else:
                o_tok.append(lambda j, r=o_refs[g]: r[:, j * LANES:(j + 1) * LANES])
                l_tok.append(lambda j, r=l_refs[g]: r[:, j * LANES:(j + 1) * LANES])
        for j in range(nblk):
            wa, wb, wc = _group_weights(l_tok[0](j), l_tok[1](j), l_tok[2](j))
            out_ref[:, j * LANES:(j + 1) * LANES] = wa * o_tok[0](j) + wb * o_tok[1](j) + wc * o_tok[2](j)

    spec = lambda d: _dil_spec(tr, width, d) if d > 1 else _rows_spec(tr, width)
    return pl.pallas_call(
        body, grid=(rows // tr,), in_specs=[spec(d) for d in dils] * 2, out_specs=_rows_spec(tr, width),
        out_shape=jax.ShapeDtypeStruct((rows, width), F32),
        scratch_shapes=[pltpu.VMEM((nblk, tr, LANES), F32)] * (2 * ndil),
        compiler_params=_cparams(("parallel",)), name=name)(*os_, *lses)


def _combine_bwd(do, o, lses, dils, pat, *, name, tr=512):
    rows, width = do.shape
    gmat, _ = pat
    nblk = width // LANES
    ndil = sum(d > 1 for d in dils)

    def body(*refs):
        do_ref, o_ref = refs[0:2]
        l_refs, gm_ref = refs[2:5], refs[5]
        d_refs, e_refs = refs[6:9], refs[9:12]
        scr = list(refs[12:])
        gm = gm_ref[...]
        l_tok, d_scr, e_scr = [], [], []
        for g, d in enumerate(dils):
            if d > 1:
                sl_, sd, se = scr.pop(0), scr.pop(0), scr.pop(0)
                _undilate(l_refs[g], sl_, d, width)
                l_tok.append(lambda j, s=sl_: s[j])
                d_scr.append(sd), e_scr.append(se)
            else:
                l_tok.append(lambda j, r=l_refs[g]: r[:, j * LANES:(j + 1) * LANES])
                d_scr.append(None), e_scr.append(None)
        for j in range(nblk):
            sl = slice(j * LANES, (j + 1) * LANES)
            ws = _group_weights(l_tok[0](j), l_tok[1](j), l_tok[2](j))
            dov = do_ref[:, sl]
            tot = _gsum(dov * o_ref[:, sl], gm)
            for g, wgt in enumerate(ws):
                if dils[g] > 1:
                    d_scr[g][j] = wgt * dov
                    e_scr[g][j] = wgt * tot
                else:
                    d_refs[g][:, sl] = wgt * dov
                    e_refs[g][:, sl] = wgt * tot
        for g, d in enumerate(dils):
            if d > 1:
                _dilate(d_scr[g], d_refs[g], d, width)
                _dilate(e_scr[g], e_refs[g], d, width)

    spec = lambda d: _dil_spec(tr, width, d) if d > 1 else _rows_spec(tr, width)
    sds = lambda d: jax.ShapeDtypeStruct((rows // d, d * width), F32)
    outs = pl.pallas_call(
        body, grid=(rows // tr,),
        in_specs=[_rows_spec(tr, width)] * 2 + [spec(d) for d in dils] + [_full_spec((LANES, LANES))],
        out_specs=tuple([spec(d) for d in dils] * 2),
        out_shape=tuple([sds(d) for d in dils] * 2),
        scratch_shapes=[pltpu.VMEM((nblk, tr, LANES), F32)] * (3 * ndil),
        compiler_params=_cparams(("parallel",)), name=name)(do, o, *lses, gmat)
    return outs[:3], outs[3:]


def _delta(do, o, pat, *, lse=None, sink_row=None, name, tr=512):
    rows, width = do.shape
    gmat, _ = pat
    nblk = width // LANES
    has_sink = sink_row is not None

    def body(*refs):
        do_ref, o_ref, gm_ref = refs[:3]
        if has_sink:
            l_ref, s_ref, dl_ref, ds_ref = refs[3:]
        else:
            dl_ref, d16_ref = refs[3:]
        gm = gm_ref[...]
        if has_sink:
            @pl.when(pl.program_id(0) == 0)
            def _():
                ds_ref[...] = jnp.zeros_like(ds_ref)
        for j in range(nblk):
            sl = slice(j * LANES, (j + 1) * LANES)
            dl = _gsum(do_ref[:, sl] * o_ref[:, sl], gm)
            dl_ref[:, sl] = dl
            if not has_sink:
                d16_ref[:, sl] = do_ref[:, sl].astype(BF16)
            if has_sink:
                ds_ref[:, sl] += jnp.sum(-jnp.exp(s_ref[:, sl] - l_ref[:, sl]) * dl, axis=0, keepdims=True)

    in_specs = [_rows_spec(tr, width), _rows_spec(tr, width), _full_spec((LANES, LANES))]
    args = [do, o, gmat]
    out_specs = [_rows_spec(tr, width)]
    out_shape = [jax.ShapeDtypeStruct((rows, width), F32)]
    if has_sink:
        s_spec, s_arr = _layer_arg(sink_row)
        in_specs += [_rows_spec(tr, width), s_spec]
        args += [lse, s_arr]
        out_specs.append(_full_spec((1, width)))
        out_shape.append(jax.ShapeDtypeStruct((1, width), F32))
    else:
        out_specs.append(_rows_spec(tr, width))
        out_shape.append(jax.ShapeDtypeStruct((rows, width), BF16))
    outs = pl.pallas_call(
        body, grid=(rows // tr,), in_specs=in_specs, out_specs=tuple(out_specs), out_shape=tuple(out_shape),
        compiler_params=_cparams(("arbitrary",)), name=name)(*args)
    return outs[0], outs[1]


def _kp_reduce(dk, *, name, tr=512):
    rows, width = dk.shape
    nblk = width // LANES

    def body(dk_ref, o_ref):
        acc = dk_ref[:, 0:LANES]
        for j in range(1, nblk):
            acc = acc + dk_ref[:, j * LANES:(j + 1) * LANES]
        lane = lax.broadcasted_iota(jnp.int32, acc.shape, 1)
        o_ref[...] = jnp.where(lane < B_ROPE, pltpu.roll(acc, 64, 1), 0.0)

    return pl.pallas_call(
        body, grid=(rows // tr,), in_specs=[_rows_spec(tr, width)], out_specs=_rows_spec(tr, LANES),
        out_shape=jax.ShapeDtypeStruct((rows, LANES), F32),
        compiler_params=_cparams(("parallel",)), name=name)(dk)


def _loss_head(y, target, *, name, tr=512):
    rows, d = y.shape

    def body(y_ref, t_ref, dy_ref, l_ref):
        @pl.when(pl.program_id(0) == 0)
        def _():
            l_ref[...] = jnp.zeros_like(l_ref)

        diff = y_ref[...] - t_ref[...]
        dy_ref[...] = diff * (1.0 / d)
        part = 0.5 * jnp.sum(jnp.mean(diff * diff, axis=-1, keepdims=True), axis=0, keepdims=True)
        l_ref[...] += jnp.broadcast_to(part, l_ref.shape)

    return pl.pallas_call(
        body, grid=(rows // tr,), in_specs=[_rows_spec(tr, d)] * 2,
        out_specs=(_rows_spec(tr, d), _full_spec((1, LANES))),
        out_shape=(jax.ShapeDtypeStruct((rows, d), F32), jax.ShapeDtypeStruct((1, LANES), F32)),
        compiler_params=_cparams(("arbitrary",)), name=name)(y, target)


def _adam(recvs, w, m, v, *, name):
    nl = len(recvs)
    rows_l, cols = recvs[0].shape[1:]
    assert w.shape == (nl * rows_l, cols)
    tr = _pick(rows_l, (128, 64, 32, 16, 8))
    nt = rows_l // tr
    c1 = 1.0 / (1.0 - ADAM_B1 ** ADAM_STEP)
    c2 = 1.0 / (1.0 - ADAM_B2 ** ADAM_STEP)

    def body(*refs):
        r_refs = refs[:nl]
        w_ref, m_ref, v_ref, g_out, d_out, m_out, v_out = refs[nl:]
        layer = pl.program_id(0)

        def update(r_ref):
            g = r_ref[0].astype(F32)
            for j in range(1, N_DEV):
                g = g + r_ref[j].astype(F32)
            mn = ADAM_B1 * m_ref[...] + (1.0 - ADAM_B1) * g
            vn = ADAM_B2 * v_ref[...] + (1.0 - ADAM_B2) * (g * g)
            g_out[...] = g
            m_out[...] = mn
            v_out[...] = vn
            d_out[...] = -ADAM_LR * ((mn * c1) / (jnp.sqrt(vn * c2) + ADAM_EPS) + ADAM_WD * w_ref[...])

        for k in range(nl):
            pl.when(layer == k)(functools.partial(update, r_refs[k]))

    spec = pl.BlockSpec((tr, cols), lambda l, i: (l * nt + i, 0))
    rspec = lambda k: pl.BlockSpec((N_DEV, tr, cols), lambda l, i: (0, jnp.where(l == k, i, 0), 0))
    sds = jax.ShapeDtypeStruct(w.shape, F32)
    return pl.pallas_call(
        body, grid=(nl, nt), in_specs=[rspec(k) for k in range(nl)] + [spec, spec, spec],
        out_specs=(spec,) * 4, out_shape=(sds,) * 4,
        compiler_params=_cparams(("arbitrary", "arbitrary")), name=name)(*recvs, w, m, v)


def _band_mask(i, maxdist, nrow_blocks):
    row = lax.broadcasted_iota(jnp.int32, (BLOCK, 2 * BLOCK), 0)
    col = lax.broadcasted_iota(jnp.int32, (BLOCK, 2 * BLOCK), 1)
    dist = row - col + BLOCK
    return (dist >= 0) & (dist <= maxdist) & ((i > 0) | (col >= BLOCK))


def _band_views(q, k, zv, dil):
    z, kvw, voff = zv
    if dil == 1:
        return q, k, z, z.shape[1] // kvw, voff
    vd = z[:, voff * kvw:(voff + 1) * kvw].reshape(z.shape[0] // dil, dil * kvw)
    return q, k, vd, 1, 0


def _band_fwd(q, k, zv, *, dil, group, maxdist, sinks=None, name, xchg=()):
    s, qw = q.shape[0] * dil, q.shape[1] // dil
    kvw = zv[1]
    qd, kd, zd, zblocks, voff = _band_views(q, k, zv, dil)
    nb = s // dil // BLOCK
    nheads = qw // 64
    scale = 64 ** -0.5
    has_sink = sinks is not None

    def body(*refs):
        q_ref, kp_ref, kc_ref, vp_ref, vc_ref = refs[:5]
        s_ref = refs[5] if has_sink else None
        o_ref, l_ref, sc_s, p_s = refs[5 + has_sink:]
        i = pl.program_id(1)
        mask = _band_mask(i, maxdist, nb)
        k2 = jnp.concatenate([kp_ref[...], kc_ref[...]], axis=0)
        v2 = jnp.concatenate([vp_ref[...], vc_ref[...]], axis=0).astype(BF16)
        for h in range(nheads):
            g = h // group
            sc_s[h] = _dot(q_ref[:, 64 * h:64 * h + 64], k2[:, 64 * g:64 * g + 64], NT_DIMS)
        sc = jnp.where(mask[None], sc_s[...] * scale, NEG)
        mx = jnp.max(sc, axis=2, keepdims=True)
        if has_sink:
            snk = s_ref[:, :, 0:1]
            mx = jnp.maximum(mx, snk)
        e = jnp.exp(sc - mx)
        den = jnp.sum(e, axis=2, keepdims=True)
        if has_sink:
            den = den + jnp.exp(snk - mx)
        p_s[...] = (e * (1.0 / den)).astype(BF16)
        lse = mx + jnp.log(den)
        for h in range(nheads):
            g = h // group
            o_ref[:, 64 * h:64 * h + 64] = _dot(p_s[h], v2[:, 64 * g:64 * g + 64])
            l_ref[:, 64 * h:64 * h + 64] = jnp.broadcast_to(lse[h], (BLOCK, 64))

    prev = lambda r, i: jnp.maximum(i - 1, 0)
    in_specs = [
        pl.BlockSpec((BLOCK, qw), lambda r, i: (i, r)),
        pl.BlockSpec((BLOCK, kvw), lambda r, i: (prev(r, i), r)),
        pl.BlockSpec((BLOCK, kvw), lambda r, i: (i, r)),
        pl.BlockSpec((BLOCK, kvw), lambda r, i: (prev(r, i), r * zblocks + voff)),
        pl.BlockSpec((BLOCK, kvw), lambda r, i: (i, r * zblocks + voff)),
    ]
    args = [qd, kd, kd, zd, zd]
    if has_sink:
        s_spec, s_arr = _layer_arg(sinks)
        in_specs.append(s_spec)
        args.append(s_arr)
    ospec = pl.BlockSpec((BLOCK, qw), lambda r, i: (i, r))
    sds = jax.ShapeDtypeStruct((s // dil, dil * qw), F32)
    (o, lse), got = _pcall(
        body, grid=(dil, nb), in_specs=in_specs, out_specs=(ospec, ospec), out_shape=(sds, sds),
        scratch_shapes=[pltpu.VMEM((nheads, BLOCK, 2 * BLOCK), F32), pltpu.VMEM((nheads, BLOCK, 2 * BLOCK), BF16)],
        sem=("parallel", "parallel"), name=name, args=args, xchg=xchg)
    return (o, lse), got


def _band_dq(q, k, zv, do, lse, delta, *, dil, group, maxdist, name):
    s, qw = q.shape[0] * dil, q.shape[1] // dil
    kvw = zv[1]
    qd, kd, zd, zblocks, voff = _band_views(q, k, zv, dil)
    nb = s // dil // BLOCK
    nheads = qw // 64
    scale = 64 ** -0.5

    def body(q_ref, kp_ref, kc_ref, vp_ref, vc_ref, do_ref, l_ref, d_ref, dq_ref, sc_s, dp_s, ds_s, st_s):
        i = pl.program_id(1)
        mask = _band_mask(i, maxdist, nb)
        k2 = jnp.concatenate([kp_ref[...], kc_ref[...]], axis=0)
        v2 = jnp.concatenate([vp_ref[...], vc_ref[...]], axis=0).astype(BF16)
        dob = do_ref[...].astype(BF16)
        for h in range(nheads):
            g = h // group
            hs = slice(64 * h, 64 * h + 64)
            sc_s[h] = _dot(q_ref[:, hs], k2[:, 64 * g:64 * g + 64], NT_DIMS)
            dp_s[h] = _dot(dob[:, hs], v2[:, 64 * g:64 * g + 64], NT_DIMS)
            st_s[0, h] = l_ref[:, 64 * h:64 * h + 1]
            st_s[1, h] = d_ref[:, 64 * h:64 * h + 1]
        p = jnp.exp(jnp.where(mask[None], sc_s[...] * scale, NEG) - st_s[0])
        ds_s[...] = (p * (dp_s[...] - st_s[1]) * scale).astype(BF16)
        for h in range(nheads):
            g = h // group
            dq_ref[:, 64 * h:64 * h + 64] = _dot(ds_s[h], k2[:, 64 * g:64 * g + 64])

    prev = lambda r, i: jnp.maximum(i - 1, 0)
    qspec = pl.BlockSpec((BLOCK, qw), lambda r, i: (i, r))
    in_specs = [
        qspec,
        pl.BlockSpec((BLOCK, kvw), lambda r, i: (prev(r, i), r)),
        pl.BlockSpec((BLOCK, kvw), lambda r, i: (i, r)),
        pl.BlockSpec((BLOCK, kvw), lambda r, i: (prev(r, i), r * zblocks + voff)),
        pl.BlockSpec((BLOCK, kvw), lambda r, i: (i, r * zblocks + voff)),
        qspec, qspec, qspec,
    ]
    dq = pl.pallas_call(
        body, grid=(dil, nb), in_specs=in_specs, out_specs=qspec,
        out_shape=jax.ShapeDtypeStruct((s // dil, dil * qw), F32),
        scratch_shapes=[pltpu.VMEM((nheads, BLOCK, 2 * BLOCK), F32), pltpu.VMEM((nheads, BLOCK, 2 * BLOCK), F32),
                        pltpu.VMEM((nheads, BLOCK, 2 * BLOCK), BF16), pltpu.VMEM((2, nheads, BLOCK, 1), F32)],
        compiler_params=_cparams(("parallel", "parallel")), name=name)(qd, kd, kd, zd, zd, do, lse, delta)
    return dq


def _band_dkv(q, k, zv, do, lse, delta, *, dil, group, maxdist, slab=None, name):
    s, qw = q.shape[0] * dil, q.shape[1] // dil
    kvw = zv[1]
    qd, kd, zd, zblocks, voff = _band_views(q, k, zv, dil)
    nb = s // dil // BLOCK
    nheads = qw // 64
    nkv = kvw // 64
    scale = 64 ** -0.5

    def body(k_ref, v_ref, q0, q1, do0, do1, l0, l1, d0, d1, dk_ref, dv_ref, sc_s, dp_s, p_s, ds_s, st_s):
        j = pl.program_id(1)
        row = lax.broadcasted_iota(jnp.int32, (2 * BLOCK, BLOCK), 0)
        col = lax.broadcasted_iota(jnp.int32, (2 * BLOCK, BLOCK), 1)
        dist = row - col
        mask = (dist >= 0) & (dist <= maxdist) & ((j < nb - 1) | (row < BLOCK))
        q2 = jnp.concatenate([q0[...], q1[...]], axis=0)
        do2 = jnp.concatenate([do0[...], do1[...]], axis=0).astype(BF16)
        vb = v_ref[...].astype(BF16)
        for h in range(nheads):
            g = h // group
            hs = slice(64 * h, 64 * h + 64)
            sc_s[h] = _dot(q2[:, hs], k_ref[:, 64 * g:64 * g + 64], NT_DIMS)
            dp_s[h] = _dot(do2[:, hs], vb[:, 64 * g:64 * g + 64], NT_DIMS)
            st_s[0, h, 0:BLOCK] = l0[:, 64 * h:64 * h + 1]
            st_s[0, h, BLOCK:2 * BLOCK] = l1[:, 64 * h:64 * h + 1]
            st_s[1, h, 0:BLOCK] = d0[:, 64 * h:64 * h + 1]
            st_s[1, h, BLOCK:2 * BLOCK] = d1[:, 64 * h:64 * h + 1]
        p = jnp.exp(jnp.where(mask[None], sc_s[...] * scale, NEG) - st_s[0])
        p_s[...] = p.astype(BF16)
        ds_s[...] = (p * (dp_s[...] - st_s[1]) * scale).astype(BF16)
        dks = [None] * nkv
        dvs = [None] * nkv
        for h in range(nheads):
            g = h // group
            hs = slice(64 * h, 64 * h + 64)
            dvh = _dot(p_s[h], do2[:, hs], TN_DIMS)
            dkh = _dot(ds_s[h], q2[:, hs], TN_DIMS)
            dks[g] = dkh if dks[g] is None else dks[g] + dkh
            dvs[g] = dvh if dvs[g] is None else dvs[g] + dvh
        for g in range(nkv):
            dk_ref[:, 64 * g:64 * g + 64] = dks[g]
            dv_ref[:, 64 * g:64 * g + 64] = dvs[g].astype(BF16)

    nxt = lambda r, j: jnp.minimum(j + 1, nb - 1)
    q0s = pl.BlockSpec((BLOCK, qw), lambda r, j: (j, r))
    q1s = pl.BlockSpec((BLOCK, qw), lambda r, j: (nxt(r, j), r))
    kspec = pl.BlockSpec((BLOCK, kvw), lambda r, j: (j, r))
    in_specs = [kspec, pl.BlockSpec((BLOCK, kvw), lambda r, j: (j, r * zblocks + voff)),
                q0s, q1s, q0s, q1s, q0s, q1s, q0s, q1s]
    args = [kd, zd, qd, qd, do, do, lse, lse, delta, delta]
    n_in = len(args)
    dv_spec, dv_shape, aliases, has_prev = kspec, jax.ShapeDtypeStruct((s // dil, dil * kvw), BF16), {}, 0
    if slab is not None:
        assert dil == 1
        dv_spec = in_specs[1]
        dv_shape, aliases, has_prev = _slab_out(slab, s, in_specs, args, 1)

    def body_skip_slab(*refs):
        return body(*refs[:n_in], *refs[n_in + has_prev:])

    dk, dv = pl.pallas_call(
        body_skip_slab, grid=(dil, nb), in_specs=in_specs, out_specs=(kspec, dv_spec),
        out_shape=(jax.ShapeDtypeStruct((s // dil, dil * kvw), F32), dv_shape), input_output_aliases=aliases,
        scratch_shapes=[pltpu.VMEM((nheads, 2 * BLOCK, BLOCK), F32), pltpu.VMEM((nheads, 2 * BLOCK, BLOCK), F32),
                        pltpu.VMEM((nheads, 2 * BLOCK, BLOCK), BF16), pltpu.VMEM((nheads, 2 * BLOCK, BLOCK), BF16),
                        pltpu.VMEM((2, nheads, 2 * BLOCK, 1), F32)],
        compiler_params=_cparams(("parallel", "parallel")), name=name)(*args)
    return dk, (dv if slab is not None else dv.reshape(s, kvw))


HPG = 2


def _flash_tiles(sq, sk):
    return _pick(sq, (512, 256, 128)), _pick(sk, (512, 256, 128))


def _diag_mask(t):
    return lax.broadcasted_iota(jnp.int32, (t, t), 1) <= lax.broadcasted_iota(jnp.int32, (t, t), 0)


def _attn_fwd(q, k, vv, *, dv, causal, scale, name, xchg=None):
    sq, qw = q.shape
    sk = k.shape[0]
    varr, vbw, voff = vv
    assert vbw == HPG * dv
    ngrp = qw // (HPG * LANES)
    tq, tk = _flash_tiles(sq, sk)
    nq, nk = sq // tq, sk // tk
    if causal:
        assert tq == tk and sq == sk

    def body(q_ref, k_ref, v_ref, o_ref, l_ref, m_s, l_s, acc_s):
        i = pl.program_id(1)
        m_s[...] = jnp.full_like(m_s, NEG)
        l_s[...] = jnp.zeros_like(l_s)
        acc_s[...] = jnp.zeros_like(acc_s)

        def step(j, masked):
            rows = pl.ds(pl.multiple_of(j * tk, tk), tk)
            if masked:
                mask = _diag_mask(tq)
            for h in range(HPG):
                hs = slice(LANES * h, LANES * (h + 1))
                sc = _dot(q_ref[:, hs], k_ref[rows, hs], NT_DIMS) * scale
                if masked:
                    sc = jnp.where(mask, sc, NEG)
                m_prev = m_s[h]
                m_new = jnp.maximum(m_prev, jnp.max(sc, axis=1, keepdims=True))
                alpha = jnp.exp(m_prev - m_new)
                p = jnp.exp(sc - m_new[:, 0:1])
                l_s[h] = alpha * l_s[h] + jnp.sum(p, axis=1, keepdims=True)
                acc_s[h] = acc_s[h] * alpha[:, 0:dv] + _dot(p, v_ref[rows, dv * h:dv * (h + 1)])
                m_s[h] = m_new

        def loop_body(j, carry):
            step(j, False)
            return carry

        if causal:
            lax.fori_loop(0, i, loop_body, 0)
            step(i, True)
        else:
            lax.fori_loop(0, nk, loop_body, 0)
        for h in range(HPG):
            o_ref[:, dv * h:dv * (h + 1)] = acc_s[h] * (1.0 / l_s[h][:, 0:dv])
            l_ref[:, LANES * h:LANES * (h + 1)] = m_s[h] + jnp.log(l_s[h])

    in_specs = [
        pl.BlockSpec((tq, HPG * LANES), lambda g, i: (i, g)),
        pl.BlockSpec((sk, HPG * LANES), lambda g, i: (0, g)),
        pl.BlockSpec((sk, vbw), lambda g, i: (0, voff + g)),
    ]
    return _pcall(
        body, grid=(ngrp, nq), in_specs=in_specs,
        out_specs=(pl.BlockSpec((tq, HPG * dv), lambda g, i: (i, g)),
                   pl.BlockSpec((tq, HPG * LANES), lambda g, i: (i, g))),
        out_shape=(jax.ShapeDtypeStruct((sq, ngrp * HPG * dv), F32), jax.ShapeDtypeStruct((sq, qw), F32)),
        scratch_shapes=[pltpu.VMEM((HPG, tq, LANES), F32), pltpu.VMEM((HPG, tq, LANES), F32),
                        pltpu.VMEM((HPG, tq, dv), F32)],
        sem=("parallel", "arbitrary"), name=name, args=(q, k, varr), xchg=xchg)


def _attn_dq(q, k, vv, do, lse, delta, *, dv, causal, scale, name, xchg=None):
    sq, qw = q.shape
    sk = k.shape[0]
    varr, vbw, voff = vv
    ngrp = qw // (HPG * LANES)
    tq, tk = _flash_tiles(sq, sk)
    nq, nk = sq // tq, sk // tk

    def body(q_ref, k_ref, v_ref, do_ref, l_ref, d_ref, dq_ref, acc_s):
        i = pl.program_id(1)
        acc_s[...] = jnp.zeros_like(acc_s)

        def step(j, masked):
            rows = pl.ds(pl.multiple_of(j * tk, tk), tk)
            if masked:
                mask = _diag_mask(tq)
            for h in range(HPG):
                hs = slice(LANES * h, LANES * (h + 1))
                kh = k_ref[rows, hs]
                sc = _dot(q_ref[:, hs], kh, NT_DIMS) * scale
                if masked:
                    sc = jnp.where(mask, sc, NEG)
                p = jnp.exp(sc - l_ref[:, LANES * h:LANES * h + 1])
                dp = _dot(do_ref[:, dv * h:dv * (h + 1)], v_ref[rows, dv * h:dv * (h + 1)], NT_DIMS)
                ds = p * (dp - d_ref[:, dv * h:dv * h + 1]) * scale
                acc_s[h] += _dot(ds, kh)

        def loop_body(j, carry):
            step(j, False)
            return carry

        if causal:
            lax.fori_loop(0, i, loop_body, 0)
            step(i, True)
        else:
            lax.fori_loop(0, nk, loop_body, 0)
        for h in range(HPG):
            dq_ref[:, LANES * h:LANES * (h + 1)] = acc_s[h]

    qspec = pl.BlockSpec((tq, HPG * LANES), lambda g, i: (i, g))
    ospec = pl.BlockSpec((tq, HPG * dv), lambda g, i: (i, g))
    in_specs = [
        qspec,
        pl.BlockSpec((sk, HPG * LANES), lambda g, i: (0, g)),
        pl.BlockSpec((sk, vbw), lambda g, i: (0, voff + g)),
        ospec, qspec, ospec,
    ]
    return _pcall(
        body, grid=(ngrp, nq), in_specs=in_specs, out_specs=qspec,
        out_shape=jax.ShapeDtypeStruct((sq, qw), F32),
        scratch_shapes=[pltpu.VMEM((HPG, tq, LANES), F32)],
        sem=("parallel", "arbitrary"), name=name, args=(q, k, varr, do, lse, delta), xchg=xchg)


def _attn_dkv(q, k, vv, do, lse, delta, *, dv, causal, scale, name, xchg=None):
    sq, qw = q.shape
    sk = k.shape[0]
    varr, vbw, voff = vv
    ngrp = qw // (HPG * LANES)
    tq, tk = _flash_tiles(sq, sk)
    nq, nk = sq // tq, sk // tk

    def body(q_ref, k_ref, v_ref, do_ref, l_ref, d_ref, dk_ref, dv_ref, dk_s, dv_s):
        j = pl.program_id(1)
        dk_s[...] = jnp.zeros_like(dk_s)
        dv_s[...] = jnp.zeros_like(dv_s)

        def step(i, masked):
            rows = pl.ds(pl.multiple_of(i * tq, tq), tq)
            if masked:
                mask = _diag_mask(tq)
            for h in range(HPG):
                qh = q_ref[rows, LANES * h:LANES * (h + 1)]
                doh = do_ref[rows, dv * h:dv * (h + 1)]
                sc = _dot(qh, k_ref[:, LANES * h:LANES * (h + 1)], NT_DIMS) * scale
                if masked:
                    sc = jnp.where(mask, sc, NEG)
                p = jnp.exp(sc - l_ref[rows, LANES * h:LANES * h + 1])
                dv_s[h] += _dot(p, doh, TN_DIMS)
                dp = _dot(doh, v_ref[:, dv * h:dv * (h + 1)], NT_DIMS)
                ds = p * (dp - d_ref[rows, dv * h:dv * h + 1]) * scale
                dk_s[h] += _dot(ds, qh, TN_DIMS)

        def loop_body(i, carry):
            step(i, False)
            return carry

        if causal:
            step(j, True)
            lax.fori_loop(j + 1, nq, loop_body, 0)
        else:
            lax.fori_loop(0, nq, loop_body, 0)
        for h in range(HPG):
            dk_ref[:, LANES * h:LANES * (h + 1)] = dk_s[h]
            dv_ref[:, dv * h:dv * (h + 1)] = dv_s[h].astype(BF16)

    qspec = pl.BlockSpec((sq, HPG * LANES), lambda g, j: (0, g))
    ospec = pl.BlockSpec((sq, HPG * dv), lambda g, j: (0, g))
    kspec = pl.BlockSpec((tk, HPG * LANES), lambda g, j: (j, g))
    in_specs = [qspec, kspec, pl.BlockSpec((tk, vbw), lambda g, j: (j, voff + g)), ospec, qspec, ospec]
    return _pcall(
        body, grid=(ngrp, nk), in_specs=in_specs,
        out_specs=(kspec, pl.BlockSpec((tk, HPG * dv), lambda g, j: (j, g))),
        out_shape=(jax.ShapeDtypeStruct((sk, qw), F32), jax.ShapeDtypeStruct((sk, ngrp * HPG * dv), BF16)),
        scratch_shapes=[pltpu.VMEM((HPG, tk, LANES), F32), pltpu.VMEM((HPG, tk, dv), F32)],
        sem=("parallel", "arbitrary"), name=name, args=(q, k, varr, do, lse, delta), xchg=xchg)


def _xchg_copies(src_ref, out_ref, send_sems, recv_sems, local_sems, k, scatter):
    me = 4 * lax.axis_index("x") + 2 * lax.axis_index("y") + lax.axis_index("c")
    own = src_ref.at[me] if scatter else src_ref
    copies = [pltpu.make_async_copy(own, out_ref.at[me], local_sems.at[k])]
    for d in range(1, N_DEV):
        peer = (me + d) % N_DEV
        piece = src_ref.at[peer] if scatter else src_ref
        sem = (N_DEV - 1) * k + d - 1
        copies.append(pltpu.make_async_remote_copy(
            src_ref=piece, dst_ref=out_ref.at[me], send_sem=send_sems.at[sem], recv_sem=recv_sems.at[sem],
            device_id=(peer // 4, (peer // 2) % 2, peer % 2), device_id_type=pl.DeviceIdType.MESH))
    return copies


def _xchg_shapes(xchg):
    return [jax.ShapeDtypeStruct((N_DEV,) + tuple(src.shape[1:] if scatter else src.shape), src.dtype)
            for src, scatter in xchg]


def _xchg_sems(nx):
    return [pltpu.SemaphoreType.DMA(((N_DEV - 1) * nx,)), pltpu.SemaphoreType.DMA(((N_DEV - 1) * nx,)),
            pltpu.SemaphoreType.DMA((nx,))]


ANY_SPEC = pl.BlockSpec(memory_space=pl.ANY)


def _exchange_many(xchg, *, name):
    nx = len(xchg)

    def body(*refs):
        srcs, outs = refs[:nx], refs[nx:2 * nx]
        send_sems, recv_sems, local_sems = refs[2 * nx:]
        copies = [c for k, (_, scatter) in enumerate(xchg)
                  for c in _xchg_copies(srcs[k], outs[k], send_sems, recv_sems, local_sems, k, scatter)]
        for c in copies:
            c.start()
        for c in copies:
            c.wait()

    return pl.pallas_call(
        body, in_specs=[ANY_SPEC] * nx, out_specs=tuple([ANY_SPEC] * nx), out_shape=tuple(_xchg_shapes(xchg)),
        scratch_shapes=_xchg_sems(nx), name=name)(*[src for src, _ in xchg])


def _pcall(body, *, grid, in_specs, out_specs, out_shape, scratch_shapes=(), sem, name, args, xchg=None):
    single = not isinstance(out_shape, (tuple, list))
    if not xchg:
        res = pl.pallas_call(
            body, grid=grid, in_specs=list(in_specs), out_specs=out_specs, out_shape=out_shape,
            scratch_shapes=list(scratch_shapes), compiler_params=_cparams(sem), name=name)(*args)
        return res if xchg is None else (res, [])
    out_specs_l = [out_specs] if single else list(out_specs)
    out_shape_l = [out_shape] if single else list(out_shape)
    nx, n_in, n_out, n_scr = len(xchg), len(in_specs), len(out_shape_l), len(scratch_shapes)

    def wrapped(*refs):
        ins, xin = refs[:n_in], refs[n_in:n_in + nx]
        pos = n_in + nx
        outs, xout = refs[pos:pos + n_out], refs[pos + n_out:pos + n_out + nx]
        pos += n_out + nx
        scr = refs[pos:pos + n_scr]
        send_sems, recv_sems, local_sems = refs[pos + n_scr:]
        first = functools.reduce(jnp.logical_and, [pl.program_id(a) == 0 for a in range(len(grid))])
        last = functools.reduce(jnp.logical_and, [pl.program_id(a) == grid[a] - 1 for a in range(len(grid))])

        def copies():
            return [c for k, (_, scatter) in enumerate(xchg)
                    for c in _xchg_copies(xin[k], xout[k], send_sems, recv_sems, local_sems, k, scatter)]

        @pl.when(first)
        def _():
            for c in copies():
                c.start()

        body(*ins, *outs, *scr)

        @pl.when(last)
        def _():
            for c in copies():
                c.wait()

    res = pl.pallas_call(
        wrapped, grid=grid, in_specs=list(in_specs) + [ANY_SPEC] * nx,
        out_specs=tuple(out_specs_l + [ANY_SPEC] * nx), out_shape=tuple(out_shape_l + _xchg_shapes(xchg)),
        scratch_shapes=list(scratch_shapes) + _xchg_sems(nx),
        compiler_params=_cparams(("arbitrary",) * len(grid)), name=name + "_x")(*args, *[src for src, _ in xchg])
    main = res[0] if single else tuple(res[:n_out])
    return main, list(res[n_out:])


def _unshard(wg, axis):
    nd = wg.ndim - 1
    perm = list(range(1, axis + 1)) + [0] + list(range(axis + 1, nd + 1))
    t = wg.transpose(perm)
    shp = list(wg.shape[1:])
    shp[axis] *= N_DEV
    return t.reshape(shp)


def _shard_major(full, axis):
    shp = list(full.shape)
    shp[axis:axis + 1] = [N_DEV, shp[axis] // N_DEV]
    t = full.reshape(shp)
    perm = [axis] + list(range(axis)) + list(range(axis + 1, len(shp)))
    return t.transpose(perm)


BIG = ("w_in", "b_w_uq", "b_w_ukv", "m_w_kv", "w_branch", "w_out", "w_up", "w_down")
BIG_AXIS = dict(w_in=2, b_w_uq=2, b_w_ukv=2, m_w_kv=1, w_branch=3, w_out=1, w_up=2, w_down=1)
SMALL = ("g_mix", "b_gate", "a_qn", "a_kn", "a_sink", "b_qa_norm", "b_kva_norm", "b_qn", "b_kn", "c_qn", "c_kn",
         "m_g_mem", "m_qn", "m_kn", "g_mlp")
WEIGHTS = ("g_mix", "w_in", "b_gate", "a_qn", "a_kn", "a_sink", "b_qa_norm", "b_kva_norm", "b_w_uq", "b_w_ukv", "b_qn",
           "b_kn", "c_qn", "c_kn", "m_g_mem", "m_w_kv", "m_qn", "m_kn", "w_branch", "w_out", "g_mlp", "w_up", "w_down")


def _rope_tables(positions, dim):
    inv = ROPE_THETA ** (-jnp.arange(0, dim, 2, dtype=F32) / dim)
    ang = positions.astype(F32)[:, None] * inv
    return jnp.cos(ang), jnp.sin(ang)


def _lane_mask(lo, hi):
    lane = np.arange(LANES)
    return jnp.asarray(((lane >= lo) & (lane < hi)).astype(np.float32)[None, :])


def _step(x, mem, positions, w, moments_m, moments_v, loss_target):
    s_len, d_model = x.shape
    n_layers = w["g_mix"].shape[0]
    lay, orig_names, order, n_in, nw = _in_layout(d_model)
    unit = lambda name: lay[name]["off"] // lay[name]["nw"]

    def ag_src(l, names):
        return [(w[n][l].astype(BF16), False) for n in names]

    def layout(name, block):
        full = _unshard(block, BIG_AXIS[name] - 1)
        if name == "w_in":
            return dict(w_in=_permute_cols(full, lay, order))
        if name == "b_w_uq":
            uq = full.reshape(B_Q_LORA, B_HEADS, B_NOPE + B_ROPE)
            return dict(uq=jnp.pad(uq, ((0, 0), (0, 0), (0, LANES - B_NOPE - B_ROPE))).reshape(B_Q_LORA, B_HEADS * LANES))
        if name == "b_w_ukv":
            ukv = full.reshape(B_KV_LORA, B_HEADS, B_NOPE + B_V)
            return dict(ukv=jnp.concatenate([
                jnp.pad(ukv[..., :B_NOPE], ((0, 0), (0, 0), (0, LANES - B_NOPE))).reshape(B_KV_LORA, B_HEADS * LANES),
                ukv[..., B_NOPE:].reshape(B_KV_LORA, B_HEADS * B_V)], axis=-1))
        return {name: full}

    def rs_src(name, g):
        return (_shard_major(g, BIG_AXIS[name] - 1).astype(BF16), True)

    p64 = _pattern([64, 64])
    p128 = _pattern([128])
    pq = _pattern([64, 32, 32])
    pkr = _pattern([32, 32, 32, 32])
    cos_h, sin_h = _rope_tables(positions, A_HD)
    cos_r, sin_r = _rope_tables(positions, B_ROPE)
    ones = lambda n: jnp.ones((s_len, n), F32)
    zeros = lambda n: jnp.zeros((s_len, n), F32)
    rope64 = (jnp.tile(cos_h, (1, 4)), jnp.tile(jnp.concatenate([-sin_h, sin_h], axis=1), (1, 2)),
              jnp.asarray((np.arange(LANES) % 64 < 32).astype(np.float32)[None, :]), 32)
    rope_q = (jnp.concatenate([ones(64), cos_r, cos_r, ones(32)], axis=1),
              jnp.concatenate([zeros(64), -sin_r, sin_r, zeros(32)], axis=1), _lane_mask(64, 80), 16)
    rope_k = (jnp.concatenate([cos_r, cos_r, ones(96)], axis=1),
              jnp.concatenate([-sin_r, sin_r, zeros(96)], axis=1), _lane_mask(0, 16), 16)

    scale_b = (B_NOPE + B_ROPE) ** -0.5
    scale_m = M_HD ** -0.5

    n_l = n_layers
    per_head = lambda g, nh: jnp.tile(g, (1, nh))[:, None, :]
    lpad = lambda g, n: jnp.pad(g, ((0, 0), (0, n)))
    gains_all = dict(
        g_mix=w["g_mix"][:, None, :], g_mlp=w["g_mlp"][:, None, :], m_g_mem=w["m_g_mem"][:, None, :],
        b_gate=w["b_gate"][:, None, :], b_qa_norm=w["b_qa_norm"][:, None, :], b_kva_norm=w["b_kva_norm"][:, None, :],
        a_q=per_head(w["a_qn"], A_HEADS), a_k=per_head(w["a_kn"], A_KV_HEADS),
        b_q=per_head(lpad(w["b_qn"], 32), B_HEADS), b_k=per_head(lpad(w["b_kn"][:, :B_NOPE], 64), B_HEADS),
        b_kr=lpad(w["b_kn"][:, B_NOPE:], 96)[:, None, :],
        c_q=[per_head(w["c_qn"][:, g], C_HEADS) for g in range(3)],
        c_k=[per_head(w["c_kn"][:, g], C_HEADS) for g in range(3)],
        m_q=per_head(w["m_qn"], M_HEADS), m_k=per_head(w["m_kn"], M_HEADS),
        sink_row=jnp.repeat(w["a_sink"], A_HD, axis=1)[:, None, :],
        sinks=jnp.broadcast_to(w["a_sink"][:, :, None, None], (n_l, A_HEADS, 1, LANES)))

    def gains(l):
        return {k: ([(a, l) for a in v] if isinstance(v, list) else (v, l)) for k, v in gains_all.items()}

    ag_hosts = dict(mm_in=("w_up",), band_a=("w_down",), flash_b=("w_in",), mm_up=("w_branch", "w_out"),
                    mm_down=("m_w_kv", "b_w_uq", "b_w_ukv"))
    first_alone = ("w_in",)
    first_hosts = dict(mm_in=("b_w_uq", "b_w_ukv"), band_c0=("w_up",), band_c1=("w_down",),
                       band_c2=("w_branch", "w_out", "m_w_kv"))
    saved = []
    weights = [dict() for _ in range(n_layers)]

    def arrived(layer, names, blocks):
        for name, blk in zip(names, blocks):
            weights[layer].update(layout(name, blk))

    cur = x
    arrived(0, first_alone, _exchange_many(ag_src(0, first_alone), name="ag_first"))
    for l in range(n_layers):
        gn = gains(l)
        sv = dict(x=cur)
        wl = weights[l]

        def carry(host):
            return ag_src(l + 1, ag_hosts[host]) if l + 1 < n_layers else []

        def carry0(host):
            return ag_src(0, first_hosts[host]) if l == 0 else []

        h = _rmsnorm_fwd((cur, d_model, 0), gn["g_mix"], name="f_norm_mix")
        nxt_w = carry("mm_in")
        z, got = _mm(h, wl["w_in"], name="f_mm_in", xchg=nxt_w + carry0("mm_in"))
        arrived(l + 1, ag_hosts["mm_in"], got[:len(nxt_w)])
        arrived(0, first_hosts["mm_in"], got[len(nxt_w):])
        sv.update(h=h, z=z)
        qa = _headnorm_fwd((z, 512, unit("a_q")), gn["a_q"], p64, rope64, name="f_hn_aq")
        ka = _headnorm_fwd((z, 128, unit("a_k")), gn["a_k"], p64, rope64, name="f_hn_ak")
        va = (z, 128, unit("a_v"))
        (o_a, lse_a), got = _band_fwd(qa, ka, va, dil=1, group=A_HEADS // A_KV_HEADS, maxdist=A_WINDOW - 1,
                                      sinks=gn["sinks"], name="f_band_a", xchg=carry("band_a"))
        arrived(l + 1, ag_hosts["band_a"], got)
        sv.update(qa=qa, ka=ka, o_a=o_a, lse_a=lse_a)
        cq = _rmsnorm_fwd((z, B_Q_LORA, unit("b_cq")), gn["b_qa_norm"], name="f_norm_cq")
        q_up = _mm(cq, wl["uq"], name="f_mm_uq")
        ckv = _rmsnorm_fwd((z, B_KV_LORA, unit("b_ckv")), gn["b_kva_norm"], name="f_norm_ckv")
        kv_up = _mm(ckv, wl["ukv"], name="f_mm_ukv")
        kp = _headnorm_fwd((z, LANES, unit("b_kr")), gn["b_kr"], pkr, rope_k, name="f_hn_kr")
        qb = _headnorm_fwd((q_up, B_HEADS * LANES, 0), gn["b_q"], pq, rope_q, name="f_hn_bq")
        kb = _headnorm_fwd((kv_up, B_HEADS * LANES, 0), gn["b_k"], p64, None, addend=kp, name="f_hn_bk")
        vb = (kv_up, HPG * B_V, B_HEADS * LANES // (HPG * B_V))
        (o_b, lse_b), got = _attn_fwd(qb, kb, vb, dv=B_V, causal=True, scale=scale_b, name="f_flash_b",
                                       xchg=carry("flash_b"))
        arrived(l + 1, ag_hosts["flash_b"], got)
        sv.update(cq=cq, q_up=q_up, ckv=ckv, kv_up=kv_up, qb=qb, kb=kb, o_b=o_b, lse_b=lse_b)
        qc, kc, o_cg, lse_cg = [], [], [], []
        for g, (win, dil) in enumerate(C_PATTERNS):
            qg = _headnorm_fwd((z, 512, unit("c%d" % (3 * g))), gn["c_q"][g], p64, rope64, dil=dil,
                               name="f_hn_cq%d" % g)
            kg = _headnorm_fwd((z, 512, unit("c%d" % (3 * g + 1))), gn["c_k"][g], p64, rope64, dil=dil,
                               name="f_hn_ck%d" % g)
            host = "band_c%d" % g
            (og, lg), got = _band_fwd(qg, kg, (z, 512, unit("c%d" % (3 * g + 2))), dil=dil, group=1,
                                      maxdist=win // dil, name="f_band_c%d" % g,
                                      xchg=carry0(host) if host in first_hosts else ())
            arrived(0, first_hosts.get(host, ()), got)
            qc.append(qg), kc.append(kg), o_cg.append(og), lse_cg.append(lg)
        c_dils = [dil for _, dil in C_PATTERNS]
        o_c = _combine_fwd(o_cg, lse_cg, c_dils, rows=s_len, width=BRANCH_W, name="f_combine")
        sv.update(qc=qc, kc=kc, o_cg=o_cg, lse_cg=lse_cg, o_c=o_c)
        memn = _rmsnorm_fwd((mem, d_model, 0), gn["m_g_mem"], name="f_norm_mem")
        mkv = _mm(memn, wl["m_w_kv"], name="f_mm_mkv")
        mk = _headnorm_fwd((mkv, M_HEADS * M_HD, 0), gn["m_k"], p128, None, name="f_hn_mk")
        mq = _headnorm_fwd((z, M_HEADS * M_HD, unit("m_q")), gn["m_q"], p128, None, name="f_hn_mq")
        mv = (mkv, HPG * M_HD, M_HEADS * M_HD // (HPG * M_HD))
        o_m, lse_m = _attn_fwd(mq, mk, mv, dv=M_HD, causal=False, scale=scale_m, name="f_flash_m")
        sv.update(memn=memn, mkv=mkv, mk=mk, mq=mq, o_m=o_m, lse_m=lse_m)
        outs = (o_a, o_b, o_c, o_m)
        ys = [_mm(outs[n], wl["w_branch"][n], name="f_mm_branch") for n in range(N_BRANCH)]
        t = _merge_fwd((z, N_BRANCH * d_model, 0), gn["b_gate"], ys, name="f_merge")
        x2 = _mm(t, wl["w_out"], mode="add", extra=cur, name="f_mm_out")
        h2 = _rmsnorm_fwd((x2, d_model, 0), gn["g_mlp"], name="f_norm_mlp")
        (u, act), got = _mm(h2, wl["w_up"], mode="relu2", name="f_mm_up", xchg=carry("mm_up"))
        arrived(l + 1, ag_hosts["mm_up"], got)
        x3, got = _mm(act, wl["w_down"], mode="add", extra=x2, name="f_mm_down", xchg=carry("mm_down"))
        arrived(l + 1, ag_hosts["mm_down"], got)
        sv.update(ys=ys, t=t, x2=x2, h2=h2, u=u, act=act)
        saved.append(sv)
        cur = x3

    dx, loss_row = _loss_head(cur, loss_target, name="loss_head")
    loss = lax.psum(loss_row[0, 0], AXES)

    recvs = {n: [None] * n_layers for n in BIG}
    rows = {}

    def keep(key, layer, row):
        rows.setdefault(key, [None] * n_layers)[layer] = row

    def item(name, layer, g):
        return (name, layer, rs_src(name, g))

    def srcs(items):
        return [it[2] for it in items]

    def store(items, got):
        for (name, layer, _), r in zip(items, got):
            recvs[name][layer] = r

    carried = []
    for l in reversed(range(n_layers)):
        sv = saved[l]
        wl = weights[l]
        gn = gains(l)
        z = sv["z"]
        g_wdown = _mm(sv["act"], dx, ta=True, name="b_mm_wdown")
        du = _mm(dx, wl["w_down"], tb=True, mode="drelu2", extra=sv["u"], out_dtype=BF16, name="b_mm_du")
        g_wup = _mm(sv["h2"], du, ta=True, name="b_mm_wup")
        dh2 = _mm(du, wl["w_up"], tb=True, name="b_mm_dh2")
        dx2, dg = _rmsnorm_bwd(dh2, (sv["x2"], d_model, 0), gn["g_mlp"], dres=dx, name="b_norm_mlp")
        keep("g_mlp", l, dg)
        g_wout = _mm(sv["t"], dx2, ta=True, name="b_mm_wout")
        dt = _mm(dx2, wl["w_out"], tb=True, name="b_mm_dt")
        dys, dz, dbg = _merge_bwd(dt, (z, N_BRANCH * d_model, 0), gn["b_gate"], sv["ys"], slab_width=nw,
                                  name="b_merge")
        keep("b_gate", l, dbg)
        outs = (sv["o_a"], sv["o_b"], sv["o_c"], sv["o_m"])
        g_wbranch = jnp.stack([_mm(outs[n], dys[n], ta=True, name="b_mm_wbranch") for n in range(N_BRANCH)])
        do_a, do_b, do_c, do_m = [_mm(dys[n], wl["w_branch"][n], tb=True, name="b_mm_dbranch")
                                  for n in range(N_BRANCH)]
        mlp_items = [item("w_down", l, g_wdown), item("w_up", l, g_wup)]
        proj_items = [item("w_out", l, g_wout), item("w_branch", l, g_wbranch)]
        va = (z, 128, unit("a_v"))
        kw_a = dict(dil=1, group=A_HEADS // A_KV_HEADS, maxdist=A_WINDOW - 1)
        dl_a, dsink = _delta(do_a, sv["o_a"], p64, lse=sv["lse_a"], sink_row=gn["sink_row"], name="b_delta_a")
        keep("a_sink", l, dsink)
        dqa = _band_dq(sv["qa"], sv["ka"], va, do_a, sv["lse_a"], dl_a, name="b_band_dq_a", **kw_a)
        dka, dz = _band_dkv(sv["qa"], sv["ka"], va, do_a, sv["lse_a"], dl_a, slab=(dz, nw), name="b_band_dkv_a",
                            **kw_a)
        dz, dg = _headnorm_bwd(dqa, (z, 512, unit("a_q")), gn["a_q"], p64, rope64, slab=(dz, nw), name="b_hn_aq")
        keep("a_qn", l, dg)
        dz, dg = _headnorm_bwd(dka, (z, 128, unit("a_k")), gn["a_k"], p64, rope64, slab=(dz, nw), name="b_hn_ak")
        keep("a_kn", l, dg)
        vb = (sv["kv_up"], HPG * B_V, B_HEADS * LANES // (HPG * B_V))
        dl_b, do_b = _delta(do_b, sv["o_b"], p64, name="b_delta_b")
        kw_b = dict(dv=B_V, causal=True, scale=scale_b)
        dqb, got = _attn_dq(sv["qb"], sv["kb"], vb, do_b, sv["lse_b"], dl_b, name="b_flash_dq_b",
                            xchg=srcs(mlp_items), **kw_b)
        store(mlp_items, got)
        dkv_items = carried + proj_items
        (dkb, dvb), got = _attn_dkv(sv["qb"], sv["kb"], vb, do_b, sv["lse_b"], dl_b, name="b_flash_dkv_b",
                                     xchg=srcs(dkv_items), **kw_b)
        store(dkv_items, got)
        d_qup, dg = _headnorm_bwd(dqb, (sv["q_up"], B_HEADS * LANES, 0), gn["b_q"], pq, rope_q, name="b_hn_bq")
        keep("b_q", l, dg)
        d_kn, dg = _headnorm_bwd(dkb, (sv["kv_up"], B_HEADS * LANES, 0), gn["b_k"], p64, None, name="b_hn_bk")
        keep("b_k", l, dg)
        dkp = _kp_reduce(dkb, name="b_kp_reduce")
        dz, dg = _headnorm_bwd(dkp, (z, LANES, unit("b_kr")), gn["b_kr"], pkr, rope_k, slab=(dz, nw), name="b_hn_kr")
        keep("b_kr", l, dg)
        d_kvup = jnp.concatenate([d_kn, dvb], axis=1)
        g_ukv = _mm(sv["ckv"], d_kvup, ta=True, name="b_mm_wukv")
        dckv = _mm(d_kvup, wl["ukv"], tb=True, name="b_mm_dckv")
        dz, dg = _rmsnorm_bwd(dckv, (z, B_KV_LORA, unit("b_ckv")), gn["b_kva_norm"], out_dtype=BF16, slab=(dz, nw),
                              name="b_norm_ckv")
        keep("b_kva_norm", l, dg)
        g_uq = _mm(sv["cq"], d_qup, ta=True, name="b_mm_wuq")
        dcq = _mm(d_qup, wl["uq"], tb=True, name="b_mm_dcq")
        dz, dg = _rmsnorm_bwd(dcq, (z, B_Q_LORA, unit("b_cq")), gn["b_qa_norm"], out_dtype=BF16, slab=(dz, nw),
                              name="b_norm_cq")
        keep("b_qa_norm", l, dg)
        g_wuq = g_uq.reshape(B_Q_LORA, B_HEADS, LANES)[:, :, :B_NOPE + B_ROPE].reshape(B_Q_LORA, -1)
        g_wukv = jnp.concatenate(
            [g_ukv[:, :B_HEADS * LANES].reshape(B_KV_LORA, B_HEADS, LANES)[:, :, :B_NOPE],
             g_ukv[:, B_HEADS * LANES:].reshape(B_KV_LORA, B_HEADS, B_V)], axis=-1).reshape(B_KV_LORA, -1)
        do_cg, dl_cg = _combine_bwd(do_c, sv["o_c"], sv["lse_cg"], [dil for _, dil in C_PATTERNS], p64,
                                    name="b_combine")
        for g, (win, dil) in enumerate(C_PATTERNS):
            vc = (z, 512, unit("c%d" % (3 * g + 2)))
            kw_c = dict(dil=dil, group=1, maxdist=win // dil)
            args = (sv["qc"][g], sv["kc"][g], vc, do_cg[g], sv["lse_cg"][g], dl_cg[g])
            dq = _band_dq(*args, name="b_band_dq_c%d" % g, **kw_c)
            if dil == 1:
                dk, dz = _band_dkv(*args, slab=(dz, nw), name="b_band_dkv_c%d" % g, **kw_c)
            else:
                dk, dv_ = _band_dkv(*args, name="b_band_dkv_c%d" % g, **kw_c)
                vcol = lay["c%d" % (3 * g + 2)]["off"]
                dz = lax.dynamic_update_slice(dz, dv_, (0, vcol))
            dz, dg = _headnorm_bwd(dq, (z, 512, unit("c%d" % (3 * g))), gn["c_q"][g], p64, rope64, dil=dil,
                                   slab=(dz, nw), name="b_hn_cq%d" % g)
            keep("c_q%d" % g, l, dg)
            dz, dg = _headnorm_bwd(dk, (z, 512, unit("c%d" % (3 * g + 1))), gn["c_k"][g], p64, rope64, dil=dil,
                                   slab=(dz, nw), name="b_hn_ck%d" % g)
            keep("c_k%d" % g, l, dg)
        mv = (sv["mkv"], HPG * M_HD, M_HEADS * M_HD // (HPG * M_HD))
        dl_m, do_m = _delta(do_m, sv["o_m"], p128, name="b_delta_m")
        kw_m = dict(dv=M_HD, causal=False, scale=scale_m)
        dmq = _attn_dq(sv["mq"], sv["mk"], mv, do_m, sv["lse_m"], dl_m, name="b_flash_dq_m", **kw_m)
        dmk, dmv = _attn_dkv(sv["mq"], sv["mk"], mv, do_m, sv["lse_m"], dl_m, name="b_flash_dkv_m", **kw_m)
        dz, dg = _headnorm_bwd(dmq, (z, M_HEADS * M_HD, unit("m_q")), gn["m_q"], p128, None, slab=(dz, nw),
                               name="b_hn_mq")
        keep("m_qn", l, dg)
        d_mk, dg = _headnorm_bwd(dmk, (sv["mkv"], M_HEADS * M_HD, 0), gn["m_k"], p128, None, name="b_hn_mk")
        keep("m_kn", l, dg)
        d_mkv = jnp.concatenate([d_mk, dmv], axis=1)
        g_wmkv = _mm(sv["memn"], d_mkv, ta=True, name="b_mm_wmkv")
        dmemn = _mm(d_mkv, wl["m_w_kv"], tb=True, name="b_mm_dmemn")
        _, dg = _rmsnorm_bwd(dmemn, (mem, d_model, 0), gn["m_g_mem"], name="b_norm_mem")
        keep("m_g_mem", l, dg)
        bm_items = [item("b_w_ukv", l, g_wukv), item("b_w_uq", l, g_wuq), item("m_w_kv", l, g_wmkv)]
        g_in, got = _mm(sv["h"], dz, ta=True, name="b_mm_win", xchg=srcs(bm_items))
        store(bm_items, got)
        carried = [item("w_in", l, _unpermute_cols(g_in, lay, orig_names))]
        last_items = carried if l == 0 else []
        dh, got = _mm(dz, wl["w_in"], tb=True, name="b_mm_dh", xchg=srcs(last_items))
        store(last_items, got)
        dx, dg = _rmsnorm_bwd(dh, (sv["x"], d_model, 0), gn["g_mix"], dres=dx2, name="b_norm_mix")
        keep("g_mix", l, dg)

    cat = {k: jnp.concatenate(v, axis=0) for k, v in rows.items()}
    heads = lambda a, nh, hd: a.reshape(n_layers, nh, hd).sum(axis=1)
    gsmall = dict(
        g_mix=cat["g_mix"], b_gate=cat["b_gate"], g_mlp=cat["g_mlp"], m_g_mem=cat["m_g_mem"],
        b_qa_norm=cat["b_qa_norm"], b_kva_norm=cat["b_kva_norm"],
        a_sink=cat["a_sink"][:, ::A_HD], a_qn=heads(cat["a_qn"], A_HEADS, A_HD), a_kn=heads(cat["a_kn"], A_KV_HEADS, A_HD),
        b_qn=heads(cat["b_q"], B_HEADS, LANES)[:, :B_NOPE + B_ROPE],
        b_kn=jnp.concatenate([heads(cat["b_k"], B_HEADS, LANES)[:, :B_NOPE], cat["b_kr"][:, :B_ROPE]], axis=1),
        c_qn=jnp.stack([heads(cat["c_q%d" % g], C_HEADS, C_HD) for g in range(3)], axis=1),
        c_kn=jnp.stack([heads(cat["c_k%d" % g], C_HEADS, C_HD) for g in range(3)], axis=1),
        m_qn=heads(cat["m_qn"], M_HEADS, M_HD), m_kn=heads(cat["m_kn"], M_HEADS, M_HD))

    out_g, out_d, out_m, out_v = {}, {}, {}, {}
    for name in BIG:
        shard = w[name].shape
        two = lambda a: a.reshape(-1, shard[-1])
        pieces = [r.reshape(N_DEV, -1, shard[-1]) for r in recvs[name]]
        g, dlt, mn, vn = _adam(pieces, two(w[name]), two(moments_m[name]), two(moments_v[name]), name="adam_" + name)
        out_g[name], out_d[name], out_m[name], out_v[name] = (a.reshape(shard) for a in (g, dlt, mn, vn))
    sizes = [int(np.prod(w[n].shape)) for n in SMALL]
    total = sum(sizes)
    padded = -(-total // (8 * LANES)) * (8 * LANES)
    pack = lambda d: jnp.pad(jnp.concatenate([jnp.reshape(d[n], (-1,)) for n in SMALL]), (0, padded - total)).reshape(-1, LANES)
    gpack = pack(gsmall)
    recv = _exchange_many([(gpack, False)], name="ag_small_grads")[0]
    packed = _adam([recv], pack(w), pack(moments_m), pack(moments_v), name="adam_small")
    offs = np.cumsum([0] + sizes)
    for a, out in zip(packed, (out_g, out_d, out_m, out_v)):
        flat = a.reshape(-1)
        for n, o, sz in zip(SMALL, offs[:-1], sizes):
            out[n] = flat[o:o + sz].reshape(w[n].shape)
    return loss, dx, out_g, out_d, out_m, out_v


def kernel(x, mem, positions, g_mix, w_in, b_gate, a_qn, a_kn, a_sink, b_qa_norm, b_kva_norm, b_w_uq, b_w_ukv, b_qn, b_kn, c_qn, c_kn, m_g_mem, m_w_kv, m_qn, m_kn, w_branch, w_out, g_mlp, w_up, w_down, loss_target, m_g_mix, m_w_in, m_b_gate, m_a_qn, m_a_kn, m_a_sink, m_b_qa_norm, m_b_kva_norm, m_b_w_uq, m_b_w_ukv, m_b_qn, m_b_kn, m_c_qn, m_c_kn, m_m_g_mem, m_m_w_kv, m_m_qn, m_m_kn, m_w_branch, m_w_out, m_g_mlp, m_w_up, m_w_down, v_g_mix, v_w_in, v_b_gate, v_a_qn, v_a_kn, v_a_sink, v_b_qa_norm, v_b_kva_norm, v_b_w_uq, v_b_w_ukv, v_b_qn, v_b_kn, v_c_qn, v_c_kn, v_m_g_mem, v_m_w_kv, v_m_qn, v_m_kn, v_w_branch, v_w_out, v_g_mlp, v_w_up, v_w_down):
    w = dict(g_mix=g_mix, w_in=w_in, b_gate=b_gate, a_qn=a_qn, a_kn=a_kn, a_sink=a_sink, b_qa_norm=b_qa_norm,
             b_kva_norm=b_kva_norm, b_w_uq=b_w_uq, b_w_ukv=b_w_ukv, b_qn=b_qn, b_kn=b_kn, c_qn=c_qn, c_kn=c_kn,
             m_g_mem=m_g_mem, m_w_kv=m_w_kv, m_qn=m_qn, m_kn=m_kn, w_branch=w_branch, w_out=w_out, g_mlp=g_mlp,
             w_up=w_up, w_down=w_down)
    mm_ = dict(g_mix=m_g_mix, w_in=m_w_in, b_gate=m_b_gate, a_qn=m_a_qn, a_kn=m_a_kn, a_sink=m_a_sink,
               b_qa_norm=m_b_qa_norm, b_kva_norm=m_b_kva_norm, b_w_uq=m_b_w_uq, b_w_ukv=m_b_w_ukv, b_qn=m_b_qn,
               b_kn=m_b_kn, c_qn=m_c_qn, c_kn=m_c_kn, m_g_mem=m_m_g_mem, m_w_kv=m_m_w_kv, m_qn=m_m_qn, m_kn=m_m_kn,
               w_branch=m_w_branch, w_out=m_w_out, g_mlp=m_g_mlp, w_up=m_w_up, w_down=m_w_down)
    vv = dict(g_mix=v_g_mix, w_in=v_w_in, b_gate=v_b_gate, a_qn=v_a_qn, a_kn=v_a_kn, a_sink=v_a_sink,
              b_qa_norm=v_b_qa_norm, b_kva_norm=v_b_kva_norm, b_w_uq=v_b_w_uq, b_w_ukv=v_b_w_ukv, b_qn=v_b_qn,
              b_kn=v_b_kn, c_qn=v_c_qn, c_kn=v_c_kn, m_g_mem=v_m_g_mem, m_w_kv=v_m_w_kv, m_qn=v_m_qn, m_kn=v_m_kn,
              w_branch=v_w_branch, w_out=v_w_out, g_mlp=v_g_mlp, w_up=v_w_up, w_down=v_w_down)
    loss, dx, g, dlt, mn, vn = _step(x[0], mem[0], positions[0], w, mm_, vv, loss_target[0])
    return (loss, dx[None], *[g[n] for n in WEIGHTS], *[dlt[n] for n in WEIGHTS], *[mn[n] for n in WEIGHTS],
            *[vn[n] for n in WEIGHTS])
```
